```python
import math
import jax, jax.numpy as jnp
from jax import lax
import numpy as np

D_MODEL = 1024
BATCH = 8
SEQ = 2048
DEPTH = 2

GRID_W = 64
NORM_EPS = 1e-6

NA_HEADS = 8
NA_HEAD_DIM = 64
NA_WIDTH = NA_HEADS * NA_HEAD_DIM
NA_KH_MAX = 8
NA_KW = 16

SSD_D_INNER = D_MODEL
SSD_HEAD_DIM = 64
SSD_HEADS = SSD_D_INNER // SSD_HEAD_DIM
SSD_GROUPS = 4
SSD_STATE = 128
SSD_CONV = 4
SSD_CHUNK = 128
SSD_CONV_DIM = SSD_D_INNER + 2 * SSD_GROUPS * SSD_STATE
DT_MIN = 0.001
DT_MAX = 0.1

EVEN_SPLITS = [NA_WIDTH, 2 * NA_WIDTH, 3 * NA_WIDTH,
               3 * NA_WIDTH + SSD_D_INNER,
               3 * NA_WIDTH + SSD_D_INNER + SSD_CONV_DIM]
EVEN_IN_WIDTH = 3 * NA_WIDTH + SSD_D_INNER + SSD_CONV_DIM + 2 * SSD_HEADS
EVEN_MIX_WIDTH = NA_WIDTH + SSD_D_INNER

GQA_HEADS = 16
GQA_KV_HEADS = 4
GQA_HEAD_DIM = 64
GQA_Q_WIDTH = GQA_HEADS * GQA_HEAD_DIM
GQA_KV_WIDTH = GQA_KV_HEADS * GQA_HEAD_DIM
ODD_IN_WIDTH = GQA_Q_WIDTH + 2 * GQA_KV_WIDTH
ROPE_THETA = 10000.0
Q_BLOCK = 128

FFN_HIDDEN = -(-8 * D_MODEL // (3 * 256)) * 256

N_EVEN = (DEPTH + 1) // 2
N_ODD = DEPTH // 2

kernel_name = "hybrid_natten_ssd_axial_gqa_encoder"


def rms_norm(x, g):
    xf = x.astype(jnp.float32)
    y = xf * lax.rsqrt(jnp.mean(xf * xf, axis=-1, keepdims=True) + NORM_EPS)
    return (y * g.astype(jnp.float32)).astype(x.dtype)


def neighbourhood_attention(q, k, v, rpb):
    B, S, H, Dh = q.shape
    rows = S // GRID_W
    kh = min(NA_KH_MAX, rows)
    qg = q.reshape(B, rows, GRID_W, H, Dh)
    kg = k.reshape(B, rows, GRID_W, H, Dh)
    vg = v.reshape(B, rows, GRID_W, H, Dh)
    r = jnp.arange(rows)
    row_start = jnp.clip(r - kh // 2, 0, rows - kh)
    key_rows = row_start[:, None] + jnp.arange(kh)[None, :]
    k_blk = kg[:, key_rows]
    v_blk = vg[:, key_rows]
    c = jnp.arange(GRID_W)
    col_start = jnp.clip(c - NA_KW // 2, 0, GRID_W - NA_KW)
    in_win = (c[None, :] >= col_start[:, None]) & (c[None, :] < col_start[:, None] + NA_KW)
    dy = key_rows - r[:, None] + (NA_KH_MAX - 1)
    dx = jnp.clip(c[None, :] - c[:, None], -(NA_KW - 1), NA_KW - 1) + (NA_KW - 1)
    bias = rpb.astype(jnp.float32)[:, dy[:, None, :, None], dx[None, :, None, :]]
    scale = Dh ** -0.5
    s = jnp.einsum('brchd,brjkhd->bhrcjk', qg, k_blk).astype(jnp.float32) * scale + bias[None]
    s = jnp.where(in_win[:, None, :], s, -jnp.inf)
    p = jax.nn.softmax(s.reshape(B, H, rows, GRID_W, kh * GRID_W), axis=-1)
    p = p.reshape(B, H, rows, GRID_W, kh, GRID_W).astype(v.dtype)
    o = jnp.einsum('bhrcjk,brjkhd->brchd', p, v_blk)
    return o.reshape(B, S, H * Dh)


def centred_depthwise_conv(x, w, b):
    K, C = w.shape
    left = K // 2
    xp = jnp.pad(x, ((0, 0), (left, K - 1 - left), (0, 0)))
    y = lax.conv_general_dilated(xp, w[:, None, :].astype(x.dtype), window_strides=(1,), padding='VALID',
                                 dimension_numbers=('NWC', 'WIO', 'NWC'), feature_group_count=C)
    return y + b.astype(x.dtype)


def ssd_scan(x, dt, A, Bm, Cm):
    out_dtype = x.dtype
    Bsz, S, H, P = x.shape
    G, N = Bm.shape[2], Bm.shape[3]
    R = H // G
    L = SSD_CHUNK
    nc = S // L
    f32 = jnp.float32
    x = x.astype(f32); dt = dt.astype(f32)
    xc = (x * dt[..., None]).reshape(Bsz, nc, L, G, R, P)
    Bc = Bm.astype(f32).reshape(Bsz, nc, L, G, N)
    Cc = Cm.astype(f32).reshape(Bsz, nc, L, G, N)
    a = (dt * A.astype(f32)).reshape(Bsz, nc, L, G, R)
    a_cum = jnp.moveaxis(jnp.cumsum(a, axis=2), 2, -1)
    seg = a_cum[..., :, None] - a_cum[..., None, :]
    lower = jnp.tril(jnp.ones((L, L), dtype=bool))
    decay = jnp.exp(jnp.where(lower, seg, -jnp.inf))
    cb = jnp.einsum('bclgn,bcsgn->bcgls', Cc, Bc)
    y_diag = jnp.einsum('bcgls,bcgrls,bcsgrp->bclgrp', cb, decay, xc)
    decay_to_end = jnp.exp(a_cum[..., -1:] - a_cum)
    states = jnp.einsum('bcsgn,bcgrs,bcsgrp->bcgrpn', Bc, decay_to_end, xc)
    chunk_decay = jnp.exp(a_cum[..., -1])

    def step(h, inp):
        s_c, d_c = inp
        return h * d_c[..., None, None] + s_c, h

    h0 = jnp.zeros((Bsz, G, R, P, N), f32)
    _, h_prev = lax.scan(step, h0, (jnp.moveaxis(states, 1, 0), jnp.moveaxis(chunk_decay, 1, 0)))
    h_prev = jnp.moveaxis(h_prev, 0, 1)
    y_off = jnp.einsum('bclgn,bcgrpn,bcgrl->bclgrp', Cc, h_prev, jnp.exp(a_cum))
    return (y_diag + y_off).reshape(Bsz, S, H, P).astype(out_dtype)


def gated_group_rms_norm(y, z, g):
    Bsz, S, Dn = y.shape
    h = (y * jax.nn.silu(z)).astype(jnp.float32).reshape(Bsz, S, SSD_GROUPS, Dn // SSD_GROUPS)
    h = h * lax.rsqrt(jnp.mean(h * h, axis=-1, keepdims=True) + NORM_EPS)
    return (h.reshape(Bsz, S, Dn) * g.astype(jnp.float32)).astype(y.dtype)


def even_mixer(x, mix_norm, w_in, na_q_norm, na_k_norm, na_rel_bias, conv_w, conv_b,
               dt_bias, A_log, D_skip, out_norm, w_out):
    Bsz, S, _ = x.shape
    proj = rms_norm(x, mix_norm) @ w_in
    q, k, v, z, xbc, dt_raw = jnp.split(proj, EVEN_SPLITS, axis=-1)
    q = rms_norm(q.reshape(Bsz, S, NA_HEADS, NA_HEAD_DIM), na_q_norm)
    k = rms_norm(k.reshape(Bsz, S, NA_HEADS, NA_HEAD_DIM), na_k_norm)
    v = v.reshape(Bsz, S, NA_HEADS, NA_HEAD_DIM)
    na_out = neighbourhood_attention(q, k, v, na_rel_bias)
    xbc = jax.nn.silu(centred_depthwise_conv(xbc, conv_w, conv_b))
    xs, bm, cm = jnp.split(xbc, [SSD_D_INNER, SSD_D_INNER + SSD_GROUPS * SSD_STATE], axis=-1)
    xs = xs.reshape(Bsz, S, SSD_HEADS, SSD_HEAD_DIM)
    bm = bm.reshape(Bsz, S, SSD_GROUPS, SSD_STATE)
    cm = cm.reshape(Bsz, S, SSD_GROUPS, SSD_STATE)
    dt = jax.nn.softplus(dt_raw.astype(jnp.float32).reshape(Bsz, S, 2, SSD_HEADS)
                         + dt_bias.astype(jnp.float32))
    A = -jnp.exp(A_log.astype(jnp.float32))
    y_fwd = ssd_scan(xs, dt[:, :, 0], A[0], bm, cm)
    y_bwd = jnp.flip(ssd_scan(jnp.flip(xs, 1), jnp.flip(dt[:, :, 1], 1), A[1],
                              jnp.flip(bm, 1), jnp.flip(cm, 1)), 1)
    y = y_fwd + y_bwd + D_skip.astype(xs.dtype)[:, None] * xs
    ssd_out = gated_group_rms_norm(y.reshape(Bsz, S, SSD_D_INNER), z, out_norm)
    return jnp.concatenate([na_out, ssd_out], axis=-1) @ w_out


def axial_rope_tables(S):
    t = jnp.arange(S)
    row = (t // GRID_W).astype(jnp.float32)
    col = (t % GRID_W).astype(jnp.float32)
    axis_dims = GQA_HEAD_DIM // 2
    freqs = ROPE_THETA ** (-jnp.arange(0, axis_dims, 2, dtype=jnp.float32) / axis_dims)
    ang = jnp.concatenate([row[:, None] * freqs, col[:, None] * freqs], axis=-1)
    return jnp.cos(ang), jnp.sin(ang)


def apply_rope(x, cos, sin):
    xf = x.astype(jnp.float32).reshape(*x.shape[:-1], x.shape[-1] // 2, 2)
    x0, x1 = xf[..., 0], xf[..., 1]
    c = cos[None, :, None, :]
    s = sin[None, :, None, :]
    out = jnp.stack([x0 * c - x1 * s, x0 * s + x1 * c], axis=-1)
    return out.reshape(x.shape).astype(x.dtype)


def gqa_block_attention(q, k, v):
    Bsz, S, H, Dh = q.shape
    KV = k.shape[2]
    rep = H // KV
    nb = S // Q_BLOCK
    qb = jnp.moveaxis(q.reshape(Bsz, nb, Q_BLOCK, KV, rep, Dh), 1, 0)
    scale = Dh ** -0.5

    def one_block(q_blk):
        s = jnp.einsum('bqgrd,bkgd->bgrqk', q_blk, k).astype(jnp.float32) * scale
        p = jax.nn.softmax(s, axis=-1).astype(v.dtype)
        return jnp.einsum('bgrqk,bkgd->bqgrd', p, v)

    o = lax.map(one_block, qb)
    return jnp.moveaxis(o, 0, 1).reshape(Bsz, S, H * Dh)


def odd_mixer(x, mix_norm, w_qkv, q_norm, k_norm, w_out):
    Bsz, S, _ = x.shape
    proj = rms_norm(x, mix_norm) @ w_qkv
    q, k, v = jnp.split(proj, [GQA_Q_WIDTH, GQA_Q_WIDTH + GQA_KV_WIDTH], axis=-1)
    q = rms_norm(q.reshape(Bsz, S, GQA_HEADS, GQA_HEAD_DIM), q_norm)
    k = rms_norm(k.reshape(Bsz, S, GQA_KV_HEADS, GQA_HEAD_DIM), k_norm)
    v = v.reshape(Bsz, S, GQA_KV_HEADS, GQA_HEAD_DIM)
    cos, sin = axial_rope_tables(S)
    q = apply_rope(q, cos, sin)
    k = apply_rope(k, cos, sin)
    return gqa_block_attention(q, k, v) @ w_out


def swiglu_ffn(x, norm_g, w13, w2):
    g, u = jnp.split(rms_norm(x, norm_g) @ w13, 2, axis=-1)
    return (jax.nn.silu(g) * u) @ w2


def setup_inputs(seed: int = 0) -> dict:
    key = jax.random.key(seed)
    ks = jax.random.split(key, 24)
    f32 = jnp.float32

    def normal(k, shape, scale):
        return jax.random.normal(k, shape, f32) * scale

    def gain(k, shape):
        return 1.0 + 0.02 * jax.random.normal(k, shape, f32)

    dt0 = jnp.exp(jax.random.uniform(ks[10], (N_EVEN, 2, SSD_HEADS), f32)
                  * (math.log(DT_MAX) - math.log(DT_MIN)) + math.log(DT_MIN))
    dt_bias = dt0 + jnp.log(-jnp.expm1(-dt0))
    A_log = jnp.log(jax.random.uniform(ks[11], (N_EVEN, 2, SSD_HEADS), f32, 1.0, 16.0))
    return {
        "x": jax.random.normal(ks[0], (BATCH, SEQ, D_MODEL), f32),
        "even_mix_norm": gain(ks[1], (N_EVEN, D_MODEL)),
        "even_w_in": normal(ks[2], (N_EVEN, D_MODEL, EVEN_IN_WIDTH), D_MODEL ** -0.5),
        "na_q_norm": gain(ks[3], (N_EVEN, NA_HEAD_DIM)),
        "na_k_norm": gain(ks[4], (N_EVEN, NA_HEAD_DIM)),
        "na_rel_bias": normal(ks[5], (N_EVEN, NA_HEADS, 2 * NA_KH_MAX - 1, 2 * NA_KW - 1), 0.05),
        "ssd_conv_w": normal(ks[6], (N_EVEN, SSD_CONV, SSD_CONV_DIM), SSD_CONV ** -0.5),
        "ssd_conv_b": normal(ks[7], (N_EVEN, SSD_CONV_DIM), 0.01),
        "ssd_dt_bias": dt_bias,
        "ssd_A_log": A_log,
        "ssd_D": gain(ks[8], (N_EVEN, SSD_HEADS)),
        "ssd_out_norm": gain(ks[9], (N_EVEN, SSD_D_INNER)),
        "even_w_out": normal(ks[12], (N_EVEN, EVEN_MIX_WIDTH, D_MODEL), EVEN_MIX_WIDTH ** -0.5),
        "odd_mix_norm": gain(ks[13], (N_ODD, D_MODEL)),
        "odd_w_qkv": normal(ks[14], (N_ODD, D_MODEL, ODD_IN_WIDTH), D_MODEL ** -0.5),
        "gqa_q_norm": gain(ks[15], (N_ODD, GQA_HEAD_DIM)),
        "gqa_k_norm": gain(ks[16], (N_ODD, GQA_HEAD_DIM)),
        "odd_w_out": normal(ks[17], (N_ODD, GQA_Q_WIDTH, D_MODEL), GQA_Q_WIDTH ** -0.5),
        "ffn_norm": gain(ks[18], (DEPTH, D_MODEL)),
        "ffn_w13": normal(ks[19], (DEPTH, D_MODEL, 2 * FFN_HIDDEN), D_MODEL ** -0.5),
        "ffn_w2": normal(ks[20], (DEPTH, FFN_HIDDEN, D_MODEL), FFN_HIDDEN ** -0.5),
    }


def reference(x, even_mix_norm, even_w_in, na_q_norm, na_k_norm, na_rel_bias, ssd_conv_w, ssd_conv_b,
              ssd_dt_bias, ssd_A_log, ssd_D, ssd_out_norm, even_w_out, odd_mix_norm, odd_w_qkv,
              gqa_q_norm, gqa_k_norm, odd_w_out, ffn_norm, ffn_w13, ffn_w2):
    for layer in range(DEPTH):
        i = layer // 2
        if layer % 2 == 0:
            x = x + even_mixer(x, even_mix_norm[i], even_w_in[i], na_q_norm[i], na_k_norm[i],
                               na_rel_bias[i], ssd_conv_w[i], ssd_conv_b[i], ssd_dt_bias[i],
                               ssd_A_log[i], ssd_D[i], ssd_out_norm[i], even_w_out[i])
        else:
            x = x + odd_mixer(x, odd_mix_norm[i], odd_w_qkv[i], gqa_q_norm[i], gqa_k_norm[i], odd_w_out[i])
        x = x + swiglu_ffn(x, ffn_norm[layer], ffn_w13[layer], ffn_w2[layer])
    return x
```

```python
import functools
import math

import jax
import jax.numpy as jnp
from jax import lax
from jax.experimental import pallas as pl
from jax.experimental.pallas import tpu as pltpu

F32 = jnp.float32
BF16 = jnp.bfloat16

D_MODEL = 1024
GRID_W = 64
NORM_EPS = 1e-6

NA_HEADS = 8
NA_HEAD_DIM = 64
NA_WIDTH = NA_HEADS * NA_HEAD_DIM
NA_KH = 8
NA_KW = 16

SSD_D_INNER = 1024
SSD_HEAD_DIM = 64
SSD_HEADS = 16
SSD_GROUPS = 4
SSD_STATE = 128
SSD_CONV = 4
SSD_CHUNK = 128
SSD_CONV_DIM = SSD_D_INNER + 2 * SSD_GROUPS * SSD_STATE
SSD_GROUP_WIDTH = SSD_D_INNER // SSD_GROUPS
SSD_HEADS_PER_GROUP = SSD_HEADS // SSD_GROUPS

EVEN_MAIN_WIDTH = 3 * NA_WIDTH + SSD_D_INNER + SSD_CONV_DIM
EVEN_XBC_OFFSET = 3 * NA_WIDTH + SSD_D_INNER

GQA_HEADS = 16
GQA_KV_HEADS = 4
GQA_HEAD_DIM = 64
GQA_REP = GQA_HEADS // GQA_KV_HEADS
GQA_Q_WIDTH = GQA_HEADS * GQA_HEAD_DIM
GQA_KV_WIDTH = GQA_KV_HEADS * GQA_HEAD_DIM
ROPE_THETA = 10000.0

FFN_HIDDEN = 2816

V7X_LANES = 128
V7X_VMEM_LIMIT = 56 * 1024 * 1024
MASK_VALUE = -1e30


def _params(dims):
    return pltpu.CompilerParams(dimension_semantics=dims, vmem_limit_bytes=V7X_VMEM_LIMIT)


def _silu(v):
    return v * (1.0 / (1.0 + jnp.exp(-v)))


def _softplus(v):
    return jnp.maximum(v, 0.0) + jnp.log(1.0 + jnp.exp(-jnp.abs(v)))


def _rms_rows(x, g):
    ms = jnp.mean(x * x, axis=-1, keepdims=True)
    return x * lax.rsqrt(ms + NORM_EPS) * g


def _dot(a, b):
    return jnp.dot(a, b, preferred_element_type=F32)


def _dot_nt(a, b):
    return lax.dot_general(a, b, (((1,), (1,)), ((), ())), preferred_element_type=F32)


def _dot_tn(a, b):
    return lax.dot_general(a, b, (((0,), (0,)), ((), ())), preferred_element_type=F32)


def _dot_exact(a, b):
    return jnp.dot(a, b, preferred_element_type=F32, precision=lax.Precision.HIGHEST)


def _norm_proj_kernel(x_ref, g_ref, w_ref, o_ref, xn_ref):
    @pl.when(pl.program_id(1) == 0)
    def _():
        xn_ref[...] = _rms_rows(x_ref[...], g_ref[...]).astype(BF16)

    o_ref[...] = _dot(xn_ref[...], w_ref[...])


def _norm_proj2_kernel(x_ref, g_ref, w_ref, w2_ref, o_ref, o2_ref, xn_ref):
    @pl.when(pl.program_id(1) == 0)
    def _():
        xn = _rms_rows(x_ref[...], g_ref[...]).astype(BF16)
        xn_ref[...] = xn
        o2_ref[...] = _dot(xn, w2_ref[...])

    o_ref[...] = _dot(xn_ref[...], w_ref[...])


def _norm_proj(x2d, g, w, *, tm, tn, w_small=None, name):
    m, d = x2d.shape
    n = w.shape[1]
    grid = (m // tm, n // tn)
    in_specs = [pl.BlockSpec((tm, d), lambda i, j: (i, 0)),
                pl.BlockSpec((1, d), lambda i, j: (0, 0)),
                pl.BlockSpec((d, tn), lambda i, j: (0, j))]
    out_specs = pl.BlockSpec((tm, tn), lambda i, j: (i, j))
    out_shape = jax.ShapeDtypeStruct((m, n), F32)
    scratch = [pltpu.VMEM((tm, d), BF16)]
    if w_small is None:
        return pl.pallas_call(_norm_proj_kernel, grid=grid, in_specs=in_specs, out_specs=out_specs,
                              out_shape=out_shape, scratch_shapes=scratch,
                              compiler_params=_params(("parallel", "arbitrary")), name=name)(x2d, g, w)
    ns = w_small.shape[1]
    in_specs.append(pl.BlockSpec((d, ns), lambda i, j: (0, 0)))
    return pl.pallas_call(_norm_proj2_kernel, grid=grid, in_specs=in_specs,
                          out_specs=[out_specs, pl.BlockSpec((tm, ns), lambda i, j: (i, 0))],
                          out_shape=[out_shape, jax.ShapeDtypeStruct((m, ns), F32)], scratch_shapes=scratch,
                          compiler_params=_params(("parallel", "arbitrary")), name=name)(x2d, g, w, w_small)


CONV_ROWS = 512
CONV_COLS = 512
CONV_HALO = 8


def _conv_kernel(cur_ref, prev_ref, next_ref, w_ref, b_ref, o_ref, *, n_row_blocks):
    c = pl.program_id(1)
    prev = jnp.where(c > 0, prev_ref[0], 0.0)
    nxt = jnp.where(c < n_row_blocks - 1, next_ref[0], 0.0)
    win = jnp.concatenate([prev, cur_ref[0], nxt], axis=0)
    n = CONV_ROWS + 2 * CONV_HALO
    left = SSD_CONV // 2
    acc = jnp.broadcast_to(b_ref[...], (CONV_ROWS, CONV_COLS))
    for k in range(SSD_CONV):
        shift = (left - k) % n
        tap = win if shift == 0 else pltpu.roll(win, shift, axis=0)
        acc = acc + tap[CONV_HALO:CONV_HALO + CONV_ROWS] * w_ref[k:k + 1, :]
    o_ref[0] = _silu(acc)


def _conv_silu(proj3d, conv_w, conv_b):
    b, s, _ = proj3d.shape
    n_row_blocks = s // CONV_ROWS
    n_col_blocks = SSD_CONV_DIM // CONV_COLS
    col0 = EVEN_XBC_OFFSET // CONV_COLS
    halo_per_block = CONV_ROWS // CONV_HALO
    n_halo_blocks = s // CONV_HALO
    kern = functools.partial(_conv_kernel, n_row_blocks=n_row_blocks)
    return pl.pallas_call(
        kern, grid=(b, n_row_blocks, n_col_blocks),
        in_specs=[
            pl.BlockSpec((1, CONV_ROWS, CONV_COLS), lambda i, c, j: (i, c, col0 + j)),
            pl.BlockSpec((1, CONV_HALO, CONV_COLS),
                         lambda i, c, j: (i, jnp.maximum(c * halo_per_block - 1, 0), col0 + j)),
            pl.BlockSpec((1, CONV_HALO, CONV_COLS),
                         lambda i, c, j: (i, jnp.minimum((c + 1) * halo_per_block, n_halo_blocks - 1), col0 + j)),
            pl.BlockSpec((SSD_CONV, CONV_COLS), lambda i, c, j: (0, j)),
            pl.BlockSpec((1, CONV_COLS), lambda i, c, j: (0, j)),
        ],
        out_specs=pl.BlockSpec((1, CONV_ROWS, CONV_COLS), lambda i, c, j: (i, c, j)),
        out_shape=jax.ShapeDtypeStruct((b, s, SSD_CONV_DIM), F32),
        compiler_params=_params(("parallel", "parallel", "parallel")), name="ssd_conv_silu",
    )(proj3d, proj3d, proj3d, conv_w, conv_b)


NA_PREP_ROWS = 256


def _pair_head_rms(x, g):
    lo = lax.broadcasted_iota(jnp.int32, x.shape, 1) < NA_HEAD_DIM
    x2 = x * x
    s_lo = jnp.sum(jnp.where(lo, x2, 0.0), axis=-1, keepdims=True)
    s_hi = jnp.sum(jnp.where(lo, 0.0, x2), axis=-1, keepdims=True)
    ms = jnp.where(lo, s_lo, s_hi) * (1.0 / NA_HEAD_DIM)
    return x * lax.rsqrt(ms + NORM_EPS) * g


def _na_kernel(q_ref, k_ref, v_ref, qg_ref, kg_ref, bias_ref, o_ref, q_s, k_s, v_s, *, rows):
    scale = NA_HEAD_DIM ** -0.5

    def prep(i, carry):
        sl = pl.ds(pl.multiple_of(i * NA_PREP_ROWS, NA_PREP_ROWS), NA_PREP_ROWS)
        q_s[sl, :] = (_pair_head_rms(q_ref[0, sl, :], qg_ref[...]) * scale).astype(BF16)
        k_s[sl, :] = _pair_head_rms(k_ref[0, sl, :], kg_ref[...]).astype(BF16)
        v_s[sl, :] = v_ref[0, sl, :].astype(BF16)
        return carry

    lax.fori_loop(0, (rows * GRID_W) // NA_PREP_ROWS, prep, 0)

    n_keys = NA_KH * GRID_W

    def row_body(r, carry):
        row_start = jnp.clip(r - NA_KH // 2, 0, rows - NA_KH)
        cls = r - row_start
        q = q_s[pl.ds(pl.multiple_of(r * GRID_W, GRID_W), GRID_W), :]
        ksl = pl.ds(pl.multiple_of(row_start * GRID_W, GRID_W), n_keys)
        kk = k_s[ksl, :]
        vv = v_s[ksl, :]
        outs = []
        for h in range(2):
            hs = slice(h * NA_HEAD_DIM, (h + 1) * NA_HEAD_DIM)
            s = _dot_nt(q[:, hs], kk[:, hs]) + bias_ref[0, cls, h]
            m = jnp.max(s, axis=-1, keepdims=True)
            p = jnp.exp(s - m)
            l = jnp.sum(p, axis=-1, keepdims=True)
            outs.append(_dot(p.astype(BF16), vv[:, hs]) * (1.0 / l))
        o_ref[0, pl.ds(pl.multiple_of(r * GRID_W, GRID_W), GRID_W), :] = (
            jnp.concatenate(outs, axis=-1).astype(BF16))
        return carry

    lax.fori_loop(0, rows, row_body, 0)


def _na_bias_table(rpb):
    c = jnp.arange(GRID_W)
    col_start = jnp.clip(c - NA_KW // 2, 0, GRID_W - NA_KW)
    in_win = (c[None, :] >= col_start[:, None]) & (c[None, :] < col_start[:, None] + NA_KW)
    dx = jnp.clip(c[None, :] - c[:, None], -(NA_KW - 1), NA_KW - 1) + (NA_KW - 1)
    cls = jnp.arange(NA_KH)
    j = jnp.arange(NA_KH)
    dy = j[None, :] + (NA_KH - 1) - cls[:, None]
    t = rpb.astype(F32)[:, dy[:, :, None, None], dx[None, None, :, :]]
    t = jnp.where(in_win[None, None, None], t, MASK_VALUE)
    t = jnp.transpose(t, (0, 1, 3, 2, 4)).reshape(NA_HEADS // 2, 2, NA_KH, GRID_W, NA_KH * GRID_W)
    return jnp.transpose(t, (0, 2, 1, 3, 4))


def _neighbourhood_attention(proj3d, q_gain, k_gain, rpb):
    b, s, _ = proj3d.shape
    rows = s // GRID_W
    assert rows >= NA_KH
    bias = _na_bias_table(rpb)
    qg = jnp.tile(q_gain.astype(F32), 2)[None, :]
    kg = jnp.tile(k_gain.astype(F32), 2)[None, :]
    n_pairs = NA_HEADS // 2
    blk = (1, s, 2 * NA_HEAD_DIM)
    kern = functools.partial(_na_kernel, rows=rows)
    return pl.pallas_call(
        kern, grid=(n_pairs, b),
        in_specs=[
            pl.BlockSpec(blk, lambda p, i: (i, 0, p)),
            pl.BlockSpec(blk, lambda p, i: (i, 0, n_pairs + p)),
            pl.BlockSpec(blk, lambda p, i: (i, 0, 2 * n_pairs + p)),
            pl.BlockSpec((1, 2 * NA_HEAD_DIM), lambda p, i: (0, 0)),
            pl.BlockSpec((1, 2 * NA_HEAD_DIM), lambda p, i: (0, 0)),
            pl.BlockSpec((1, NA_KH, 2, GRID_W, NA_KH * GRID_W), lambda p, i: (p, 0, 0, 0, 0)),
        ],
        out_specs=pl.BlockSpec(blk, lambda p, i: (i, 0, p)),
        out_shape=jax.ShapeDtypeStruct((b, s, NA_WIDTH), BF16),
        scratch_shapes=[pltpu.VMEM((s, 2 * NA_HEAD_DIM), BF16)] * 3,
        compiler_params=_params(("parallel", "parallel")), name="neighbourhood_attention",
    )(proj3d, proj3d, proj3d, qg, kg, bias)


def _expand_heads(v, first):
    return jnp.concatenate(
        [jnp.broadcast_to(v[:, first + j:first + j + 1], (v.shape[0], SSD_HEAD_DIM))
         for j in range(SSD_HEADS_PER_GROUP)], axis=1)


def _ssd_kernel(xs_ref, b_ref, c_ref, z_ref, dtc_ref, dtr_ref, biasc_ref, biasr_ref, alogc_ref, alogr_ref,
                dskip_ref, gain_ref, o_ref, y_s, *, n_chunks):
    L = SSD_CHUNK
    hpg = SSD_HEADS_PER_GROUP
    li = lax.broadcasted_iota(jnp.int32, (L, L), 0)
    si = lax.broadcasted_iota(jnp.int32, (L, L), 1)
    lower = si <= li
    upper = si >= li
    tri_l = lower.astype(F32)
    tri_u = upper.astype(F32)
    a_col = -jnp.exp(alogc_ref[0])
    a_row = -jnp.exp(alogr_ref[0])
    col_is_fwd = lax.broadcasted_iota(jnp.int32, (L, 2 * hpg), 1) < hpg
    row_is_fwd = lax.broadcasted_iota(jnp.int32, (2 * hpg, L), 0) < hpg

    def chunk_slice(c):
        return pl.ds(pl.multiple_of(c * L, L), L)

    def col_terms(c):
        dt = _softplus(dtc_ref[0, 0, chunk_slice(c), :] + biasc_ref[0])
        a = dt * a_col
        cum = jnp.where(col_is_fwd, _dot_exact(tri_l, a), _dot_exact(tri_u, a))
        return dt, cum

    def fwd_body(c, h):
        sl = chunk_slice(c)
        xs = xs_ref[0, sl, :]
        bm = b_ref[0, sl, :].astype(BF16)
        cm = c_ref[0, sl, :].astype(BF16)
        dt_c, cum_c = col_terms(c)
        dt_r = _softplus(dtr_ref[0, 0, :, sl] + biasr_ref[0])
        a_r = dt_r * a_row
        cum_r = jnp.where(row_is_fwd, _dot_exact(a_r, tri_u), _dot_exact(a_r, tri_l))
        cb = _dot_nt(cm, bm)
        xs_bf = xs.astype(BF16)
        ys = []
        for j in range(hpg):
            seg_f = cum_c[:, j:j + 1] - cum_r[j:j + 1, :]
            seg_b = cum_c[:, hpg + j:hpg + j + 1] - cum_r[hpg + j:hpg + j + 1, :]
            w = (jnp.exp(jnp.where(lower, seg_f, -jnp.inf)) * dt_r[j:j + 1, :]
                 + jnp.exp(jnp.where(upper, seg_b, -jnp.inf)) * dt_r[hpg + j:hpg + j + 1, :])
            ys.append(_dot((cb * w).astype(BF16), xs_bf[:, j * SSD_HEAD_DIM:(j + 1) * SSD_HEAD_DIM]))
        y = jnp.concatenate(ys, axis=1)
        y = y + _dot(cm, h.astype(BF16)) * _expand_heads(jnp.exp(cum_c), 0)
        last = cum_c[L - 1:L, :]
        wts = dt_c * jnp.exp(last - cum_c)
        st = _dot_tn(bm, (xs * _expand_heads(wts, 0)).astype(BF16))
        y_s[sl, :] = y
        return h * _expand_heads(jnp.exp(last), 0) + st

    h0 = jnp.zeros((SSD_STATE, SSD_GROUP_WIDTH), F32)
    lax.fori_loop(0, n_chunks, fwd_body, h0)

    def bwd_body(i, h):
        c = n_chunks - 1 - i
        sl = chunk_slice(c)
        xs = xs_ref[0, sl, :]
        bm = b_ref[0, sl, :].astype(BF16)
        cm = c_ref[0, sl, :].astype(BF16)
        dt_c, cum_c = col_terms(c)
        y = y_s[sl, :] + _dot(cm, h.astype(BF16)) * _expand_heads(jnp.exp(cum_c), hpg)
        first = cum_c[0:1, :]
        wts = dt_c * jnp.exp(first - cum_c)
        st = _dot_tn(bm, (xs * _expand_heads(wts, hpg)).astype(BF16))
        y = y + dskip_ref[0] * xs
        gated = y * _silu(z_ref[0, sl, :])
        ms = jnp.mean(gated * gated, axis=-1, keepdims=True)
        o_ref[0, sl, :] = (gated * lax.rsqrt(ms + NORM_EPS) * gain_ref[0]).astype(BF16)
        return h * _expand_heads(jnp.exp(first), hpg) + st

    lax.fori_loop(0, n_chunks, bwd_body, h0)


def _group_major(v):
    return jnp.transpose(v.astype(F32).reshape(2, SSD_GROUPS, SSD_HEADS_PER_GROUP), (1, 0, 2)).reshape(
        SSD_GROUPS, 2 * SSD_HEADS_PER_GROUP)


def _ssd_mixer(xbc3d, proj3d, dt_raw, dt_bias, a_log, d_skip, out_gain):
    b, s, _ = xbc3d.shape
    n_chunks = s // SSD_CHUNK
    g, hpg = SSD_GROUPS, SSD_HEADS_PER_GROUP
    dt5 = dt_raw.reshape(b, s, 2, g, hpg)
    dt_col = jnp.transpose(dt5, (0, 3, 1, 2, 4)).reshape(b, g, s, 2 * hpg)
    dt_row = jnp.transpose(dt5, (0, 3, 2, 4, 1)).reshape(b, g, 2 * hpg, s)
    bias_g = _group_major(dt_bias)
    alog_g = _group_major(a_log)
    dskip = jnp.repeat(d_skip.astype(F32), SSD_HEAD_DIM).reshape(g, 1, SSD_GROUP_WIDTH)
    gain = out_gain.astype(F32).reshape(g, 1, SSD_GROUP_WIDTH)
    z_col0 = (3 * NA_WIDTH) // SSD_GROUP_WIDTH
    b_col0 = SSD_D_INNER // SSD_STATE
    c_col0 = b_col0 + SSD_GROUPS
    kern = functools.partial(_ssd_kernel, n_chunks=n_chunks)
    small = lambda shape: pl.BlockSpec((1,) + shape, lambda i, k: (k, 0, 0))
    return pl.pallas_call(
        kern, grid=(b, g),
        in_specs=[
            pl.BlockSpec((1, s, SSD_GROUP_WIDTH), lambda i, k: (i, 0, k)),
            pl.BlockSpec((1, s, SSD_STATE), lambda i, k: (i, 0, b_col0 + k)),
            pl.BlockSpec((1, s, SSD_STATE), lambda i, k: (i, 0, c_col0 + k)),
            pl.BlockSpec((1, s, SSD_GROUP_WIDTH), lambda i, k: (i, 0, z_col0 + k)),
            pl.BlockSpec((1, 1, s, 2 * hpg), lambda i, k: (i, k, 0, 0)),
            pl.BlockSpec((1, 1, 2 * hpg, s), lambda i, k: (i, k, 0, 0)),
            small((1, 2 * hpg)), small((2 * hpg, 1)), small((1, 2 * hpg)), small((2 * hpg, 1)),
            small((1, SSD_GROUP_WIDTH)), small((1, SSD_GROUP_WIDTH)),
        ],
        out_specs=pl.BlockSpec((1, s, SSD_GROUP_WIDTH), lambda i, k: (i, 0, k)),
        out_shape=jax.ShapeDtypeStruct((b, s, SSD_D_INNER), BF16),
        scratch_shapes=[pltpu.VMEM((s, SSD_GROUP_WIDTH), F32)],
        compiler_params=_params(("parallel", "parallel")), name="ssd_bidirectional",
    )(xbc3d, xbc3d, xbc3d, proj3d, dt_col, dt_row,
      bias_g[:, None, :], bias_g[:, :, None], alog_g[:, None, :], alog_g[:, :, None], dskip, gain)


def _out_proj2_kernel(a_ref, b_ref, wa_ref, wb_ref, x_ref, o_ref):
    o_ref[...] = x_ref[...] + _dot(a_ref[...], wa_ref[...]) + _dot(b_ref[...], wb_ref[...])


def _out_proj1_kernel(a_ref, wa_ref, x_ref, o_ref):
    o_ref[...] = x_ref[...] + _dot(a_ref[...], wa_ref[...])


def _out_proj(acts, weights, x2d, *, tm, name):
    m, d = x2d.shape
    row = lambda width: pl.BlockSpec((tm, width), lambda i: (i, 0))
    full = lambda w: pl.BlockSpec(w.shape, lambda i: (0, 0))
    kern = _out_proj2_kernel if len(acts) == 2 else _out_proj1_kernel
    return pl.pallas_call(
        kern, grid=(m // tm,),
        in_specs=[row(a.shape[1]) for a in acts] + [full(w) for w in weights] + [row(d)],
        out_specs=row(d), out_shape=jax.ShapeDtypeStruct((m, d), F32),
        compiler_params=_params(("parallel",)), name=name,
    )(*acts, *weights, x2d)


def _ffn_kernel(x_ref, g_ref, w1_ref, w3_ref, w2_ref, o_ref, xn_ref, acc_ref):
    j = pl.program_id(1)

    @pl.when(j == 0)
    def _():
        xn_ref[...] = _rms_rows(x_ref[...], g_ref[...]).astype(BF16)
        acc_ref[...] = x_ref[...]

    xn = xn_ref[...]
    act = _silu(_dot(xn, w1_ref[...])) * _dot(xn, w3_ref[...])
    acc_ref[...] += _dot(act.astype(BF16), w2_ref[...])

    @pl.when(j == pl.num_programs(1) - 1)
    def _():
        o_ref[...] = acc_ref[...]


def _ffn(x2d, g, w13, w2, *, tm, th):
    m, d = x2d.shape
    hid = w2.shape[0]
    nh = hid // th
    return pl.pallas_call(
        _ffn_kernel, grid=(m // tm, nh),
        in_specs=[
            pl.BlockSpec((tm, d), lambda i, j: (i, 0)),
            pl.BlockSpec((1, d), lambda i, j: (0, 0)),
            pl.BlockSpec((d, th), lambda i, j: (0, j)),
            pl.BlockSpec((d, th), lambda i, j: (0, nh + j)),
            pl.BlockSpec((th, d), lambda i, j: (j, 0)),
        ],
        out_specs=pl.BlockSpec((tm, d), lambda i, j: (i, 0)),
        out_shape=jax.ShapeDtypeStruct((m, d), F32),
        scratch_shapes=[pltpu.VMEM((tm, d), BF16), pltpu.VMEM((tm, d), F32)],
        compiler_params=_params(("parallel", "arbitrary")), name="swiglu_ffn",
    )(x2d, g, w13, w13, w2)


def _rope_prep_kernel(p_ref, qg_ref, kg_ref, cos_ref, sin_ref, q_ref, k_ref, v_ref):
    cos = cos_ref[...]
    sin = sin_ref[...]
    even = (lax.broadcasted_iota(jnp.int32, cos.shape, 1) % 2) == 0
    scale = GQA_HEAD_DIM ** -0.5

    def norm_rope(xb, gain):
        xn = _pair_head_rms(xb, gain)
        swapped = jnp.where(even, pltpu.roll(xn, V7X_LANES - 1, axis=1), pltpu.roll(xn, 1, axis=1))
        return xn * cos + swapped * sin

    for pair in range(GQA_HEADS // 2):
        blk = norm_rope(p_ref[0, :, pair * 128:(pair + 1) * 128], qg_ref[...]) * scale
        q_ref[0, 2 * pair] = blk[:, :GQA_HEAD_DIM].astype(BF16)
        q_ref[0, 2 * pair + 1] = blk[:, GQA_HEAD_DIM:].astype(BF16)
    for pair in range(GQA_KV_HEADS // 2):
        c0 = GQA_Q_WIDTH + pair * 128
        blk = norm_rope(p_ref[0, :, c0:c0 + 128], kg_ref[...])
        k_ref[0, 2 * pair] = blk[:, :GQA_HEAD_DIM].astype(BF16)
        k_ref[0, 2 * pair + 1] = blk[:, GQA_HEAD_DIM:].astype(BF16)
        c1 = GQA_Q_WIDTH + GQA_KV_WIDTH + pair * 128
        vb = p_ref[0, :, c1:c1 + 128].astype(BF16)
        v_ref[0, 2 * pair] = vb[:, :GQA_HEAD_DIM]
        v_ref[0, 2 * pair + 1] = vb[:, GQA_HEAD_DIM:]


def _axial_rope_tables(s):
    t = jnp.arange(s)
    row = (t // GRID_W).astype(F32)
    col = (t % GRID_W).astype(F32)
    axis_dims = GQA_HEAD_DIM // 2
    freqs = ROPE_THETA ** (-jnp.arange(0, axis_dims, 2, dtype=F32) / axis_dims)
    ang = jnp.concatenate([row[:, None] * freqs, col[:, None] * freqs], axis=-1)
    cos = jnp.repeat(jnp.cos(ang), 2, axis=-1)
    sin = jnp.stack([-jnp.sin(ang), jnp.sin(ang)], axis=-1).reshape(s, GQA_HEAD_DIM)
    return jnp.tile(cos, (1, 2)), jnp.tile(sin, (1, 2))


def _rope_prep(proj3d, q_gain, k_gain, *, ts):
    b, s, width = proj3d.shape
    cos, sin = _axial_rope_tables(s)
    qg = jnp.tile(q_gain.astype(F32), 2)[None, :]
    kg = jnp.tile(k_gain.astype(F32), 2)[None, :]
    head_out = lambda n: pl.BlockSpec((1, n, ts, GQA_HEAD_DIM), lambda i, t: (i, 0, t, 0))
    shape = lambda n: jax.ShapeDtypeStruct((b, n, s, GQA_HEAD_DIM), BF16)
    return pl.pallas_call(
        _rope_prep_kernel, grid=(b, s // ts),
        in_specs=[
            pl.BlockSpec((1, ts, width), lambda i, t: (i, t, 0)),
            pl.BlockSpec((1, 128), lambda i, t: (0, 0)),
            pl.BlockSpec((1, 128), lambda i, t: (0, 0)),
            pl.BlockSpec((ts, 128), lambda i, t: (t, 0)),
            pl.BlockSpec((ts, 128), lambda i, t: (t, 0)),
        ],
        out_specs=[head_out(GQA_HEADS), head_out(GQA_KV_HEADS), head_out(GQA_KV_HEADS)],
        out_shape=[shape(GQA_HEADS), shape(GQA_KV_HEADS), shape(GQA_KV_HEADS)],
        compiler_params=_params(("parallel", "parallel")), name="gqa_norm_rope",
    )(proj3d, qg, kg, cos, sin)


def _gqa_kernel(q_ref, k_ref, v_ref, o_ref, *, tq):
    q = q_ref[0].reshape(GQA_REP * tq, GQA_HEAD_DIM)
    s = _dot_nt(q, k_ref[0, 0])
    m = jnp.max(s, axis=-1, keepdims=True)
    p = jnp.exp(s - m)
    l = jnp.sum(p, axis=-1, keepdims=True)
    o = _dot(p.astype(BF16), v_ref[0, 0]) * (1.0 / l)
    for r in range(GQA_REP):
        o_ref[0, :, r * GQA_HEAD_DIM:(r + 1) * GQA_HEAD_DIM] = o[r * tq:(r + 1) * tq].astype(BF16)


def _gqa_attention(q, k, v, *, tq):
    b, _, s, _ = q.shape
    kern = functools.partial(_gqa_kernel, tq=tq)
    return pl.pallas_call(
        kern, grid=(b, GQA_KV_HEADS, s // tq),
        in_specs=[
            pl.BlockSpec((1, GQA_REP, tq, GQA_HEAD_DIM), lambda i, g, t: (i, g, t, 0)),
            pl.BlockSpec((1, 1, s, GQA_HEAD_DIM), lambda i, g, t: (i, g, 0, 0)),
            pl.BlockSpec((1, 1, s, GQA_HEAD_DIM), lambda i, g, t: (i, g, 0, 0)),
        ],
        out_specs=pl.BlockSpec((1, tq, GQA_REP * GQA_HEAD_DIM), lambda i, g, t: (i, t, g)),
        out_shape=jax.ShapeDtypeStruct((b, s, GQA_Q_WIDTH), BF16),
        compiler_params=_params(("parallel", "parallel", "parallel")), name="gqa_attention",
    )(q, k, v)


def _even_layer(x2d, b, s, mix_norm, w_in, q_gain, k_gain, rpb, conv_w, conv_b, dt_bias, a_log, d_skip,
                out_gain, w_out):
    w_main = w_in[:, :EVEN_MAIN_WIDTH].astype(BF16)
    w_dt = w_in[:, EVEN_MAIN_WIDTH:].astype(BF16)
    proj, dt_raw = _norm_proj(x2d, mix_norm.astype(F32)[None, :], w_main, tm=1024, tn=1152, w_small=w_dt,
                              name="even_in_proj")
    proj3d = proj.reshape(b, s, EVEN_MAIN_WIDTH)
    na_out = _neighbourhood_attention(proj3d, q_gain, k_gain, rpb)
    xbc = _conv_silu(proj3d, conv_w.astype(F32), conv_b.astype(F32)[None, :])
    ssd_out = _ssd_mixer(xbc, proj3d, dt_raw, dt_bias, a_log, d_skip, out_gain)
    w_out_bf = w_out.astype(BF16)
    return _out_proj([na_out.reshape(b * s, NA_WIDTH), ssd_out.reshape(b * s, SSD_D_INNER)],
                     [w_out_bf[:NA_WIDTH], w_out_bf[NA_WIDTH:]], x2d, tm=512, name="even_out_proj")


def _odd_layer(x2d, b, s, mix_norm, w_qkv, q_gain, k_gain, w_out):
    proj = _norm_proj(x2d, mix_norm.astype(F32)[None, :], w_qkv.astype(BF16), tm=1024, tn=w_qkv.shape[1],
                      name="odd_qkv_proj")
    q, k, v = _rope_prep(proj.reshape(b, s, -1), q_gain, k_gain, ts=512)
    attn = _gqa_attention(q, k, v, tq=128)
    return _out_proj([attn.reshape(b * s, GQA_Q_WIDTH)], [w_out.astype(BF16)], x2d, tm=512, name="odd_out_proj")


def kernel(x, even_mix_norm, even_w_in, na_q_norm, na_k_norm, na_rel_bias, ssd_conv_w, ssd_conv_b, ssd_dt_bias, ssd_A_log, ssd_D, ssd_out_norm, even_w_out, odd_mix_norm, odd_w_qkv, gqa_q_norm, gqa_k_norm, odd_w_out, ffn_norm, ffn_w13, ffn_w2):
    b, s, d = x.shape
    depth = ffn_norm.shape[0]
    h = x.reshape(b * s, d)
    for layer in range(depth):
        i = layer // 2
        if layer % 2 == 0:
            h = _even_layer(h, b, s, even_mix_norm[i], even_w_in[i], na_q_norm[i], na_k_norm[i], na_rel_bias[i],
                            ssd_conv_w[i], ssd_conv_b[i], ssd_dt_bias[i], ssd_A_log[i], ssd_D[i], ssd_out_norm[i],
                            even_w_out[i])
        else:
            h = _odd_layer(h, b, s, odd_mix_norm[i], odd_w_qkv[i], gqa_q_norm[i], gqa_k_norm[i], odd_w_out[i])
        h = _ffn(h, ffn_norm[layer].astype(F32)[None, :], ffn_w13[layer].astype(BF16), ffn_w2[layer].astype(BF16),
                 tm=512, th=1408)
    return h.reshape(b, s, d)
```

```python
import functools
import math

import jax
import jax.numpy as jnp
from jax import lax
from jax.experimental import pallas as pl
from jax.experimental.pallas import tpu as pltpu

F32 = jnp.float32
BF16 = jnp.bfloat16

D_MODEL = 1024
GRID_W = 64
NORM_EPS = 1e-6

NA_HEADS = 8
NA_HEAD_DIM = 64
NA_WIDTH = NA_HEADS * NA_HEAD_DIM
NA_KH = 8
NA_KW = 16

SSD_D_INNER = 1024
SSD_HEAD_DIM = 64
SSD_HEADS = 16
SSD_GROUPS = 4
SSD_STATE = 128
SSD_CONV = 4
SSD_CHUNK = 128
SSD_CONV_DIM = SSD_D_INNER + 2 * SSD_GROUPS * SSD_STATE
SSD_GROUP_WIDTH = SSD_D_INNER // SSD_GROUPS
SSD_HEADS_PER_GROUP = SSD_HEADS // SSD_GROUPS

EVEN_MAIN_WIDTH = 3 * NA_WIDTH + SSD_D_INNER + SSD_CONV_DIM
EVEN_XBC_OFFSET = 3 * NA_WIDTH + SSD_D_INNER

GQA_HEADS = 16
GQA_KV_HEADS = 4
GQA_HEAD_DIM = 64
GQA_REP = GQA_HEADS // GQA_KV_HEADS
GQA_Q_WIDTH = GQA_HEADS * GQA_HEAD_DIM
GQA_KV_WIDTH = GQA_KV_HEADS * GQA_HEAD_DIM
ROPE_THETA = 10000.0

FFN_HIDDEN = 2816

V7X_LANES = 128
V7X_VMEM_LIMIT = 56 * 1024 * 1024
MASK_VALUE = -1e30


def _params(dims):
    return pltpu.CompilerParams(dimension_semantics=dims, vmem_limit_bytes=V7X_VMEM_LIMIT)


def _silu(v):
    return v * (1.0 / (1.0 + jnp.exp(-v)))


def _softplus(v):
    return jnp.maximum(v, 0.0) + jnp.log(1.0 + jnp.exp(-jnp.abs(v)))


def _rms_rows(x, g):
    ms = jnp.mean(x * x, axis=-1, keepdims=True)
    return x * lax.rsqrt(ms + NORM_EPS) * g


def _dot(a, b):
    return jnp.dot(a, b, preferred_element_type=F32)


def _dot_nt(a, b):
    return lax.dot_general(a, b, (((1,), (1,)), ((), ())), preferred_element_type=F32)


def _dot_tn(a, b):
    return lax.dot_general(a, b, (((0,), (0,)), ((), ())), preferred_element_type=F32)


def _dot_exact(a, b):
    return jnp.dot(a, b, preferred_element_type=F32, precision=lax.Precision.HIGHEST)


def _norm_proj_kernel(x_ref, g_ref, w_ref, o_ref, xn_ref):
    @pl.when(pl.program_id(1) == 0)
    def _():
        xn_ref[...] = _rms_rows(x_ref[...], g_ref[...]).astype(BF16)

    o_ref[...] = _dot(xn_ref[...], w_ref[...])


def _norm_proj2_kernel(x_ref, g_ref, w_ref, w2_ref, o_ref, o2_ref, xn_ref):
    @pl.when(pl.program_id(1) == 0)
    def _():
        xn = _rms_rows(x_ref[...], g_ref[...]).astype(BF16)
        xn_ref[...] = xn
        o2_ref[...] = _dot(xn, w2_ref[...])

    o_ref[...] = _dot(xn_ref[...], w_ref[...])


def _norm_proj(x2d, g, w, *, tm, tn, w_small=None, name):
    m, d = x2d.shape
    n = w.shape[1]
    grid = (m // tm, n // tn)
    in_specs = [pl.BlockSpec((tm, d), lambda i, j: (i, 0)),
                pl.BlockSpec((1, d), lambda i, j: (0, 0)),
                pl.BlockSpec((d, tn), lambda i, j: (0, j))]
    out_specs = pl.BlockSpec((tm, tn), lambda i, j: (i, j))
    out_shape = jax.ShapeDtypeStruct((m, n), F32)
    scratch = [pltpu.VMEM((tm, d), BF16)]
    if w_small is None:
        return pl.pallas_call(_norm_proj_kernel, grid=grid, in_specs=in_specs, out_specs=out_specs,
                              out_shape=out_shape, scratch_shapes=scratch,
                              compiler_params=_params(("parallel", "arbitrary")), name=name)(x2d, g, w)
    ns = w_small.shape[1]
    in_specs.append(pl.BlockSpec((d, ns), lambda i, j: (0, 0)))
    return pl.pallas_call(_norm_proj2_kernel, grid=grid, in_specs=in_specs,
                          out_specs=[out_specs, pl.BlockSpec((tm, ns), lambda i, j: (i, 0))],
                          out_shape=[out_shape, jax.ShapeDtypeStruct((m, ns), F32)], scratch_shapes=scratch,
                          compiler_params=_params(("parallel", "arbitrary")), name=name)(x2d, g, w, w_small)


CONV_ROWS = 512
CONV_COLS = 512
CONV_HALO = 8


def _conv_kernel(cur_ref, prev_ref, next_ref, w_ref, b_ref, o_ref, *, n_row_blocks):
    c = pl.program_id(1)
    prev = jnp.where(c > 0, prev_ref[0], 0.0)
    nxt = jnp.where(c < n_row_blocks - 1, next_ref[0], 0.0)
    win = jnp.concatenate([prev, cur_ref[0], nxt], axis=0)
    n = CONV_ROWS + 2 * CONV_HALO
    left = SSD_CONV // 2
    acc = jnp.broadcast_to(b_ref[...], (CONV_ROWS, CONV_COLS))
    for k in range(SSD_CONV):
        shift = (left - k) % n
        tap = win if shift == 0 else pltpu.roll(win, shift, axis=0)
        acc = acc + tap[CONV_HALO:CONV_HALO + CONV_ROWS] * w_ref[k:k + 1, :]
    o_ref[0] = _silu(acc)


def _conv_silu(proj3d, conv_w, conv_b):
    b, s, _ = proj3d.shape
    n_row_blocks = s // CONV_ROWS
    n_col_blocks = SSD_CONV_DIM // CONV_COLS
    col0 = EVEN_XBC_OFFSET // CONV_COLS
    halo_per_block = CONV_ROWS // CONV_HALO
    n_halo_blocks = s // CONV_HALO
    kern = functools.partial(_conv_kernel, n_row_blocks=n_row_blocks)
    return pl.pallas_call(
        kern, grid=(b, n_row_blocks, n_col_blocks),
        in_specs=[
            pl.BlockSpec((1, CONV_ROWS, CONV_COLS), lambda i, c, j: (i, c, col0 + j)),
            pl.BlockSpec((1, CONV_HALO, CONV_COLS),
                         lambda i, c, j: (i, jnp.maximum(c * halo_per_block - 1, 0), col0 + j)),
            pl.BlockSpec((1, CONV_HALO, CONV_COLS),
                         lambda i, c, j: (i, jnp.minimum((c + 1) * halo_per_block, n_halo_blocks - 1), col0 + j)),
            pl.BlockSpec((SSD_CONV, CONV_COLS), lambda i, c, j: (0, j)),
            pl.BlockSpec((1, CONV_COLS), lambda i, c, j: (0, j)),
        ],
        out_specs=pl.BlockSpec((1, CONV_ROWS, CONV_COLS), lambda i, c, j: (i, c, j)),
        out_shape=jax.ShapeDtypeStruct((b, s, SSD_CONV_DIM), F32),
        compiler_params=_params(("parallel", "parallel", "parallel")), name="ssd_conv_silu",
    )(proj3d, proj3d, proj3d, conv_w, conv_b)


NA_PREP_ROWS = 256


def _pair_head_rms(x, g):
    lo = lax.broadcasted_iota(jnp.int32, x.shape, 1) < NA_HEAD_DIM
    x2 = x * x
    s_lo = jnp.sum(jnp.where(lo, x2, 0.0), axis=-1, keepdims=True)
    s_hi = jnp.sum(jnp.where(lo, 0.0, x2), axis=-1, keepdims=True)
    ms = jnp.where(lo, s_lo, s_hi) * (1.0 / NA_HEAD_DIM)
    return x * lax.rsqrt(ms + NORM_EPS) * g


def _na_kernel(q_ref, k_ref, v_ref, qg_ref, kg_ref, bias_ref, o_ref, q_s, k_s, v_s, *, rows):
    scale = NA_HEAD_DIM ** -0.5

    def prep(i, carry):
        sl = pl.ds(pl.multiple_of(i * NA_PREP_ROWS, NA_PREP_ROWS), NA_PREP_ROWS)
        q_s[sl, :] = (_pair_head_rms(q_ref[0, sl, :], qg_ref[...]) * scale).astype(BF16)
        k_s[sl, :] = _pair_head_rms(k_ref[0, sl, :], kg_ref[...]).astype(BF16)
        v_s[sl, :] = v_ref[0, sl, :].astype(BF16)
        return carry

    lax.fori_loop(0, (rows * GRID_W) // NA_PREP_ROWS, prep, 0)

    n_keys = NA_KH * GRID_W

    def row_body(r, carry):
        row_start = jnp.clip(r - NA_KH // 2, 0, rows - NA_KH)
        cls = r - row_start
        q = q_s[pl.ds(pl.multiple_of(r * GRID_W, GRID_W), GRID_W), :]
        ksl = pl.ds(pl.multiple_of(row_start * GRID_W, GRID_W), n_keys)
        kk = k_s[ksl, :]
        vv = v_s[ksl, :]
        outs = []
        for h in range(2):
            hs = slice(h * NA_HEAD_DIM, (h + 1) * NA_HEAD_DIM)
            s = _dot_nt(q[:, hs], kk[:, hs]) + bias_ref[0, cls, h]
            m = jnp.max(s, axis=-1, keepdims=True)
            p = jnp.exp(s - m)
            l = jnp.sum(p, axis=-1, keepdims=True)
            outs.append(_dot(p.astype(BF16), vv[:, hs]) * (1.0 / l))
        o_ref[0, pl.ds(pl.multiple_of(r * GRID_W, GRID_W), GRID_W), :] = (
            jnp.concatenate(outs, axis=-1).astype(BF16))
        return carry

    lax.fori_loop(0, rows, row_body, 0, unroll=4)


NA_DY = 2 * NA_KH - 1
NA_DX = 2 * NA_KW - 1


def _na_bias_kernel(rpb_ref, o_ref):
    h = pl.program_id(0)
    q = lax.broadcasted_iota(jnp.int32, (GRID_W, GRID_W), 0)
    k = lax.broadcasted_iota(jnp.int32, (GRID_W, GRID_W), 1)
    dx = jnp.clip(k - q, -(NA_KW - 1), NA_KW - 1) + (NA_KW - 1)
    col_start = jnp.clip(q - NA_KW // 2, 0, GRID_W - NA_KW)
    in_win = (k >= col_start) & (k < col_start + NA_KW)
    for dy in range(NA_DY):
        base = (h * NA_DY + dy) * NA_DX
        t = jnp.full((GRID_W, GRID_W), MASK_VALUE, F32)
        for d in range(NA_DX):
            t = jnp.where(dx == d, rpb_ref[base + d], t)
        t = jnp.where(in_win, t, MASK_VALUE)
        for cls in range(NA_KH):
            j = dy - (NA_KH - 1) + cls
            if 0 <= j < NA_KH:
                o_ref[0, cls, 0, :, j * GRID_W:(j + 1) * GRID_W] = t


def _na_bias_table(rpb):
    return pl.pallas_call(
        _na_bias_kernel, grid=(NA_HEADS,),
        in_specs=[pl.BlockSpec(memory_space=pltpu.SMEM)],
        out_specs=pl.BlockSpec((1, NA_KH, 1, GRID_W, NA_KH * GRID_W), lambda h: (h // 2, 0, h % 2, 0, 0)),
        out_shape=jax.ShapeDtypeStruct((NA_HEADS // 2, NA_KH, 2, GRID_W, NA_KH * GRID_W), F32),
        compiler_params=_params(("parallel",)), name="na_bias_table",
    )(rpb.astype(F32).reshape(-1))


def _neighbourhood_attention(proj3d, q_gain, k_gain, rpb):
    b, s, _ = proj3d.shape
    rows = s // GRID_W
    assert rows >= NA_KH
    bias = _na_bias_table(rpb)
    qg = jnp.tile(q_gain.astype(F32), 2)[None, :]
    kg = jnp.tile(k_gain.astype(F32), 2)[None, :]
    n_pairs = NA_HEADS // 2
    blk = (1, s, 2 * NA_HEAD_DIM)
    kern = functools.partial(_na_kernel, rows=rows)
    return pl.pallas_call(
        kern, grid=(n_pairs, b),
        in_specs=[
            pl.BlockSpec(blk, lambda p, i: (i, 0, p)),
            pl.BlockSpec(blk, lambda p, i: (i, 0, n_pairs + p)),
            pl.BlockSpec(blk, lambda p, i: (i, 0, 2 * n_pairs + p)),
            pl.BlockSpec((1, 2 * NA_HEAD_DIM), lambda p, i: (0, 0)),
            pl.BlockSpec((1, 2 * NA_HEAD_DIM), lambda p, i: (0, 0)),
            pl.BlockSpec((1, NA_KH, 2, GRID_W, NA_KH * GRID_W), lambda p, i: (p, 0, 0, 0, 0)),
        ],
        out_specs=pl.BlockSpec(blk, lambda p, i: (i, 0, p)),
        out_shape=jax.ShapeDtypeStruct((b, s, NA_WIDTH), BF16),
        scratch_shapes=[pltpu.VMEM((s, 2 * NA_HEAD_DIM), BF16)] * 3,
        compiler_params=_params(("parallel", "parallel")), name="neighbourhood_attention",
    )(proj3d, proj3d, proj3d, qg, kg, bias)


def _expand_heads(v, first):
    return jnp.concatenate(
        [jnp.broadcast_to(v[:, first + j:first + j + 1], (v.shape[0], SSD_HEAD_DIM))
         for j in range(SSD_HEADS_PER_GROUP)], axis=1)


def _ssd_kernel(xs_ref, b_ref, c_ref, z_ref, dtc_ref, dtr_ref, biasc_ref, biasr_ref, alogc_ref, alogr_ref,
                dskip_ref, gain_ref, o_ref, y_s, *, n_chunks):
    L = SSD_CHUNK
    hpg = SSD_HEADS_PER_GROUP
    li = lax.broadcasted_iota(jnp.int32, (L, L), 0)
    si = lax.broadcasted_iota(jnp.int32, (L, L), 1)
    lower = si <= li
    upper = si >= li
    tri_l = lower.astype(F32)
    tri_u = upper.astype(F32)
    a_col = -jnp.exp(alogc_ref[0])
    a_row = -jnp.exp(alogr_ref[0])
    col_is_fwd = lax.broadcasted_iota(jnp.int32, (L, 2 * hpg), 1) < hpg
    row_is_fwd = lax.broadcasted_iota(jnp.int32, (2 * hpg, L), 0) < hpg

    def chunk_slice(c):
        return pl.ds(pl.multiple_of(c * L, L), L)

    def col_terms(c):
        dt = _softplus(dtc_ref[0, 0, chunk_slice(c), :] + biasc_ref[0])
        a = dt * a_col
        cum = jnp.where(col_is_fwd, _dot_exact(tri_l, a), _dot_exact(tri_u, a))
        return dt, cum

    def fwd_body(c, h):
        sl = chunk_slice(c)
        xs = xs_ref[0, sl, :]
        bm = b_ref[0, sl, :].astype(BF16)
        cm = c_ref[0, sl, :].astype(BF16)
        dt_c, cum_c = col_terms(c)
        dt_r = _softplus(dtr_ref[0, 0, :, sl] + biasr_ref[0])
        a_r = dt_r * a_row
        cum_r = jnp.where(row_is_fwd, _dot_exact(a_r, tri_u), _dot_exact(a_r, tri_l))
        cb = _dot_nt(cm, bm)
        xs_bf = xs.astype(BF16)
        ys = []
        for j in range(hpg):
            seg_f = cum_c[:, j:j + 1] - cum_r[j:j + 1, :]
            seg_b = cum_c[:, hpg + j:hpg + j + 1] - cum_r[hpg + j:hpg + j + 1, :]
            w = (jnp.exp(jnp.where(lower, seg_f, -jnp.inf)) * dt_r[j:j + 1, :]
                 + jnp.exp(jnp.where(upper, seg_b, -jnp.inf)) * dt_r[hpg + j:hpg + j + 1, :])
            ys.append(_dot((cb * w).astype(BF16), xs_bf[:, j * SSD_HEAD_DIM:(j + 1) * SSD_HEAD_DIM]))
        y = jnp.concatenate(ys, axis=1)
        y = y + _dot(cm, h.astype(BF16)) * _expand_heads(jnp.exp(cum_c), 0)
        last = cum_c[L - 1:L, :]
        wts = dt_c * jnp.exp(last - cum_c)
        st = _dot_tn(bm, (xs * _expand_heads(wts, 0)).astype(BF16))
        y_s[sl, :] = y
        return h * _expand_heads(jnp.exp(last), 0) + st

    h0 = jnp.zeros((SSD_STATE, SSD_GROUP_WIDTH), F32)
    lax.fori_loop(0, n_chunks, fwd_body, h0)

    def bwd_body(i, h):
        c = n_chunks - 1 - i
        sl = chunk_slice(c)
        xs = xs_ref[0, sl, :]
        bm = b_ref[0, sl, :].astype(BF16)
        cm = c_ref[0, sl, :].astype(BF16)
        dt_c, cum_c = col_terms(c)
        y = y_s[sl, :] + _dot(cm, h.astype(BF16)) * _expand_heads(jnp.exp(cum_c), hpg)
        first = cum_c[0:1, :]
        wts = dt_c * jnp.exp(first - cum_c)
        st = _dot_tn(bm, (xs * _expand_heads(wts, hpg)).astype(BF16))
        y = y + dskip_ref[0] * xs
        gated = y * _silu(z_ref[0, sl, :])
        ms = jnp.mean(gated * gated, axis=-1, keepdims=True)
        o_ref[0, sl, :] = (gated * lax.rsqrt(ms + NORM_EPS) * gain_ref[0]).astype(BF16)
        return h * _expand_heads(jnp.exp(first), hpg) + st

    lax.fori_loop(0, n_chunks, bwd_body, h0)


def _group_major(v):
    return jnp.transpose(v.astype(F32).reshape(2, SSD_GROUPS, SSD_HEADS_PER_GROUP), (1, 0, 2)).reshape(
        SSD_GROUPS, 2 * SSD_HEADS_PER_GROUP)


def _ssd_mixer(xbc3d, proj3d, dt_raw, dt_bias, a_log, d_skip, out_gain):
    b, s, _ = xbc3d.shape
    n_chunks = s // SSD_CHUNK
    g, hpg = SSD_GROUPS, SSD_HEADS_PER_GROUP
    dt5 = dt_raw.reshape(b, s, 2, g, hpg)
    dt_col = jnp.transpose(dt5, (0, 3, 1, 2, 4)).reshape(b, g, s, 2 * hpg)
    dt_row = jnp.transpose(dt5, (0, 3, 2, 4, 1)).reshape(b, g, 2 * hpg, s)
    bias_g = _group_major(dt_bias)
    alog_g = _group_major(a_log)
    dskip = jnp.repeat(d_skip.astype(F32), SSD_HEAD_DIM).reshape(g, 1, SSD_GROUP_WIDTH)
    gain = out_gain.astype(F32).reshape(g, 1, SSD_GROUP_WIDTH)
    z_col0 = (3 * NA_WIDTH) // SSD_GROUP_WIDTH
    b_col0 = SSD_D_INNER // SSD_STATE
    c_col0 = b_col0 + SSD_GROUPS
    kern = functools.partial(_ssd_kernel, n_chunks=n_chunks)
    small = lambda shape: pl.BlockSpec((1,) + shape, lambda i, k: (k, 0, 0))
    return pl.pallas_call(
        kern, grid=(b, g),
        in_specs=[
            pl.BlockSpec((1, s, SSD_GROUP_WIDTH), lambda i, k: (i, 0, k)),
            pl.BlockSpec((1, s, SSD_STATE), lambda i, k: (i, 0, b_col0 + k)),
            pl.BlockSpec((1, s, SSD_STATE), lambda i, k: (i, 0, c_col0 + k)),
            pl.BlockSpec((1, s, SSD_GROUP_WIDTH), lambda i, k: (i, 0, z_col0 + k)),
            pl.BlockSpec((1, 1, s, 2 * hpg), lambda i, k: (i, k, 0, 0)),
            pl.BlockSpec((1, 1, 2 * hpg, s), lambda i, k: (i, k, 0, 0)),
            small((1, 2 * hpg)), small((2 * hpg, 1)), small((1, 2 * hpg)), small((2 * hpg, 1)),
            small((1, SSD_GROUP_WIDTH)), small((1, SSD_GROUP_WIDTH)),
        ],
        out_specs=pl.BlockSpec((1, s, SSD_GROUP_WIDTH), lambda i, k: (i, 0, k)),
        out_shape=jax.ShapeDtypeStruct((b, s, SSD_D_INNER), BF16),
        scratch_shapes=[pltpu.VMEM((s, SSD_GROUP_WIDTH), F32)],
        compiler_params=_params(("parallel", "parallel")), name="ssd_bidirectional",
    )(xbc3d, xbc3d, xbc3d, proj3d, dt_col, dt_row,
      bias_g[:, None, :], bias_g[:, :, None], alog_g[:, None, :], alog_g[:, :, None], dskip, gain)


def _out_proj2_kernel(a_ref, b_ref, wa_ref, wb_ref, x_ref, o_ref):
    o_ref[...] = x_ref[...] + _dot(a_ref[...], wa_ref[...]) + _dot(b_ref[...], wb_ref[...])


def _out_proj1_kernel(a_ref, wa_ref, x_ref, o_ref):
    o_ref[...] = x_ref[...] + _dot(a_ref[...], wa_ref[...])


def _out_proj(acts, weights, x2d, *, tm, name):
    m, d = x2d.shape
    row = lambda width: pl.BlockSpec((tm, width), lambda i: (i, 0))
    full = lambda w: pl.BlockSpec(w.shape, lambda i: (0, 0))
    kern = _out_proj2_kernel if len(acts) == 2 else _out_proj1_kernel
    return pl.pallas_call(
        kern, grid=(m // tm,),
        in_specs=[row(a.shape[1]) for a in acts] + [full(w) for w in weights] + [row(d)],
        out_specs=row(d), out_shape=jax.ShapeDtypeStruct((m, d), F32),
        compiler_params=_params(("parallel",)), name=name,
    )(*acts, *weights, x2d)


def _ffn_kernel(x_ref, g_ref, w1_ref, w3_ref, w2_ref, o_ref, xn_ref, acc_ref):
    j = pl.program_id(1)

    @pl.when(j == 0)
    def _():
        xn_ref[...] = _rms_rows(x_ref[...], g_ref[...]).astype(BF16)
        acc_ref[...] = x_ref[...]

    xn = xn_ref[...]
    act = _silu(_dot(xn, w1_ref[...])) * _dot(xn, w3_ref[...])
    acc_ref[...] += _dot(act.astype(BF16), w2_ref[...])

    @pl.when(j == pl.num_programs(1) - 1)
    def _():
        o_ref[...] = acc_ref[...]


def _ffn(x2d, g, w13, w2, *, tm, th):
    m, d = x2d.shape
    hid = w2.shape[0]
    nh = hid // th
    return pl.pallas_call(
        _ffn_kernel, grid=(m // tm, nh),
        in_specs=[
            pl.BlockSpec((tm, d), lambda i, j: (i, 0)),
            pl.BlockSpec((1, d), lambda i, j: (0, 0)),
            pl.BlockSpec((d, th), lambda i, j: (0, j)),
            pl.BlockSpec((d, th), lambda i, j: (0, nh + j)),
            pl.BlockSpec((th, d), lambda i, j: (j, 0)),
        ],
        out_specs=pl.BlockSpec((tm, d), lambda i, j: (i, 0)),
        out_shape=jax.ShapeDtypeStruct((m, d), F32),
        scratch_shapes=[pltpu.VMEM((tm, d), BF16), pltpu.VMEM((tm, d), F32)],
        compiler_params=_params(("parallel", "arbitrary")), name="swiglu_ffn",
    )(x2d, g, w13, w13, w2)


def _rope_prep_kernel(p_ref, qg_ref, kg_ref, cos_ref, sin_ref, q_ref, k_ref, v_ref):
    cos = cos_ref[...]
    sin = sin_ref[...]
    even = (lax.broadcasted_iota(jnp.int32, cos.shape, 1) % 2) == 0
    scale = GQA_HEAD_DIM ** -0.5

    def norm_rope(xb, gain):
        xn = _pair_head_rms(xb, gain)
        swapped = jnp.where(even, pltpu.roll(xn, V7X_LANES - 1, axis=1), pltpu.roll(xn, 1, axis=1))
        return xn * cos + swapped * sin

    for pair in range(GQA_HEADS // 2):
        blk = norm_rope(p_ref[0, :, pair * 128:(pair + 1) * 128], qg_ref[...]) * scale
        q_ref[0, 2 * pair] = blk[:, :GQA_HEAD_DIM].astype(BF16)
        q_ref[0, 2 * pair + 1] = blk[:, GQA_HEAD_DIM:].astype(BF16)
    for pair in range(GQA_KV_HEADS // 2):
        c0 = GQA_Q_WIDTH + pair * 128
        blk = norm_rope(p_ref[0, :, c0:c0 + 128], kg_ref[...])
        k_ref[0, 2 * pair] = blk[:, :GQA_HEAD_DIM].astype(BF16)
        k_ref[0, 2 * pair + 1] = blk[:, GQA_HEAD_DIM:].astype(BF16)
        c1 = GQA_Q_WIDTH + GQA_KV_WIDTH + pair * 128
        vt = p_ref[0, :, c1:c1 + 128].T.astype(BF16)
        v_ref[0, 2 * pair] = vt[:GQA_HEAD_DIM]
        v_ref[0, 2 * pair + 1] = vt[GQA_HEAD_DIM:]


def _axial_rope_tables(s):
    t = jnp.arange(s)
    row = (t // GRID_W).astype(F32)
    col = (t % GRID_W).astype(F32)
    axis_dims = GQA_HEAD_DIM // 2
    freqs = ROPE_THETA ** (-jnp.arange(0, axis_dims, 2, dtype=F32) / axis_dims)
    ang = jnp.concatenate([row[:, None] * freqs, col[:, None] * freqs], axis=-1)
    cos = jnp.repeat(jnp.cos(ang), 2, axis=-1)
    sin = jnp.stack([-jnp.sin(ang), jnp.sin(ang)], axis=-1).reshape(s, GQA_HEAD_DIM)
    return jnp.tile(cos, (1, 2)), jnp.tile(sin, (1, 2))


def _rope_prep(proj3d, q_gain, k_gain, *, ts):
    b, s, width = proj3d.shape
    cos, sin = _axial_rope_tables(s)
    qg = jnp.tile(q_gain.astype(F32), 2)[None, :]
    kg = jnp.tile(k_gain.astype(F32), 2)[None, :]
    head_out = lambda n: pl.BlockSpec((1, n, ts, GQA_HEAD_DIM), lambda i, t: (i, 0, t, 0))
    shape = lambda n: jax.ShapeDtypeStruct((b, n, s, GQA_HEAD_DIM), BF16)
    vt_out = pl.BlockSpec((1, GQA_KV_HEADS, GQA_HEAD_DIM, ts), lambda i, t: (i, 0, 0, t))
    vt_shape = jax.ShapeDtypeStruct((b, GQA_KV_HEADS, GQA_HEAD_DIM, s), BF16)
    return pl.pallas_call(
        _rope_prep_kernel, grid=(b, s // ts),
        in_specs=[
            pl.BlockSpec((1, ts, width), lambda i, t: (i, t, 0)),
            pl.BlockSpec((1, 128), lambda i, t: (0, 0)),
            pl.BlockSpec((1, 128), lambda i, t: (0, 0)),
            pl.BlockSpec((ts, 128), lambda i, t: (t, 0)),
            pl.BlockSpec((ts, 128), lambda i, t: (t, 0)),
        ],
        out_specs=[head_out(GQA_HEADS), head_out(GQA_KV_HEADS), vt_out],
        out_shape=[shape(GQA_HEADS), shape(GQA_KV_HEADS), vt_shape],
        compiler_params=_params(("parallel", "parallel")), name="gqa_norm_rope",
    )(proj3d, qg, kg, cos, sin)


GQA_HEADS_PER_CHAIN = 2


def _gqa_kernel(q_ref, k_ref, vt_ref, o_ref, *, tq):
    k = k_ref[0, 0]
    vt = vt_ref[0, 0]
    hpc = GQA_HEADS_PER_CHAIN
    for c in range(GQA_REP // hpc):
        q = q_ref[0, c * hpc:(c + 1) * hpc].reshape(hpc * tq, GQA_HEAD_DIM)
        st = _dot_nt(k, q)
        m = jnp.max(st, axis=0, keepdims=True)
        p = jnp.exp(st - m)
        l = jnp.sum(p, axis=0, keepdims=True)
        ot = _dot(vt, p.astype(BF16)) * (1.0 / l)
        for r in range(hpc):
            h = c * hpc + r
            o_ref[0, :, h * GQA_HEAD_DIM:(h + 1) * GQA_HEAD_DIM] = ot[:, r * tq:(r + 1) * tq].T.astype(BF16)


def _gqa_attention(q, k, vt, *, tq):
    b, _, s, _ = q.shape
    kern = functools.partial(_gqa_kernel, tq=tq)
    return pl.pallas_call(
        kern, grid=(b, GQA_KV_HEADS, s // tq),
        in_specs=[
            pl.BlockSpec((1, GQA_REP, tq, GQA_HEAD_DIM), lambda i, g, t: (i, g, t, 0)),
            pl.BlockSpec((1, 1, s, GQA_HEAD_DIM), lambda i, g, t: (i, g, 0, 0)),
            pl.BlockSpec((1, 1, GQA_HEAD_DIM, s), lambda i, g, t: (i, g, 0, 0)),
        ],
        out_specs=pl.BlockSpec((1, tq, GQA_REP * GQA_HEAD_DIM), lambda i, g, t: (i, t, g)),
        out_shape=jax.ShapeDtypeStruct((b, s, GQA_Q_WIDTH), BF16),
        compiler_params=_params(("parallel", "parallel", "parallel")), name="gqa_attention",
    )(q, k, vt)


def _even_layer(x2d, b, s, mix_norm, w_in, q_gain, k_gain, rpb, conv_w, conv_b, dt_bias, a_log, d_skip,
                out_gain, w_out):
    w_main = w_in[:, :EVEN_MAIN_WIDTH].astype(BF16)
    w_dt = w_in[:, EVEN_MAIN_WIDTH:].astype(BF16)
    proj, dt_raw = _norm_proj(x2d, mix_norm.astype(F32)[None, :], w_main, tm=1024, tn=1152, w_small=w_dt,
                              name="even_in_proj")
    proj3d = proj.reshape(b, s, EVEN_MAIN_WIDTH)
    na_out = _neighbourhood_attention(proj3d, q_gain, k_gain, rpb)
    xbc = _conv_silu(proj3d, conv_w.astype(F32), conv_b.astype(F32)[None, :])
    ssd_out = _ssd_mixer(xbc, proj3d, dt_raw, dt_bias, a_log, d_skip, out_gain)
    w_out_bf = w_out.astype(BF16)
    return _out_proj([na_out.reshape(b * s, NA_WIDTH), ssd_out.reshape(b * s, SSD_D_INNER)],
                     [w_out_bf[:NA_WIDTH], w_out_bf[NA_WIDTH:]], x2d, tm=512, name="even_out_proj")


def _odd_layer(x2d, b, s, mix_norm, w_qkv, q_gain, k_gain, w_out):
    proj = _norm_proj(x2d, mix_norm.astype(F32)[None, :], w_qkv.astype(BF16), tm=1024, tn=w_qkv.shape[1],
                      name="odd_qkv_proj")
    q, k, vt = _rope_prep(proj.reshape(b, s, -1), q_gain, k_gain, ts=512)
    attn = _gqa_attention(q, k, vt, tq=256)
    return _out_proj([attn.reshape(b * s, GQA_Q_WIDTH)], [w_out.astype(BF16)], x2d, tm=512, name="odd_out_proj")


def kernel(x, even_mix_norm, even_w_in, na_q_norm, na_k_norm, na_rel_bias, ssd_conv_w, ssd_conv_b, ssd_dt_bias, ssd_A_log, ssd_D, ssd_out_norm, even_w_out, odd_mix_norm, odd_w_qkv, gqa_q_norm, gqa_k_norm, odd_w_out, ffn_norm, ffn_w13, ffn_w2):
    b, s, d = x.shape
    depth = ffn_norm.shape[0]
    h = x.reshape(b * s, d)
    for layer in range(depth):
        i = layer // 2
        if layer % 2 == 0:
            h = _even_layer(h, b, s, even_mix_norm[i], even_w_in[i], na_q_norm[i], na_k_norm[i], na_rel_bias[i],
                            ssd_conv_w[i], ssd_conv_b[i], ssd_dt_bias[i], ssd_A_log[i], ssd_D[i], ssd_out_norm[i],
                            even_w_out[i])
        else:
            h = _odd_layer(h, b, s, odd_mix_norm[i], odd_w_qkv[i], gqa_q_norm[i], gqa_k_norm[i], odd_w_out[i])
        h = _ffn(h, ffn_norm[layer].astype(F32)[None, :], ffn_w13[layer].astype(BF16), ffn_w2[layer].astype(BF16),
                 tm=512, th=1408)
    return h.reshape(b, s, d)
```

```python
import functools

import jax
import jax.numpy as jnp
from jax import lax
from jax.experimental import pallas as pl
from jax.experimental.pallas import tpu as pltpu

F32 = jnp.float32
BF16 = jnp.bfloat16

D_MODEL = 1024
GRID_W = 64
NORM_EPS = 1e-6

NA_HEADS = 8
NA_HEAD_DIM = 64
NA_WIDTH = NA_HEADS * NA_HEAD_DIM
NA_KH = 8
NA_KW = 16

SSD_D_INNER = 1024
SSD_HEAD_DIM = 64
SSD_HEADS = 16
SSD_GROUPS = 4
SSD_STATE = 128
SSD_CONV = 4
SSD_CHUNK = 128
SSD_CONV_DIM = SSD_D_INNER + 2 * SSD_GROUPS * SSD_STATE
SSD_GROUP_WIDTH = SSD_D_INNER // SSD_GROUPS
SSD_HEADS_PER_GROUP = SSD_HEADS // SSD_GROUPS

EVEN_MAIN_WIDTH = 3 * NA_WIDTH + SSD_D_INNER + SSD_CONV_DIM
EVEN_XBC_OFFSET = 3 * NA_WIDTH + SSD_D_INNER

GQA_HEADS = 16
GQA_KV_HEADS = 4
GQA_HEAD_DIM = 64
GQA_REP = GQA_HEADS // GQA_KV_HEADS
GQA_Q_WIDTH = GQA_HEADS * GQA_HEAD_DIM
GQA_KV_WIDTH = GQA_KV_HEADS * GQA_HEAD_DIM
ROPE_THETA = 10000.0

FFN_HIDDEN = 2816

V7X_LANES = 128
V7X_VMEM_LIMIT = 56 * 1024 * 1024
MASK_VALUE = -1e30
LOG2E = 1.4426950408889634


def _params(dims):
    return pltpu.CompilerParams(dimension_semantics=dims, vmem_limit_bytes=V7X_VMEM_LIMIT)


def _silu(v):
    return v * (1.0 / (1.0 + jnp.exp(-v)))


def _softplus(v):
    return jnp.maximum(v, 0.0) + jnp.log(1.0 + jnp.exp(-jnp.abs(v)))


def _rms_rows(x, g):
    ms = jnp.mean(x * x, axis=-1, keepdims=True)
    return x * lax.rsqrt(ms + NORM_EPS) * g


def _dot(a, b):
    return jnp.dot(a, b, preferred_element_type=F32)


def _dot_nt(a, b):
    return lax.dot_general(a, b, (((1,), (1,)), ((), ())), preferred_element_type=F32)


def _norm_proj_kernel(x_ref, g_ref, w_ref, o_ref, xn_ref):
    @pl.when(pl.program_id(1) == 0)
    def _():
        xn_ref[...] = _rms_rows(x_ref[...], g_ref[...]).astype(BF16)

    o_ref[...] = _dot(xn_ref[...], w_ref[...])


def _norm_proj2_kernel(x_ref, g_ref, w_ref, w2_ref, o_ref, o2_ref, xn_ref):
    @pl.when(pl.program_id(1) == 0)
    def _():
        xn = _rms_rows(x_ref[...], g_ref[...]).astype(BF16)
        xn_ref[...] = xn
        o2_ref[...] = _dot(xn, w2_ref[...])

    o_ref[...] = _dot(xn_ref[...], w_ref[...])


def _norm_proj(x2d, g, w, *, tm, tn, w_small=None, name):
    m, d = x2d.shape
    n = w.shape[1]
    grid = (m // tm, n // tn)
    in_specs = [pl.BlockSpec((tm, d), lambda i, j: (i, 0)),
                pl.BlockSpec((1, d), lambda i, j: (0, 0)),
                pl.BlockSpec((d, tn), lambda i, j: (0, j))]
    out_specs = pl.BlockSpec((tm, tn), lambda i, j: (i, j))
    out_shape = jax.ShapeDtypeStruct((m, n), F32)
    scratch = [pltpu.VMEM((tm, d), BF16)]
    if w_small is None:
        return pl.pallas_call(_norm_proj_kernel, grid=grid, in_specs=in_specs, out_specs=out_specs,
                              out_shape=out_shape, scratch_shapes=scratch,
                              compiler_params=_params(("parallel", "arbitrary")), name=name)(x2d, g, w)
    ns = w_small.shape[1]
    in_specs.append(pl.BlockSpec((d, ns), lambda i, j: (0, 0)))
    return pl.pallas_call(_norm_proj2_kernel, grid=grid, in_specs=in_specs,
                          out_specs=[out_specs, pl.BlockSpec((tm, ns), lambda i, j: (i, 0))],
                          out_shape=[out_shape, jax.ShapeDtypeStruct((m, ns), F32)], scratch_shapes=scratch,
                          compiler_params=_params(("parallel", "arbitrary")), name=name)(x2d, g, w, w_small)


NA_PREP_ROWS = 256
NA_GROUP_ROWS = 4
NA_WIN_ROWS = NA_KH + NA_GROUP_ROWS
NA_DY = 2 * NA_KH - 1
NA_DX = 2 * NA_KW - 1


def _na_group_plan(rows):
    sigs, starts, classes = [], [], []
    for gq in range(rows // NA_GROUP_ROWS):
        ks = min(max(gq * NA_GROUP_ROWS - NA_KH // 2, 0), rows - NA_WIN_ROWS)
        sig = tuple((min(max(r - NA_KH // 2, 0), rows - NA_KH) - ks, r - ks)
                    for r in range(gq * NA_GROUP_ROWS, (gq + 1) * NA_GROUP_ROWS))
        assert all(0 <= first and first + NA_KH <= NA_WIN_ROWS for first, _ in sig)
        if sig not in sigs:
            sigs.append(sig)
        starts.append(ks)
        classes.append(sigs.index(sig))
    return sigs, starts, classes


def _pair_head_rms(x, g):
    lo = lax.broadcasted_iota(jnp.int32, x.shape, 1) < NA_HEAD_DIM
    x2 = x * x
    s_lo = jnp.sum(jnp.where(lo, x2, 0.0), axis=-1, keepdims=True)
    s_hi = jnp.sum(jnp.where(lo, 0.0, x2), axis=-1, keepdims=True)
    ms = jnp.where(lo, s_lo, s_hi) * (1.0 / NA_HEAD_DIM)
    return x * lax.rsqrt(ms + NORM_EPS) * g


def _na_kernel(plan_ref, q_ref, k_ref, v_ref, qg_ref, kg_ref, bias_ref, o_ref, q_s, k_s, v_s, *, rows):
    scale = NA_HEAD_DIM ** -0.5

    def prep(i, carry):
        sl = pl.ds(pl.multiple_of(i * NA_PREP_ROWS, NA_PREP_ROWS), NA_PREP_ROWS)
        q_s[sl, :] = (_pair_head_rms(q_ref[0, sl, :], qg_ref[...]) * scale).astype(BF16)
        k_s[sl, :] = _pair_head_rms(k_ref[0, sl, :], kg_ref[...]).astype(BF16)
        v_s[sl, :] = v_ref[0, sl, :].astype(BF16)
        return carry

    lax.fori_loop(0, (rows * GRID_W) // NA_PREP_ROWS, prep, 0)

    n_q = NA_GROUP_ROWS * GRID_W
    n_keys = NA_WIN_ROWS * GRID_W

    def group_body(gq, carry):
        ks = plan_ref[0, gq]
        cls = plan_ref[1, gq]
        qsl = pl.ds(pl.multiple_of(gq * n_q, n_q), n_q)
        ksl = pl.ds(pl.multiple_of(ks * GRID_W, GRID_W), n_keys)
        q = q_s[qsl, :]
        kk = k_s[ksl, :]
        vv = v_s[ksl, :]
        outs = []
        for h in range(2):
            hs = slice(h * NA_HEAD_DIM, (h + 1) * NA_HEAD_DIM)
            s = _dot_nt(q[:, hs], kk[:, hs]) + bias_ref[0, cls, h]
            m = jnp.max(s, axis=-1, keepdims=True)
            p = jnp.exp(s - m)
            l = jnp.sum(p, axis=-1, keepdims=True)
            outs.append(_dot(p.astype(BF16), vv[:, hs]) * (1.0 / l))
        o_ref[0, qsl, :] = jnp.concatenate(outs, axis=-1).astype(BF16)
        return carry

    lax.fori_loop(0, rows // NA_GROUP_ROWS, group_body, 0, unroll=2)


def _na_bias_kernel(rpb_ref, o_ref, t_s, *, sigs):
    h = pl.program_id(0)
    q = lax.broadcasted_iota(jnp.int32, (GRID_W, GRID_W), 0)
    k = lax.broadcasted_iota(jnp.int32, (GRID_W, GRID_W), 1)
    dx = jnp.clip(k - q, -(NA_KW - 1), NA_KW - 1) + (NA_KW - 1)
    col_start = jnp.clip(q - NA_KW // 2, 0, GRID_W - NA_KW)
    in_win = (k >= col_start) & (k < col_start + NA_KW)
    masked = jnp.full((GRID_W, GRID_W), MASK_VALUE, F32)
    for dy in range(NA_DY):
        base = (h * NA_DY + dy) * NA_DX
        t = masked
        for d in range(NA_DX):
            t = jnp.where(dx == d, rpb_ref[base + d], t)
        t_s[dy] = jnp.where(in_win, t, MASK_VALUE)
    for cls, sig in enumerate(sigs):
        for rq, (first, qrow) in enumerate(sig):
            for jk in range(NA_WIN_ROWS):
                attended = first <= jk < first + NA_KH
                tile = t_s[jk - qrow + NA_KH - 1] if attended else masked
                o_ref[0, cls, 0, rq * GRID_W:(rq + 1) * GRID_W, jk * GRID_W:(jk + 1) * GRID_W] = tile


def _na_bias_table(rpb, sigs):
    n_q = NA_GROUP_ROWS * GRID_W
    n_keys = NA_WIN_ROWS * GRID_W
    kern = functools.partial(_na_bias_kernel, sigs=sigs)
    return pl.pallas_call(
        kern, grid=(NA_HEADS,),
        in_specs=[pl.BlockSpec(memory_space=pltpu.SMEM)],
        out_specs=pl.BlockSpec((1, len(sigs), 1, n_q, n_keys), lambda h: (h // 2, 0, h % 2, 0, 0)),
        out_shape=jax.ShapeDtypeStruct((NA_HEADS // 2, len(sigs), 2, n_q, n_keys), F32),
        scratch_shapes=[pltpu.VMEM((NA_DY, GRID_W, GRID_W), F32)],
        compiler_params=_params(("parallel",)), name="na_bias_table",
    )(rpb.astype(F32).reshape(-1))


def _neighbourhood_attention(proj3d, q_gain, k_gain, rpb):
    b, s, _ = proj3d.shape
    rows = s // GRID_W
    assert rows >= NA_WIN_ROWS and rows % NA_GROUP_ROWS == 0
    sigs, starts, classes = _na_group_plan(rows)
    bias = _na_bias_table(rpb, sigs)
    plan = jnp.array([starts, classes], jnp.int32)
    qg = jnp.tile(q_gain.astype(F32), 2)[None, :]
    kg = jnp.tile(k_gain.astype(F32), 2)[None, :]
    n_pairs = NA_HEADS // 2
    blk = (1, s, 2 * NA_HEAD_DIM)
    kern = functools.partial(_na_kernel, rows=rows)
    return pl.pallas_call(
        kern, grid=(n_pairs, b),
        in_specs=[
            pl.BlockSpec(memory_space=pltpu.SMEM),
            pl.BlockSpec(blk, lambda p, i: (i, 0, p)),
            pl.BlockSpec(blk, lambda p, i: (i, 0, n_pairs + p)),
            pl.BlockSpec(blk, lambda p, i: (i, 0, 2 * n_pairs + p)),
            pl.BlockSpec((1, 2 * NA_HEAD_DIM), lambda p, i: (0, 0)),
            pl.BlockSpec((1, 2 * NA_HEAD_DIM), lambda p, i: (0, 0)),
            pl.BlockSpec((1,) + bias.shape[1:], lambda p, i: (p, 0, 0, 0, 0)),
        ],
        out_specs=pl.BlockSpec(blk, lambda p, i: (i, 0, p)),
        out_shape=jax.ShapeDtypeStruct((b, s, NA_WIDTH), BF16),
        scratch_shapes=[pltpu.VMEM((s, 2 * NA_HEAD_DIM), BF16)] * 3,
        compiler_params=_params(("parallel", "parallel")), name="neighbourhood_attention",
    )(plan, proj3d, proj3d, proj3d, qg, kg, bias)


SSD_CONV_ROWS = 256
SSD_HALO = 8


def _chunk_scan(a, reverse):
    n = a.shape[1]
    pos = lax.broadcasted_iota(jnp.int32, a.shape, 1) % SSD_CHUNK
    sh = 1
    while sh < SSD_CHUNK:
        if reverse:
            a = a + jnp.where(pos < SSD_CHUNK - sh, pltpu.roll(a, n - sh, axis=1), 0.0)
        else:
            a = a + jnp.where(pos >= sh, pltpu.roll(a, sh, axis=1), 0.0)
        sh *= 2
    return a


def _conv_silu_into(raw_ref, w_ref, b_ref, dst_ref, seq, transpose_out):
    rows = SSD_CONV_ROWS
    n_blocks = seq // rows
    n = rows + 2 * SSD_HALO
    left = SSD_CONV // 2
    width = raw_ref.shape[-1]

    def body(i, carry):
        r0 = pl.multiple_of(i * rows, rows)
        p0 = pl.multiple_of(jnp.maximum(r0 - SSD_HALO, 0), SSD_HALO)
        n0 = pl.multiple_of(jnp.minimum(r0 + rows, seq - SSD_HALO), SSD_HALO)
        prev = jnp.where(i > 0, raw_ref[0, pl.ds(p0, SSD_HALO), :], 0.0)
        nxt = jnp.where(i < n_blocks - 1, raw_ref[0, pl.ds(n0, SSD_HALO), :], 0.0)
        win = jnp.concatenate([prev, raw_ref[0, pl.ds(r0, rows), :], nxt], axis=0)
        acc = jnp.broadcast_to(b_ref[...], (rows, width))
        for k in range(SSD_CONV):
            shift = (left - k) % n
            tap = win if shift == 0 else pltpu.roll(win, shift, axis=0)
            acc = acc + tap[SSD_HALO:SSD_HALO + rows] * w_ref[k:k + 1, :]
        out = _silu(acc)
        if transpose_out:
            dst_ref[:, pl.ds(r0, rows)] = out.T.astype(dst_ref.dtype)
        else:
            dst_ref[pl.ds(r0, rows), :] = out.astype(dst_ref.dtype)
        return carry

    lax.fori_loop(0, n_blocks, body, 0)


def _head_row(tile, first):
    lo = lax.broadcasted_iota(jnp.int32, (1, V7X_LANES), 1) < SSD_HEAD_DIM
    halves = [jnp.where(lo, tile[first + 2 * i:first + 2 * i + 1, :], tile[first + 2 * i + 1:first + 2 * i + 2, :])
              for i in range(SSD_HEADS_PER_GROUP // 2)]
    return jnp.concatenate(halves, axis=1)


def _ssd_kernel(xs_ref, b_ref, c_ref, z_ref, wx_ref, wb_ref, wc_ref, cbx_ref, cbb_ref, cbc_ref,
                dtr_ref, biasr_ref, alogr_ref, dskip_ref, gain_ref, o_ref,
                xs_s, bt_s, c_s, cumr_s, g2r_s, ld2r_s, wr_s, decr_s, st_s, h_s, *, n_chunks):
    L = SSD_CHUNK
    hpg = SSD_HEADS_PER_GROUP
    gw = SSD_GROUP_WIDTH
    ns = SSD_STATE
    seq = n_chunks * L

    _conv_silu_into(xs_ref, wx_ref, cbx_ref, xs_s, seq, False)
    _conv_silu_into(b_ref, wb_ref, cbb_ref, bt_s, seq, True)
    _conv_silu_into(c_ref, wc_ref, cbc_ref, c_s, seq, False)

    dt = _softplus(dtr_ref[0, 0] + biasr_ref[0])
    a = dt * (-jnp.exp(alogr_ref[0]))
    is_fwd = lax.broadcasted_iota(jnp.int32, a.shape, 0) < hpg
    prefix = _chunk_scan(a, False)
    suffix = _chunk_scan(a, True)
    cum2 = jnp.where(is_fwd, prefix, suffix) * LOG2E
    cumr_s[...] = cum2
    g2r_s[...] = cum2 - jnp.log2(dt)
    ld2r_s[...] = jnp.log2(dt + pltpu.roll(dt, hpg, axis=0))
    wr_s[...] = dt * jnp.exp(jnp.where(is_fwd, suffix, prefix) - a)
    decr_s[...] = jnp.exp(prefix + suffix - a)

    def chunk_slice(c):
        return pl.ds(pl.multiple_of(c * L, L), L)

    lane_head = lax.broadcasted_iota(jnp.int32, (L, gw), 1) // SSD_HEAD_DIM

    def block_diag_x(xs):
        xs_bf = xs.astype(BF16)
        return jnp.concatenate([jnp.where(lane_head == j, xs_bf, jnp.zeros_like(xs_bf)) for j in range(hpg)], axis=0)

    def state_body(c, carry):
        sl = chunk_slice(c)
        bt = bt_s[:, sl].astype(F32)
        w = wr_s[:, sl]
        lhs = jnp.concatenate(
            [jnp.concatenate([(bt * w[d * hpg + j:d * hpg + j + 1, :]).astype(BF16) for j in range(hpg)], axis=1)
             for d in range(2)], axis=0)
        st_s[c] = _dot(lhs, block_diag_x(xs_s[sl, :]))
        return carry

    lax.fori_loop(0, n_chunks, state_body, 0, unroll=2)

    def fwd_rec(c, h):
        h_s[c, :, 0:gw] = h.astype(BF16)
        return h * _head_row(decr_s[:, chunk_slice(c)], 0) + st_s[c, 0:ns, :]

    def bwd_rec(i, h):
        c = n_chunks - 1 - i
        h_s[c, :, gw:2 * gw] = h.astype(BF16)
        return h * _head_row(decr_s[:, chunk_slice(c)], hpg) + st_s[c, ns:2 * ns, :]

    h0 = jnp.zeros((ns, gw), F32)
    lax.fori_loop(0, n_chunks, fwd_rec, h0)
    lax.fori_loop(0, n_chunks, bwd_rec, h0)

    li = lax.broadcasted_iota(jnp.int32, (L, L), 0)
    si = lax.broadcasted_iota(jnp.int32, (L, L), 1)
    below = si < li
    above = si > li
    lane_lo = lax.broadcasted_iota(jnp.int32, (L, V7X_LANES), 1) < SSD_HEAD_DIM

    def out_body(c, carry):
        sl = chunk_slice(c)
        xs = xs_s[sl, :]
        cm = c_s[sl, :]
        cum_r = cumr_s[:, sl]
        g2 = g2r_s[:, sl]
        ld2 = ld2r_s[:, sl]
        cum_t = [jnp.broadcast_to(cum_r[k:k + 1, :], (L, L)).T for k in range(2 * hpg)]
        cb = _dot(cm, bt_s[:, sl])
        mats = []
        for j in range(hpg):
            seg_f = cum_t[j] - g2[j:j + 1, :]
            seg_b = cum_t[hpg + j] - g2[hpg + j:hpg + j + 1, :]
            arg = jnp.where(below, seg_f, jnp.where(above, seg_b, ld2[j:j + 1, :]))
            mats.append((cb * jnp.exp2(arg)).astype(BF16))
        y = _dot(jnp.concatenate(mats, axis=1), block_diag_x(xs))
        carried = _dot(cm, h_s[c])
        for d in range(2):
            decay = jnp.exp2(jnp.concatenate(
                [jnp.where(lane_lo, cum_t[d * hpg + 2 * i], cum_t[d * hpg + 2 * i + 1]) for i in range(hpg // 2)],
                axis=1))
            y = y + carried[:, d * gw:(d + 1) * gw] * decay
        y = y + dskip_ref[0] * xs
        gated = y * _silu(z_ref[0, sl, :])
        ms = jnp.mean(gated * gated, axis=-1, keepdims=True)
        o_ref[0, sl, :] = (gated * lax.rsqrt(ms + NORM_EPS) * gain_ref[0]).astype(BF16)
        return carry

    lax.fori_loop(0, n_chunks, out_body, 0, unroll=2)


def _group_major(v):
    return jnp.transpose(v.astype(F32).reshape(2, SSD_GROUPS, SSD_HEADS_PER_GROUP), (1, 0, 2)).reshape(
        SSD_GROUPS, 2 * SSD_HEADS_PER_GROUP)


def _ssd_mixer(proj3d, dt_raw, conv_w, conv_b, dt_bias, a_log, d_skip, out_gain):
    b, s, _ = proj3d.shape
    n_chunks = s // SSD_CHUNK
    g, hpg, gw, ns = SSD_GROUPS, SSD_HEADS_PER_GROUP, SSD_GROUP_WIDTH, SSD_STATE
    dt_row = jnp.transpose(dt_raw.reshape(b, s, 2, g, hpg), (0, 3, 2, 4, 1)).reshape(b, g, 2 * hpg, s)
    bias_g = _group_major(dt_bias)[:, :, None]
    alog_g = _group_major(a_log)[:, :, None]
    dskip = jnp.repeat(d_skip.astype(F32), SSD_HEAD_DIM).reshape(g, 1, gw)
    gain = out_gain.astype(F32).reshape(g, 1, gw)
    conv_w = conv_w.astype(F32)
    conv_b = conv_b.astype(F32)[None, :]
    z_blk = (3 * NA_WIDTH) // gw
    x_blk = EVEN_XBC_OFFSET // gw
    b_blk = (EVEN_XBC_OFFSET + SSD_D_INNER) // ns
    c_blk = b_blk + g
    wb_blk = SSD_D_INNER // ns
    wc_blk = wb_blk + g
    kern = functools.partial(_ssd_kernel, n_chunks=n_chunks)
    small = lambda shape: pl.BlockSpec((1,) + shape, lambda i, k: (k, 0, 0))
    seq_blk = lambda width, blk0: pl.BlockSpec((1, s, width), lambda i, k: (i, 0, blk0 + k))
    par_blk = lambda rows, width, blk0: pl.BlockSpec((rows, width), lambda i, k: (0, blk0 + k))
    row_scratch = pltpu.VMEM((2 * hpg, s), F32)
    return pl.pallas_call(
        kern, grid=(b, g),
        in_specs=[
            seq_blk(gw, x_blk), seq_blk(ns, b_blk), seq_blk(ns, c_blk), seq_blk(gw, z_blk),
            par_blk(SSD_CONV, gw, 0), par_blk(SSD_CONV, ns, wb_blk), par_blk(SSD_CONV, ns, wc_blk),
            par_blk(1, gw, 0), par_blk(1, ns, wb_blk), par_blk(1, ns, wc_blk),
            pl.BlockSpec((1, 1, 2 * hpg, s), lambda i, k: (i, k, 0, 0)),
            small((2 * hpg, 1)), small((2 * hpg, 1)), small((1, gw)), small((1, gw)),
        ],
        out_specs=pl.BlockSpec((1, s, gw), lambda i, k: (i, 0, k)),
        out_shape=jax.ShapeDtypeStruct((b, s, SSD_D_INNER), BF16),
        scratch_shapes=[
            pltpu.VMEM((s, gw), F32), pltpu.VMEM((ns, s), BF16), pltpu.VMEM((s, ns), BF16),
            row_scratch, row_scratch, row_scratch, row_scratch, row_scratch,
            pltpu.VMEM((n_chunks, 2 * ns, gw), F32), pltpu.VMEM((n_chunks, ns, 2 * gw), BF16),
        ],
        compiler_params=_params(("parallel", "parallel")), name="ssd_bidirectional",
    )(proj3d, proj3d, proj3d, proj3d, conv_w, conv_w, conv_w, conv_b, conv_b, conv_b, dt_row,
      bias_g, alog_g, dskip, gain)


def _out_proj2_kernel(a_ref, b_ref, wa_ref, wb_ref, x_ref, o_ref):
    o_ref[...] = x_ref[...] + _dot(a_ref[...], wa_ref[...]) + _dot(b_ref[...], wb_ref[...])


def _out_proj1_kernel(a_ref, wa_ref, x_ref, o_ref):
    o_ref[...] = x_ref[...] + _dot(a_ref[...], wa_ref[...])


def _out_proj(acts, weights, x2d, *, tm, name):
    m, d = x2d.shape
    row = lambda width: pl.BlockSpec((tm, width), lambda i: (i, 0))
    full = lambda w: pl.BlockSpec(w.shape, lambda i: (0, 0))
    kern = _out_proj2_kernel if len(acts) == 2 else _out_proj1_kernel
    return pl.pallas_call(
        kern, grid=(m // tm,),
        in_specs=[row(a.shape[1]) for a in acts] + [full(w) for w in weights] + [row(d)],
        out_specs=row(d), out_shape=jax.ShapeDtypeStruct((m, d), F32),
        compiler_params=_params(("parallel",)), name=name,
    )(*acts, *weights, x2d)


def _ffn_kernel(x_ref, g_ref, w1_ref, w3_ref, w2_ref, o_ref, xn_ref, acc_ref):
    j = pl.program_id(1)

    @pl.when(j == 0)
    def _():
        xn_ref[...] = _rms_rows(x_ref[...], g_ref[...]).astype(BF16)
        acc_ref[...] = x_ref[...]

    xn = xn_ref[...]
    act = _silu(_dot(xn, w1_ref[...])) * _dot(xn, w3_ref[...])
    acc_ref[...] += _dot(act.astype(BF16), w2_ref[...])

    @pl.when(j == pl.num_programs(1) - 1)
    def _():
        o_ref[...] = acc_ref[...]


def _ffn(x2d, g, w13, w2, *, tm, th):
    m, d = x2d.shape
    hid = w2.shape[0]
    nh = hid // th
    return pl.pallas_call(
        _ffn_kernel, grid=(m // tm, nh),
        in_specs=[
            pl.BlockSpec((tm, d), lambda i, j: (i, 0)),
            pl.BlockSpec((1, d), lambda i, j: (0, 0)),
            pl.BlockSpec((d, th), lambda i, j: (0, j)),
            pl.BlockSpec((d, th), lambda i, j: (0, nh + j)),
            pl.BlockSpec((th, d), lambda i, j: (j, 0)),
        ],
        out_specs=pl.BlockSpec((tm, d), lambda i, j: (i, 0)),
        out_shape=jax.ShapeDtypeStruct((m, d), F32),
        scratch_shapes=[pltpu.VMEM((tm, d), BF16), pltpu.VMEM((tm, d), F32)],
        compiler_params=_params(("parallel", "arbitrary")), name="swiglu_ffn",
    )(x2d, g, w13, w13, w2)


def _rope_prep_kernel(p_ref, qg_ref, kg_ref, cos_ref, sin_ref, q_ref, k_ref, v_ref):
    cos = cos_ref[...]
    sin = sin_ref[...]
    even = (lax.broadcasted_iota(jnp.int32, cos.shape, 1) % 2) == 0
    scale = GQA_HEAD_DIM ** -0.5

    def norm_rope(xb, gain):
        xn = _pair_head_rms(xb, gain)
        swapped = jnp.where(even, pltpu.roll(xn, V7X_LANES - 1, axis=1), pltpu.roll(xn, 1, axis=1))
        return xn * cos + swapped * sin

    for pair in range(GQA_HEADS // 2):
        blk = norm_rope(p_ref[0, :, pair * 128:(pair + 1) * 128], qg_ref[...]) * scale
        q_ref[0, 2 * pair] = blk[:, :GQA_HEAD_DIM].astype(BF16)
        q_ref[0, 2 * pair + 1] = blk[:, GQA_HEAD_DIM:].astype(BF16)
    for pair in range(GQA_KV_HEADS // 2):
        c0 = GQA_Q_WIDTH + pair * 128
        blk = norm_rope(p_ref[0, :, c0:c0 + 128], kg_ref[...])
        k_ref[0, 2 * pair] = blk[:, :GQA_HEAD_DIM].astype(BF16)
        k_ref[0, 2 * pair + 1] = blk[:, GQA_HEAD_DIM:].astype(BF16)
        c1 = GQA_Q_WIDTH + GQA_KV_WIDTH + pair * 128
        vt = p_ref[0, :, c1:c1 + 128].T.astype(BF16)
        v_ref[0, 2 * pair] = vt[:GQA_HEAD_DIM]
        v_ref[0, 2 * pair + 1] = vt[GQA_HEAD_DIM:]


def _axial_rope_tables(s):
    t = jnp.arange(s)
    row = (t // GRID_W).astype(F32)
    col = (t % GRID_W).astype(F32)
    axis_dims = GQA_HEAD_DIM // 2
    freqs = ROPE_THETA ** (-jnp.arange(0, axis_dims, 2, dtype=F32) / axis_dims)
    ang = jnp.concatenate([row[:, None] * freqs, col[:, None] * freqs], axis=-1)
    cos = jnp.repeat(jnp.cos(ang), 2, axis=-1)
    sin = jnp.stack([-jnp.sin(ang), jnp.sin(ang)], axis=-1).reshape(s, GQA_HEAD_DIM)
    return jnp.tile(cos, (1, 2)), jnp.tile(sin, (1, 2))


def _rope_prep(proj3d, q_gain, k_gain, *, ts):
    b, s, width = proj3d.shape
    cos, sin = _axial_rope_tables(s)
    qg = jnp.tile(q_gain.astype(F32), 2)[None, :]
    kg = jnp.tile(k_gain.astype(F32), 2)[None, :]
    head_out = lambda n: pl.BlockSpec((1, n, ts, GQA_HEAD_DIM), lambda i, t: (i, 0, t, 0))
    shape = lambda n: jax.ShapeDtypeStruct((b, n, s, GQA_HEAD_DIM), BF16)
    vt_out = pl.BlockSpec((1, GQA_KV_HEADS, GQA_HEAD_DIM, ts), lambda i, t: (i, 0, 0, t))
    vt_shape = jax.ShapeDtypeStruct((b, GQA_KV_HEADS, GQA_HEAD_DIM, s), BF16)
    return pl.pallas_call(
        _rope_prep_kernel, grid=(b, s // ts),
        in_specs=[
            pl.BlockSpec((1, ts, width), lambda i, t: (i, t, 0)),
            pl.BlockSpec((1, 128), lambda i, t: (0, 0)),
            pl.BlockSpec((1, 128), lambda i, t: (0, 0)),
            pl.BlockSpec((ts, 128), lambda i, t: (t, 0)),
            pl.BlockSpec((ts, 128), lambda i, t: (t, 0)),
        ],
        out_specs=[head_out(GQA_HEADS), head_out(GQA_KV_HEADS), vt_out],
        out_shape=[shape(GQA_HEADS), shape(GQA_KV_HEADS), vt_shape],
        compiler_params=_params(("parallel", "parallel")), name="gqa_norm_rope",
    )(proj3d, qg, kg, cos, sin)


GQA_HEADS_PER_CHAIN = 2


def _gqa_kernel(q_ref, k_ref, vt_ref, o_ref, *, tq):
    k = k_ref[0, 0]
    vt = vt_ref[0, 0]
    hpc = GQA_HEADS_PER_CHAIN
    for c in range(GQA_REP // hpc):
        q = q_ref[0, c * hpc:(c + 1) * hpc].reshape(hpc * tq, GQA_HEAD_DIM)
        st = _dot_nt(k, q)
        m = jnp.max(st, axis=0, keepdims=True)
        p = jnp.exp(st - m)
        l = jnp.sum(p, axis=0, keepdims=True)
        ot = _dot(vt, p.astype(BF16)) * (1.0 / l)
        for r in range(hpc):
            h = c * hpc + r
            o_ref[0, :, h * GQA_HEAD_DIM:(h + 1) * GQA_HEAD_DIM] = ot[:, r * tq:(r + 1) * tq].T.astype(BF16)


def _gqa_attention(q, k, vt, *, tq):
    b, _, s, _ = q.shape
    kern = functools.partial(_gqa_kernel, tq=tq)
    return pl.pallas_call(
        kern, grid=(b, GQA_KV_HEADS, s // tq),
        in_specs=[
            pl.BlockSpec((1, GQA_REP, tq, GQA_HEAD_DIM), lambda i, g, t: (i, g, t, 0)),
            pl.BlockSpec((1, 1, s, GQA_HEAD_DIM), lambda i, g, t: (i, g, 0, 0)),
            pl.BlockSpec((1, 1, GQA_HEAD_DIM, s), lambda i, g, t: (i, g, 0, 0)),
        ],
        out_specs=pl.BlockSpec((1, tq, GQA_REP * GQA_HEAD_DIM), lambda i, g, t: (i, t, g)),
        out_shape=jax.ShapeDtypeStruct((b, s, GQA_Q_WIDTH), BF16),
        compiler_params=_params(("parallel", "parallel", "parallel")), name="gqa_attention",
    )(q, k, vt)


def _even_layer(x2d, b, s, mix_norm, w_in, q_gain, k_gain, rpb, conv_w, conv_b, dt_bias, a_log, d_skip,
                out_gain, w_out):
    w_main = w_in[:, :EVEN_MAIN_WIDTH].astype(BF16)
    w_dt = w_in[:, EVEN_MAIN_WIDTH:].astype(BF16)
    proj, dt_raw = _norm_proj(x2d, mix_norm.astype(F32)[None, :], w_main, tm=1024, tn=1152, w_small=w_dt,
                              name="even_in_proj")
    proj3d = proj.reshape(b, s, EVEN_MAIN_WIDTH)
    na_out = _neighbourhood_attention(proj3d, q_gain, k_gain, rpb)
    ssd_out = _ssd_mixer(proj3d, dt_raw, conv_w, conv_b, dt_bias, a_log, d_skip, out_gain)
    w_out_bf = w_out.astype(BF16)
    return _out_proj([na_out.reshape(b * s, NA_WIDTH), ssd_out.reshape(b * s, SSD_D_INNER)],
                     [w_out_bf[:NA_WIDTH], w_out_bf[NA_WIDTH:]], x2d, tm=512, name="even_out_proj")


def _odd_layer(x2d, b, s, mix_norm, w_qkv, q_gain, k_gain, w_out):
    proj = _norm_proj(x2d, mix_norm.astype(F32)[None, :], w_qkv.astype(BF16), tm=1024, tn=w_qkv.shape[1],
                      name="odd_qkv_proj")
    q, k, vt = _rope_prep(proj.reshape(b, s, -1), q_gain, k_gain, ts=512)
    attn = _gqa_attention(q, k, vt, tq=256)
    return _out_proj([attn.reshape(b * s, GQA_Q_WIDTH)], [w_out.astype(BF16)], x2d, tm=512, name="odd_out_proj")


def kernel(x, even_mix_norm, even_w_in, na_q_norm, na_k_norm, na_rel_bias, ssd_conv_w, ssd_conv_b, ssd_dt_bias, ssd_A_log, ssd_D, ssd_out_norm, even_w_out, odd_mix_norm, odd_w_qkv, gqa_q_norm, gqa_k_norm, odd_w_out, ffn_norm, ffn_w13, ffn_w2):
    b, s, d = x.shape
    depth = ffn_norm.shape[0]
    h = x.reshape(b * s, d)
    for layer in range(depth):
        i = layer // 2
        if layer % 2 == 0:
            h = _even_layer(h, b, s, even_mix_norm[i], even_w_in[i], na_q_norm[i], na_k_norm[i], na_rel_bias[i],
                            ssd_conv_w[i], ssd_conv_b[i], ssd_dt_bias[i], ssd_A_log[i], ssd_D[i], ssd_out_norm[i],
                            even_w_out[i])
        else:
            h = _odd_layer(h, b, s, odd_mix_norm[i], odd_w_qkv[i], gqa_q_norm[i], gqa_k_norm[i], odd_w_out[i])
        h = _ffn(h, ffn_norm[layer].astype(F32)[None, :], ffn_w13[layer].astype(BF16), ffn_w2[layer].astype(BF16),
                 tm=512, th=1408)
    return h.reshape(b, s, d)
```

```python
import functools

import jax
import jax.numpy as jnp
from jax import lax
from jax.experimental import pallas as pl
from jax.experimental.pallas import tpu as pltpu

F32 = jnp.float32
BF16 = jnp.bfloat16

D_MODEL = 1024
GRID_W = 64
NORM_EPS = 1e-6

NA_HEADS = 8
NA_HEAD_DIM = 64
NA_WIDTH = NA_HEADS * NA_HEAD_DIM
NA_KH = 8
NA_KW = 16

SSD_D_INNER = 1024
SSD_HEAD_DIM = 64
SSD_HEADS = 16
SSD_GROUPS = 4
SSD_STATE = 128
SSD_CONV = 4
SSD_CHUNK = 128
SSD_CONV_DIM = SSD_D_INNER + 2 * SSD_GROUPS * SSD_STATE
SSD_GROUP_WIDTH = SSD_D_INNER // SSD_GROUPS
SSD_HEADS_PER_GROUP = SSD_HEADS // SSD_GROUPS

EVEN_MAIN_WIDTH = 3 * NA_WIDTH + SSD_D_INNER + SSD_CONV_DIM
EVEN_XBC_OFFSET = 3 * NA_WIDTH + SSD_D_INNER

GQA_HEADS = 16
GQA_KV_HEADS = 4
GQA_HEAD_DIM = 64
GQA_REP = GQA_HEADS // GQA_KV_HEADS
GQA_Q_WIDTH = GQA_HEADS * GQA_HEAD_DIM
GQA_KV_WIDTH = GQA_KV_HEADS * GQA_HEAD_DIM
ROPE_THETA = 10000.0

FFN_HIDDEN = 2816

V7X_LANES = 128
V7X_VMEM_LIMIT = 56 * 1024 * 1024
MASK_VALUE = -1e30
LOG2E = 1.4426950408889634


def _params(dims):
    return pltpu.CompilerParams(dimension_semantics=dims, vmem_limit_bytes=V7X_VMEM_LIMIT)


def _silu(v):
    return v * (1.0 / (1.0 + jnp.exp(-v)))


def _softplus(v):
    return jnp.maximum(v, 0.0) + jnp.log(1.0 + jnp.exp(-jnp.abs(v)))


def _rms_rows(x, g):
    ms = jnp.mean(x * x, axis=-1, keepdims=True)
    return x * lax.rsqrt(ms + NORM_EPS) * g


def _dot(a, b):
    return jnp.dot(a, b, preferred_element_type=F32)


def _dot_nt(a, b):
    return lax.dot_general(a, b, (((1,), (1,)), ((), ())), preferred_element_type=F32)


def _norm_proj_kernel(x_ref, g_ref, w_ref, o_ref, xn_ref):
    @pl.when(pl.program_id(1) == 0)
    def _():
        xn_ref[...] = _rms_rows(x_ref[...], g_ref[...]).astype(BF16)

    o_ref[...] = _dot(xn_ref[...], w_ref[...])


def _norm_proj2_kernel(x_ref, g_ref, w_ref, w2_ref, o_ref, o2_ref, xn_ref):
    @pl.when(pl.program_id(1) == 0)
    def _():
        xn = _rms_rows(x_ref[...], g_ref[...]).astype(BF16)
        xn_ref[...] = xn
        o2_ref[...] = _dot_nt(w2_ref[...], xn)

    o_ref[...] = _dot(xn_ref[...], w_ref[...])


def _norm_proj(x2d, g, w, *, tm, tn, w_small=None, name):
    m, d = x2d.shape
    n = w.shape[1]
    grid = (m // tm, n // tn)
    in_specs = [pl.BlockSpec((tm, d), lambda i, j: (i, 0)),
                pl.BlockSpec((1, d), lambda i, j: (0, 0)),
                pl.BlockSpec((d, tn), lambda i, j: (0, j))]
    out_specs = pl.BlockSpec((tm, tn), lambda i, j: (i, j))
    out_shape = jax.ShapeDtypeStruct((m, n), F32)
    scratch = [pltpu.VMEM((tm, d), BF16)]
    if w_small is None:
        return pl.pallas_call(_norm_proj_kernel, grid=grid, in_specs=in_specs, out_specs=out_specs,
                              out_shape=out_shape, scratch_shapes=scratch,
                              compiler_params=_params(("parallel", "arbitrary")), name=name)(x2d, g, w)
    ns = w_small.shape[0]
    in_specs.append(pl.BlockSpec((ns, d), lambda i, j: (0, 0)))
    return pl.pallas_call(_norm_proj2_kernel, grid=grid, in_specs=in_specs,
                          out_specs=[out_specs, pl.BlockSpec((ns, tm), lambda i, j: (0, i))],
                          out_shape=[out_shape, jax.ShapeDtypeStruct((ns, m), F32)], scratch_shapes=scratch,
                          compiler_params=_params(("parallel", "arbitrary")), name=name)(x2d, g, w, w_small)


NA_PREP_ROWS = 256
NA_GROUP_ROWS = 4
NA_WIN_ROWS = NA_KH + NA_GROUP_ROWS
NA_DY = 2 * NA_KH - 1
NA_DX = 2 * NA_KW - 1


def _na_group_plan(rows):
    sigs, starts, classes = [], [], []
    for gq in range(rows // NA_GROUP_ROWS):
        ks = min(max(gq * NA_GROUP_ROWS - NA_KH // 2, 0), rows - NA_WIN_ROWS)
        sig = tuple((min(max(r - NA_KH // 2, 0), rows - NA_KH) - ks, r - ks)
                    for r in range(gq * NA_GROUP_ROWS, (gq + 1) * NA_GROUP_ROWS))
        assert all(0 <= first and first + NA_KH <= NA_WIN_ROWS for first, _ in sig)
        if sig not in sigs:
            sigs.append(sig)
        starts.append(ks)
        classes.append(sigs.index(sig))
    return sigs, starts, classes


def _pair_head_rms(x, g):
    lo = lax.broadcasted_iota(jnp.int32, x.shape, 1) < NA_HEAD_DIM
    x2 = x * x
    s_lo = jnp.sum(jnp.where(lo, x2, 0.0), axis=-1, keepdims=True)
    s_hi = jnp.sum(jnp.where(lo, 0.0, x2), axis=-1, keepdims=True)
    ms = jnp.where(lo, s_lo, s_hi) * (1.0 / NA_HEAD_DIM)
    return x * lax.rsqrt(ms + NORM_EPS) * g


def _na_kernel(plan_ref, q_ref, k_ref, v_ref, qg_ref, kg_ref, bias_ref, o_ref, q_s, k_s, v_s, *, rows):
    scale = NA_HEAD_DIM ** -0.5

    def prep(i, carry):
        sl = pl.ds(pl.multiple_of(i * NA_PREP_ROWS, NA_PREP_ROWS), NA_PREP_ROWS)
        q_s[sl, :] = (_pair_head_rms(q_ref[0, sl, :], qg_ref[...]) * scale).astype(BF16)
        k_s[sl, :] = _pair_head_rms(k_ref[0, sl, :], kg_ref[...]).astype(BF16)
        v_s[sl, :] = v_ref[0, sl, :].astype(BF16)
        return carry

    lax.fori_loop(0, (rows * GRID_W) // NA_PREP_ROWS, prep, 0)

    n_q = NA_GROUP_ROWS * GRID_W
    n_keys = NA_WIN_ROWS * GRID_W

    def group_body(gq, carry):
        ks = plan_ref[0, gq]
        cls = plan_ref[1, gq]
        qsl = pl.ds(pl.multiple_of(gq * n_q, n_q), n_q)
        ksl = pl.ds(pl.multiple_of(ks * GRID_W, GRID_W), n_keys)
        q = q_s[qsl, :]
        kk = k_s[ksl, :]
        vv = v_s[ksl, :]
        outs = []
        for h in range(2):
            hs = slice(h * NA_HEAD_DIM, (h + 1) * NA_HEAD_DIM)
            s = _dot_nt(q[:, hs], kk[:, hs]) + bias_ref[0, cls, h]
            m = jnp.max(s, axis=-1, keepdims=True)
            p = jnp.exp(s - m)
            l = jnp.sum(p, axis=-1, keepdims=True)
            outs.append(_dot(p.astype(BF16), vv[:, hs]) * (1.0 / l))
        o_ref[0, qsl, :] = jnp.concatenate(outs, axis=-1).astype(BF16)
        return carry

    lax.fori_loop(0, rows // NA_GROUP_ROWS, group_body, 0, unroll=2)


def _na_bias_kernel(rpb_ref, o_ref, t_s, *, sigs):
    h = pl.program_id(0)
    q = lax.broadcasted_iota(jnp.int32, (GRID_W, GRID_W), 0)
    k = lax.broadcasted_iota(jnp.int32, (GRID_W, GRID_W), 1)
    dx = jnp.clip(k - q, -(NA_KW - 1), NA_KW - 1) + (NA_KW - 1)
    col_start = jnp.clip(q - NA_KW // 2, 0, GRID_W - NA_KW)
    in_win = (k >= col_start) & (k < col_start + NA_KW)
    masked = jnp.full((GRID_W, GRID_W), MASK_VALUE, F32)
    for dy in range(NA_DY):
        base = (h * NA_DY + dy) * NA_DX
        t = masked
        for d in range(NA_DX):
            t = jnp.where(dx == d, rpb_ref[base + d], t)
        t_s[dy] = jnp.where(in_win, t, MASK_VALUE)
    for cls, sig in enumerate(sigs):
        for rq, (first, qrow) in enumerate(sig):
            for jk in range(NA_WIN_ROWS):
                attended = first <= jk < first + NA_KH
                tile = t_s[jk - qrow + NA_KH - 1] if attended else masked
                o_ref[0, cls, 0, rq * GRID_W:(rq + 1) * GRID_W, jk * GRID_W:(jk + 1) * GRID_W] = tile


def _na_bias_table(rpb, sigs):
    n_q = NA_GROUP_ROWS * GRID_W
    n_keys = NA_WIN_ROWS * GRID_W
    kern = functools.partial(_na_bias_kernel, sigs=sigs)
    return pl.pallas_call(
        kern, grid=(NA_HEADS,),
        in_specs=[pl.BlockSpec(memory_space=pltpu.SMEM)],
        out_specs=pl.BlockSpec((1, len(sigs), 1, n_q, n_keys), lambda h: (h // 2, 0, h % 2, 0, 0)),
        out_shape=jax.ShapeDtypeStruct((NA_HEADS // 2, len(sigs), 2, n_q, n_keys), F32),
        scratch_shapes=[pltpu.VMEM((NA_DY, GRID_W, GRID_W), F32)],
        compiler_params=_params(("parallel",)), name="na_bias_table",
    )(rpb.astype(F32).reshape(-1))


def _neighbourhood_attention(proj3d, q_gain, k_gain, rpb):
    b, s, _ = proj3d.shape
    rows = s // GRID_W
    assert rows >= NA_WIN_ROWS and rows % NA_GROUP_ROWS == 0
    sigs, starts, classes = _na_group_plan(rows)
    bias = _na_bias_table(rpb, sigs)
    plan = jnp.array([starts, classes], jnp.int32)
    qg = jnp.tile(q_gain.astype(F32), 2)[None, :]
    kg = jnp.tile(k_gain.astype(F32), 2)[None, :]
    n_pairs = NA_HEADS // 2
    blk = (1, s, 2 * NA_HEAD_DIM)
    kern = functools.partial(_na_kernel, rows=rows)
    return pl.pallas_call(
        kern, grid=(n_pairs, b),
        in_specs=[
            pl.BlockSpec(memory_space=pltpu.SMEM),
            pl.BlockSpec(blk, lambda p, i: (i, 0, p)),
            pl.BlockSpec(blk, lambda p, i: (i, 0, n_pairs + p)),
            pl.BlockSpec(blk, lambda p, i: (i, 0, 2 * n_pairs + p)),
            pl.BlockSpec((1, 2 * NA_HEAD_DIM), lambda p, i: (0, 0)),
            pl.BlockSpec((1, 2 * NA_HEAD_DIM), lambda p, i: (0, 0)),
            pl.BlockSpec((1,) + bias.shape[1:], lambda p, i: (p, 0, 0, 0, 0)),
        ],
        out_specs=pl.BlockSpec(blk, lambda p, i: (i, 0, p)),
        out_shape=jax.ShapeDtypeStruct((b, s, NA_WIDTH), BF16),
        scratch_shapes=[pltpu.VMEM((s, 2 * NA_HEAD_DIM), BF16)] * 3,
        compiler_params=_params(("parallel", "parallel")), name="neighbourhood_attention",
    )(plan, proj3d, proj3d, proj3d, qg, kg, bias)


SSD_CONV_ROWS = 256
SSD_HALO = 8


def _chunk_scan(a, reverse):
    n = a.shape[1]
    pos = lax.broadcasted_iota(jnp.int32, a.shape, 1) % SSD_CHUNK
    sh = 1
    while sh < SSD_CHUNK:
        if reverse:
            a = a + jnp.where(pos < SSD_CHUNK - sh, pltpu.roll(a, n - sh, axis=1), 0.0)
        else:
            a = a + jnp.where(pos >= sh, pltpu.roll(a, sh, axis=1), 0.0)
        sh *= 2
    return a


def _conv_silu_into(raw_ref, w_ref, b_ref, dst_ref, seq, transpose_out):
    rows = SSD_CONV_ROWS
    n_blocks = seq // rows
    n = rows + 2 * SSD_HALO
    left = SSD_CONV // 2
    width = raw_ref.shape[-1]

    def body(i, carry):
        r0 = pl.multiple_of(i * rows, rows)
        p0 = pl.multiple_of(jnp.maximum(r0 - SSD_HALO, 0), SSD_HALO)
        n0 = pl.multiple_of(jnp.minimum(r0 + rows, seq - SSD_HALO), SSD_HALO)
        prev = jnp.where(i > 0, raw_ref[0, pl.ds(p0, SSD_HALO), :], 0.0)
        nxt = jnp.where(i < n_blocks - 1, raw_ref[0, pl.ds(n0, SSD_HALO), :], 0.0)
        win = jnp.concatenate([prev, raw_ref[0, pl.ds(r0, rows), :], nxt], axis=0)
        acc = jnp.broadcast_to(b_ref[...], (rows, width))
        for k in range(SSD_CONV):
            shift = (left - k) % n
            tap = win if shift == 0 else pltpu.roll(win, shift, axis=0)
            acc = acc + tap[SSD_HALO:SSD_HALO + rows] * w_ref[k:k + 1, :]
        out = _silu(acc)
        if transpose_out:
            dst_ref[:, pl.ds(r0, rows)] = out.T.astype(dst_ref.dtype)
        else:
            dst_ref[pl.ds(r0, rows), :] = out.astype(dst_ref.dtype)
        return carry

    lax.fori_loop(0, n_blocks, body, 0)


def _head_row(tile, first):
    lo = lax.broadcasted_iota(jnp.int32, (1, V7X_LANES), 1) < SSD_HEAD_DIM
    halves = [jnp.where(lo, tile[first + 2 * i:first + 2 * i + 1, :], tile[first + 2 * i + 1:first + 2 * i + 2, :])
              for i in range(SSD_HEADS_PER_GROUP // 2)]
    return jnp.concatenate(halves, axis=1)


def _ssd_kernel(xs_ref, b_ref, c_ref, z_ref, wx_ref, wb_ref, wc_ref, cbx_ref, cbb_ref, cbc_ref,
                dtr_ref, biasr_ref, alogr_ref, dskip_ref, gain_ref, o_ref,
                xs_s, bt_s, c_s, cumr_s, g2r_s, ld2r_s, wr_s, decr_s, st_s, h_s, *, n_chunks):
    L = SSD_CHUNK
    hpg = SSD_HEADS_PER_GROUP
    gw = SSD_GROUP_WIDTH
    ns = SSD_STATE
    seq = n_chunks * L

    _conv_silu_into(xs_ref, wx_ref, cbx_ref, xs_s, seq, False)
    _conv_silu_into(b_ref, wb_ref, cbb_ref, bt_s, seq, True)
    _conv_silu_into(c_ref, wc_ref, cbc_ref, c_s, seq, False)

    dt = _softplus(dtr_ref[...] + biasr_ref[0])
    a = dt * (-jnp.exp(alogr_ref[0]))
    is_fwd = lax.broadcasted_iota(jnp.int32, a.shape, 0) < hpg
    prefix = _chunk_scan(a, False)
    suffix = _chunk_scan(a, True)
    cum2 = jnp.where(is_fwd, prefix, suffix) * LOG2E
    cumr_s[...] = cum2
    g2r_s[...] = cum2 - jnp.log2(dt)
    ld2r_s[...] = jnp.log2(dt + pltpu.roll(dt, hpg, axis=0))
    wr_s[...] = dt * jnp.exp(jnp.where(is_fwd, suffix, prefix) - a)
    decr_s[...] = jnp.exp(prefix + suffix - a)

    def chunk_slice(c):
        return pl.ds(pl.multiple_of(c * L, L), L)

    lane_head = lax.broadcasted_iota(jnp.int32, (L, gw), 1) // SSD_HEAD_DIM

    def block_diag_x(xs):
        xs_bf = xs.astype(BF16)
        return jnp.concatenate([jnp.where(lane_head == j, xs_bf, jnp.zeros_like(xs_bf)) for j in range(hpg)], axis=0)

    def state_body(c, carry):
        sl = chunk_slice(c)
        bt = bt_s[:, sl].astype(F32)
        w = wr_s[:, sl]
        lhs = jnp.concatenate(
            [jnp.concatenate([(bt * w[d * hpg + j:d * hpg + j + 1, :]).astype(BF16) for j in range(hpg)], axis=1)
             for d in range(2)], axis=0)
        st_s[c] = _dot(lhs, block_diag_x(xs_s[sl, :]))
        return carry

    lax.fori_loop(0, n_chunks, state_body, 0, unroll=2)

    def fwd_rec(c, h):
        h_s[c, :, 0:gw] = h.astype(BF16)
        return h * _head_row(decr_s[:, chunk_slice(c)], 0) + st_s[c, 0:ns, :]

    def bwd_rec(i, h):
        c = n_chunks - 1 - i
        h_s[c, :, gw:2 * gw] = h.astype(BF16)
        return h * _head_row(decr_s[:, chunk_slice(c)], hpg) + st_s[c, ns:2 * ns, :]

    h0 = jnp.zeros((ns, gw), F32)
    lax.fori_loop(0, n_chunks, fwd_rec, h0)
    lax.fori_loop(0, n_chunks, bwd_rec, h0)

    li = lax.broadcasted_iota(jnp.int32, (L, L), 0)
    si = lax.broadcasted_iota(jnp.int32, (L, L), 1)
    below = si < li
    above = si > li
    lane_lo = lax.broadcasted_iota(jnp.int32, (L, V7X_LANES), 1) < SSD_HEAD_DIM

    def out_body(c, carry):
        sl = chunk_slice(c)
        xs = xs_s[sl, :]
        cm = c_s[sl, :]
        cum_r = cumr_s[:, sl]
        g2 = g2r_s[:, sl]
        ld2 = ld2r_s[:, sl]
        cum_t = [jnp.broadcast_to(cum_r[k:k + 1, :], (L, L)).T for k in range(2 * hpg)]
        cb = _dot(cm, bt_s[:, sl])
        mats = []
        for j in range(hpg):
            seg_f = cum_t[j] - g2[j:j + 1, :]
            seg_b = cum_t[hpg + j] - g2[hpg + j:hpg + j + 1, :]
            arg = jnp.where(below, seg_f, jnp.where(above, seg_b, ld2[j:j + 1, :]))
            mats.append((cb * jnp.exp2(arg)).astype(BF16))
        y = _dot(jnp.concatenate(mats, axis=1), block_diag_x(xs))
        carried = _dot(cm, h_s[c])
        for d in range(2):
            decay = jnp.exp2(jnp.concatenate(
                [jnp.where(lane_lo, cum_t[d * hpg + 2 * i], cum_t[d * hpg + 2 * i + 1]) for i in range(hpg // 2)],
                axis=1))
            y = y + carried[:, d * gw:(d + 1) * gw] * decay
        y = y + dskip_ref[0] * xs
        gated = y * _silu(z_ref[0, sl, :])
        ms = jnp.mean(gated * gated, axis=-1, keepdims=True)
        o_ref[0, sl, :] = (gated * lax.rsqrt(ms + NORM_EPS) * gain_ref[0]).astype(BF16)
        return carry

    lax.fori_loop(0, n_chunks, out_body, 0, unroll=2)


def _group_major(v):
    return jnp.transpose(v.astype(F32).reshape(2, SSD_GROUPS, SSD_HEADS_PER_GROUP), (1, 0, 2)).reshape(
        SSD_GROUPS, 2 * SSD_HEADS_PER_GROUP)


def _ssd_mixer(proj3d, dt_rows, conv_w, conv_b, dt_bias, a_log, d_skip, out_gain):
    b, s, _ = proj3d.shape
    n_chunks = s // SSD_CHUNK
    g, hpg, gw, ns = SSD_GROUPS, SSD_HEADS_PER_GROUP, SSD_GROUP_WIDTH, SSD_STATE
    bias_g = _group_major(dt_bias)[:, :, None]
    alog_g = _group_major(a_log)[:, :, None]
    dskip = jnp.repeat(d_skip.astype(F32), SSD_HEAD_DIM).reshape(g, 1, gw)
    gain = out_gain.astype(F32).reshape(g, 1, gw)
    conv_w = conv_w.astype(F32)
    conv_b = conv_b.astype(F32)[None, :]
    z_blk = (3 * NA_WIDTH) // gw
    x_blk = EVEN_XBC_OFFSET // gw
    b_blk = (EVEN_XBC_OFFSET + SSD_D_INNER) // ns
    c_blk = b_blk + g
    wb_blk = SSD_D_INNER // ns
    wc_blk = wb_blk + g
    kern = functools.partial(_ssd_kernel, n_chunks=n_chunks)
    small = lambda shape: pl.BlockSpec((1,) + shape, lambda i, k: (k, 0, 0))
    seq_blk = lambda width, blk0: pl.BlockSpec((1, s, width), lambda i, k: (i, 0, blk0 + k))
    par_blk = lambda rows, width, blk0: pl.BlockSpec((rows, width), lambda i, k: (0, blk0 + k))
    row_scratch = pltpu.VMEM((2 * hpg, s), F32)
    return pl.pallas_call(
        kern, grid=(b, g),
        in_specs=[
            seq_blk(gw, x_blk), seq_blk(ns, b_blk), seq_blk(ns, c_blk), seq_blk(gw, z_blk),
            par_blk(SSD_CONV, gw, 0), par_blk(SSD_CONV, ns, wb_blk), par_blk(SSD_CONV, ns, wc_blk),
            par_blk(1, gw, 0), par_blk(1, ns, wb_blk), par_blk(1, ns, wc_blk),
            pl.BlockSpec((2 * hpg, s), lambda i, k: (k, i)),
            small((2 * hpg, 1)), small((2 * hpg, 1)), small((1, gw)), small((1, gw)),
        ],
        out_specs=pl.BlockSpec((1, s, gw), lambda i, k: (i, 0, k)),
        out_shape=jax.ShapeDtypeStruct((b, s, SSD_D_INNER), BF16),
        scratch_shapes=[
            pltpu.VMEM((s, gw), F32), pltpu.VMEM((ns, s), BF16), pltpu.VMEM((s, ns), BF16),
            row_scratch, row_scratch, row_scratch, row_scratch, row_scratch,
            pltpu.VMEM((n_chunks, 2 * ns, gw), F32), pltpu.VMEM((n_chunks, ns, 2 * gw), BF16),
        ],
        compiler_params=_params(("parallel", "parallel")), name="ssd_bidirectional",
    )(proj3d, proj3d, proj3d, proj3d, conv_w, conv_w, conv_w, conv_b, conv_b, conv_b, dt_rows,
      bias_g, alog_g, dskip, gain)


def _out_proj2_kernel(a_ref, b_ref, wa_ref, wb_ref, x_ref, o_ref):
    o_ref[...] = x_ref[...] + _dot(a_ref[...], wa_ref[...]) + _dot(b_ref[...], wb_ref[...])


def _out_proj1_kernel(a_ref, wa_ref, x_ref, o_ref):
    o_ref[...] = x_ref[...] + _dot(a_ref[...], wa_ref[...])


def _out_proj(acts, weights, x2d, *, tm, name):
    m, d = x2d.shape
    row = lambda width: pl.BlockSpec((tm, width), lambda i: (i, 0))
    full = lambda w: pl.BlockSpec(w.shape, lambda i: (0, 0))
    kern = _out_proj2_kernel if len(acts) == 2 else _out_proj1_kernel
    return pl.pallas_call(
        kern, grid=(m // tm,),
        in_specs=[row(a.shape[1]) for a in acts] + [full(w) for w in weights] + [row(d)],
        out_specs=row(d), out_shape=jax.ShapeDtypeStruct((m, d), F32),
        compiler_params=_params(("parallel",)), name=name,
    )(*acts, *weights, x2d)


FFN_CHUNK = 256


def _ffn_kernel(x_ref, g_ref, w13_ref, w2_ref, o_ref, act_ref, *, hidden):
    x = x_ref[...]
    xn = _rms_rows(x, g_ref[...]).astype(BF16)
    for c in range(hidden // FFN_CHUNK):
        gate = slice(c * FFN_CHUNK, (c + 1) * FFN_CHUNK)
        up = slice(hidden + c * FFN_CHUNK, hidden + (c + 1) * FFN_CHUNK)
        act_ref[:, gate] = (_silu(_dot(xn, w13_ref[:, gate])) * _dot(xn, w13_ref[:, up])).astype(BF16)
    o_ref[...] = x + _dot(act_ref[...], w2_ref[...])


def _ffn(x2d, g, w13, w2, *, tm):
    m, d = x2d.shape
    hid = w2.shape[0]
    assert hid % FFN_CHUNK == 0
    resident = lambda shape: pl.BlockSpec(shape, lambda i: (0, 0), pipeline_mode=pl.Buffered(1))
    return pl.pallas_call(
        functools.partial(_ffn_kernel, hidden=hid), grid=(m // tm,),
        in_specs=[
            pl.BlockSpec((tm, d), lambda i: (i, 0)),
            pl.BlockSpec((1, d), lambda i: (0, 0)),
            resident((d, 2 * hid)),
            resident((hid, d)),
        ],
        out_specs=pl.BlockSpec((tm, d), lambda i: (i, 0)),
        out_shape=jax.ShapeDtypeStruct((m, d), F32),
        scratch_shapes=[pltpu.VMEM((tm, hid), BF16)],
        compiler_params=_params(("parallel",)), name="swiglu_ffn",
    )(x2d, g, w13, w2)


def _rope_prep_kernel(p_ref, qg_ref, kg_ref, cos_ref, sin_ref, q_ref, k_ref, v_ref):
    cos = cos_ref[...]
    sin = sin_ref[...]
    even = (lax.broadcasted_iota(jnp.int32, cos.shape, 1) % 2) == 0
    scale = GQA_HEAD_DIM ** -0.5 * LOG2E

    def norm_rope(xb, gain):
        xn = _pair_head_rms(xb, gain)
        swapped = jnp.where(even, pltpu.roll(xn, V7X_LANES - 1, axis=1), pltpu.roll(xn, 1, axis=1))
        return xn * cos + swapped * sin

    for pair in range(GQA_HEADS // 2):
        blk = norm_rope(p_ref[0, :, pair * 128:(pair + 1) * 128], qg_ref[...]) * scale
        q_ref[0, 2 * pair] = blk[:, :GQA_HEAD_DIM].astype(BF16)
        q_ref[0, 2 * pair + 1] = blk[:, GQA_HEAD_DIM:].astype(BF16)
    for pair in range(GQA_KV_HEADS // 2):
        c0 = GQA_Q_WIDTH + pair * 128
        blk = norm_rope(p_ref[0, :, c0:c0 + 128], kg_ref[...])
        k_ref[0, 2 * pair] = blk[:, :GQA_HEAD_DIM].astype(BF16)
        k_ref[0, 2 * pair + 1] = blk[:, GQA_HEAD_DIM:].astype(BF16)
        c1 = GQA_Q_WIDTH + GQA_KV_WIDTH + pair * 128
        vt = p_ref[0, :, c1:c1 + 128].T.astype(BF16)
        v_ref[0, 2 * pair] = vt[:GQA_HEAD_DIM]
        v_ref[0, 2 * pair + 1] = vt[GQA_HEAD_DIM:]


def _axial_rope_tables(s):
    t = jnp.arange(s)
    row = (t // GRID_W).astype(F32)
    col = (t % GRID_W).astype(F32)
    axis_dims = GQA_HEAD_DIM // 2
    freqs = ROPE_THETA ** (-jnp.arange(0, axis_dims, 2, dtype=F32) / axis_dims)
    ang = jnp.concatenate([row[:, None] * freqs, col[:, None] * freqs], axis=-1)
    cos = jnp.repeat(jnp.cos(ang), 2, axis=-1)
    sin = jnp.stack([-jnp.sin(ang), jnp.sin(ang)], axis=-1).reshape(s, GQA_HEAD_DIM)
    return jnp.tile(cos, (1, 2)), jnp.tile(sin, (1, 2))


def _rope_prep(proj3d, q_gain, k_gain, *, ts):
    b, s, width = proj3d.shape
    cos, sin = _axial_rope_tables(s)
    qg = jnp.tile(q_gain.astype(F32), 2)[None, :]
    kg = jnp.tile(k_gain.astype(F32), 2)[None, :]
    head_out = lambda n: pl.BlockSpec((1, n, ts, GQA_HEAD_DIM), lambda i, t: (i, 0, t, 0))
    shape = lambda n: jax.ShapeDtypeStruct((b, n, s, GQA_HEAD_DIM), BF16)
    vt_out = pl.BlockSpec((1, GQA_KV_HEADS, GQA_HEAD_DIM, ts), lambda i, t: (i, 0, 0, t))
    vt_shape = jax.ShapeDtypeStruct((b, GQA_KV_HEADS, GQA_HEAD_DIM, s), BF16)
    return pl.pallas_call(
        _rope_prep_kernel, grid=(b, s // ts),
        in_specs=[
            pl.BlockSpec((1, ts, width), lambda i, t: (i, t, 0)),
            pl.BlockSpec((1, 128), lambda i, t: (0, 0)),
            pl.BlockSpec((1, 128), lambda i, t: (0, 0)),
            pl.BlockSpec((ts, 128), lambda i, t: (t, 0)),
            pl.BlockSpec((ts, 128), lambda i, t: (t, 0)),
        ],
        out_specs=[head_out(GQA_HEADS), head_out(GQA_KV_HEADS), vt_out],
        out_shape=[shape(GQA_HEADS), shape(GQA_KV_HEADS), vt_shape],
        compiler_params=_params(("parallel", "parallel")), name="gqa_norm_rope",
    )(proj3d, qg, kg, cos, sin)


GQA_KV_CHUNK = 256


def _gqa_kernel(q_ref, k_ref, vt_ref, o_ref, s_a, s_b, m_a, m_b, *, tq, seq):
    t = pl.program_id(0)
    cols = GQA_REP * tq

    @pl.when(t == 0)
    def _():
        s_b[...] = jnp.zeros(s_b.shape, F32)
        m_b[...] = jnp.zeros(m_b.shape, F32)

    def step(s_cur, m_cur, s_prev, m_prev_ref):
        q = q_ref[0].reshape(cols, GQA_HEAD_DIM)
        m_prev = m_prev_ref[...]
        m_run = None
        l = jnp.zeros((1, cols), F32)
        acc = jnp.zeros((GQA_HEAD_DIM, cols), F32)
        for i in range(seq // GQA_KV_CHUNK):
            rows = slice(i * GQA_KV_CHUNK, (i + 1) * GQA_KV_CHUNK)
            st = _dot_nt(k_ref[0, 0, rows, :], q)
            s_cur[rows, :] = st
            cm = jnp.max(st, axis=0, keepdims=True)
            m_run = cm if m_run is None else jnp.maximum(m_run, cm)
            p = jnp.exp2(s_prev[rows, :] - m_prev)
            l = l + jnp.sum(p, axis=0, keepdims=True)
            acc = acc + _dot(vt_ref[0, 0, :, rows], p.astype(BF16))
        m_cur[...] = m_run
        ot = acc * (1.0 / l)
        for r in range(GQA_REP):
            o_ref[0, :, r * GQA_HEAD_DIM:(r + 1) * GQA_HEAD_DIM] = ot[:, r * tq:(r + 1) * tq].T.astype(BF16)

    pl.when(t % 2 == 0)(lambda: step(s_a, m_a, s_b, m_b))
    pl.when(t % 2 == 1)(lambda: step(s_b, m_b, s_a, m_a))


def _gqa_attention(q, k, vt, *, tq):
    b, _, s, _ = q.shape
    nq = s // tq
    n_blocks = b * GQA_KV_HEADS * nq
    cols = GQA_REP * tq

    def unravel(u):
        return u // (nq * GQA_KV_HEADS), (u // nq) % GQA_KV_HEADS, u % nq

    def score_block(t):
        return unravel(jnp.minimum(t, n_blocks - 1))

    def finish_block(t):
        return unravel(jnp.maximum(t - 1, 0))

    def q_map(t):
        i, g, j = score_block(t)
        return (i, g, j, 0)

    def k_map(t):
        i, g, _ = score_block(t)
        return (i, g, 0, 0)

    def vt_map(t):
        i, g, _ = finish_block(t)
        return (i, g, 0, 0)

    def o_map(t):
        i, g, j = finish_block(t)
        return (i, j, g)

    kern = functools.partial(_gqa_kernel, tq=tq, seq=s)
    return pl.pallas_call(
        kern, grid=(n_blocks + 1,),
        in_specs=[
            pl.BlockSpec((1, GQA_REP, tq, GQA_HEAD_DIM), q_map),
            pl.BlockSpec((1, 1, s, GQA_HEAD_DIM), k_map),
            pl.BlockSpec((1, 1, GQA_HEAD_DIM, s), vt_map),
        ],
        out_specs=pl.BlockSpec((1, tq, GQA_REP * GQA_HEAD_DIM), o_map),
        out_shape=jax.ShapeDtypeStruct((b, s, GQA_Q_WIDTH), BF16),
        scratch_shapes=[pltpu.VMEM((s, cols), F32), pltpu.VMEM((s, cols), F32),
                        pltpu.VMEM((1, cols), F32), pltpu.VMEM((1, cols), F32)],
        compiler_params=_params(("arbitrary",)), name="gqa_attention",
    )(q, k, vt)


def _even_layer(x2d, b, s, mix_norm, w_in, q_gain, k_gain, rpb, conv_w, conv_b, dt_bias, a_log, d_skip,
                out_gain, w_out):
    w_main = w_in[:, :EVEN_MAIN_WIDTH].astype(BF16)
    w_dt = jnp.transpose(w_in[:, EVEN_MAIN_WIDTH:].reshape(-1, 2, SSD_GROUPS, SSD_HEADS_PER_GROUP),
                         (2, 1, 3, 0)).reshape(2 * SSD_HEADS, -1).astype(BF16)
    proj, dt_rows = _norm_proj(x2d, mix_norm.astype(F32)[None, :], w_main, tm=1024, tn=1152, w_small=w_dt,
                               name="even_in_proj")
    proj3d = proj.reshape(b, s, EVEN_MAIN_WIDTH)
    na_out = _neighbourhood_attention(proj3d, q_gain, k_gain, rpb)
    ssd_out = _ssd_mixer(proj3d, dt_rows, conv_w, conv_b, dt_bias, a_log, d_skip, out_gain)
    w_out_bf = w_out.astype(BF16)
    return _out_proj([na_out.reshape(b * s, NA_WIDTH), ssd_out.reshape(b * s, SSD_D_INNER)],
                     [w_out_bf[:NA_WIDTH], w_out_bf[NA_WIDTH:]], x2d, tm=512, name="even_out_proj")


def _odd_layer(x2d, b, s, mix_norm, w_qkv, q_gain, k_gain, w_out):
    proj = _norm_proj(x2d, mix_norm.astype(F32)[None, :], w_qkv.astype(BF16), tm=1024, tn=w_qkv.shape[1],
                      name="odd_qkv_proj")
    q, k, vt = _rope_prep(proj.reshape(b, s, -1), q_gain, k_gain, ts=512)
    attn = _gqa_attention(q, k, vt, tq=256)
    return _out_proj([attn.reshape(b * s, GQA_Q_WIDTH)], [w_out.astype(BF16)], x2d, tm=512, name="odd_out_proj")


def kernel(x, even_mix_norm, even_w_in, na_q_norm, na_k_norm, na_rel_bias, ssd_conv_w, ssd_conv_b, ssd_dt_bias, ssd_A_log, ssd_D, ssd_out_norm, even_w_out, odd_mix_norm, odd_w_qkv, gqa_q_norm, gqa_k_norm, odd_w_out, ffn_norm, ffn_w13, ffn_w2):
    b, s, d = x.shape
    depth = ffn_norm.shape[0]
    h = x.reshape(b * s, d)
    for layer in range(depth):
        i = layer // 2
        if layer % 2 == 0:
            h = _even_layer(h, b, s, even_mix_norm[i], even_w_in[i], na_q_norm[i], na_k_norm[i], na_rel_bias[i],
                            ssd_conv_w[i], ssd_conv_b[i], ssd_dt_bias[i], ssd_A_log[i], ssd_D[i], ssd_out_norm[i],
                            even_w_out[i])
        else:
            h = _odd_layer(h, b, s, odd_mix_norm[i], odd_w_qkv[i], gqa_q_norm[i], gqa_k_norm[i], odd_w_out[i])
        h = _ffn(h, ffn_norm[layer].astype(F32)[None, :], ffn_w13[layer].astype(BF16), ffn_w2[layer].astype(BF16),
                 tm=512)
    return h.reshape(b, s, d)
```

```python
import functools

import jax
import jax.numpy as jnp
from jax import lax
from jax.experimental import pallas as pl
from jax.experimental.pallas import tpu as pltpu

F32 = jnp.float32
BF16 = jnp.bfloat16

D_MODEL = 1024
GRID_W = 64
NORM_EPS = 1e-6

NA_HEADS = 8
NA_HEAD_DIM = 64
NA_WIDTH = NA_HEADS * NA_HEAD_DIM
NA_KH = 8
NA_KW = 16

SSD_D_INNER = 1024
SSD_HEAD_DIM = 64
SSD_HEADS = 16
SSD_GROUPS = 4
SSD_STATE = 128
SSD_CONV = 4
SSD_CHUNK = 128
SSD_CONV_DIM = SSD_D_INNER + 2 * SSD_GROUPS * SSD_STATE
SSD_GROUP_WIDTH = SSD_D_INNER // SSD_GROUPS
SSD_HEADS_PER_GROUP = SSD_HEADS // SSD_GROUPS

EVEN_MAIN_WIDTH = 3 * NA_WIDTH + SSD_D_INNER + SSD_CONV_DIM
EVEN_XBC_OFFSET = 3 * NA_WIDTH + SSD_D_INNER

GQA_HEADS = 16
GQA_KV_HEADS = 4
GQA_HEAD_DIM = 64
GQA_REP = GQA_HEADS // GQA_KV_HEADS
GQA_Q_WIDTH = GQA_HEADS * GQA_HEAD_DIM
GQA_KV_WIDTH = GQA_KV_HEADS * GQA_HEAD_DIM
ROPE_THETA = 10000.0

FFN_HIDDEN = 2816

V7X_LANES = 128
V7X_VMEM_LIMIT = 56 * 1024 * 1024
MASK_VALUE = -1e30
LOG2E = 1.4426950408889634


def _params(dims):
    return pltpu.CompilerParams(dimension_semantics=dims, vmem_limit_bytes=V7X_VMEM_LIMIT)


def _silu(v):
    return v * (1.0 / (1.0 + jnp.exp(-v)))


def _softplus(v):
    return jnp.maximum(v, 0.0) + jnp.log(1.0 + jnp.exp(-jnp.abs(v)))


def _rms_rows(x, g):
    ms = jnp.mean(x * x, axis=-1, keepdims=True)
    return x * lax.rsqrt(ms + NORM_EPS) * g


def _dot(a, b):
    return jnp.dot(a, b, preferred_element_type=F32)


def _dot_nt(a, b):
    return lax.dot_general(a, b, (((1,), (1,)), ((), ())), preferred_element_type=F32)


def _norm_proj_kernel(x_ref, g_ref, w_ref, o_ref, xn_ref):
    @pl.when(pl.program_id(1) == 0)
    def _():
        xn_ref[...] = _rms_rows(x_ref[...], g_ref[...]).astype(BF16)

    o_ref[...] = _dot(xn_ref[...], w_ref[...]).astype(o_ref.dtype)


def _norm_proj2_kernel(x_ref, g_ref, w_ref, w2_ref, o_ref, o2_ref, xn_ref):
    @pl.when(pl.program_id(1) == 0)
    def _():
        xn = _rms_rows(x_ref[...], g_ref[...]).astype(BF16)
        xn_ref[...] = xn
        o2_ref[...] = _dot_nt(w2_ref[...], xn)

    o_ref[...] = _dot(xn_ref[...], w_ref[...]).astype(o_ref.dtype)


def _norm_proj(x2d, g, w, *, tm, tn, w_small=None, name):
    m, d = x2d.shape
    n = w.shape[1]
    grid = (m // tm, n // tn)
    in_specs = [pl.BlockSpec((tm, d), lambda i, j: (i, 0)),
                pl.BlockSpec((1, d), lambda i, j: (0, 0)),
                pl.BlockSpec((d, tn), lambda i, j: (0, j))]
    out_specs = pl.BlockSpec((tm, tn), lambda i, j: (i, j))
    out_shape = jax.ShapeDtypeStruct((m, n), BF16)
    scratch = [pltpu.VMEM((tm, d), BF16)]
    if w_small is None:
        return pl.pallas_call(_norm_proj_kernel, grid=grid, in_specs=in_specs, out_specs=out_specs,
                              out_shape=out_shape, scratch_shapes=scratch,
                              compiler_params=_params(("parallel", "arbitrary")), name=name)(x2d, g, w)
    ns = w_small.shape[0]
    in_specs.append(pl.BlockSpec((ns, d), lambda i, j: (0, 0)))
    return pl.pallas_call(_norm_proj2_kernel, grid=grid, in_specs=in_specs,
                          out_specs=[out_specs, pl.BlockSpec((ns, tm), lambda i, j: (0, i))],
                          out_shape=[out_shape, jax.ShapeDtypeStruct((ns, m), F32)], scratch_shapes=scratch,
                          compiler_params=_params(("parallel", "arbitrary")), name=name)(x2d, g, w, w_small)


NA_PREP_ROWS = 256
NA_GROUP_ROWS = 4
NA_WIN_ROWS = NA_KH + NA_GROUP_ROWS
NA_DY = 2 * NA_KH - 1
NA_DX = 2 * NA_KW - 1


def _na_group_plan(rows):
    sigs, starts, classes = [], [], []
    for gq in range(rows // NA_GROUP_ROWS):
        ks = min(max(gq * NA_GROUP_ROWS - NA_KH // 2, 0), rows - NA_WIN_ROWS)
        sig = tuple((min(max(r - NA_KH // 2, 0), rows - NA_KH) - ks, r - ks)
                    for r in range(gq * NA_GROUP_ROWS, (gq + 1) * NA_GROUP_ROWS))
        assert all(0 <= first and first + NA_KH <= NA_WIN_ROWS for first, _ in sig)
        if sig not in sigs:
            sigs.append(sig)
        starts.append(ks)
        classes.append(sigs.index(sig))
    return sigs, starts, classes


def _pair_head_rms(x, g):
    lo = lax.broadcasted_iota(jnp.int32, x.shape, 1) < NA_HEAD_DIM
    x2 = x * x
    s_lo = jnp.sum(jnp.where(lo, x2, 0.0), axis=-1, keepdims=True)
    s_hi = jnp.sum(jnp.where(lo, 0.0, x2), axis=-1, keepdims=True)
    ms = jnp.where(lo, s_lo, s_hi) * (1.0 / NA_HEAD_DIM)
    return x * lax.rsqrt(ms + NORM_EPS) * g


def _na_kernel(plan_ref, q_ref, k_ref, v_ref, qg_ref, kg_ref, bias_ref, o_ref, q_s, k_s, *, rows):
    scale = NA_HEAD_DIM ** -0.5

    def prep(i, carry):
        sl = pl.ds(pl.multiple_of(i * NA_PREP_ROWS, NA_PREP_ROWS), NA_PREP_ROWS)
        q_s[sl, :] = (_pair_head_rms(q_ref[0, sl, :].astype(F32), qg_ref[...]) * scale).astype(BF16)
        k_s[sl, :] = _pair_head_rms(k_ref[0, sl, :].astype(F32), kg_ref[...]).astype(BF16)
        return carry

    lax.fori_loop(0, (rows * GRID_W) // NA_PREP_ROWS, prep, 0)

    n_q = NA_GROUP_ROWS * GRID_W
    n_keys = NA_WIN_ROWS * GRID_W

    def group_body(gq, carry):
        ks = plan_ref[0, gq]
        cls = plan_ref[1, gq]
        qsl = pl.ds(pl.multiple_of(gq * n_q, n_q), n_q)
        ksl = pl.ds(pl.multiple_of(ks * GRID_W, GRID_W), n_keys)
        q = q_s[qsl, :]
        kk = k_s[ksl, :]
        vv = v_ref[0, ksl, :]
        outs = []
        for h in range(2):
            hs = slice(h * NA_HEAD_DIM, (h + 1) * NA_HEAD_DIM)
            s = _dot_nt(q[:, hs], kk[:, hs]) + bias_ref[0, cls, h]
            m = jnp.max(s, axis=-1, keepdims=True)
            p = jnp.exp(s - m)
            l = jnp.sum(p, axis=-1, keepdims=True)
            outs.append(_dot(p.astype(BF16), vv[:, hs]) * (1.0 / l))
        o_ref[0, qsl, :] = jnp.concatenate(outs, axis=-1).astype(BF16)
        return carry

    lax.fori_loop(0, rows // NA_GROUP_ROWS, group_body, 0, unroll=2)


def _na_bias_kernel(rpb_ref, o_ref, t_s, *, sigs):
    h = pl.program_id(0)
    q = lax.broadcasted_iota(jnp.int32, (GRID_W, GRID_W), 0)
    k = lax.broadcasted_iota(jnp.int32, (GRID_W, GRID_W), 1)
    dx = jnp.clip(k - q, -(NA_KW - 1), NA_KW - 1) + (NA_KW - 1)
    col_start = jnp.clip(q - NA_KW // 2, 0, GRID_W - NA_KW)
    in_win = (k >= col_start) & (k < col_start + NA_KW)
    masked = jnp.full((GRID_W, GRID_W), MASK_VALUE, F32)
    for dy in range(NA_DY):
        base = (h * NA_DY + dy) * NA_DX
        t = masked
        for d in range(NA_DX):
            t = jnp.where(dx == d, rpb_ref[base + d], t)
        t_s[dy] = jnp.where(in_win, t, MASK_VALUE)
    for cls, sig in enumerate(sigs):
        for rq, (first, qrow) in enumerate(sig):
            for jk in range(NA_WIN_ROWS):
                attended = first <= jk < first + NA_KH
                tile = t_s[jk - qrow + NA_KH - 1] if attended else masked
                o_ref[0, cls, 0, rq * GRID_W:(rq + 1) * GRID_W, jk * GRID_W:(jk + 1) * GRID_W] = tile


def _na_bias_table(rpb, sigs):
    n_q = NA_GROUP_ROWS * GRID_W
    n_keys = NA_WIN_ROWS * GRID_W
    kern = functools.partial(_na_bias_kernel, sigs=sigs)
    return pl.pallas_call(
        kern, grid=(NA_HEADS,),
        in_specs=[pl.BlockSpec(memory_space=pltpu.SMEM)],
        out_specs=pl.BlockSpec((1, len(sigs), 1, n_q, n_keys), lambda h: (h // 2, 0, h % 2, 0, 0)),
        out_shape=jax.ShapeDtypeStruct((NA_HEADS // 2, len(sigs), 2, n_q, n_keys), F32),
        scratch_shapes=[pltpu.VMEM((NA_DY, GRID_W, GRID_W), F32)],
        compiler_params=_params(("parallel",)), name="na_bias_table",
    )(rpb.astype(F32).reshape(-1))


def _neighbourhood_attention(proj3d, q_gain, k_gain, rpb):
    b, s, _ = proj3d.shape
    rows = s // GRID_W
    assert rows >= NA_WIN_ROWS and rows % NA_GROUP_ROWS == 0
    sigs, starts, classes = _na_group_plan(rows)
    bias = _na_bias_table(rpb, sigs)
    plan = jnp.array([starts, classes], jnp.int32)
    qg = jnp.tile(q_gain.astype(F32), 2)[None, :]
    kg = jnp.tile(k_gain.astype(F32), 2)[None, :]
    n_pairs = NA_HEADS // 2
    blk = (1, s, 2 * NA_HEAD_DIM)
    kern = functools.partial(_na_kernel, rows=rows)
    return pl.pallas_call(
        kern, grid=(n_pairs, b),
        in_specs=[
            pl.BlockSpec(memory_space=pltpu.SMEM),
            pl.BlockSpec(blk, lambda p, i: (i, 0, p)),
            pl.BlockSpec(blk, lambda p, i: (i, 0, n_pairs + p)),
            pl.BlockSpec(blk, lambda p, i: (i, 0, 2 * n_pairs + p)),
            pl.BlockSpec((1, 2 * NA_HEAD_DIM), lambda p, i: (0, 0)),
            pl.BlockSpec((1, 2 * NA_HEAD_DIM), lambda p, i: (0, 0)),
            pl.BlockSpec((1,) + bias.shape[1:], lambda p, i: (p, 0, 0, 0, 0)),
        ],
        out_specs=pl.BlockSpec(blk, lambda p, i: (i, 0, p)),
        out_shape=jax.ShapeDtypeStruct((b, s, NA_WIDTH), BF16),
        scratch_shapes=[pltpu.VMEM((s, 2 * NA_HEAD_DIM), BF16)] * 2,
        compiler_params=_params(("parallel", "parallel")), name="neighbourhood_attention",
    )(plan, proj3d, proj3d, proj3d, qg, kg, bias)


SSD_CONV_ROWS = 256
SSD_HALO = 16


def _chunk_scan(a, reverse):
    n = a.shape[1]
    pos = lax.broadcasted_iota(jnp.int32, a.shape, 1) % SSD_CHUNK
    sh = 1
    while sh < SSD_CHUNK:
        if reverse:
            a = a + jnp.where(pos < SSD_CHUNK - sh, pltpu.roll(a, n - sh, axis=1), 0.0)
        else:
            a = a + jnp.where(pos >= sh, pltpu.roll(a, sh, axis=1), 0.0)
        sh *= 2
    return a


def _conv_silu_into(raw_ref, w_ref, b_ref, dst_ref, seq, transpose_out):
    rows = SSD_CONV_ROWS
    n_blocks = seq // rows
    n = rows + 2 * SSD_HALO
    left = SSD_CONV // 2
    width = raw_ref.shape[-1]

    def body(i, carry):
        r0 = pl.multiple_of(i * rows, rows)
        p0 = pl.multiple_of(jnp.maximum(r0 - SSD_HALO, 0), SSD_HALO)
        n0 = pl.multiple_of(jnp.minimum(r0 + rows, seq - SSD_HALO), SSD_HALO)
        prev = jnp.where(i > 0, raw_ref[0, pl.ds(p0, SSD_HALO), :].astype(F32), 0.0)
        nxt = jnp.where(i < n_blocks - 1, raw_ref[0, pl.ds(n0, SSD_HALO), :].astype(F32), 0.0)
        win = jnp.concatenate([prev, raw_ref[0, pl.ds(r0, rows), :].astype(F32), nxt], axis=0)
        acc = jnp.broadcast_to(b_ref[...], (rows, width))
        for k in range(SSD_CONV):
            shift = (left - k) % n
            tap = win if shift == 0 else pltpu.roll(win, shift, axis=0)
            acc = acc + tap[SSD_HALO:SSD_HALO + rows] * w_ref[k:k + 1, :]
        out = _silu(acc)
        if transpose_out:
            dst_ref[:, pl.ds(r0, rows)] = out.T.astype(dst_ref.dtype)
        else:
            dst_ref[pl.ds(r0, rows), :] = out.astype(dst_ref.dtype)
        return carry

    lax.fori_loop(0, n_blocks, body, 0)


def _head_row(tile, first):
    lo = lax.broadcasted_iota(jnp.int32, (1, V7X_LANES), 1) < SSD_HEAD_DIM
    halves = [jnp.where(lo, tile[first + 2 * i:first + 2 * i + 1, :], tile[first + 2 * i + 1:first + 2 * i + 2, :])
              for i in range(SSD_HEADS_PER_GROUP // 2)]
    return jnp.concatenate(halves, axis=1)


def _ssd_kernel(xs_ref, b_ref, c_ref, z_ref, wx_ref, wb_ref, wc_ref, cbx_ref, cbb_ref, cbc_ref,
                dtr_ref, biasr_ref, alogr_ref, dskip_ref, gain_ref, o_ref,
                xs_s, bt_s, c_s, cumr_s, g2r_s, ld2r_s, wr_s, decr_s, st_s, h_s, *, n_chunks):
    L = SSD_CHUNK
    hpg = SSD_HEADS_PER_GROUP
    gw = SSD_GROUP_WIDTH
    ns = SSD_STATE
    seq = n_chunks * L

    _conv_silu_into(xs_ref, wx_ref, cbx_ref, xs_s, seq, False)
    _conv_silu_into(b_ref, wb_ref, cbb_ref, bt_s, seq, True)
    _conv_silu_into(c_ref, wc_ref, cbc_ref, c_s, seq, False)

    dt = _softplus(dtr_ref[...] + biasr_ref[0])
    a = dt * (-jnp.exp(alogr_ref[0]))
    is_fwd = lax.broadcasted_iota(jnp.int32, a.shape, 0) < hpg
    prefix = _chunk_scan(a, False)
    suffix = _chunk_scan(a, True)
    cum2 = jnp.where(is_fwd, prefix, suffix) * LOG2E
    cumr_s[...] = cum2
    g2r_s[...] = cum2 - jnp.log2(dt)
    ld2r_s[...] = jnp.log2(dt + pltpu.roll(dt, hpg, axis=0))
    wr_s[...] = dt * jnp.exp(jnp.where(is_fwd, suffix, prefix) - a)
    decr_s[...] = jnp.exp(prefix + suffix - a)

    def chunk_slice(c):
        return pl.ds(pl.multiple_of(c * L, L), L)

    lane_head = lax.broadcasted_iota(jnp.int32, (L, gw), 1) // SSD_HEAD_DIM

    def block_diag_x(xs):
        xs_bf = xs.astype(BF16)
        return jnp.concatenate([jnp.where(lane_head == j, xs_bf, jnp.zeros_like(xs_bf)) for j in range(hpg)], axis=0)

    def state_body(c, carry):
        sl = chunk_slice(c)
        bt = bt_s[:, sl].astype(F32)
        w = wr_s[:, sl]
        lhs = jnp.concatenate(
            [jnp.concatenate([(bt * w[d * hpg + j:d * hpg + j + 1, :]).astype(BF16) for j in range(hpg)], axis=1)
             for d in range(2)], axis=0)
        st_s[c] = _dot(lhs, block_diag_x(xs_s[sl, :]))
        return carry

    lax.fori_loop(0, n_chunks, state_body, 0, unroll=2)

    def fwd_rec(c, h):
        h_s[c, :, 0:gw] = h.astype(BF16)
        return h * _head_row(decr_s[:, chunk_slice(c)], 0) + st_s[c, 0:ns, :]

    def bwd_rec(i, h):
        c = n_chunks - 1 - i
        h_s[c, :, gw:2 * gw] = h.astype(BF16)
        return h * _head_row(decr_s[:, chunk_slice(c)], hpg) + st_s[c, ns:2 * ns, :]

    h0 = jnp.zeros((ns, gw), F32)
    lax.fori_loop(0, n_chunks, fwd_rec, h0)
    lax.fori_loop(0, n_chunks, bwd_rec, h0)

    li = lax.broadcasted_iota(jnp.int32, (L, L), 0)
    si = lax.broadcasted_iota(jnp.int32, (L, L), 1)
    below = si < li
    above = si > li
    lane_lo = lax.broadcasted_iota(jnp.int32, (L, V7X_LANES), 1) < SSD_HEAD_DIM

    def out_body(c, carry):
        sl = chunk_slice(c)
        xs = xs_s[sl, :]
        cm = c_s[sl, :]
        cum_r = cumr_s[:, sl]
        g2 = g2r_s[:, sl]
        ld2 = ld2r_s[:, sl]
        cum_t = [jnp.broadcast_to(cum_r[k:k + 1, :], (L, L)).T for k in range(2 * hpg)]
        cb = _dot(cm, bt_s[:, sl])
        mats = []
        for j in range(hpg):
            seg_f = cum_t[j] - g2[j:j + 1, :]
            seg_b = cum_t[hpg + j] - g2[hpg + j:hpg + j + 1, :]
            arg = jnp.where(below, seg_f, jnp.where(above, seg_b, ld2[j:j + 1, :]))
            mats.append((cb * jnp.exp2(arg)).astype(BF16))
        y = _dot(jnp.concatenate(mats, axis=1), block_diag_x(xs))
        carried = _dot(cm, h_s[c])
        for d in range(2):
            decay = jnp.exp2(jnp.concatenate(
                [jnp.where(lane_lo, cum_t[d * hpg + 2 * i], cum_t[d * hpg + 2 * i + 1]) for i in range(hpg // 2)],
                axis=1))
            y = y + carried[:, d * gw:(d + 1) * gw] * decay
        y = y + dskip_ref[0] * xs
        gated = y * _silu(z_ref[0, sl, :].astype(F32))
        ms = jnp.mean(gated * gated, axis=-1, keepdims=True)
        o_ref[0, sl, :] = (gated * lax.rsqrt(ms + NORM_EPS) * gain_ref[0]).astype(BF16)
        return carry

    lax.fori_loop(0, n_chunks, out_body, 0, unroll=2)


def _group_major(v):
    return jnp.transpose(v.astype(F32).reshape(2, SSD_GROUPS, SSD_HEADS_PER_GROUP), (1, 0, 2)).reshape(
        SSD_GROUPS, 2 * SSD_HEADS_PER_GROUP)


def _ssd_mixer(proj3d, dt_rows, conv_w, conv_b, dt_bias, a_log, d_skip, out_gain):
    b, s, _ = proj3d.shape
    n_chunks = s // SSD_CHUNK
    g, hpg, gw, ns = SSD_GROUPS, SSD_HEADS_PER_GROUP, SSD_GROUP_WIDTH, SSD_STATE
    bias_g = _group_major(dt_bias)[:, :, None]
    alog_g = _group_major(a_log)[:, :, None]
    dskip = jnp.repeat(d_skip.astype(F32), SSD_HEAD_DIM).reshape(g, 1, gw)
    gain = out_gain.astype(F32).reshape(g, 1, gw)
    conv_w = conv_w.astype(F32)
    conv_b = conv_b.astype(F32)[None, :]
    z_blk = (3 * NA_WIDTH) // gw
    x_blk = EVEN_XBC_OFFSET // gw
    b_blk = (EVEN_XBC_OFFSET + SSD_D_INNER) // ns
    c_blk = b_blk + g
    wb_blk = SSD_D_INNER // ns
    wc_blk = wb_blk + g
    kern = functools.partial(_ssd_kernel, n_chunks=n_chunks)
    small = lambda shape: pl.BlockSpec((1,) + shape, lambda i, k: (k, 0, 0))
    seq_blk = lambda width, blk0: pl.BlockSpec((1, s, width), lambda i, k: (i, 0, blk0 + k))
    par_blk = lambda rows, width, blk0: pl.BlockSpec((rows, width), lambda i, k: (0, blk0 + k))
    row_scratch = pltpu.VMEM((2 * hpg, s), F32)
    return pl.pallas_call(
        kern, grid=(b, g),
        in_specs=[
            seq_blk(gw, x_blk), seq_blk(ns, b_blk), seq_blk(ns, c_blk), seq_blk(gw, z_blk),
            par_blk(SSD_CONV, gw, 0), par_blk(SSD_CONV, ns, wb_blk), par_blk(SSD_CONV, ns, wc_blk),
            par_blk(1, gw, 0), par_blk(1, ns, wb_blk), par_blk(1, ns, wc_blk),
            pl.BlockSpec((2 * hpg, s), lambda i, k: (k, i)),
            small((2 * hpg, 1)), small((2 * hpg, 1)), small((1, gw)), small((1, gw)),
        ],
        out_specs=pl.BlockSpec((1, s, gw), lambda i, k: (i, 0, k)),
        out_shape=jax.ShapeDtypeStruct((b, s, SSD_D_INNER), BF16),
        scratch_shapes=[
            pltpu.VMEM((s, gw), F32), pltpu.VMEM((ns, s), BF16), pltpu.VMEM((s, ns), BF16),
            row_scratch, row_scratch, row_scratch, row_scratch, row_scratch,
            pltpu.VMEM((n_chunks, 2 * ns, gw), F32), pltpu.VMEM((n_chunks, ns, 2 * gw), BF16),
        ],
        compiler_params=_params(("parallel", "parallel")), name="ssd_bidirectional",
    )(proj3d, proj3d, proj3d, proj3d, conv_w, conv_w, conv_w, conv_b, conv_b, conv_b, dt_rows,
      bias_g, alog_g, dskip, gain)


def _out_proj2_kernel(a_ref, b_ref, wa_ref, wb_ref, x_ref, o_ref):
    o_ref[...] = x_ref[...] + _dot(a_ref[...], wa_ref[...]) + _dot(b_ref[...], wb_ref[...])


def _out_proj1_kernel(a_ref, wa_ref, x_ref, o_ref):
    o_ref[...] = x_ref[...] + _dot(a_ref[...], wa_ref[...])


def _out_proj(acts, weights, x2d, *, tm, name):
    m, d = x2d.shape
    row = lambda width: pl.BlockSpec((tm, width), lambda i: (i, 0))
    full = lambda w: pl.BlockSpec(w.shape, lambda i: (0, 0))
    kern = _out_proj2_kernel if len(acts) == 2 else _out_proj1_kernel
    return pl.pallas_call(
        kern, grid=(m // tm,),
        in_specs=[row(a.shape[1]) for a in acts] + [full(w) for w in weights] + [row(d)],
        out_specs=row(d), out_shape=jax.ShapeDtypeStruct((m, d), F32),
        compiler_params=_params(("parallel",)), name=name,
    )(*acts, *weights, x2d)


FFN_CHUNK = 256


def _ffn_kernel(x_ref, g_ref, w13_ref, w2_ref, o_ref, act_ref, *, hidden):
    x = x_ref[...]
    xn = _rms_rows(x, g_ref[...]).astype(BF16)
    for c in range(hidden // FFN_CHUNK):
        gate = slice(c * FFN_CHUNK, (c + 1) * FFN_CHUNK)
        up = slice(hidden + c * FFN_CHUNK, hidden + (c + 1) * FFN_CHUNK)
        act_ref[:, gate] = (_silu(_dot(xn, w13_ref[:, gate])) * _dot(xn, w13_ref[:, up])).astype(BF16)
    o_ref[...] = x + _dot(act_ref[...], w2_ref[...])


def _ffn(x2d, g, w13, w2, *, tm):
    m, d = x2d.shape
    hid = w2.shape[0]
    assert hid % FFN_CHUNK == 0
    resident = lambda shape: pl.BlockSpec(shape, lambda i: (0, 0), pipeline_mode=pl.Buffered(1))
    return pl.pallas_call(
        functools.partial(_ffn_kernel, hidden=hid), grid=(m // tm,),
        in_specs=[
            pl.BlockSpec((tm, d), lambda i: (i, 0)),
            pl.BlockSpec((1, d), lambda i: (0, 0)),
            resident((d, 2 * hid)),
            resident((hid, d)),
        ],
        out_specs=pl.BlockSpec((tm, d), lambda i: (i, 0)),
        out_shape=jax.ShapeDtypeStruct((m, d), F32),
        scratch_shapes=[pltpu.VMEM((tm, hid), BF16)],
        compiler_params=_params(("parallel",)), name="swiglu_ffn",
    )(x2d, g, w13, w2)


def _rope_prep_kernel(p_ref, qg_ref, kg_ref, cos_ref, sin_ref, q_ref, k_ref, v_ref):
    cos = cos_ref[...]
    sin = sin_ref[...]
    even = (lax.broadcasted_iota(jnp.int32, cos.shape, 1) % 2) == 0
    scale = GQA_HEAD_DIM ** -0.5 * LOG2E

    def norm_rope(xb, gain):
        xn = _pair_head_rms(xb, gain)
        swapped = jnp.where(even, pltpu.roll(xn, V7X_LANES - 1, axis=1), pltpu.roll(xn, 1, axis=1))
        return xn * cos + swapped * sin

    for pair in range(GQA_HEADS // 2):
        blk = norm_rope(p_ref[0, :, pair * 128:(pair + 1) * 128].astype(F32), qg_ref[...]) * scale
        q_ref[0, 2 * pair] = blk[:, :GQA_HEAD_DIM].astype(BF16)
        q_ref[0, 2 * pair + 1] = blk[:, GQA_HEAD_DIM:].astype(BF16)
    for pair in range(GQA_KV_HEADS // 2):
        c0 = GQA_Q_WIDTH + pair * 128
        blk = norm_rope(p_ref[0, :, c0:c0 + 128].astype(F32), kg_ref[...])
        k_ref[0, 2 * pair] = blk[:, :GQA_HEAD_DIM].astype(BF16)
        k_ref[0, 2 * pair + 1] = blk[:, GQA_HEAD_DIM:].astype(BF16)
        c1 = GQA_Q_WIDTH + GQA_KV_WIDTH + pair * 128
        vt = p_ref[0, :, c1:c1 + 128].astype(F32).T.astype(BF16)
        v_ref[0, 2 * pair] = vt[:GQA_HEAD_DIM]
        v_ref[0, 2 * pair + 1] = vt[GQA_HEAD_DIM:]


def _axial_rope_tables(s):
    t = jnp.arange(s)
    row = (t // GRID_W).astype(F32)
    col = (t % GRID_W).astype(F32)
    axis_dims = GQA_HEAD_DIM // 2
    freqs = ROPE_THETA ** (-jnp.arange(0, axis_dims, 2, dtype=F32) / axis_dims)
    ang = jnp.concatenate([row[:, None] * freqs, col[:, None] * freqs], axis=-1)
    cos = jnp.repeat(jnp.cos(ang), 2, axis=-1)
    sin = jnp.stack([-jnp.sin(ang), jnp.sin(ang)], axis=-1).reshape(s, GQA_HEAD_DIM)
    return jnp.tile(cos, (1, 2)), jnp.tile(sin, (1, 2))


def _rope_prep(proj3d, q_gain, k_gain, *, ts):
    b, s, width = proj3d.shape
    cos, sin = _axial_rope_tables(s)
    qg = jnp.tile(q_gain.astype(F32), 2)[None, :]
    kg = jnp.tile(k_gain.astype(F32), 2)[None, :]
    head_out = lambda n: pl.BlockSpec((1, n, ts, GQA_HEAD_DIM), lambda i, t: (i, 0, t, 0))
    shape = lambda n: jax.ShapeDtypeStruct((b, n, s, GQA_HEAD_DIM), BF16)
    vt_out = pl.BlockSpec((1, GQA_KV_HEADS, GQA_HEAD_DIM, ts), lambda i, t: (i, 0, 0, t))
    vt_shape = jax.ShapeDtypeStruct((b, GQA_KV_HEADS, GQA_HEAD_DIM, s), BF16)
    return pl.pallas_call(
        _rope_prep_kernel, grid=(b, s // ts),
        in_specs=[
            pl.BlockSpec((1, ts, width), lambda i, t: (i, t, 0)),
            pl.BlockSpec((1, 128), lambda i, t: (0, 0)),
            pl.BlockSpec((1, 128), lambda i, t: (0, 0)),
            pl.BlockSpec((ts, 128), lambda i, t: (t, 0)),
            pl.BlockSpec((ts, 128), lambda i, t: (t, 0)),
        ],
        out_specs=[head_out(GQA_HEADS), head_out(GQA_KV_HEADS), vt_out],
        out_shape=[shape(GQA_HEADS), shape(GQA_KV_HEADS), vt_shape],
        compiler_params=_params(("parallel", "parallel")), name="gqa_norm_rope",
    )(proj3d, qg, kg, cos, sin)


GQA_KV_CHUNK = 256


def _gqa_kernel(q_ref, k_ref, vt_ref, o_ref, s_a, s_b, m_a, m_b, *, tq, seq):
    t = pl.program_id(0)
    cols = GQA_REP * tq

    @pl.when(t == 0)
    def _():
        s_b[...] = jnp.zeros(s_b.shape, F32)
        m_b[...] = jnp.zeros(m_b.shape, F32)

    def step(s_cur, m_cur, s_prev, m_prev_ref):
        q = q_ref[0].reshape(cols, GQA_HEAD_DIM)
        m_prev = m_prev_ref[...]
        m_run = None
        l = jnp.zeros((1, cols), F32)
        acc = jnp.zeros((GQA_HEAD_DIM, cols), F32)
        for i in range(seq // GQA_KV_CHUNK):
            rows = slice(i * GQA_KV_CHUNK, (i + 1) * GQA_KV_CHUNK)
            st = _dot_nt(k_ref[0, 0, rows, :], q)
            s_cur[rows, :] = st
            cm = jnp.max(st, axis=0, keepdims=True)
            m_run = cm if m_run is None else jnp.maximum(m_run, cm)
            p = jnp.exp2(s_prev[rows, :] - m_prev)
            l = l + jnp.sum(p, axis=0, keepdims=True)
            acc = acc + _dot(vt_ref[0, 0, :, rows], p.astype(BF16))
        m_cur[...] = m_run
        ot = acc * (1.0 / l)
        for r in range(GQA_REP):
            o_ref[0, :, r * GQA_HEAD_DIM:(r + 1) * GQA_HEAD_DIM] = ot[:, r * tq:(r + 1) * tq].T.astype(BF16)

    pl.when(t % 2 == 0)(lambda: step(s_a, m_a, s_b, m_b))
    pl.when(t % 2 == 1)(lambda: step(s_b, m_b, s_a, m_a))


def _gqa_attention(q, k, vt, *, tq):
    b, _, s, _ = q.shape
    nq = s // tq
    n_blocks = b * GQA_KV_HEADS * nq
    cols = GQA_REP * tq

    def unravel(u):
        return u // (nq * GQA_KV_HEADS), (u // nq) % GQA_KV_HEADS, u % nq

    def score_block(t):
        return unravel(jnp.minimum(t, n_blocks - 1))

    def finish_block(t):
        return unravel(jnp.maximum(t - 1, 0))

    def q_map(t):
        i, g, j = score_block(t)
        return (i, g, j, 0)

    def k_map(t):
        i, g, _ = score_block(t)
        return (i, g, 0, 0)

    def vt_map(t):
        i, g, _ = finish_block(t)
        return (i, g, 0, 0)

    def o_map(t):
        i, g, j = finish_block(t)
        return (i, j, g)

    kern = functools.partial(_gqa_kernel, tq=tq, seq=s)
    return pl.pallas_call(
        kern, grid=(n_blocks + 1,),
        in_specs=[
            pl.BlockSpec((1, GQA_REP, tq, GQA_HEAD_DIM), q_map),
            pl.BlockSpec((1, 1, s, GQA_HEAD_DIM), k_map),
            pl.BlockSpec((1, 1, GQA_HEAD_DIM, s), vt_map),
        ],
        out_specs=pl.BlockSpec((1, tq, GQA_REP * GQA_HEAD_DIM), o_map),
        out_shape=jax.ShapeDtypeStruct((b, s, GQA_Q_WIDTH), BF16),
        scratch_shapes=[pltpu.VMEM((s, cols), F32), pltpu.VMEM((s, cols), F32),
                        pltpu.VMEM((1, cols), F32), pltpu.VMEM((1, cols), F32)],
        compiler_params=_params(("arbitrary",)), name="gqa_attention",
    )(q, k, vt)


def _even_layer(x2d, b, s, mix_norm, w_in, q_gain, k_gain, rpb, conv_w, conv_b, dt_bias, a_log, d_skip,
                out_gain, w_out):
    w_main = w_in[:, :EVEN_MAIN_WIDTH].astype(BF16)
    w_dt = jnp.transpose(w_in[:, EVEN_MAIN_WIDTH:].reshape(-1, 2, SSD_GROUPS, SSD_HEADS_PER_GROUP),
                         (2, 1, 3, 0)).reshape(2 * SSD_HEADS, -1).astype(BF16)
    proj, dt_rows = _norm_proj(x2d, mix_norm.astype(F32)[None, :], w_main, tm=1024, tn=1152, w_small=w_dt,
                               name="even_in_proj")
    proj3d = proj.reshape(b, s, EVEN_MAIN_WIDTH)
    na_out = _neighbourhood_attention(proj3d, q_gain, k_gain, rpb)
    ssd_out = _ssd_mixer(proj3d, dt_rows, conv_w, conv_b, dt_bias, a_log, d_skip, out_gain)
    w_out_bf = w_out.astype(BF16)
    return _out_proj([na_out.reshape(b * s, NA_WIDTH), ssd_out.reshape(b * s, SSD_D_INNER)],
                     [w_out_bf[:NA_WIDTH], w_out_bf[NA_WIDTH:]], x2d, tm=512, name="even_out_proj")


def _odd_layer(x2d, b, s, mix_norm, w_qkv, q_gain, k_gain, w_out):
    proj = _norm_proj(x2d, mix_norm.astype(F32)[None, :], w_qkv.astype(BF16), tm=1024, tn=w_qkv.shape[1],
                      name="odd_qkv_proj")
    q, k, vt = _rope_prep(proj.reshape(b, s, -1), q_gain, k_gain, ts=512)
    attn = _gqa_attention(q, k, vt, tq=256)
    return _out_proj([attn.reshape(b * s, GQA_Q_WIDTH)], [w_out.astype(BF16)], x2d, tm=512, name="odd_out_proj")


def kernel(x, even_mix_norm, even_w_in, na_q_norm, na_k_norm, na_rel_bias, ssd_conv_w, ssd_conv_b, ssd_dt_bias, ssd_A_log, ssd_D, ssd_out_norm, even_w_out, odd_mix_norm, odd_w_qkv, gqa_q_norm, gqa_k_norm, odd_w_out, ffn_norm, ffn_w13, ffn_w2):
    b, s, d = x.shape
    depth = ffn_norm.shape[0]
    h = x.reshape(b * s, d)
    for layer in range(depth):
        i = layer // 2
        if layer % 2 == 0:
            h = _even_layer(h, b, s, even_mix_norm[i], even_w_in[i], na_q_norm[i], na_k_norm[i], na_rel_bias[i],
                            ssd_conv_w[i], ssd_conv_b[i], ssd_dt_bias[i], ssd_A_log[i], ssd_D[i], ssd_out_norm[i],
                            even_w_out[i])
        else:
            h = _odd_layer(h, b, s, odd_mix_norm[i], odd_w_qkv[i], gqa_q_norm[i], gqa_k_norm[i], odd_w_out[i])
        h = _ffn(h, ffn_norm[layer].astype(F32)[None, :], ffn_w13[layer].astype(BF16), ffn_w2[layer].astype(BF16),
                 tm=512)
    return h.reshape(b, s, d)
```

```python
import functools

import jax
import jax.numpy as jnp
from jax import lax
from jax.experimental import pallas as pl
from jax.experimental.pallas import tpu as pltpu

F32 = jnp.float32
BF16 = jnp.bfloat16

D_MODEL = 1024
GRID_W = 64
NORM_EPS = 1e-6

NA_HEADS = 8
NA_HEAD_DIM = 64
NA_WIDTH = NA_HEADS * NA_HEAD_DIM
NA_KH = 8
NA_KW = 16

SSD_D_INNER = 1024
SSD_HEAD_DIM = 64
SSD_HEADS = 16
SSD_GROUPS = 4
SSD_STATE = 128
SSD_CONV = 4
SSD_CHUNK = 128
SSD_CONV_DIM = SSD_D_INNER + 2 * SSD_GROUPS * SSD_STATE
SSD_GROUP_WIDTH = SSD_D_INNER // SSD_GROUPS
SSD_HEADS_PER_GROUP = SSD_HEADS // SSD_GROUPS

EVEN_MAIN_WIDTH = 3 * NA_WIDTH + SSD_D_INNER + SSD_CONV_DIM
EVEN_XBC_OFFSET = 3 * NA_WIDTH + SSD_D_INNER

GQA_HEADS = 16
GQA_KV_HEADS = 4
GQA_HEAD_DIM = 64
GQA_REP = GQA_HEADS // GQA_KV_HEADS
GQA_Q_WIDTH = GQA_HEADS * GQA_HEAD_DIM
GQA_KV_WIDTH = GQA_KV_HEADS * GQA_HEAD_DIM
ROPE_THETA = 10000.0

FFN_HIDDEN = 2816

V7X_LANES = 128
V7X_VMEM_LIMIT = 56 * 1024 * 1024
MASK_VALUE = -1e30
LOG2E = 1.4426950408889634


def _params(dims):
    return pltpu.CompilerParams(dimension_semantics=dims, vmem_limit_bytes=V7X_VMEM_LIMIT)


def _silu(v):
    return v * (1.0 / (1.0 + jnp.exp(-v)))


def _softplus(v):
    return jnp.maximum(v, 0.0) + jnp.log(1.0 + jnp.exp(-jnp.abs(v)))


def _rms_rows(x, g):
    ms = jnp.mean(x * x, axis=-1, keepdims=True)
    return x * lax.rsqrt(ms + NORM_EPS) * g


def _dot(a, b):
    return jnp.dot(a, b, preferred_element_type=F32)


def _dot_nt(a, b):
    return lax.dot_general(a, b, (((1,), (1,)), ((), ())), preferred_element_type=F32)


PROJ_CHUNK = 512


def _norm_proj_kernel(x_ref, g_ref, w_ref, o_ref):
    xn = _rms_rows(x_ref[...], g_ref[...]).astype(BF16)
    for c in range(o_ref.shape[1] // PROJ_CHUNK):
        cols = slice(c * PROJ_CHUNK, (c + 1) * PROJ_CHUNK)
        o_ref[:, cols] = _dot(xn, w_ref[:, cols]).astype(o_ref.dtype)


def _norm_proj2_kernel(x_ref, g_ref, w_ref, w2_ref, o_ref, o2_ref):
    xn = _rms_rows(x_ref[...], g_ref[...]).astype(BF16)
    o2_ref[...] = _dot_nt(w2_ref[...], xn)
    for c in range(o_ref.shape[1] // PROJ_CHUNK):
        cols = slice(c * PROJ_CHUNK, (c + 1) * PROJ_CHUNK)
        o_ref[:, cols] = _dot(xn, w_ref[:, cols]).astype(o_ref.dtype)


def _norm_proj(x2d, g, w, *, tm, w_small=None, name):
    m, d = x2d.shape
    n = w.shape[1]
    assert n % PROJ_CHUNK == 0
    resident = lambda shape: pl.BlockSpec(shape, lambda i: (0, 0), pipeline_mode=pl.Buffered(1))
    in_specs = [pl.BlockSpec((tm, d), lambda i: (i, 0)), pl.BlockSpec((1, d), lambda i: (0, 0)), resident((d, n))]
    out_specs = pl.BlockSpec((tm, n), lambda i: (i, 0))
    out_shape = jax.ShapeDtypeStruct((m, n), BF16)
    if w_small is None:
        return pl.pallas_call(_norm_proj_kernel, grid=(m // tm,), in_specs=in_specs, out_specs=out_specs,
                              out_shape=out_shape, compiler_params=_params(("parallel",)), name=name)(x2d, g, w)
    ns = w_small.shape[0]
    in_specs.append(resident((ns, d)))
    return pl.pallas_call(_norm_proj2_kernel, grid=(m // tm,), in_specs=in_specs,
                          out_specs=[out_specs, pl.BlockSpec((ns, tm), lambda i: (0, i))],
                          out_shape=[out_shape, jax.ShapeDtypeStruct((ns, m), F32)],
                          compiler_params=_params(("parallel",)), name=name)(x2d, g, w, w_small)


NA_PREP_ROWS = 256
NA_GROUP_ROWS = 4
NA_WIN_ROWS = NA_KH + NA_GROUP_ROWS
NA_DY = 2 * NA_KH - 1
NA_DX = 2 * NA_KW - 1


def _na_group_plan(rows):
    sigs, starts, classes = [], [], []
    for gq in range(rows // NA_GROUP_ROWS):
        ks = min(max(gq * NA_GROUP_ROWS - NA_KH // 2, 0), rows - NA_WIN_ROWS)
        sig = tuple((min(max(r - NA_KH // 2, 0), rows - NA_KH) - ks, r - ks)
                    for r in range(gq * NA_GROUP_ROWS, (gq + 1) * NA_GROUP_ROWS))
        assert all(0 <= first and first + NA_KH <= NA_WIN_ROWS for first, _ in sig)
        if sig not in sigs:
            sigs.append(sig)
        starts.append(ks)
        classes.append(sigs.index(sig))
    return sigs, starts, classes


def _pair_head_rms(x, g):
    lo = lax.broadcasted_iota(jnp.int32, x.shape, 1) < NA_HEAD_DIM
    x2 = x * x
    s_lo = jnp.sum(jnp.where(lo, x2, 0.0), axis=-1, keepdims=True)
    s_hi = jnp.sum(jnp.where(lo, 0.0, x2), axis=-1, keepdims=True)
    ms = jnp.where(lo, s_lo, s_hi) * (1.0 / NA_HEAD_DIM)
    return x * lax.rsqrt(ms + NORM_EPS) * g


def _na_kernel(plan_ref, q_ref, k_ref, v_ref, qg_ref, kg_ref, bias_ref, o_ref, q_s, k_s, *, rows):
    scale = NA_HEAD_DIM ** -0.5

    def prep(i, carry):
        sl = pl.ds(pl.multiple_of(i * NA_PREP_ROWS, NA_PREP_ROWS), NA_PREP_ROWS)
        q_s[sl, :] = (_pair_head_rms(q_ref[0, sl, :].astype(F32), qg_ref[...]) * scale).astype(BF16)
        k_s[sl, :] = _pair_head_rms(k_ref[0, sl, :].astype(F32), kg_ref[...]).astype(BF16)
        return carry

    lax.fori_loop(0, (rows * GRID_W) // NA_PREP_ROWS, prep, 0)

    n_q = NA_GROUP_ROWS * GRID_W
    n_keys = NA_WIN_ROWS * GRID_W

    def group_body(gq, carry):
        ks = plan_ref[0, gq]
        cls = plan_ref[1, gq]
        qsl = pl.ds(pl.multiple_of(gq * n_q, n_q), n_q)
        ksl = pl.ds(pl.multiple_of(ks * GRID_W, GRID_W), n_keys)
        q = q_s[qsl, :]
        kk = k_s[ksl, :]
        vv = v_ref[0, ksl, :]
        outs = []
        for h in range(2):
            hs = slice(h * NA_HEAD_DIM, (h + 1) * NA_HEAD_DIM)
            s = _dot_nt(q[:, hs], kk[:, hs]) + bias_ref[0, cls, h]
            m = jnp.max(s, axis=-1, keepdims=True)
            p = jnp.exp(s - m)
            l = jnp.sum(p, axis=-1, keepdims=True)
            outs.append(_dot(p.astype(BF16), vv[:, hs]) * (1.0 / l))
        o_ref[0, qsl, :] = jnp.concatenate(outs, axis=-1).astype(BF16)
        return carry

    lax.fori_loop(0, rows // NA_GROUP_ROWS, group_body, 0, unroll=2)


def _na_bias_kernel(rpb_ref, o_ref, t_s, *, sigs):
    h = pl.program_id(0)
    q = lax.broadcasted_iota(jnp.int32, (GRID_W, GRID_W), 0)
    k = lax.broadcasted_iota(jnp.int32, (GRID_W, GRID_W), 1)
    dx = jnp.clip(k - q, -(NA_KW - 1), NA_KW - 1) + (NA_KW - 1)
    col_start = jnp.clip(q - NA_KW // 2, 0, GRID_W - NA_KW)
    in_win = (k >= col_start) & (k < col_start + NA_KW)
    masked = jnp.full((GRID_W, GRID_W), MASK_VALUE, F32)
    for dy in range(NA_DY):
        base = (h * NA_DY + dy) * NA_DX
        t = masked
        for d in range(NA_DX):
            t = jnp.where(dx == d, rpb_ref[base + d], t)
        t_s[dy] = jnp.where(in_win, t, MASK_VALUE)
    for cls, sig in enumerate(sigs):
        for rq, (first, qrow) in enumerate(sig):
            for jk in range(NA_WIN_ROWS):
                attended = first <= jk < first + NA_KH
                tile = t_s[jk - qrow + NA_KH - 1] if attended else masked
                o_ref[0, cls, 0, rq * GRID_W:(rq + 1) * GRID_W, jk * GRID_W:(jk + 1) * GRID_W] = tile


def _na_bias_table(rpb, sigs):
    n_q = NA_GROUP_ROWS * GRID_W
    n_keys = NA_WIN_ROWS * GRID_W
    kern = functools.partial(_na_bias_kernel, sigs=sigs)
    return pl.pallas_call(
        kern, grid=(NA_HEADS,),
        in_specs=[pl.BlockSpec(memory_space=pltpu.SMEM)],
        out_specs=pl.BlockSpec((1, len(sigs), 1, n_q, n_keys), lambda h: (h // 2, 0, h % 2, 0, 0)),
        out_shape=jax.ShapeDtypeStruct((NA_HEADS // 2, len(sigs), 2, n_q, n_keys), F32),
        scratch_shapes=[pltpu.VMEM((NA_DY, GRID_W, GRID_W), F32)],
        compiler_params=_params(("parallel",)), name="na_bias_table",
    )(rpb.astype(F32).reshape(-1))


def _neighbourhood_attention(proj3d, q_gain, k_gain, rpb):
    b, s, _ = proj3d.shape
    rows = s // GRID_W
    assert rows >= NA_WIN_ROWS and rows % NA_GROUP_ROWS == 0
    sigs, starts, classes = _na_group_plan(rows)
    bias = _na_bias_table(rpb, sigs)
    plan = jnp.array([starts, classes], jnp.int32)
    qg = jnp.tile(q_gain.astype(F32), 2)[None, :]
    kg = jnp.tile(k_gain.astype(F32), 2)[None, :]
    n_pairs = NA_HEADS // 2
    blk = (1, s, 2 * NA_HEAD_DIM)
    kern = functools.partial(_na_kernel, rows=rows)
    return pl.pallas_call(
        kern, grid=(n_pairs, b),
        in_specs=[
            pl.BlockSpec(memory_space=pltpu.SMEM),
            pl.BlockSpec(blk, lambda p, i: (i, 0, p)),
            pl.BlockSpec(blk, lambda p, i: (i, 0, n_pairs + p)),
            pl.BlockSpec(blk, lambda p, i: (i, 0, 2 * n_pairs + p)),
            pl.BlockSpec((1, 2 * NA_HEAD_DIM), lambda p, i: (0, 0)),
            pl.BlockSpec((1, 2 * NA_HEAD_DIM), lambda p, i: (0, 0)),
            pl.BlockSpec((1,) + bias.shape[1:], lambda p, i: (p, 0, 0, 0, 0)),
        ],
        out_specs=pl.BlockSpec(blk, lambda p, i: (i, 0, p)),
        out_shape=jax.ShapeDtypeStruct((b, s, NA_WIDTH), BF16),
        scratch_shapes=[pltpu.VMEM((s, 2 * NA_HEAD_DIM), BF16)] * 2,
        compiler_params=_params(("parallel", "parallel")), name="neighbourhood_attention",
    )(plan, proj3d, proj3d, proj3d, qg, kg, bias)


SSD_CONV_ROWS = 256
SSD_HALO = 16


def _chunk_scan(a, reverse):
    n = a.shape[1]
    pos = lax.broadcasted_iota(jnp.int32, a.shape, 1) % SSD_CHUNK
    sh = 1
    while sh < SSD_CHUNK:
        if reverse:
            a = a + jnp.where(pos < SSD_CHUNK - sh, pltpu.roll(a, n - sh, axis=1), 0.0)
        else:
            a = a + jnp.where(pos >= sh, pltpu.roll(a, sh, axis=1), 0.0)
        sh *= 2
    return a


def _conv_silu_into(raw_ref, w_ref, b_ref, dst_ref, seq, transpose_out):
    rows = SSD_CONV_ROWS
    n_blocks = seq // rows
    n = rows + 2 * SSD_HALO
    left = SSD_CONV // 2
    width = raw_ref.shape[-1]

    def body(i, carry):
        r0 = pl.multiple_of(i * rows, rows)
        p0 = pl.multiple_of(jnp.maximum(r0 - SSD_HALO, 0), SSD_HALO)
        n0 = pl.multiple_of(jnp.minimum(r0 + rows, seq - SSD_HALO), SSD_HALO)
        prev = jnp.where(i > 0, raw_ref[0, pl.ds(p0, SSD_HALO), :].astype(F32), 0.0)
        nxt = jnp.where(i < n_blocks - 1, raw_ref[0, pl.ds(n0, SSD_HALO), :].astype(F32), 0.0)
        win = jnp.concatenate([prev, raw_ref[0, pl.ds(r0, rows), :].astype(F32), nxt], axis=0)
        acc = jnp.broadcast_to(b_ref[...], (rows, width))
        for k in range(SSD_CONV):
            shift = (left - k) % n
            tap = win if shift == 0 else pltpu.roll(win, shift, axis=0)
            acc = acc + tap[SSD_HALO:SSD_HALO + rows] * w_ref[k:k + 1, :]
        out = _silu(acc)
        if transpose_out:
            dst_ref[:, pl.ds(r0, rows)] = out.T.astype(dst_ref.dtype)
        else:
            dst_ref[pl.ds(r0, rows), :] = out.astype(dst_ref.dtype)
        return carry

    lax.fori_loop(0, n_blocks, body, 0)


def _head_row(tile, first):
    lo = lax.broadcasted_iota(jnp.int32, (1, V7X_LANES), 1) < SSD_HEAD_DIM
    halves = [jnp.where(lo, tile[first + 2 * i:first + 2 * i + 1, :], tile[first + 2 * i + 1:first + 2 * i + 2, :])
              for i in range(SSD_HEADS_PER_GROUP // 2)]
    return jnp.concatenate(halves, axis=1)


def _ssd_kernel(xs_ref, b_ref, c_ref, z_ref, wx_ref, wb_ref, wc_ref, cbx_ref, cbb_ref, cbc_ref,
                dtr_ref, biasr_ref, alogr_ref, dskip_ref, gain_ref, o_ref,
                xs_s, bt_s, c_s, cumr_s, g2r_s, ld2r_s, wr_s, decr_s, st_s, h_s, *, n_chunks):
    L = SSD_CHUNK
    hpg = SSD_HEADS_PER_GROUP
    gw = SSD_GROUP_WIDTH
    ns = SSD_STATE
    seq = n_chunks * L

    _conv_silu_into(xs_ref, wx_ref, cbx_ref, xs_s, seq, False)
    _conv_silu_into(b_ref, wb_ref, cbb_ref, bt_s, seq, True)
    _conv_silu_into(c_ref, wc_ref, cbc_ref, c_s, seq, False)

    dt = _softplus(dtr_ref[...] + biasr_ref[0])
    a = dt * (-jnp.exp(alogr_ref[0]))
    is_fwd = lax.broadcasted_iota(jnp.int32, a.shape, 0) < hpg
    prefix = _chunk_scan(a, False)
    suffix = _chunk_scan(a, True)
    cum2 = jnp.where(is_fwd, prefix, suffix) * LOG2E
    cumr_s[...] = cum2
    g2r_s[...] = cum2 - jnp.log2(dt)
    ld2r_s[...] = jnp.log2(dt + pltpu.roll(dt, hpg, axis=0))
    wr_s[...] = dt * jnp.exp(jnp.where(is_fwd, suffix, prefix) - a)
    decr_s[...] = jnp.exp(prefix + suffix - a)

    def chunk_slice(c):
        return pl.ds(pl.multiple_of(c * L, L), L)

    lane_head = lax.broadcasted_iota(jnp.int32, (L, gw), 1) // SSD_HEAD_DIM

    def block_diag_x(xs):
        xs_bf = xs.astype(BF16)
        return jnp.concatenate([jnp.where(lane_head == j, xs_bf, jnp.zeros_like(xs_bf)) for j in range(hpg)], axis=0)

    def state_body(c, carry):
        sl = chunk_slice(c)
        bt = bt_s[:, sl].astype(F32)
        w = wr_s[:, sl]
        lhs = jnp.concatenate(
            [jnp.concatenate([(bt * w[d * hpg + j:d * hpg + j + 1, :]).astype(BF16) for j in range(hpg)], axis=1)
             for d in range(2)], axis=0)
        st_s[c] = _dot(lhs, block_diag_x(xs_s[sl, :]))
        return carry

    lax.fori_loop(0, n_chunks, state_body, 0, unroll=2)

    def fwd_rec(c, h):
        h_s[c, :, 0:gw] = h.astype(BF16)
        return h * _head_row(decr_s[:, chunk_slice(c)], 0) + st_s[c, 0:ns, :]

    def bwd_rec(i, h):
        c = n_chunks - 1 - i
        h_s[c, :, gw:2 * gw] = h.astype(BF16)
        return h * _head_row(decr_s[:, chunk_slice(c)], hpg) + st_s[c, ns:2 * ns, :]

    h0 = jnp.zeros((ns, gw), F32)
    lax.fori_loop(0, n_chunks, fwd_rec, h0)
    lax.fori_loop(0, n_chunks, bwd_rec, h0)

    li = lax.broadcasted_iota(jnp.int32, (L, L), 0)
    si = lax.broadcasted_iota(jnp.int32, (L, L), 1)
    below = si < li
    above = si > li
    lane_lo = lax.broadcasted_iota(jnp.int32, (L, V7X_LANES), 1) < SSD_HEAD_DIM

    def out_body(c, carry):
        sl = chunk_slice(c)
        xs = xs_s[sl, :]
        cm = c_s[sl, :]
        cum_r = cumr_s[:, sl]
        g2 = g2r_s[:, sl]
        ld2 = ld2r_s[:, sl]
        cum_t = [jnp.broadcast_to(cum_r[k:k + 1, :], (L, L)).T for k in range(2 * hpg)]
        cb = _dot(cm, bt_s[:, sl])
        mats = []
        for j in range(hpg):
            seg_f = cum_t[j] - g2[j:j + 1, :]
            seg_b = cum_t[hpg + j] - g2[hpg + j:hpg + j + 1, :]
            arg = jnp.where(below, seg_f, jnp.where(above, seg_b, ld2[j:j + 1, :]))
            mats.append((cb * jnp.exp2(arg)).astype(BF16))
        y = _dot(jnp.concatenate(mats, axis=1), block_diag_x(xs))
        carried = _dot(cm, h_s[c])
        for d in range(2):
            decay = jnp.exp2(jnp.concatenate(
                [jnp.where(lane_lo, cum_t[d * hpg + 2 * i], cum_t[d * hpg + 2 * i + 1]) for i in range(hpg // 2)],
                axis=1))
            y = y + carried[:, d * gw:(d + 1) * gw] * decay
        y = y + dskip_ref[0] * xs
        gated = y * _silu(z_ref[0, sl, :].astype(F32))
        ms = jnp.mean(gated * gated, axis=-1, keepdims=True)
        o_ref[0, sl, :] = (gated * lax.rsqrt(ms + NORM_EPS) * gain_ref[0]).astype(BF16)
        return carry

    lax.fori_loop(0, n_chunks, out_body, 0, unroll=2)


def _group_major(v):
    return jnp.transpose(v.astype(F32).reshape(2, SSD_GROUPS, SSD_HEADS_PER_GROUP), (1, 0, 2)).reshape(
        SSD_GROUPS, 2 * SSD_HEADS_PER_GROUP)


def _ssd_mixer(proj3d, dt_rows, conv_w, conv_b, dt_bias, a_log, d_skip, out_gain):
    b, s, _ = proj3d.shape
    n_chunks = s // SSD_CHUNK
    g, hpg, gw, ns = SSD_GROUPS, SSD_HEADS_PER_GROUP, SSD_GROUP_WIDTH, SSD_STATE
    bias_g = _group_major(dt_bias)[:, :, None]
    alog_g = _group_major(a_log)[:, :, None]
    dskip = jnp.repeat(d_skip.astype(F32), SSD_HEAD_DIM).reshape(g, 1, gw)
    gain = out_gain.astype(F32).reshape(g, 1, gw)
    conv_w = conv_w.astype(F32)
    conv_b = conv_b.astype(F32)[None, :]
    z_blk = (3 * NA_WIDTH) // gw
    x_blk = EVEN_XBC_OFFSET // gw
    b_blk = (EVEN_XBC_OFFSET + SSD_D_INNER) // ns
    c_blk = b_blk + g
    wb_blk = SSD_D_INNER // ns
    wc_blk = wb_blk + g
    kern = functools.partial(_ssd_kernel, n_chunks=n_chunks)
    small = lambda shape: pl.BlockSpec((1,) + shape, lambda i, k: (k, 0, 0))
    seq_blk = lambda width, blk0: pl.BlockSpec((1, s, width), lambda i, k: (i, 0, blk0 + k))
    par_blk = lambda rows, width, blk0: pl.BlockSpec((rows, width), lambda i, k: (0, blk0 + k))
    row_scratch = pltpu.VMEM((2 * hpg, s), F32)
    return pl.pallas_call(
        kern, grid=(b, g),
        in_specs=[
            seq_blk(gw, x_blk), seq_blk(ns, b_blk), seq_blk(ns, c_blk), seq_blk(gw, z_blk),
            par_blk(SSD_CONV, gw, 0), par_blk(SSD_CONV, ns, wb_blk), par_blk(SSD_CONV, ns, wc_blk),
            par_blk(1, gw, 0), par_blk(1, ns, wb_blk), par_blk(1, ns, wc_blk),
            pl.BlockSpec((2 * hpg, s), lambda i, k: (k, i)),
            small((2 * hpg, 1)), small((2 * hpg, 1)), small((1, gw)), small((1, gw)),
        ],
        out_specs=pl.BlockSpec((1, s, gw), lambda i, k: (i, 0, k)),
        out_shape=jax.ShapeDtypeStruct((b, s, SSD_D_INNER), BF16),
        scratch_shapes=[
            pltpu.VMEM((s, gw), F32), pltpu.VMEM((ns, s), BF16), pltpu.VMEM((s, ns), BF16),
            row_scratch, row_scratch, row_scratch, row_scratch, row_scratch,
            pltpu.VMEM((n_chunks, 2 * ns, gw), F32), pltpu.VMEM((n_chunks, ns, 2 * gw), BF16),
        ],
        compiler_params=_params(("parallel", "parallel")), name="ssd_bidirectional",
    )(proj3d, proj3d, proj3d, proj3d, conv_w, conv_w, conv_w, conv_b, conv_b, conv_b, dt_rows,
      bias_g, alog_g, dskip, gain)


FFN_CHUNK = 256


def _mix_ffn_kernel(*refs, n_acts, hidden):
    act_refs = refs[:n_acts]
    wout_refs = refs[n_acts:2 * n_acts]
    x_ref, g_ref, w13_ref, w2_ref, o_ref, hid_ref = refs[2 * n_acts:]
    h = x_ref[...]
    for a_ref, w_ref in zip(act_refs, wout_refs):
        h = h + _dot(a_ref[...], w_ref[...])
    hn = _rms_rows(h, g_ref[...]).astype(BF16)
    for c in range(hidden // FFN_CHUNK):
        gate = slice(c * FFN_CHUNK, (c + 1) * FFN_CHUNK)
        up = slice(hidden + c * FFN_CHUNK, hidden + (c + 1) * FFN_CHUNK)
        hid_ref[:, gate] = (_silu(_dot(hn, w13_ref[:, gate])) * _dot(hn, w13_ref[:, up])).astype(BF16)
    o_ref[...] = h + _dot(hid_ref[...], w2_ref[...])


def _mix_ffn(acts, w_outs, x2d, g, w13, w2, *, tm, name):
    m, d = x2d.shape
    hid = w2.shape[0]
    assert hid % FFN_CHUNK == 0
    row = lambda width: pl.BlockSpec((tm, width), lambda i: (i, 0))
    resident = lambda shape: pl.BlockSpec(shape, lambda i: (0, 0), pipeline_mode=pl.Buffered(1))
    return pl.pallas_call(
        functools.partial(_mix_ffn_kernel, n_acts=len(acts), hidden=hid), grid=(m // tm,),
        in_specs=([row(a.shape[1]) for a in acts] + [resident(w.shape) for w in w_outs]
                  + [row(d), pl.BlockSpec((1, d), lambda i: (0, 0)), resident((d, 2 * hid)), resident((hid, d))]),
        out_specs=row(d),
        out_shape=jax.ShapeDtypeStruct((m, d), F32),
        scratch_shapes=[pltpu.VMEM((tm, hid), BF16)],
        compiler_params=_params(("parallel",)), name=name,
    )(*acts, *w_outs, x2d, g, w13, w2)


def _rope_prep_kernel(p_ref, qg_ref, kg_ref, cos_ref, sin_ref, q_ref, k_ref, v_ref):
    cos = cos_ref[...]
    sin = sin_ref[...]
    even = (lax.broadcasted_iota(jnp.int32, cos.shape, 1) % 2) == 0
    scale = GQA_HEAD_DIM ** -0.5 * LOG2E

    def norm_rope(xb, gain):
        xn = _pair_head_rms(xb, gain)
        swapped = jnp.where(even, pltpu.roll(xn, V7X_LANES - 1, axis=1), pltpu.roll(xn, 1, axis=1))
        return xn * cos + swapped * sin

    for pair in range(GQA_HEADS // 2):
        blk = norm_rope(p_ref[0, :, pair * 128:(pair + 1) * 128].astype(F32), qg_ref[...]) * scale
        q_ref[0, 2 * pair] = blk[:, :GQA_HEAD_DIM].astype(BF16)
        q_ref[0, 2 * pair + 1] = blk[:, GQA_HEAD_DIM:].astype(BF16)
    for pair in range(GQA_KV_HEADS // 2):
        c0 = GQA_Q_WIDTH + pair * 128
        blk = norm_rope(p_ref[0, :, c0:c0 + 128].astype(F32), kg_ref[...])
        k_ref[0, 2 * pair] = blk[:, :GQA_HEAD_DIM].astype(BF16)
        k_ref[0, 2 * pair + 1] = blk[:, GQA_HEAD_DIM:].astype(BF16)
        c1 = GQA_Q_WIDTH + GQA_KV_WIDTH + pair * 128
        vt = p_ref[0, :, c1:c1 + 128].astype(F32).T.astype(BF16)
        v_ref[0, 2 * pair] = vt[:GQA_HEAD_DIM]
        v_ref[0, 2 * pair + 1] = vt[GQA_HEAD_DIM:]


def _axial_rope_tables(s):
    t = jnp.arange(s)
    row = (t // GRID_W).astype(F32)
    col = (t % GRID_W).astype(F32)
    axis_dims = GQA_HEAD_DIM // 2
    freqs = ROPE_THETA ** (-jnp.arange(0, axis_dims, 2, dtype=F32) / axis_dims)
    ang = jnp.concatenate([row[:, None] * freqs, col[:, None] * freqs], axis=-1)
    cos = jnp.repeat(jnp.cos(ang), 2, axis=-1)
    sin = jnp.stack([-jnp.sin(ang), jnp.sin(ang)], axis=-1).reshape(s, GQA_HEAD_DIM)
    return jnp.tile(cos, (1, 2)), jnp.tile(sin, (1, 2))


def _rope_prep(proj3d, q_gain, k_gain, *, ts):
    b, s, width = proj3d.shape
    cos, sin = _axial_rope_tables(s)
    qg = jnp.tile(q_gain.astype(F32), 2)[None, :]
    kg = jnp.tile(k_gain.astype(F32), 2)[None, :]
    head_out = lambda n: pl.BlockSpec((1, n, ts, GQA_HEAD_DIM), lambda i, t: (i, 0, t, 0))
    shape = lambda n: jax.ShapeDtypeStruct((b, n, s, GQA_HEAD_DIM), BF16)
    vt_out = pl.BlockSpec((1, GQA_KV_HEADS, GQA_HEAD_DIM, ts), lambda i, t: (i, 0, 0, t))
    vt_shape = jax.ShapeDtypeStruct((b, GQA_KV_HEADS, GQA_HEAD_DIM, s), BF16)
    return pl.pallas_call(
        _rope_prep_kernel, grid=(b, s // ts),
        in_specs=[
            pl.BlockSpec((1, ts, width), lambda i, t: (i, t, 0)),
            pl.BlockSpec((1, 128), lambda i, t: (0, 0)),
            pl.BlockSpec((1, 128), lambda i, t: (0, 0)),
            pl.BlockSpec((ts, 128), lambda i, t: (t, 0)),
            pl.BlockSpec((ts, 128), lambda i, t: (t, 0)),
        ],
        out_specs=[head_out(GQA_HEADS), head_out(GQA_KV_HEADS), vt_out],
        out_shape=[shape(GQA_HEADS), shape(GQA_KV_HEADS), vt_shape],
        compiler_params=_params(("parallel", "parallel")), name="gqa_norm_rope",
    )(proj3d, qg, kg, cos, sin)


GQA_KV_CHUNK = 256


def _gqa_kernel(q_ref, k_ref, vt_ref, o_ref, s_a, s_b, m_a, m_b, *, tq, seq):
    t = pl.program_id(0)
    cols = GQA_REP * tq

    @pl.when(t == 0)
    def _():
        s_b[...] = jnp.zeros(s_b.shape, F32)
        m_b[...] = jnp.zeros(m_b.shape, F32)

    def step(s_cur, m_cur, s_prev, m_prev_ref):
        q = q_ref[0].reshape(cols, GQA_HEAD_DIM)
        m_prev = m_prev_ref[...]
        m_run = None
        l = jnp.zeros((1, cols), F32)
        acc = jnp.zeros((GQA_HEAD_DIM, cols), F32)
        for i in range(seq // GQA_KV_CHUNK):
            rows = slice(i * GQA_KV_CHUNK, (i + 1) * GQA_KV_CHUNK)
            st = _dot_nt(k_ref[0, 0, rows, :], q)
            s_cur[rows, :] = st
            cm = jnp.max(st, axis=0, keepdims=True)
            m_run = cm if m_run is None else jnp.maximum(m_run, cm)
            p = jnp.exp2(s_prev[rows, :] - m_prev)
            l = l + jnp.sum(p, axis=0, keepdims=True)
            acc = acc + _dot(vt_ref[0, 0, :, rows], p.astype(BF16))
        m_cur[...] = m_run
        ot = acc * (1.0 / l)
        for r in range(GQA_REP):
            o_ref[0, :, r * GQA_HEAD_DIM:(r + 1) * GQA_HEAD_DIM] = ot[:, r * tq:(r + 1) * tq].T.astype(BF16)

    pl.when(t % 2 == 0)(lambda: step(s_a, m_a, s_b, m_b))
    pl.when(t % 2 == 1)(lambda: step(s_b, m_b, s_a, m_a))


def _gqa_attention(q, k, vt, *, tq):
    b, _, s, _ = q.shape
    nq = s // tq
    n_blocks = b * GQA_KV_HEADS * nq
    cols = GQA_REP * tq

    def unravel(u):
        return u // (nq * GQA_KV_HEADS), (u // nq) % GQA_KV_HEADS, u % nq

    def score_block(t):
        return unravel(jnp.minimum(t, n_blocks - 1))

    def finish_block(t):
        return unravel(jnp.maximum(t - 1, 0))

    def q_map(t):
        i, g, j = score_block(t)
        return (i, g, j, 0)

    def k_map(t):
        i, g, _ = score_block(t)
        return (i, g, 0, 0)

    def vt_map(t):
        i, g, _ = finish_block(t)
        return (i, g, 0, 0)

    def o_map(t):
        i, g, j = finish_block(t)
        return (i, j, g)

    kern = functools.partial(_gqa_kernel, tq=tq, seq=s)
    return pl.pallas_call(
        kern, grid=(n_blocks + 1,),
        in_specs=[
            pl.BlockSpec((1, GQA_REP, tq, GQA_HEAD_DIM), q_map),
            pl.BlockSpec((1, 1, s, GQA_HEAD_DIM), k_map),
            pl.BlockSpec((1, 1, GQA_HEAD_DIM, s), vt_map),
        ],
        out_specs=pl.BlockSpec((1, tq, GQA_REP * GQA_HEAD_DIM), o_map),
        out_shape=jax.ShapeDtypeStruct((b, s, GQA_Q_WIDTH), BF16),
        scratch_shapes=[pltpu.VMEM((s, cols), F32), pltpu.VMEM((s, cols), F32),
                        pltpu.VMEM((1, cols), F32), pltpu.VMEM((1, cols), F32)],
        compiler_params=_params(("arbitrary",)), name="gqa_attention",
    )(q, k, vt)


def _even_mixer(x2d, b, s, mix_norm, w_in, q_gain, k_gain, rpb, conv_w, conv_b, dt_bias, a_log, d_skip, out_gain,
                w_out):
    w_main = w_in[:, :EVEN_MAIN_WIDTH].astype(BF16)
    w_dt = jnp.transpose(w_in[:, EVEN_MAIN_WIDTH:].reshape(-1, 2, SSD_GROUPS, SSD_HEADS_PER_GROUP),
                         (2, 1, 3, 0)).reshape(2 * SSD_HEADS, -1).astype(BF16)
    proj, dt_rows = _norm_proj(x2d, mix_norm.astype(F32)[None, :], w_main, tm=512, w_small=w_dt,
                               name="even_in_proj")
    proj3d = proj.reshape(b, s, EVEN_MAIN_WIDTH)
    na_out = _neighbourhood_attention(proj3d, q_gain, k_gain, rpb)
    ssd_out = _ssd_mixer(proj3d, dt_rows, conv_w, conv_b, dt_bias, a_log, d_skip, out_gain)
    w_out_bf = w_out.astype(BF16)
    return ([na_out.reshape(b * s, NA_WIDTH), ssd_out.reshape(b * s, SSD_D_INNER)],
            [w_out_bf[:NA_WIDTH], w_out_bf[NA_WIDTH:]])


def _odd_mixer(x2d, b, s, mix_norm, w_qkv, q_gain, k_gain, w_out):
    proj = _norm_proj(x2d, mix_norm.astype(F32)[None, :], w_qkv.astype(BF16), tm=512, name="odd_qkv_proj")
    q, k, vt = _rope_prep(proj.reshape(b, s, -1), q_gain, k_gain, ts=512)
    attn = _gqa_attention(q, k, vt, tq=256)
    return [attn.reshape(b * s, GQA_Q_WIDTH)], [w_out.astype(BF16)]


def kernel(x, even_mix_norm, even_w_in, na_q_norm, na_k_norm, na_rel_bias, ssd_conv_w, ssd_conv_b, ssd_dt_bias, ssd_A_log, ssd_D, ssd_out_norm, even_w_out, odd_mix_norm, odd_w_qkv, gqa_q_norm, gqa_k_norm, odd_w_out, ffn_norm, ffn_w13, ffn_w2):
    b, s, d = x.shape
    depth = ffn_norm.shape[0]
    h = x.reshape(b * s, d)
    for layer in range(depth):
        i = layer // 2
        if layer % 2 == 0:
            acts, w_outs = _even_mixer(h, b, s, even_mix_norm[i], even_w_in[i], na_q_norm[i], na_k_norm[i],
                                       na_rel_bias[i], ssd_conv_w[i], ssd_conv_b[i], ssd_dt_bias[i], ssd_A_log[i],
                                       ssd_D[i], ssd_out_norm[i], even_w_out[i])
        else:
            acts, w_outs = _odd_mixer(h, b, s, odd_mix_norm[i], odd_w_qkv[i], gqa_q_norm[i], gqa_k_norm[i],
                                      odd_w_out[i])
        h = _mix_ffn(acts, w_outs, h, ffn_norm[layer].astype(F32)[None, :], ffn_w13[layer].astype(BF16),
                     ffn_w2[layer].astype(BF16), tm=512, name="mix_out_ffn_even" if layer % 2 == 0 else "mix_out_ffn_odd")
    return h.reshape(b, s, d)
```

```python
import functools

import jax
import jax.numpy as jnp
from jax import lax
from jax.experimental import pallas as pl
from jax.experimental.pallas import tpu as pltpu

F32 = jnp.float32
BF16 = jnp.bfloat16

D_MODEL = 1024
GRID_W = 64
NORM_EPS = 1e-6

NA_HEADS = 8
NA_HEAD_DIM = 64
NA_WIDTH = NA_HEADS * NA_HEAD_DIM
NA_KH = 8
NA_KW = 16

SSD_D_INNER = 1024
SSD_HEAD_DIM = 64
SSD_HEADS = 16
SSD_GROUPS = 4
SSD_STATE = 128
SSD_CONV = 4
SSD_CHUNK = 128
SSD_CONV_DIM = SSD_D_INNER + 2 * SSD_GROUPS * SSD_STATE
SSD_GROUP_WIDTH = SSD_D_INNER // SSD_GROUPS
SSD_HEADS_PER_GROUP = SSD_HEADS // SSD_GROUPS

EVEN_MAIN_WIDTH = 3 * NA_WIDTH + SSD_D_INNER + SSD_CONV_DIM
EVEN_XBC_OFFSET = 3 * NA_WIDTH + SSD_D_INNER

GQA_HEADS = 16
GQA_KV_HEADS = 4
GQA_HEAD_DIM = 64
GQA_REP = GQA_HEADS // GQA_KV_HEADS
GQA_Q_WIDTH = GQA_HEADS * GQA_HEAD_DIM
GQA_KV_WIDTH = GQA_KV_HEADS * GQA_HEAD_DIM
ROPE_THETA = 10000.0

FFN_HIDDEN = 2816

V7X_LANES = 128
V7X_VMEM_LIMIT = 56 * 1024 * 1024
MASK_VALUE = -1e30
LOG2E = 1.4426950408889634


def _params(dims):
    return pltpu.CompilerParams(dimension_semantics=dims, vmem_limit_bytes=V7X_VMEM_LIMIT)


def _silu(v):
    return v * (1.0 / (1.0 + jnp.exp(-v)))


def _softplus(v):
    return jnp.maximum(v, 0.0) + jnp.log(1.0 + jnp.exp(-jnp.abs(v)))


def _rms_rows(x, g):
    ms = jnp.mean(x * x, axis=-1, keepdims=True)
    return x * lax.rsqrt(ms + NORM_EPS) * g


def _dot(a, b):
    return jnp.dot(a, b, preferred_element_type=F32)


def _dot_nt(a, b):
    return lax.dot_general(a, b, (((1,), (1,)), ((), ())), preferred_element_type=F32)


PROJ_CHUNK = 512


def _norm_proj_kernel(x_ref, g_ref, w_ref, o_ref):
    xn = _rms_rows(x_ref[...], g_ref[...]).astype(BF16)
    for c in range(o_ref.shape[1] // PROJ_CHUNK):
        cols = slice(c * PROJ_CHUNK, (c + 1) * PROJ_CHUNK)
        o_ref[:, cols] = _dot(xn, w_ref[:, cols]).astype(o_ref.dtype)


def _norm_proj2_kernel(x_ref, g_ref, w_ref, w2_ref, o_ref, o2_ref):
    xn = _rms_rows(x_ref[...], g_ref[...]).astype(BF16)
    o2_ref[...] = _dot_nt(w2_ref[...], xn)
    for c in range(o_ref.shape[1] // PROJ_CHUNK):
        cols = slice(c * PROJ_CHUNK, (c + 1) * PROJ_CHUNK)
        o_ref[:, cols] = _dot(xn, w_ref[:, cols]).astype(o_ref.dtype)


def _norm_proj(x2d, g, w, *, tm, w_small=None, name):
    m, d = x2d.shape
    n = w.shape[1]
    assert n % PROJ_CHUNK == 0
    resident = lambda shape: pl.BlockSpec(shape, lambda i: (0, 0), pipeline_mode=pl.Buffered(1))
    in_specs = [pl.BlockSpec((tm, d), lambda i: (i, 0)), pl.BlockSpec((1, d), lambda i: (0, 0)), resident((d, n))]
    out_specs = pl.BlockSpec((tm, n), lambda i: (i, 0))
    out_shape = jax.ShapeDtypeStruct((m, n), BF16)
    if w_small is None:
        return pl.pallas_call(_norm_proj_kernel, grid=(m // tm,), in_specs=in_specs, out_specs=out_specs,
                              out_shape=out_shape, compiler_params=_params(("parallel",)), name=name)(x2d, g, w)
    ns = w_small.shape[0]
    in_specs.append(resident((ns, d)))
    return pl.pallas_call(_norm_proj2_kernel, grid=(m // tm,), in_specs=in_specs,
                          out_specs=[out_specs, pl.BlockSpec((ns, tm), lambda i: (0, i))],
                          out_shape=[out_shape, jax.ShapeDtypeStruct((ns, m), F32)],
                          compiler_params=_params(("parallel",)), name=name)(x2d, g, w, w_small)


NA_PREP_ROWS = 256
NA_GROUP_ROWS = 4
NA_WIN_ROWS = NA_KH + NA_GROUP_ROWS
NA_DY = 2 * NA_KH - 1
NA_DX = 2 * NA_KW - 1


def _na_group_plan(rows):
    sigs, starts, classes = [], [], []
    for gq in range(rows // NA_GROUP_ROWS):
        ks = min(max(gq * NA_GROUP_ROWS - NA_KH // 2, 0), rows - NA_WIN_ROWS)
        sig = tuple((min(max(r - NA_KH // 2, 0), rows - NA_KH) - ks, r - ks)
                    for r in range(gq * NA_GROUP_ROWS, (gq + 1) * NA_GROUP_ROWS))
        assert all(0 <= first and first + NA_KH <= NA_WIN_ROWS for first, _ in sig)
        if sig not in sigs:
            sigs.append(sig)
        starts.append(ks)
        classes.append(sigs.index(sig))
    return sigs, starts, classes


def _pair_head_rms(x, g):
    lo = lax.broadcasted_iota(jnp.int32, x.shape, 1) < NA_HEAD_DIM
    x2 = x * x
    s_lo = jnp.sum(jnp.where(lo, x2, 0.0), axis=-1, keepdims=True)
    s_hi = jnp.sum(jnp.where(lo, 0.0, x2), axis=-1, keepdims=True)
    ms = jnp.where(lo, s_lo, s_hi) * (1.0 / NA_HEAD_DIM)
    return x * lax.rsqrt(ms + NORM_EPS) * g


def _na_kernel(plan_ref, q_ref, k_ref, v_ref, qg_ref, kg_ref, bias_ref, o_ref, q_s, k_s, *, rows):
    scale = NA_HEAD_DIM ** -0.5

    def prep(i, carry):
        sl = pl.ds(pl.multiple_of(i * NA_PREP_ROWS, NA_PREP_ROWS), NA_PREP_ROWS)
        q_s[sl, :] = (_pair_head_rms(q_ref[0, sl, :].astype(F32), qg_ref[...]) * scale).astype(BF16)
        k_s[sl, :] = _pair_head_rms(k_ref[0, sl, :].astype(F32), kg_ref[...]).astype(BF16)
        return carry

    lax.fori_loop(0, (rows * GRID_W) // NA_PREP_ROWS, prep, 0)

    n_q = NA_GROUP_ROWS * GRID_W
    n_keys = NA_WIN_ROWS * GRID_W

    def group_body(gq, carry):
        ks = plan_ref[0, gq]
        cls = plan_ref[1, gq]
        qsl = pl.ds(pl.multiple_of(gq * n_q, n_q), n_q)
        ksl = pl.ds(pl.multiple_of(ks * GRID_W, GRID_W), n_keys)
        q = q_s[qsl, :]
        kk = k_s[ksl, :]
        vv = v_ref[0, ksl, :]
        outs = []
        for h in range(2):
            hs = slice(h * NA_HEAD_DIM, (h + 1) * NA_HEAD_DIM)
            s = _dot_nt(q[:, hs], kk[:, hs]) + bias_ref[0, cls, h]
            m = jnp.max(s, axis=-1, keepdims=True)
            p = jnp.exp(s - m)
            l = jnp.sum(p, axis=-1, keepdims=True)
            outs.append(_dot(p.astype(BF16), vv[:, hs]) * (1.0 / l))
        o_ref[0, qsl, :] = jnp.concatenate(outs, axis=-1).astype(BF16)
        return carry

    lax.fori_loop(0, rows // NA_GROUP_ROWS, group_body, 0, unroll=4)


def _na_bias_kernel(rpb_ref, o_ref, t_s, *, sigs):
    h = pl.program_id(0)
    q = lax.broadcasted_iota(jnp.int32, (GRID_W, GRID_W), 0)
    k = lax.broadcasted_iota(jnp.int32, (GRID_W, GRID_W), 1)
    dx = jnp.clip(k - q, -(NA_KW - 1), NA_KW - 1) + (NA_KW - 1)
    col_start = jnp.clip(q - NA_KW // 2, 0, GRID_W - NA_KW)
    in_win = (k >= col_start) & (k < col_start + NA_KW)
    masked = jnp.full((GRID_W, GRID_W), MASK_VALUE, F32)
    for dy in range(NA_DY):
        base = (h * NA_DY + dy) * NA_DX
        t = masked
        for d in range(NA_DX):
            t = jnp.where(dx == d, rpb_ref[base + d], t)
        t_s[dy] = jnp.where(in_win, t, MASK_VALUE)
    for cls, sig in enumerate(sigs):
        for rq, (first, qrow) in enumerate(sig):
            for jk in range(NA_WIN_ROWS):
                attended = first <= jk < first + NA_KH
                tile = t_s[jk - qrow + NA_KH - 1] if attended else masked
                o_ref[0, cls, 0, rq * GRID_W:(rq + 1) * GRID_W, jk * GRID_W:(jk + 1) * GRID_W] = tile


def _na_bias_table(rpb, sigs):
    n_q = NA_GROUP_ROWS * GRID_W
    n_keys = NA_WIN_ROWS * GRID_W
    kern = functools.partial(_na_bias_kernel, sigs=sigs)
    return pl.pallas_call(
        kern, grid=(NA_HEADS,),
        in_specs=[pl.BlockSpec(memory_space=pltpu.SMEM)],
        out_specs=pl.BlockSpec((1, len(sigs), 1, n_q, n_keys), lambda h: (h // 2, 0, h % 2, 0, 0)),
        out_shape=jax.ShapeDtypeStruct((NA_HEADS // 2, len(sigs), 2, n_q, n_keys), F32),
        scratch_shapes=[pltpu.VMEM((NA_DY, GRID_W, GRID_W), F32)],
        compiler_params=_params(("parallel",)), name="na_bias_table",
    )(rpb.astype(F32).reshape(-1))


def _neighbourhood_attention(proj3d, q_gain, k_gain, rpb):
    b, s, _ = proj3d.shape
    rows = s // GRID_W
    assert rows >= NA_WIN_ROWS and rows % NA_GROUP_ROWS == 0
    sigs, starts, classes = _na_group_plan(rows)
    bias = _na_bias_table(rpb, sigs)
    plan = jnp.array([starts, classes], jnp.int32)
    qg = jnp.tile(q_gain.astype(F32), 2)[None, :]
    kg = jnp.tile(k_gain.astype(F32), 2)[None, :]
    n_pairs = NA_HEADS // 2
    blk = (1, s, 2 * NA_HEAD_DIM)
    kern = functools.partial(_na_kernel, rows=rows)
    return pl.pallas_call(
        kern, grid=(n_pairs, b),
        in_specs=[
            pl.BlockSpec(memory_space=pltpu.SMEM),
            pl.BlockSpec(blk, lambda p, i: (i, 0, p)),
            pl.BlockSpec(blk, lambda p, i: (i, 0, n_pairs + p)),
            pl.BlockSpec(blk, lambda p, i: (i, 0, 2 * n_pairs + p)),
            pl.BlockSpec((1, 2 * NA_HEAD_DIM), lambda p, i: (0, 0)),
            pl.BlockSpec((1, 2 * NA_HEAD_DIM), lambda p, i: (0, 0)),
            pl.BlockSpec((1,) + bias.shape[1:], lambda p, i: (p, 0, 0, 0, 0)),
        ],
        out_specs=pl.BlockSpec(blk, lambda p, i: (i, 0, p)),
        out_shape=jax.ShapeDtypeStruct((b, s, NA_WIDTH), BF16),
        scratch_shapes=[pltpu.VMEM((s, 2 * NA_HEAD_DIM), BF16)] * 2,
        compiler_params=_params(("parallel", "parallel")), name="neighbourhood_attention",
    )(plan, proj3d, proj3d, proj3d, qg, kg, bias)


SSD_CONV_ROWS = 256
SSD_HALO = 16


def _chunk_scan(a, reverse):
    n = a.shape[1]
    pos = lax.broadcasted_iota(jnp.int32, a.shape, 1) % SSD_CHUNK
    sh = 1
    while sh < SSD_CHUNK:
        if reverse:
            a = a + jnp.where(pos < SSD_CHUNK - sh, pltpu.roll(a, n - sh, axis=1), 0.0)
        else:
            a = a + jnp.where(pos >= sh, pltpu.roll(a, sh, axis=1), 0.0)
        sh *= 2
    return a


def _conv_silu_into(raw_ref, w_ref, b_ref, dst_ref, seq, transpose_out):
    rows = SSD_CONV_ROWS
    n_blocks = seq // rows
    n = rows + 2 * SSD_HALO
    left = SSD_CONV // 2
    width = raw_ref.shape[-1]

    def body(i, carry):
        r0 = pl.multiple_of(i * rows, rows)
        p0 = pl.multiple_of(jnp.maximum(r0 - SSD_HALO, 0), SSD_HALO)
        n0 = pl.multiple_of(jnp.minimum(r0 + rows, seq - SSD_HALO), SSD_HALO)
        prev = jnp.where(i > 0, raw_ref[0, pl.ds(p0, SSD_HALO), :].astype(F32), 0.0)
        nxt = jnp.where(i < n_blocks - 1, raw_ref[0, pl.ds(n0, SSD_HALO), :].astype(F32), 0.0)
        win = jnp.concatenate([prev, raw_ref[0, pl.ds(r0, rows), :].astype(F32), nxt], axis=0)
        acc = jnp.broadcast_to(b_ref[...], (rows, width))
        for k in range(SSD_CONV):
            shift = (left - k) % n
            tap = win if shift == 0 else pltpu.roll(win, shift, axis=0)
            acc = acc + tap[SSD_HALO:SSD_HALO + rows] * w_ref[k:k + 1, :]
        out = _silu(acc)
        if transpose_out:
            dst_ref[:, pl.ds(r0, rows)] = out.T.astype(dst_ref.dtype)
        else:
            dst_ref[pl.ds(r0, rows), :] = out.astype(dst_ref.dtype)
        return carry

    lax.fori_loop(0, n_blocks, body, 0)


def _head_row(tile, first):
    lo = lax.broadcasted_iota(jnp.int32, (1, V7X_LANES), 1) < SSD_HEAD_DIM
    halves = [jnp.where(lo, tile[first + 2 * i:first + 2 * i + 1, :], tile[first + 2 * i + 1:first + 2 * i + 2, :])
              for i in range(SSD_HEADS_PER_GROUP // 2)]
    return jnp.concatenate(halves, axis=1)


def _ssd_kernel(xs_ref, b_ref, c_ref, z_ref, wx_ref, wb_ref, wc_ref, cbx_ref, cbb_ref, cbc_ref,
                dtr_ref, biasr_ref, alogr_ref, dskip_ref, gain_ref, o_ref,
                xs_s, bt_s, c_s, cumr_s, g2r_s, ld2r_s, wr_s, decr_s, st_s, h_s, *, n_chunks):
    L = SSD_CHUNK
    hpg = SSD_HEADS_PER_GROUP
    gw = SSD_GROUP_WIDTH
    ns = SSD_STATE
    seq = n_chunks * L

    _conv_silu_into(xs_ref, wx_ref, cbx_ref, xs_s, seq, False)
    _conv_silu_into(b_ref, wb_ref, cbb_ref, bt_s, seq, True)
    _conv_silu_into(c_ref, wc_ref, cbc_ref, c_s, seq, False)

    dt = _softplus(dtr_ref[...] + biasr_ref[0])
    a = dt * (-jnp.exp(alogr_ref[0]))
    is_fwd = lax.broadcasted_iota(jnp.int32, a.shape, 0) < hpg
    prefix = _chunk_scan(a, False)
    suffix = _chunk_scan(a, True)
    cum2 = jnp.where(is_fwd, prefix, suffix) * LOG2E
    cumr_s[...] = cum2
    g2r_s[...] = cum2 - jnp.log2(dt)
    ld2r_s[...] = jnp.log2(dt + pltpu.roll(dt, hpg, axis=0))
    wr_s[...] = dt * jnp.exp(jnp.where(is_fwd, suffix, prefix) - a)
    decr_s[...] = jnp.exp(prefix + suffix - a)

    def chunk_slice(c):
        return pl.ds(pl.multiple_of(c * L, L), L)

    lane_head = lax.broadcasted_iota(jnp.int32, (L, gw), 1) // SSD_HEAD_DIM

    def block_diag_x(xs):
        xs_bf = xs.astype(BF16)
        return jnp.concatenate([jnp.where(lane_head == j, xs_bf, jnp.zeros_like(xs_bf)) for j in range(hpg)], axis=0)

    def state_body(c, carry):
        sl = chunk_slice(c)
        bt = bt_s[:, sl].astype(F32)
        w = wr_s[:, sl]
        lhs = jnp.concatenate(
            [jnp.concatenate([(bt * w[d * hpg + j:d * hpg + j + 1, :]).astype(BF16) for j in range(hpg)], axis=1)
             for d in range(2)], axis=0)
        st_s[c] = _dot(lhs, block_diag_x(xs_s[sl, :]))
        return carry

    lax.fori_loop(0, n_chunks, state_body, 0, unroll=8)

    def fwd_rec(c, h):
        h_s[c, :, 0:gw] = h.astype(BF16)
        return h * _head_row(decr_s[:, chunk_slice(c)], 0) + st_s[c, 0:ns, :]

    def bwd_rec(i, h):
        c = n_chunks - 1 - i
        h_s[c, :, gw:2 * gw] = h.astype(BF16)
        return h * _head_row(decr_s[:, chunk_slice(c)], hpg) + st_s[c, ns:2 * ns, :]

    h0 = jnp.zeros((ns, gw), F32)
    lax.fori_loop(0, n_chunks, fwd_rec, h0)
    lax.fori_loop(0, n_chunks, bwd_rec, h0)

    li = lax.broadcasted_iota(jnp.int32, (L, L), 0)
    si = lax.broadcasted_iota(jnp.int32, (L, L), 1)
    below = si < li
    above = si > li
    lane_lo = lax.broadcasted_iota(jnp.int32, (L, V7X_LANES), 1) < SSD_HEAD_DIM

    def out_body(c, carry):
        sl = chunk_slice(c)
        xs = xs_s[sl, :]
        cm = c_s[sl, :]
        cum_r = cumr_s[:, sl]
        g2 = g2r_s[:, sl]
        ld2 = ld2r_s[:, sl]
        cum_t = [jnp.broadcast_to(cum_r[k:k + 1, :], (L, L)).T for k in range(2 * hpg)]
        cb = _dot(cm, bt_s[:, sl])
        mats = []
        for j in range(hpg):
            seg_f = cum_t[j] - g2[j:j + 1, :]
            seg_b = cum_t[hpg + j] - g2[hpg + j:hpg + j + 1, :]
            arg = jnp.where(below, seg_f, jnp.where(above, seg_b, ld2[j:j + 1, :]))
            mats.append((cb * jnp.exp2(arg)).astype(BF16))
        y = _dot(jnp.concatenate(mats, axis=1), block_diag_x(xs))
        carried = _dot(cm, h_s[c])
        for d in range(2):
            decay = jnp.exp2(jnp.concatenate(
                [jnp.where(lane_lo, cum_t[d * hpg + 2 * i], cum_t[d * hpg + 2 * i + 1]) for i in range(hpg // 2)],
                axis=1))
            y = y + carried[:, d * gw:(d + 1) * gw] * decay
        y = y + dskip_ref[0] * xs
        gated = y * _silu(z_ref[0, sl, :].astype(F32))
        ms = jnp.mean(gated * gated, axis=-1, keepdims=True)
        o_ref[0, sl, :] = (gated * lax.rsqrt(ms + NORM_EPS) * gain_ref[0]).astype(BF16)
        return carry

    lax.fori_loop(0, n_chunks, out_body, 0, unroll=8)


def _group_major(v):
    return jnp.transpose(v.astype(F32).reshape(2, SSD_GROUPS, SSD_HEADS_PER_GROUP), (1, 0, 2)).reshape(
        SSD_GROUPS, 2 * SSD_HEADS_PER_GROUP)


def _ssd_mixer(proj3d, dt_rows, conv_w, conv_b, dt_bias, a_log, d_skip, out_gain):
    b, s, _ = proj3d.shape
    n_chunks = s // SSD_CHUNK
    g, hpg, gw, ns = SSD_GROUPS, SSD_HEADS_PER_GROUP, SSD_GROUP_WIDTH, SSD_STATE
    bias_g = _group_major(dt_bias)[:, :, None]
    alog_g = _group_major(a_log)[:, :, None]
    dskip = jnp.repeat(d_skip.astype(F32), SSD_HEAD_DIM).reshape(g, 1, gw)
    gain = out_gain.astype(F32).reshape(g, 1, gw)
    conv_w = conv_w.astype(F32)
    conv_b = conv_b.astype(F32)[None, :]
    z_blk = (3 * NA_WIDTH) // gw
    x_blk = EVEN_XBC_OFFSET // gw
    b_blk = (EVEN_XBC_OFFSET + SSD_D_INNER) // ns
    c_blk = b_blk + g
    wb_blk = SSD_D_INNER // ns
    wc_blk = wb_blk + g
    kern = functools.partial(_ssd_kernel, n_chunks=n_chunks)
    small = lambda shape: pl.BlockSpec((1,) + shape, lambda i, k: (k, 0, 0))
    seq_blk = lambda width, blk0: pl.BlockSpec((1, s, width), lambda i, k: (i, 0, blk0 + k))
    par_blk = lambda rows, width, blk0: pl.BlockSpec((rows, width), lambda i, k: (0, blk0 + k))
    row_scratch = pltpu.VMEM((2 * hpg, s), F32)
    return pl.pallas_call(
        kern, grid=(b, g),
        in_specs=[
            seq_blk(gw, x_blk), seq_blk(ns, b_blk), seq_blk(ns, c_blk), seq_blk(gw, z_blk),
            par_blk(SSD_CONV, gw, 0), par_blk(SSD_CONV, ns, wb_blk), par_blk(SSD_CONV, ns, wc_blk),
            par_blk(1, gw, 0), par_blk(1, ns, wb_blk), par_blk(1, ns, wc_blk),
            pl.BlockSpec((2 * hpg, s), lambda i, k: (k, i)),
            small((2 * hpg, 1)), small((2 * hpg, 1)), small((1, gw)), small((1, gw)),
        ],
        out_specs=pl.BlockSpec((1, s, gw), lambda i, k: (i, 0, k)),
        out_shape=jax.ShapeDtypeStruct((b, s, SSD_D_INNER), BF16),
        scratch_shapes=[
            pltpu.VMEM((s, gw), F32), pltpu.VMEM((ns, s), BF16), pltpu.VMEM((s, ns), BF16),
            row_scratch, row_scratch, row_scratch, row_scratch, row_scratch,
            pltpu.VMEM((n_chunks, 2 * ns, gw), F32), pltpu.VMEM((n_chunks, ns, 2 * gw), BF16),
        ],
        compiler_params=_params(("parallel", "parallel")), name="ssd_bidirectional",
    )(proj3d, proj3d, proj3d, proj3d, conv_w, conv_w, conv_w, conv_b, conv_b, conv_b, dt_rows,
      bias_g, alog_g, dskip, gain)


FFN_CHUNK = 256


def _mix_ffn_kernel(*refs, n_acts, hidden):
    act_refs = refs[:n_acts]
    wout_refs = refs[n_acts:2 * n_acts]
    x_ref, g_ref, w13_ref, w2_ref, o_ref, hid_ref = refs[2 * n_acts:]
    h = x_ref[...]
    for a_ref, w_ref in zip(act_refs, wout_refs):
        h = h + _dot(a_ref[...], w_ref[...])
    hn = _rms_rows(h, g_ref[...]).astype(BF16)
    for c in range(hidden // FFN_CHUNK):
        gate = slice(c * FFN_CHUNK, (c + 1) * FFN_CHUNK)
        up = slice(hidden + c * FFN_CHUNK, hidden + (c + 1) * FFN_CHUNK)
        hid_ref[:, gate] = (_silu(_dot(hn, w13_ref[:, gate])) * _dot(hn, w13_ref[:, up])).astype(BF16)
    o_ref[...] = h + _dot(hid_ref[...], w2_ref[...])


def _mix_ffn(acts, w_outs, x2d, g, w13, w2, *, tm, name):
    m, d = x2d.shape
    hid = w2.shape[0]
    assert hid % FFN_CHUNK == 0
    row = lambda width: pl.BlockSpec((tm, width), lambda i: (i, 0))
    resident = lambda shape: pl.BlockSpec(shape, lambda i: (0, 0), pipeline_mode=pl.Buffered(1))
    return pl.pallas_call(
        functools.partial(_mix_ffn_kernel, n_acts=len(acts), hidden=hid), grid=(m // tm,),
        in_specs=([row(a.shape[1]) for a in acts] + [resident(w.shape) for w in w_outs]
                  + [row(d), pl.BlockSpec((1, d), lambda i: (0, 0)), resident((d, 2 * hid)), resident((hid, d))]),
        out_specs=row(d),
        out_shape=jax.ShapeDtypeStruct((m, d), F32),
        scratch_shapes=[pltpu.VMEM((tm, hid), BF16)],
        compiler_params=_params(("parallel",)), name=name,
    )(*acts, *w_outs, x2d, g, w13, w2)


def _rope_prep_kernel(p_ref, qg_ref, kg_ref, cos_ref, sin_ref, q_ref, k_ref, v_ref):
    cos = cos_ref[...]
    sin = sin_ref[...]
    even = (lax.broadcasted_iota(jnp.int32, cos.shape, 1) % 2) == 0
    scale = GQA_HEAD_DIM ** -0.5 * LOG2E

    def norm_rope(xb, gain):
        xn = _pair_head_rms(xb, gain)
        swapped = jnp.where(even, pltpu.roll(xn, V7X_LANES - 1, axis=1), pltpu.roll(xn, 1, axis=1))
        return xn * cos + swapped * sin

    for pair in range(GQA_HEADS // 2):
        blk = norm_rope(p_ref[0, :, pair * 128:(pair + 1) * 128].astype(F32), qg_ref[...]) * scale
        q_ref[0, 2 * pair] = blk[:, :GQA_HEAD_DIM].astype(BF16)
        q_ref[0, 2 * pair + 1] = blk[:, GQA_HEAD_DIM:].astype(BF16)
    for pair in range(GQA_KV_HEADS // 2):
        c0 = GQA_Q_WIDTH + pair * 128
        blk = norm_rope(p_ref[0, :, c0:c0 + 128].astype(F32), kg_ref[...])
        k_ref[0, 2 * pair] = blk[:, :GQA_HEAD_DIM].astype(BF16)
        k_ref[0, 2 * pair + 1] = blk[:, GQA_HEAD_DIM:].astype(BF16)
        c1 = GQA_Q_WIDTH + GQA_KV_WIDTH + pair * 128
        vt = p_ref[0, :, c1:c1 + 128].astype(F32).T.astype(BF16)
        v_ref[0, 2 * pair] = vt[:GQA_HEAD_DIM]
        v_ref[0, 2 * pair + 1] = vt[GQA_HEAD_DIM:]


def _axial_rope_tables(s):
    t = jnp.arange(s)
    row = (t // GRID_W).astype(F32)
    col = (t % GRID_W).astype(F32)
    axis_dims = GQA_HEAD_DIM // 2
    freqs = ROPE_THETA ** (-jnp.arange(0, axis_dims, 2, dtype=F32) / axis_dims)
    ang = jnp.concatenate([row[:, None] * freqs, col[:, None] * freqs], axis=-1)
    cos = jnp.repeat(jnp.cos(ang), 2, axis=-1)
    sin = jnp.stack([-jnp.sin(ang), jnp.sin(ang)], axis=-1).reshape(s, GQA_HEAD_DIM)
    return jnp.tile(cos, (1, 2)), jnp.tile(sin, (1, 2))


def _rope_prep(proj3d, q_gain, k_gain, *, ts):
    b, s, width = proj3d.shape
    cos, sin = _axial_rope_tables(s)
    qg = jnp.tile(q_gain.astype(F32), 2)[None, :]
    kg = jnp.tile(k_gain.astype(F32), 2)[None, :]
    head_out = lambda n: pl.BlockSpec((1, n, ts, GQA_HEAD_DIM), lambda i, t: (i, 0, t, 0))
    shape = lambda n: jax.ShapeDtypeStruct((b, n, s, GQA_HEAD_DIM), BF16)
    vt_out = pl.BlockSpec((1, GQA_KV_HEADS, GQA_HEAD_DIM, ts), lambda i, t: (i, 0, 0, t))
    vt_shape = jax.ShapeDtypeStruct((b, GQA_KV_HEADS, GQA_HEAD_DIM, s), BF16)
    return pl.pallas_call(
        _rope_prep_kernel, grid=(b, s // ts),
        in_specs=[
            pl.BlockSpec((1, ts, width), lambda i, t: (i, t, 0)),
            pl.BlockSpec((1, 128), lambda i, t: (0, 0)),
            pl.BlockSpec((1, 128), lambda i, t: (0, 0)),
            pl.BlockSpec((ts, 128), lambda i, t: (t, 0)),
            pl.BlockSpec((ts, 128), lambda i, t: (t, 0)),
        ],
        out_specs=[head_out(GQA_HEADS), head_out(GQA_KV_HEADS), vt_out],
        out_shape=[shape(GQA_HEADS), shape(GQA_KV_HEADS), vt_shape],
        compiler_params=_params(("parallel", "parallel")), name="gqa_norm_rope",
    )(proj3d, qg, kg, cos, sin)


GQA_KV_CHUNK = 256
GQA_SUM_ROWS = 16


def _gqa_kernel(q_ref, k_ref, vt_ref, o_ref, s_a, s_b, m_a, m_b, *, tq, seq):
    t = pl.program_id(0)
    cols = GQA_REP * tq

    @pl.when(t == 0)
    def _():
        s_b[...] = jnp.zeros(s_b.shape, F32)
        m_b[...] = jnp.zeros(m_b.shape, F32)

    def step(s_cur, m_cur, s_prev, m_prev_ref):
        q = q_ref[0].reshape(cols, GQA_HEAD_DIM)
        m_prev = m_prev_ref[...]
        m_run = None
        ones = jnp.ones((GQA_SUM_ROWS, GQA_KV_CHUNK), BF16)
        acc = jnp.zeros((GQA_HEAD_DIM + GQA_SUM_ROWS, cols), F32)
        for i in range(seq // GQA_KV_CHUNK):
            rows = slice(i * GQA_KV_CHUNK, (i + 1) * GQA_KV_CHUNK)
            st = _dot_nt(k_ref[0, 0, rows, :], q)
            s_cur[rows, :] = st
            cm = jnp.max(st, axis=0, keepdims=True)
            m_run = cm if m_run is None else jnp.maximum(m_run, cm)
            p = jnp.exp2(s_prev[rows, :] - m_prev)
            vt_aug = jnp.concatenate([vt_ref[0, 0, :, rows], ones], axis=0)
            acc = acc + _dot(vt_aug, p.astype(BF16))
        m_cur[...] = m_run
        ot = acc[:GQA_HEAD_DIM] * (1.0 / acc[GQA_HEAD_DIM:GQA_HEAD_DIM + 1])
        for r in range(GQA_REP):
            o_ref[0, :, r * GQA_HEAD_DIM:(r + 1) * GQA_HEAD_DIM] = ot[:, r * tq:(r + 1) * tq].T.astype(BF16)

    pl.when(t % 2 == 0)(lambda: step(s_a, m_a, s_b, m_b))
    pl.when(t % 2 == 1)(lambda: step(s_b, m_b, s_a, m_a))


def _gqa_attention(q, k, vt, *, tq):
    b, _, s, _ = q.shape
    nq = s // tq
    n_blocks = b * GQA_KV_HEADS * nq
    cols = GQA_REP * tq

    def unravel(u):
        return u // (nq * GQA_KV_HEADS), (u // nq) % GQA_KV_HEADS, u % nq

    def score_block(t):
        return unravel(jnp.minimum(t, n_blocks - 1))

    def finish_block(t):
        return unravel(jnp.maximum(t - 1, 0))

    def q_map(t):
        i, g, j = score_block(t)
        return (i, g, j, 0)

    def k_map(t):
        i, g, _ = score_block(t)
        return (i, g, 0, 0)

    def vt_map(t):
        i, g, _ = finish_block(t)
        return (i, g, 0, 0)

    def o_map(t):
        i, g, j = finish_block(t)
        return (i, j, g)

    kern = functools.partial(_gqa_kernel, tq=tq, seq=s)
    return pl.pallas_call(
        kern, grid=(n_blocks + 1,),
        in_specs=[
            pl.BlockSpec((1, GQA_REP, tq, GQA_HEAD_DIM), q_map),
            pl.BlockSpec((1, 1, s, GQA_HEAD_DIM), k_map),
            pl.BlockSpec((1, 1, GQA_HEAD_DIM, s), vt_map),
        ],
        out_specs=pl.BlockSpec((1, tq, GQA_REP * GQA_HEAD_DIM), o_map),
        out_shape=jax.ShapeDtypeStruct((b, s, GQA_Q_WIDTH), BF16),
        scratch_shapes=[pltpu.VMEM((s, cols), F32), pltpu.VMEM((s, cols), F32),
                        pltpu.VMEM((1, cols), F32), pltpu.VMEM((1, cols), F32)],
        compiler_params=_params(("arbitrary",)), name="gqa_attention",
    )(q, k, vt)


def _even_mixer(x2d, b, s, mix_norm, w_in, q_gain, k_gain, rpb, conv_w, conv_b, dt_bias, a_log, d_skip, out_gain,
                w_out):
    w_main = w_in[:, :EVEN_MAIN_WIDTH].astype(BF16)
    w_dt = jnp.transpose(w_in[:, EVEN_MAIN_WIDTH:].reshape(-1, 2, SSD_GROUPS, SSD_HEADS_PER_GROUP),
                         (2, 1, 3, 0)).reshape(2 * SSD_HEADS, -1).astype(BF16)
    proj, dt_rows = _norm_proj(x2d, mix_norm.astype(F32)[None, :], w_main, tm=512, w_small=w_dt,
                               name="even_in_proj")
    proj3d = proj.reshape(b, s, EVEN_MAIN_WIDTH)
    na_out = _neighbourhood_attention(proj3d, q_gain, k_gain, rpb)
    ssd_out = _ssd_mixer(proj3d, dt_rows, conv_w, conv_b, dt_bias, a_log, d_skip, out_gain)
    w_out_bf = w_out.astype(BF16)
    return ([na_out.reshape(b * s, NA_WIDTH), ssd_out.reshape(b * s, SSD_D_INNER)],
            [w_out_bf[:NA_WIDTH], w_out_bf[NA_WIDTH:]])


def _odd_mixer(x2d, b, s, mix_norm, w_qkv, q_gain, k_gain, w_out):
    proj = _norm_proj(x2d, mix_norm.astype(F32)[None, :], w_qkv.astype(BF16), tm=512, name="odd_qkv_proj")
    q, k, vt = _rope_prep(proj.reshape(b, s, -1), q_gain, k_gain, ts=512)
    attn = _gqa_attention(q, k, vt, tq=256)
    return [attn.reshape(b * s, GQA_Q_WIDTH)], [w_out.astype(BF16)]


def kernel(x, even_mix_norm, even_w_in, na_q_norm, na_k_norm, na_rel_bias, ssd_conv_w, ssd_conv_b, ssd_dt_bias, ssd_A_log, ssd_D, ssd_out_norm, even_w_out, odd_mix_norm, odd_w_qkv, gqa_q_norm, gqa_k_norm, odd_w_out, ffn_norm, ffn_w13, ffn_w2):
    b, s, d = x.shape
    depth = ffn_norm.shape[0]
    h = x.reshape(b * s, d)
    for layer in range(depth):
        i = layer // 2
        if layer % 2 == 0:
            acts, w_outs = _even_mixer(h, b, s, even_mix_norm[i], even_w_in[i], na_q_norm[i], na_k_norm[i],
                                       na_rel_bias[i], ssd_conv_w[i], ssd_conv_b[i], ssd_dt_bias[i], ssd_A_log[i],
                                       ssd_D[i], ssd_out_norm[i], even_w_out[i])
        else:
            acts, w_outs = _odd_mixer(h, b, s, odd_mix_norm[i], odd_w_qkv[i], gqa_q_norm[i], gqa_k_norm[i],
                                      odd_w_out[i])
        h = _mix_ffn(acts, w_outs, h, ffn_norm[layer].astype(F32)[None, :], ffn_w13[layer].astype(BF16),
                     ffn_w2[layer].astype(BF16), tm=512, name="mix_out_ffn_even" if layer % 2 == 0 else "mix_out_ffn_odd")
    return h.reshape(b, s, d)
```

```python
import functools

import jax
import jax.numpy as jnp
from jax import lax
from jax.experimental import pallas as pl
from jax.experimental.pallas import tpu as pltpu

F32 = jnp.float32
BF16 = jnp.bfloat16

D_MODEL = 1024
GRID_W = 64
NORM_EPS = 1e-6

NA_HEADS = 8
NA_HEAD_DIM = 64
NA_WIDTH = NA_HEADS * NA_HEAD_DIM
NA_KH = 8
NA_KW = 16

SSD_D_INNER = 1024
SSD_HEAD_DIM = 64
SSD_HEADS = 16
SSD_GROUPS = 4
SSD_STATE = 128
SSD_CONV = 4
SSD_CHUNK = 128
SSD_CONV_DIM = SSD_D_INNER + 2 * SSD_GROUPS * SSD_STATE
SSD_GROUP_WIDTH = SSD_D_INNER // SSD_GROUPS
SSD_HEADS_PER_GROUP = SSD_HEADS // SSD_GROUPS

EVEN_MAIN_WIDTH = 3 * NA_WIDTH + SSD_D_INNER + SSD_CONV_DIM
EVEN_XBC_OFFSET = 3 * NA_WIDTH + SSD_D_INNER

GQA_HEADS = 16
GQA_KV_HEADS = 4
GQA_HEAD_DIM = 64
GQA_REP = GQA_HEADS // GQA_KV_HEADS
GQA_Q_WIDTH = GQA_HEADS * GQA_HEAD_DIM
GQA_KV_WIDTH = GQA_KV_HEADS * GQA_HEAD_DIM
ROPE_THETA = 10000.0

FFN_HIDDEN = 2816

V7X_LANES = 128
V7X_VMEM_LIMIT = 56 * 1024 * 1024
MASK_VALUE = -1e30
LOG2E = 1.4426950408889634


def _params(dims):
    return pltpu.CompilerParams(dimension_semantics=dims, vmem_limit_bytes=V7X_VMEM_LIMIT)


def _silu(v):
    return v * (1.0 / (1.0 + jnp.exp(-v)))


def _softplus(v):
    return jnp.maximum(v, 0.0) + jnp.log(1.0 + jnp.exp(-jnp.abs(v)))


def _rms_rows(x, g):
    ms = jnp.mean(x * x, axis=-1, keepdims=True)
    return x * lax.rsqrt(ms + NORM_EPS) * g


def _dot(a, b):
    return jnp.dot(a, b, preferred_element_type=F32)


def _dot_nt(a, b):
    return lax.dot_general(a, b, (((1,), (1,)), ((), ())), preferred_element_type=F32)


PROJ_CHUNK = 512


def _norm_proj_kernel(x_ref, g_ref, w_ref, o_ref):
    xn = _rms_rows(x_ref[...], g_ref[...]).astype(BF16)
    for c in range(o_ref.shape[1] // PROJ_CHUNK):
        cols = slice(c * PROJ_CHUNK, (c + 1) * PROJ_CHUNK)
        o_ref[:, cols] = _dot(xn, w_ref[:, cols]).astype(o_ref.dtype)


def _norm_proj2_kernel(x_ref, g_ref, w_ref, w2_ref, o_ref, o2_ref):
    xn = _rms_rows(x_ref[...], g_ref[...]).astype(BF16)
    o2_ref[...] = _dot_nt(w2_ref[...], xn)
    for c in range(o_ref.shape[1] // PROJ_CHUNK):
        cols = slice(c * PROJ_CHUNK, (c + 1) * PROJ_CHUNK)
        o_ref[:, cols] = _dot(xn, w_ref[:, cols]).astype(o_ref.dtype)


def _norm_proj(x2d, g, w, *, tm, w_small=None, name):
    m, d = x2d.shape
    n = w.shape[1]
    assert n % PROJ_CHUNK == 0
    resident = lambda shape: pl.BlockSpec(shape, lambda i: (0, 0), pipeline_mode=pl.Buffered(1))
    in_specs = [pl.BlockSpec((tm, d), lambda i: (i, 0)), pl.BlockSpec((1, d), lambda i: (0, 0)), resident((d, n))]
    out_specs = pl.BlockSpec((tm, n), lambda i: (i, 0))
    out_shape = jax.ShapeDtypeStruct((m, n), BF16)
    if w_small is None:
        return pl.pallas_call(_norm_proj_kernel, grid=(m // tm,), in_specs=in_specs, out_specs=out_specs,
                              out_shape=out_shape, compiler_params=_params(("parallel",)), name=name)(x2d, g, w)
    ns = w_small.shape[0]
    in_specs.append(resident((ns, d)))
    return pl.pallas_call(_norm_proj2_kernel, grid=(m // tm,), in_specs=in_specs,
                          out_specs=[out_specs, pl.BlockSpec((ns, tm), lambda i: (0, i))],
                          out_shape=[out_shape, jax.ShapeDtypeStruct((ns, m), F32)],
                          compiler_params=_params(("parallel",)), name=name)(x2d, g, w, w_small)


NA_PREP_ROWS = 256
NA_GROUP_ROWS = 4
NA_WIN_ROWS = NA_KH + NA_GROUP_ROWS
NA_DY = 2 * NA_KH - 1
NA_DX = 2 * NA_KW - 1


def _na_group_plan(rows):
    sigs, starts, classes = [], [], []
    for gq in range(rows // NA_GROUP_ROWS):
        ks = min(max(gq * NA_GROUP_ROWS - NA_KH // 2, 0), rows - NA_WIN_ROWS)
        sig = tuple((min(max(r - NA_KH // 2, 0), rows - NA_KH) - ks, r - ks)
                    for r in range(gq * NA_GROUP_ROWS, (gq + 1) * NA_GROUP_ROWS))
        assert all(0 <= first and first + NA_KH <= NA_WIN_ROWS for first, _ in sig)
        if sig not in sigs:
            sigs.append(sig)
        starts.append(ks)
        classes.append(sigs.index(sig))
    return sigs, starts, classes


def _pair_head_rms(x, g):
    lo = lax.broadcasted_iota(jnp.int32, x.shape, 1) < NA_HEAD_DIM
    x2 = x * x
    s_lo = jnp.sum(jnp.where(lo, x2, 0.0), axis=-1, keepdims=True)
    s_hi = jnp.sum(jnp.where(lo, 0.0, x2), axis=-1, keepdims=True)
    ms = jnp.where(lo, s_lo, s_hi) * (1.0 / NA_HEAD_DIM)
    return x * lax.rsqrt(ms + NORM_EPS) * g


def _na_kernel(plan_ref, q_ref, k_ref, v_ref, qg_ref, kg_ref, bias_ref, o_ref, q_s, k_s, *, rows):
    scale = NA_HEAD_DIM ** -0.5

    def prep(i, carry):
        sl = pl.ds(pl.multiple_of(i * NA_PREP_ROWS, NA_PREP_ROWS), NA_PREP_ROWS)
        q_s[sl, :] = (_pair_head_rms(q_ref[0, sl, :].astype(F32), qg_ref[...]) * scale).astype(BF16)
        k_s[sl, :] = _pair_head_rms(k_ref[0, sl, :].astype(F32), kg_ref[...]).astype(BF16)
        return carry

    lax.fori_loop(0, (rows * GRID_W) // NA_PREP_ROWS, prep, 0)

    n_q = NA_GROUP_ROWS * GRID_W
    n_keys = NA_WIN_ROWS * GRID_W

    def group_body(gq, carry):
        ks = plan_ref[0, gq]
        cls = plan_ref[1, gq]
        qsl = pl.ds(pl.multiple_of(gq * n_q, n_q), n_q)
        ksl = pl.ds(pl.multiple_of(ks * GRID_W, GRID_W), n_keys)
        q = q_s[qsl, :]
        kk = k_s[ksl, :]
        vv = v_ref[0, ksl, :]
        outs = []
        for h in range(2):
            hs = slice(h * NA_HEAD_DIM, (h + 1) * NA_HEAD_DIM)
            s = _dot_nt(q[:, hs], kk[:, hs]) + bias_ref[0, cls, h]
            m = jnp.max(s, axis=-1, keepdims=True)
            p = jnp.exp(s - m)
            l = jnp.sum(p, axis=-1, keepdims=True)
            outs.append(_dot(p.astype(BF16), vv[:, hs]) * (1.0 / l))
        o_ref[0, qsl, :] = jnp.concatenate(outs, axis=-1).astype(BF16)
        return carry

    lax.fori_loop(0, rows // NA_GROUP_ROWS, group_body, 0, unroll=4)


def _na_bias_kernel(rpb_ref, o_ref, t_s, *, sigs):
    h = pl.program_id(0)
    q = lax.broadcasted_iota(jnp.int32, (GRID_W, GRID_W), 0)
    k = lax.broadcasted_iota(jnp.int32, (GRID_W, GRID_W), 1)
    dx = jnp.clip(k - q, -(NA_KW - 1), NA_KW - 1) + (NA_KW - 1)
    col_start = jnp.clip(q - NA_KW // 2, 0, GRID_W - NA_KW)
    in_win = (k >= col_start) & (k < col_start + NA_KW)
    masked = jnp.full((GRID_W, GRID_W), MASK_VALUE, F32)
    for dy in range(NA_DY):
        base = (h * NA_DY + dy) * NA_DX
        t = masked
        for d in range(NA_DX):
            t = jnp.where(dx == d, rpb_ref[base + d], t)
        t_s[dy] = jnp.where(in_win, t, MASK_VALUE)
    for cls, sig in enumerate(sigs):
        for rq, (first, qrow) in enumerate(sig):
            for jk in range(NA_WIN_ROWS):
                attended = first <= jk < first + NA_KH
                tile = t_s[jk - qrow + NA_KH - 1] if attended else masked
                o_ref[0, cls, 0, rq * GRID_W:(rq + 1) * GRID_W, jk * GRID_W:(jk + 1) * GRID_W] = tile


def _na_bias_table(rpb, sigs):
    n_q = NA_GROUP_ROWS * GRID_W
    n_keys = NA_WIN_ROWS * GRID_W
    kern = functools.partial(_na_bias_kernel, sigs=sigs)
    return pl.pallas_call(
        kern, grid=(NA_HEADS,),
        in_specs=[pl.BlockSpec(memory_space=pltpu.SMEM)],
        out_specs=pl.BlockSpec((1, len(sigs), 1, n_q, n_keys), lambda h: (h // 2, 0, h % 2, 0, 0)),
        out_shape=jax.ShapeDtypeStruct((NA_HEADS // 2, len(sigs), 2, n_q, n_keys), F32),
        scratch_shapes=[pltpu.VMEM((NA_DY, GRID_W, GRID_W), F32)],
        compiler_params=_params(("parallel",)), name="na_bias_table",
    )(rpb.astype(F32).reshape(-1))


def _neighbourhood_attention(proj3d, q_gain, k_gain, rpb):
    b, s, _ = proj3d.shape
    rows = s // GRID_W
    assert rows >= NA_WIN_ROWS and rows % NA_GROUP_ROWS == 0
    sigs, starts, classes = _na_group_plan(rows)
    bias = _na_bias_table(rpb, sigs)
    plan = jnp.array([starts, classes], jnp.int32)
    qg = jnp.tile(q_gain.astype(F32), 2)[None, :]
    kg = jnp.tile(k_gain.astype(F32), 2)[None, :]
    n_pairs = NA_HEADS // 2
    blk = (1, s, 2 * NA_HEAD_DIM)
    kern = functools.partial(_na_kernel, rows=rows)
    return pl.pallas_call(
        kern, grid=(n_pairs, b),
        in_specs=[
            pl.BlockSpec(memory_space=pltpu.SMEM),
            pl.BlockSpec(blk, lambda p, i: (i, 0, p)),
            pl.BlockSpec(blk, lambda p, i: (i, 0, n_pairs + p)),
            pl.BlockSpec(blk, lambda p, i: (i, 0, 2 * n_pairs + p)),
            pl.BlockSpec((1, 2 * NA_HEAD_DIM), lambda p, i: (0, 0)),
            pl.BlockSpec((1, 2 * NA_HEAD_DIM), lambda p, i: (0, 0)),
            pl.BlockSpec((1,) + bias.shape[1:], lambda p, i: (p, 0, 0, 0, 0)),
        ],
        out_specs=pl.BlockSpec(blk, lambda p, i: (i, 0, p)),
        out_shape=jax.ShapeDtypeStruct((b, s, NA_WIDTH), BF16),
        scratch_shapes=[pltpu.VMEM((s, 2 * NA_HEAD_DIM), BF16)] * 2,
        compiler_params=_params(("parallel", "parallel")), name="neighbourhood_attention",
    )(plan, proj3d, proj3d, proj3d, qg, kg, bias)


SSD_CONV_ROWS = 256
SSD_HALO = 16


def _chunk_scan(a, reverse):
    n = a.shape[1]
    pos = lax.broadcasted_iota(jnp.int32, a.shape, 1) % SSD_CHUNK
    sh = 1
    while sh < SSD_CHUNK:
        if reverse:
            a = a + jnp.where(pos < SSD_CHUNK - sh, pltpu.roll(a, n - sh, axis=1), 0.0)
        else:
            a = a + jnp.where(pos >= sh, pltpu.roll(a, sh, axis=1), 0.0)
        sh *= 2
    return a


def _conv_silu_into(raw_ref, w_ref, b_ref, dst_ref, seq, transpose_out):
    rows = SSD_CONV_ROWS
    n_blocks = seq // rows
    n = rows + 2 * SSD_HALO
    left = SSD_CONV // 2
    width = raw_ref.shape[-1]

    def body(i, carry):
        r0 = pl.multiple_of(i * rows, rows)
        p0 = pl.multiple_of(jnp.maximum(r0 - SSD_HALO, 0), SSD_HALO)
        n0 = pl.multiple_of(jnp.minimum(r0 + rows, seq - SSD_HALO), SSD_HALO)
        prev = jnp.where(i > 0, raw_ref[0, pl.ds(p0, SSD_HALO), :].astype(F32), 0.0)
        nxt = jnp.where(i < n_blocks - 1, raw_ref[0, pl.ds(n0, SSD_HALO), :].astype(F32), 0.0)
        win = jnp.concatenate([prev, raw_ref[0, pl.ds(r0, rows), :].astype(F32), nxt], axis=0)
        acc = jnp.broadcast_to(b_ref[...], (rows, width))
        for k in range(SSD_CONV):
            shift = (left - k) % n
            tap = win if shift == 0 else pltpu.roll(win, shift, axis=0)
            acc = acc + tap[SSD_HALO:SSD_HALO + rows] * w_ref[k:k + 1, :]
        out = _silu(acc)
        if transpose_out:
            dst_ref[:, pl.ds(r0, rows)] = out.T.astype(dst_ref.dtype)
        else:
            dst_ref[pl.ds(r0, rows), :] = out.astype(dst_ref.dtype)
        return carry

    lax.fori_loop(0, n_blocks, body, 0)


def _head_row(tile, first):
    lo = lax.broadcasted_iota(jnp.int32, (1, V7X_LANES), 1) < SSD_HEAD_DIM
    halves = [jnp.where(lo, tile[first + 2 * i:first + 2 * i + 1, :], tile[first + 2 * i + 1:first + 2 * i + 2, :])
              for i in range(SSD_HEADS_PER_GROUP // 2)]
    return jnp.concatenate(halves, axis=1)


def _ssd_kernel(xs_ref, b_ref, c_ref, z_ref, wx_ref, wb_ref, wc_ref, cbx_ref, cbb_ref, cbc_ref,
                dtr_ref, biasr_ref, alogr_ref, dskip_ref, gain_ref, o_ref,
                xs_s, bt_s, c_s, cumr_s, g2r_s, ld2r_s, wr_s, decr_s, st_s, h_s, *, n_chunks):
    L = SSD_CHUNK
    hpg = SSD_HEADS_PER_GROUP
    gw = SSD_GROUP_WIDTH
    ns = SSD_STATE
    seq = n_chunks * L

    _conv_silu_into(xs_ref, wx_ref, cbx_ref, xs_s, seq, False)
    _conv_silu_into(b_ref, wb_ref, cbb_ref, bt_s, seq, True)
    _conv_silu_into(c_ref, wc_ref, cbc_ref, c_s, seq, False)

    dt = _softplus(dtr_ref[...] + biasr_ref[0])
    a = dt * (-jnp.exp(alogr_ref[0]))
    is_fwd = lax.broadcasted_iota(jnp.int32, a.shape, 0) < hpg
    prefix = _chunk_scan(a, False)
    suffix = _chunk_scan(a, True)
    cum2 = jnp.where(is_fwd, prefix, suffix) * LOG2E
    cumr_s[...] = cum2
    g2r_s[...] = cum2 - jnp.log2(dt)
    ld2r_s[...] = jnp.log2(dt + pltpu.roll(dt, hpg, axis=0))
    wr_s[...] = dt * jnp.exp(jnp.where(is_fwd, suffix, prefix) - a)
    decr_s[...] = jnp.exp(prefix + suffix - a)

    def chunk_slice(c):
        return pl.ds(pl.multiple_of(c * L, L), L)

    lane_head = lax.broadcasted_iota(jnp.int32, (L, gw), 1) // SSD_HEAD_DIM

    def block_diag_x(xs):
        xs_bf = xs.astype(BF16)
        return jnp.concatenate([jnp.where(lane_head == j, xs_bf, jnp.zeros_like(xs_bf)) for j in range(hpg)], axis=0)

    def state_body(c, carry):
        sl = chunk_slice(c)
        bt = bt_s[:, sl].astype(F32)
        w = wr_s[:, sl]
        lhs = jnp.concatenate(
            [jnp.concatenate([(bt * w[d * hpg + j:d * hpg + j + 1, :]).astype(BF16) for j in range(hpg)], axis=1)
             for d in range(2)], axis=0)
        st_s[c] = _dot(lhs, block_diag_x(xs_s[sl, :]))
        return carry

    lax.fori_loop(0, n_chunks, state_body, 0, unroll=8)

    def fwd_rec(c, h):
        h_s[c, :, 0:gw] = h.astype(BF16)
        return h * _head_row(decr_s[:, chunk_slice(c)], 0) + st_s[c, 0:ns, :]

    def bwd_rec(i, h):
        c = n_chunks - 1 - i
        h_s[c, :, gw:2 * gw] = h.astype(BF16)
        return h * _head_row(decr_s[:, chunk_slice(c)], hpg) + st_s[c, ns:2 * ns, :]

    h0 = jnp.zeros((ns, gw), F32)
    lax.fori_loop(0, n_chunks, fwd_rec, h0)
    lax.fori_loop(0, n_chunks, bwd_rec, h0)

    li = lax.broadcasted_iota(jnp.int32, (L, L), 0)
    si = lax.broadcasted_iota(jnp.int32, (L, L), 1)
    below = si < li
    above = si > li
    lane_lo = lax.broadcasted_iota(jnp.int32, (L, V7X_LANES), 1) < SSD_HEAD_DIM

    def out_body(c, carry):
        sl = chunk_slice(c)
        xs = xs_s[sl, :]
        cm = c_s[sl, :]
        cum_r = cumr_s[:, sl]
        g2 = g2r_s[:, sl]
        ld2 = ld2r_s[:, sl]
        cum_t = [jnp.broadcast_to(cum_r[k:k + 1, :], (L, L)).T for k in range(2 * hpg)]
        cb = _dot(cm, bt_s[:, sl])
        mats = []
        for j in range(hpg):
            seg_f = cum_t[j] - g2[j:j + 1, :]
            seg_b = cum_t[hpg + j] - g2[hpg + j:hpg + j + 1, :]
            arg = jnp.where(below, seg_f, jnp.where(above, seg_b, ld2[j:j + 1, :]))
            mats.append((cb * jnp.exp2(arg)).astype(BF16))
        y = _dot(jnp.concatenate(mats, axis=1), block_diag_x(xs))
        carried = _dot(cm, h_s[c])
        for d in range(2):
            decay = jnp.exp2(jnp.concatenate(
                [jnp.where(lane_lo, cum_t[d * hpg + 2 * i], cum_t[d * hpg + 2 * i + 1]) for i in range(hpg // 2)],
                axis=1))
            y = y + carried[:, d * gw:(d + 1) * gw] * decay
        y = y + dskip_ref[0] * xs
        gated = y * _silu(z_ref[0, sl, :].astype(F32))
        ms = jnp.mean(gated * gated, axis=-1, keepdims=True)
        o_ref[0, sl, :] = (gated * lax.rsqrt(ms + NORM_EPS) * gain_ref[0]).astype(BF16)
        return carry

    lax.fori_loop(0, n_chunks, out_body, 0, unroll=8)


def _group_major(v):
    return jnp.transpose(v.astype(F32).reshape(2, SSD_GROUPS, SSD_HEADS_PER_GROUP), (1, 0, 2)).reshape(
        SSD_GROUPS, 2 * SSD_HEADS_PER_GROUP)


def _ssd_mixer(proj3d, dt_rows, conv_w, conv_b, dt_bias, a_log, d_skip, out_gain):
    b, s, _ = proj3d.shape
    n_chunks = s // SSD_CHUNK
    g, hpg, gw, ns = SSD_GROUPS, SSD_HEADS_PER_GROUP, SSD_GROUP_WIDTH, SSD_STATE
    bias_g = _group_major(dt_bias)[:, :, None]
    alog_g = _group_major(a_log)[:, :, None]
    dskip = jnp.repeat(d_skip.astype(F32), SSD_HEAD_DIM).reshape(g, 1, gw)
    gain = out_gain.astype(F32).reshape(g, 1, gw)
    conv_w = conv_w.astype(F32)
    conv_b = conv_b.astype(F32)[None, :]
    z_blk = (3 * NA_WIDTH) // gw
    x_blk = EVEN_XBC_OFFSET // gw
    b_blk = (EVEN_XBC_OFFSET + SSD_D_INNER) // ns
    c_blk = b_blk + g
    wb_blk = SSD_D_INNER // ns
    wc_blk = wb_blk + g
    kern = functools.partial(_ssd_kernel, n_chunks=n_chunks)
    small = lambda shape: pl.BlockSpec((1,) + shape, lambda i, k: (k, 0, 0))
    seq_blk = lambda width, blk0: pl.BlockSpec((1, s, width), lambda i, k: (i, 0, blk0 + k))
    par_blk = lambda rows, width, blk0: pl.BlockSpec((rows, width), lambda i, k: (0, blk0 + k))
    row_scratch = pltpu.VMEM((2 * hpg, s), F32)
    return pl.pallas_call(
        kern, grid=(b, g),
        in_specs=[
            seq_blk(gw, x_blk), seq_blk(ns, b_blk), seq_blk(ns, c_blk), seq_blk(gw, z_blk),
            par_blk(SSD_CONV, gw, 0), par_blk(SSD_CONV, ns, wb_blk), par_blk(SSD_CONV, ns, wc_blk),
            par_blk(1, gw, 0), par_blk(1, ns, wb_blk), par_blk(1, ns, wc_blk),
            pl.BlockSpec((2 * hpg, s), lambda i, k: (k, i)),
            small((2 * hpg, 1)), small((2 * hpg, 1)), small((1, gw)), small((1, gw)),
        ],
        out_specs=pl.BlockSpec((1, s, gw), lambda i, k: (i, 0, k)),
        out_shape=jax.ShapeDtypeStruct((b, s, SSD_D_INNER), BF16),
        scratch_shapes=[
            pltpu.VMEM((s, gw), F32), pltpu.VMEM((ns, s), BF16), pltpu.VMEM((s, ns), BF16),
            row_scratch, row_scratch, row_scratch, row_scratch, row_scratch,
            pltpu.VMEM((n_chunks, 2 * ns, gw), F32), pltpu.VMEM((n_chunks, ns, 2 * gw), BF16),
        ],
        compiler_params=_params(("parallel", "parallel")), name="ssd_bidirectional",
    )(proj3d, proj3d, proj3d, proj3d, conv_w, conv_w, conv_w, conv_b, conv_b, conv_b, dt_rows,
      bias_g, alog_g, dskip, gain)


FFN_CHUNK = 256


def _mix_ffn_kernel(*refs, n_acts, hidden):
    act_refs = refs[:n_acts]
    wout_refs = refs[n_acts:2 * n_acts]
    x_ref, g_ref, w13_ref, w2_ref, o_ref, hid_ref = refs[2 * n_acts:]
    h = x_ref[...]
    for a_ref, w_ref in zip(act_refs, wout_refs):
        h = h + _dot(a_ref[...], w_ref[...])
    hn = _rms_rows(h, g_ref[...]).astype(BF16)
    for c in range(hidden // FFN_CHUNK):
        gate = slice(c * FFN_CHUNK, (c + 1) * FFN_CHUNK)
        up = slice(hidden + c * FFN_CHUNK, hidden + (c + 1) * FFN_CHUNK)
        hid_ref[:, gate] = (_silu(_dot(hn, w13_ref[:, gate])) * _dot(hn, w13_ref[:, up])).astype(BF16)
    o_ref[...] = h + _dot(hid_ref[...], w2_ref[...])


def _mix_ffn(acts, w_outs, x2d, g, w13, w2, *, tm, name):
    m, d = x2d.shape
    hid = w2.shape[0]
    assert hid % FFN_CHUNK == 0
    row = lambda width: pl.BlockSpec((tm, width), lambda i: (i, 0))
    resident = lambda shape: pl.BlockSpec(shape, lambda i: (0, 0), pipeline_mode=pl.Buffered(1))
    return pl.pallas_call(
        functools.partial(_mix_ffn_kernel, n_acts=len(acts), hidden=hid), grid=(m // tm,),
        in_specs=([row(a.shape[1]) for a in acts] + [resident(w.shape) for w in w_outs]
                  + [row(d), pl.BlockSpec((1, d), lambda i: (0, 0)), resident((d, 2 * hid)), resident((hid, d))]),
        out_specs=row(d),
        out_shape=jax.ShapeDtypeStruct((m, d), F32),
        scratch_shapes=[pltpu.VMEM((tm, hid), BF16)],
        compiler_params=_params(("parallel",)), name=name,
    )(*acts, *w_outs, x2d, g, w13, w2)


def _rope_prep_kernel(p_ref, qg_ref, kg_ref, cos_ref, sin_ref, cost_ref, sint_ref, q_ref, k_ref, v_ref):
    scale = GQA_HEAD_DIM ** -0.5 * LOG2E
    ts = p_ref.shape[1]
    cos_t = cost_ref[...]
    sin_t = sint_ref[...]
    even_row = (lax.broadcasted_iota(jnp.int32, (2 * GQA_HEAD_DIM, ts), 0) % 2) == 0
    q_gain = qg_ref[...] * scale
    hd = GQA_HEAD_DIM
    for pair in range(GQA_HEADS // 2):
        xt = p_ref[0, :, pair * 128:(pair + 1) * 128].astype(F32).T
        x2 = xt * xt
        inv = [lax.rsqrt(jnp.mean(x2[h * hd:(h + 1) * hd], axis=0, keepdims=True) + NORM_EPS) for h in range(2)]
        xn = jnp.concatenate([xt[:hd] * inv[0], xt[hd:] * inv[1]], axis=0) * q_gain
        swapped = jnp.where(even_row, pltpu.roll(xn, 2 * hd - 1, axis=0), pltpu.roll(xn, 1, axis=0))
        out = xn * cos_t + swapped * sin_t
        q_ref[0, 2 * pair] = out[:hd].astype(BF16)
        q_ref[0, 2 * pair + 1] = out[hd:].astype(BF16)

    cos = cos_ref[...]
    sin = sin_ref[...]
    even = (lax.broadcasted_iota(jnp.int32, cos.shape, 1) % 2) == 0
    for pair in range(GQA_KV_HEADS // 2):
        c0 = GQA_Q_WIDTH + pair * 128
        xn = _pair_head_rms(p_ref[0, :, c0:c0 + 128].astype(F32), kg_ref[...])
        swapped = jnp.where(even, pltpu.roll(xn, V7X_LANES - 1, axis=1), pltpu.roll(xn, 1, axis=1))
        blk = xn * cos + swapped * sin
        k_ref[0, 2 * pair] = blk[:, :hd].astype(BF16)
        k_ref[0, 2 * pair + 1] = blk[:, hd:].astype(BF16)
        c1 = GQA_Q_WIDTH + GQA_KV_WIDTH + pair * 128
        vt = p_ref[0, :, c1:c1 + 128].astype(F32).T.astype(BF16)
        v_ref[0, 2 * pair] = vt[:hd]
        v_ref[0, 2 * pair + 1] = vt[hd:]


def _axial_rope_tables(s):
    t = jnp.arange(s)
    row = (t // GRID_W).astype(F32)
    col = (t % GRID_W).astype(F32)
    axis_dims = GQA_HEAD_DIM // 2
    freqs = ROPE_THETA ** (-jnp.arange(0, axis_dims, 2, dtype=F32) / axis_dims)
    ang = jnp.concatenate([row[:, None] * freqs, col[:, None] * freqs], axis=-1)
    cos = jnp.repeat(jnp.cos(ang), 2, axis=-1)
    sin = jnp.stack([-jnp.sin(ang), jnp.sin(ang)], axis=-1).reshape(s, GQA_HEAD_DIM)
    return jnp.tile(cos, (1, 2)), jnp.tile(sin, (1, 2))


def _rope_prep(proj3d, q_gain, k_gain, *, ts):
    b, s, width = proj3d.shape
    cos, sin = _axial_rope_tables(s)
    qg = jnp.tile(q_gain.astype(F32), 2)[:, None]
    kg = jnp.tile(k_gain.astype(F32), 2)[None, :]
    head_out = lambda n: pl.BlockSpec((1, n, ts, GQA_HEAD_DIM), lambda i, t: (i, 0, t, 0))
    shape = lambda n: jax.ShapeDtypeStruct((b, n, s, GQA_HEAD_DIM), BF16)
    t_out = lambda n: pl.BlockSpec((1, n, GQA_HEAD_DIM, ts), lambda i, t: (i, 0, 0, t))
    t_shape = lambda n: jax.ShapeDtypeStruct((b, n, GQA_HEAD_DIM, s), BF16)
    return pl.pallas_call(
        _rope_prep_kernel, grid=(b, s // ts),
        in_specs=[
            pl.BlockSpec((1, ts, width), lambda i, t: (i, t, 0)),
            pl.BlockSpec((128, 1), lambda i, t: (0, 0)),
            pl.BlockSpec((1, 128), lambda i, t: (0, 0)),
            pl.BlockSpec((ts, 128), lambda i, t: (t, 0)),
            pl.BlockSpec((ts, 128), lambda i, t: (t, 0)),
            pl.BlockSpec((128, ts), lambda i, t: (0, t)),
            pl.BlockSpec((128, ts), lambda i, t: (0, t)),
        ],
        out_specs=[t_out(GQA_HEADS), head_out(GQA_KV_HEADS), t_out(GQA_KV_HEADS)],
        out_shape=[t_shape(GQA_HEADS), shape(GQA_KV_HEADS), t_shape(GQA_KV_HEADS)],
        compiler_params=_params(("parallel", "parallel")), name="gqa_norm_rope",
    )(proj3d, qg, kg, cos, sin, cos.T, sin.T)


GQA_KV_CHUNK = 256
GQA_SUM_ROWS = 16


def _gqa_kernel(q_ref, k_ref, vt_ref, o_ref, s_a, s_b, m_a, m_b, *, tq, seq):
    t = pl.program_id(0)
    cols = GQA_REP * tq

    @pl.when(t == 0)
    def _():
        s_b[...] = jnp.zeros(s_b.shape, F32)
        m_b[...] = jnp.zeros(m_b.shape, F32)

    def step(s_cur, m_cur, s_prev, m_prev_ref):
        qt = jnp.concatenate([q_ref[0, r] for r in range(GQA_REP)], axis=1)
        m_prev = m_prev_ref[...]
        m_run = None
        ones = jnp.ones((GQA_SUM_ROWS, GQA_KV_CHUNK), BF16)
        acc = jnp.zeros((GQA_HEAD_DIM + GQA_SUM_ROWS, cols), F32)
        for i in range(seq // GQA_KV_CHUNK):
            rows = slice(i * GQA_KV_CHUNK, (i + 1) * GQA_KV_CHUNK)
            st = _dot(k_ref[0, 0, rows, :], qt)
            s_cur[rows, :] = st
            cm = jnp.max(st, axis=0, keepdims=True)
            m_run = cm if m_run is None else jnp.maximum(m_run, cm)
            p = jnp.exp2(s_prev[rows, :] - m_prev)
            vt_aug = jnp.concatenate([vt_ref[0, 0, :, rows], ones], axis=0)
            acc = acc + _dot(vt_aug, p.astype(BF16))
        m_cur[...] = m_run
        ot = acc[:GQA_HEAD_DIM] * (1.0 / acc[GQA_HEAD_DIM:GQA_HEAD_DIM + 1])
        for r in range(GQA_REP):
            o_ref[0, :, r * GQA_HEAD_DIM:(r + 1) * GQA_HEAD_DIM] = ot[:, r * tq:(r + 1) * tq].T.astype(BF16)

    pl.when(t % 2 == 0)(lambda: step(s_a, m_a, s_b, m_b))
    pl.when(t % 2 == 1)(lambda: step(s_b, m_b, s_a, m_a))


def _gqa_attention(q, k, vt, *, tq):
    b, _, _, s = q.shape
    nq = s // tq
    n_blocks = b * GQA_KV_HEADS * nq
    cols = GQA_REP * tq

    def unravel(u):
        return u // (nq * GQA_KV_HEADS), (u // nq) % GQA_KV_HEADS, u % nq

    def score_block(t):
        return unravel(jnp.minimum(t, n_blocks - 1))

    def finish_block(t):
        return unravel(jnp.maximum(t - 1, 0))

    def q_map(t):
        i, g, j = score_block(t)
        return (i, g, 0, j)

    def k_map(t):
        i, g, _ = score_block(t)
        return (i, g, 0, 0)

    def vt_map(t):
        i, g, _ = finish_block(t)
        return (i, g, 0, 0)

    def o_map(t):
        i, g, j = finish_block(t)
        return (i, j, g)

    kern = functools.partial(_gqa_kernel, tq=tq, seq=s)
    return pl.pallas_call(
        kern, grid=(n_blocks + 1,),
        in_specs=[
            pl.BlockSpec((1, GQA_REP, GQA_HEAD_DIM, tq), q_map),
            pl.BlockSpec((1, 1, s, GQA_HEAD_DIM), k_map),
            pl.BlockSpec((1, 1, GQA_HEAD_DIM, s), vt_map),
        ],
        out_specs=pl.BlockSpec((1, tq, GQA_REP * GQA_HEAD_DIM), o_map),
        out_shape=jax.ShapeDtypeStruct((b, s, GQA_Q_WIDTH), BF16),
        scratch_shapes=[pltpu.VMEM((s, cols), F32), pltpu.VMEM((s, cols), F32),
                        pltpu.VMEM((1, cols), F32), pltpu.VMEM((1, cols), F32)],
        compiler_params=_params(("arbitrary",)), name="gqa_attention",
    )(q, k, vt)


def _even_mixer(x2d, b, s, mix_norm, w_in, q_gain, k_gain, rpb, conv_w, conv_b, dt_bias, a_log, d_skip, out_gain,
                w_out):
    w_main = w_in[:, :EVEN_MAIN_WIDTH].astype(BF16)
    w_dt = jnp.transpose(w_in[:, EVEN_MAIN_WIDTH:].reshape(-1, 2, SSD_GROUPS, SSD_HEADS_PER_GROUP),
                         (2, 1, 3, 0)).reshape(2 * SSD_HEADS, -1).astype(BF16)
    proj, dt_rows = _norm_proj(x2d, mix_norm.astype(F32)[None, :], w_main, tm=512, w_small=w_dt,
                               name="even_in_proj")
    proj3d = proj.reshape(b, s, EVEN_MAIN_WIDTH)
    na_out = _neighbourhood_attention(proj3d, q_gain, k_gain, rpb)
    ssd_out = _ssd_mixer(proj3d, dt_rows, conv_w, conv_b, dt_bias, a_log, d_skip, out_gain)
    w_out_bf = w_out.astype(BF16)
    return ([na_out.reshape(b * s, NA_WIDTH), ssd_out.reshape(b * s, SSD_D_INNER)],
            [w_out_bf[:NA_WIDTH], w_out_bf[NA_WIDTH:]])


def _odd_mixer(x2d, b, s, mix_norm, w_qkv, q_gain, k_gain, w_out):
    proj = _norm_proj(x2d, mix_norm.astype(F32)[None, :], w_qkv.astype(BF16), tm=512, name="odd_qkv_proj")
    q, k, vt = _rope_prep(proj.reshape(b, s, -1), q_gain, k_gain, ts=512)
    attn = _gqa_attention(q, k, vt, tq=256)
    return [attn.reshape(b * s, GQA_Q_WIDTH)], [w_out.astype(BF16)]


def kernel(x, even_mix_norm, even_w_in, na_q_norm, na_k_norm, na_rel_bias, ssd_conv_w, ssd_conv_b, ssd_dt_bias, ssd_A_log, ssd_D, ssd_out_norm, even_w_out, odd_mix_norm, odd_w_qkv, gqa_q_norm, gqa_k_norm, odd_w_out, ffn_norm, ffn_w13, ffn_w2):
    b, s, d = x.shape
    depth = ffn_norm.shape[0]
    h = x.reshape(b * s, d)
    for layer in range(depth):
        i = layer // 2
        if layer % 2 == 0:
            acts, w_outs = _even_mixer(h, b, s, even_mix_norm[i], even_w_in[i], na_q_norm[i], na_k_norm[i],
                                       na_rel_bias[i], ssd_conv_w[i], ssd_conv_b[i], ssd_dt_bias[i], ssd_A_log[i],
                                       ssd_D[i], ssd_out_norm[i], even_w_out[i])
        else:
            acts, w_outs = _odd_mixer(h, b, s, odd_mix_norm[i], odd_w_qkv[i], gqa_q_norm[i], gqa_k_norm[i],
                                      odd_w_out[i])
        h = _mix_ffn(acts, w_outs, h, ffn_norm[layer].astype(F32)[None, :], ffn_w13[layer].astype(BF16),
                     ffn_w2[layer].astype(BF16), tm=512, name="mix_out_ffn_even" if layer % 2 == 0 else "mix_out_ffn_odd")
    return h.reshape(b, s, d)
```

```python
import functools

import jax
import jax.numpy as jnp
from jax import lax
from jax.experimental import pallas as pl
from jax.experimental.pallas import tpu as pltpu

F32 = jnp.float32
BF16 = jnp.bfloat16

D_MODEL = 1024
GRID_W = 64
NORM_EPS = 1e-6

NA_HEADS = 8
NA_HEAD_DIM = 64
NA_WIDTH = NA_HEADS * NA_HEAD_DIM
NA_KH = 8
NA_KW = 16

SSD_D_INNER = 1024
SSD_HEAD_DIM = 64
SSD_HEADS = 16
SSD_GROUPS = 4
SSD_STATE = 128
SSD_CONV = 4
SSD_CHUNK = 128
SSD_CONV_DIM = SSD_D_INNER + 2 * SSD_GROUPS * SSD_STATE
SSD_GROUP_WIDTH = SSD_D_INNER // SSD_GROUPS
SSD_HEADS_PER_GROUP = SSD_HEADS // SSD_GROUPS

EVEN_MAIN_WIDTH = 3 * NA_WIDTH + SSD_D_INNER + SSD_CONV_DIM
EVEN_XBC_OFFSET = 3 * NA_WIDTH + SSD_D_INNER

GQA_HEADS = 16
GQA_KV_HEADS = 4
GQA_HEAD_DIM = 64
GQA_REP = GQA_HEADS // GQA_KV_HEADS
GQA_Q_WIDTH = GQA_HEADS * GQA_HEAD_DIM
GQA_KV_WIDTH = GQA_KV_HEADS * GQA_HEAD_DIM
ROPE_THETA = 10000.0

FFN_HIDDEN = 2816

V7X_LANES = 128
V7X_VMEM_LIMIT = 56 * 1024 * 1024
MASK_VALUE = -1e30
LOG2E = 1.4426950408889634


def _params(dims):
    return pltpu.CompilerParams(dimension_semantics=dims, vmem_limit_bytes=V7X_VMEM_LIMIT)


def _silu(v):
    return v * (1.0 / (1.0 + jnp.exp(-v)))


def _softplus(v):
    return jnp.maximum(v, 0.0) + jnp.log(1.0 + jnp.exp(-jnp.abs(v)))


def _rms_rows(x, g):
    ms = jnp.mean(x * x, axis=-1, keepdims=True)
    return x * lax.rsqrt(ms + NORM_EPS) * g


def _dot(a, b):
    return jnp.dot(a, b, preferred_element_type=F32)


def _dot_nt(a, b):
    return lax.dot_general(a, b, (((1,), (1,)), ((), ())), preferred_element_type=F32)


PROJ_CHUNK = 512


def _norm_proj_kernel(x_ref, g_ref, w_ref, o_ref):
    xn = _rms_rows(x_ref[...], g_ref[...]).astype(BF16)
    for c in range(o_ref.shape[1] // PROJ_CHUNK):
        cols = slice(c * PROJ_CHUNK, (c + 1) * PROJ_CHUNK)
        o_ref[:, cols] = _dot(xn, w_ref[:, cols]).astype(o_ref.dtype)


CONV_HALO = 8


def _even_in_proj_kernel(x_ref, xp_ref, xn_ref, g_ref, w_ref, wdt_ref, cw_ref, cb_ref, o_ref, ox_ref, obt_ref, oc_ref,
                         odt_ref, *, tiles_per_seq):
    tm = x_ref.shape[0]
    n_main = o_ref.shape[1]
    i = pl.program_id(0) % tiles_per_seq
    gain = g_ref[...]
    xc = _rms_rows(x_ref[...], gain)
    xp = jnp.where(i > 0, _rms_rows(xp_ref[...], gain), 0.0)
    xn = jnp.where(i < tiles_per_seq - 1, _rms_rows(xn_ref[...], gain), 0.0)
    xc_bf = xc.astype(BF16)
    xe_bf = jnp.concatenate([xp, xc, xn], axis=0).astype(BF16)
    odt_ref[...] = _dot_nt(wdt_ref[...], xc_bf)
    n_ext = tm + 2 * CONV_HALO
    left = SSD_CONV // 2
    n_main_chunks = n_main // PROJ_CHUNK
    nx, nb, ncc = (r // PROJ_CHUNK for r in (ox_ref.shape[1], obt_ref.shape[0], oc_ref.shape[1]))
    n_conv_chunks = nx + nb + ncc

    def main_chunk(c):
        cols = slice(c * PROJ_CHUNK, (c + 1) * PROJ_CHUNK)
        o_ref[:, cols] = _dot(xc_bf, w_ref[:, cols]).astype(o_ref.dtype)

    def conv_chunk(c, pr):
        cols = slice(c * PROJ_CHUNK, (c + 1) * PROJ_CHUNK)
        acc = jnp.broadcast_to(cb_ref[:, cols], (tm, PROJ_CHUNK))
        for k in range(SSD_CONV):
            shift = (left - k) % n_ext
            tap = pr if shift == 0 else pltpu.roll(pr, shift, axis=0)
            acc = acc + tap[CONV_HALO:CONV_HALO + tm] * cw_ref[k:k + 1, cols]
        out = _silu(acc)
        if c < nx:
            ox_ref[:, cols] = out.astype(ox_ref.dtype)
        elif c < nx + nb:
            obt_ref[(c - nx) * PROJ_CHUNK:(c - nx + 1) * PROJ_CHUNK, :] = out.T.astype(obt_ref.dtype)
        else:
            oc_ref[:, (c - nx - nb) * PROJ_CHUNK:(c - nx - nb + 1) * PROJ_CHUNK] = out.astype(oc_ref.dtype)

    for c in range(max(n_main_chunks, n_conv_chunks)):
        pr = None
        if c < n_conv_chunks:
            pr = _dot(xe_bf, w_ref[:, n_main + c * PROJ_CHUNK:n_main + (c + 1) * PROJ_CHUNK])
        if c < n_main_chunks:
            main_chunk(c)
        if pr is not None:
            conv_chunk(c, pr)


def _even_in_proj(x2d, g, w, w_dt, conv_w, conv_b, *, tm, seq, n_main, name):
    m, d = x2d.shape
    n = w.shape[1]
    n_x = SSD_D_INNER
    n_bc = SSD_GROUPS * SSD_STATE
    assert n - n_main == n_x + 2 * n_bc and n_x % PROJ_CHUNK == 0 and n_bc % PROJ_CHUNK == 0
    ns = w_dt.shape[0]
    assert n_main % PROJ_CHUNK == 0 and seq % tm == 0 and tm % CONV_HALO == 0
    halo_per_tile = tm // CONV_HALO
    n_halo_blocks = m // CONV_HALO
    resident = lambda shape: pl.BlockSpec(shape, lambda i: (0, 0), pipeline_mode=pl.Buffered(1))
    kern = functools.partial(_even_in_proj_kernel, tiles_per_seq=seq // tm)
    return pl.pallas_call(
        kern, grid=(m // tm,),
        in_specs=[
            pl.BlockSpec((tm, d), lambda i: (i, 0)),
            pl.BlockSpec((CONV_HALO, d), lambda i: (jnp.maximum(i * halo_per_tile - 1, 0), 0)),
            pl.BlockSpec((CONV_HALO, d), lambda i: (jnp.minimum((i + 1) * halo_per_tile, n_halo_blocks - 1), 0)),
            pl.BlockSpec((1, d), lambda i: (0, 0)),
            resident((d, n)), resident((ns, d)), resident(conv_w.shape), resident(conv_b.shape),
        ],
        out_specs=[pl.BlockSpec((tm, n_main), lambda i: (i, 0)), pl.BlockSpec((tm, n_x), lambda i: (i, 0)),
                   pl.BlockSpec((n_bc, tm), lambda i: (0, i)), pl.BlockSpec((tm, n_bc), lambda i: (i, 0)),
                   pl.BlockSpec((ns, tm), lambda i: (0, i))],
        out_shape=[jax.ShapeDtypeStruct((m, n_main), BF16), jax.ShapeDtypeStruct((m, n_x), BF16),
                   jax.ShapeDtypeStruct((n_bc, m), BF16), jax.ShapeDtypeStruct((m, n_bc), BF16),
                   jax.ShapeDtypeStruct((ns, m), F32)],
        compiler_params=_params(("parallel",)), name=name,
    )(x2d, x2d, x2d, g, w, w_dt, conv_w, conv_b)


def _norm_proj(x2d, g, w, *, tm, name):
    m, d = x2d.shape
    n = w.shape[1]
    assert n % PROJ_CHUNK == 0
    resident = lambda shape: pl.BlockSpec(shape, lambda i: (0, 0), pipeline_mode=pl.Buffered(1))
    return pl.pallas_call(
        _norm_proj_kernel, grid=(m // tm,),
        in_specs=[pl.BlockSpec((tm, d), lambda i: (i, 0)), pl.BlockSpec((1, d), lambda i: (0, 0)), resident((d, n))],
        out_specs=pl.BlockSpec((tm, n), lambda i: (i, 0)), out_shape=jax.ShapeDtypeStruct((m, n), BF16),
        compiler_params=_params(("parallel",)), name=name)(x2d, g, w)


NA_PREP_ROWS = 256
NA_GROUP_ROWS = 4
NA_WIN_ROWS = NA_KH + NA_GROUP_ROWS
NA_DY = 2 * NA_KH - 1
NA_DX = 2 * NA_KW - 1


def _na_group_plan(rows):
    sigs, starts, classes = [], [], []
    for gq in range(rows // NA_GROUP_ROWS):
        ks = min(max(gq * NA_GROUP_ROWS - NA_KH // 2, 0), rows - NA_WIN_ROWS)
        sig = tuple((min(max(r - NA_KH // 2, 0), rows - NA_KH) - ks, r - ks)
                    for r in range(gq * NA_GROUP_ROWS, (gq + 1) * NA_GROUP_ROWS))
        assert all(0 <= first and first + NA_KH <= NA_WIN_ROWS for first, _ in sig)
        if sig not in sigs:
            sigs.append(sig)
        starts.append(ks)
        classes.append(sigs.index(sig))
    return sigs, starts, classes


def _pair_head_rms(x, g):
    lo = lax.broadcasted_iota(jnp.int32, x.shape, 1) < NA_HEAD_DIM
    x2 = x * x
    s_lo = jnp.sum(jnp.where(lo, x2, 0.0), axis=-1, keepdims=True)
    s_hi = jnp.sum(jnp.where(lo, 0.0, x2), axis=-1, keepdims=True)
    ms = jnp.where(lo, s_lo, s_hi) * (1.0 / NA_HEAD_DIM)
    return x * lax.rsqrt(ms + NORM_EPS) * g


def _na_kernel(plan_ref, q_ref, k_ref, v_ref, qg_ref, kg_ref, bias_ref, o_ref, q_s, k_s, *, rows):
    scale = NA_HEAD_DIM ** -0.5

    def prep(i, carry):
        sl = pl.ds(pl.multiple_of(i * NA_PREP_ROWS, NA_PREP_ROWS), NA_PREP_ROWS)
        q_s[sl, :] = (_pair_head_rms(q_ref[0, sl, :].astype(F32), qg_ref[...]) * scale).astype(BF16)
        k_s[sl, :] = _pair_head_rms(k_ref[0, sl, :].astype(F32), kg_ref[...]).astype(BF16)
        return carry

    lax.fori_loop(0, (rows * GRID_W) // NA_PREP_ROWS, prep, 0)

    n_q = NA_GROUP_ROWS * GRID_W
    n_keys = NA_WIN_ROWS * GRID_W

    def group_body(gq, carry):
        ks = plan_ref[0, gq]
        cls = plan_ref[1, gq]
        qsl = pl.ds(pl.multiple_of(gq * n_q, n_q), n_q)
        ksl = pl.ds(pl.multiple_of(ks * GRID_W, GRID_W), n_keys)
        q = q_s[qsl, :]
        kk = k_s[ksl, :]
        vv = v_ref[0, ksl, :]
        outs = []
        for h in range(2):
            hs = slice(h * NA_HEAD_DIM, (h + 1) * NA_HEAD_DIM)
            s = _dot_nt(q[:, hs], kk[:, hs]) + bias_ref[0, cls, h]
            m = jnp.max(s, axis=-1, keepdims=True)
            p = jnp.exp(s - m)
            l = jnp.sum(p, axis=-1, keepdims=True)
            outs.append(_dot(p.astype(BF16), vv[:, hs]) * (1.0 / l))
        o_ref[0, qsl, :] = jnp.concatenate(outs, axis=-1).astype(BF16)
        return carry

    lax.fori_loop(0, rows // NA_GROUP_ROWS, group_body, 0, unroll=4)


def _na_bias_kernel(rpb_ref, o_ref, t_s, *, sigs):
    h = pl.program_id(0)
    q = lax.broadcasted_iota(jnp.int32, (GRID_W, GRID_W), 0)
    k = lax.broadcasted_iota(jnp.int32, (GRID_W, GRID_W), 1)
    dx = jnp.clip(k - q, -(NA_KW - 1), NA_KW - 1) + (NA_KW - 1)
    col_start = jnp.clip(q - NA_KW // 2, 0, GRID_W - NA_KW)
    in_win = (k >= col_start) & (k < col_start + NA_KW)
    masked = jnp.full((GRID_W, GRID_W), MASK_VALUE, F32)
    for dy in range(NA_DY):
        base = (h * NA_DY + dy) * NA_DX
        t = masked
        for d in range(NA_DX):
            t = jnp.where(dx == d, rpb_ref[base + d], t)
        t_s[dy] = jnp.where(in_win, t, MASK_VALUE)
    for cls, sig in enumerate(sigs):
        for rq, (first, qrow) in enumerate(sig):
            for jk in range(NA_WIN_ROWS):
                attended = first <= jk < first + NA_KH
                tile = t_s[jk - qrow + NA_KH - 1] if attended else masked
                o_ref[0, cls, 0, rq * GRID_W:(rq + 1) * GRID_W, jk * GRID_W:(jk + 1) * GRID_W] = tile


def _na_bias_table(rpb, sigs):
    n_q = NA_GROUP_ROWS * GRID_W
    n_keys = NA_WIN_ROWS * GRID_W
    kern = functools.partial(_na_bias_kernel, sigs=sigs)
    return pl.pallas_call(
        kern, grid=(NA_HEADS,),
        in_specs=[pl.BlockSpec(memory_space=pltpu.SMEM)],
        out_specs=pl.BlockSpec((1, len(sigs), 1, n_q, n_keys), lambda h: (h // 2, 0, h % 2, 0, 0)),
        out_shape=jax.ShapeDtypeStruct((NA_HEADS // 2, len(sigs), 2, n_q, n_keys), F32),
        scratch_shapes=[pltpu.VMEM((NA_DY, GRID_W, GRID_W), F32)],
        compiler_params=_params(("parallel",)), name="na_bias_table",
    )(rpb.astype(F32).reshape(-1))


def _neighbourhood_attention(proj3d, q_gain, k_gain, rpb):
    b, s, _ = proj3d.shape
    rows = s // GRID_W
    assert rows >= NA_WIN_ROWS and rows % NA_GROUP_ROWS == 0
    sigs, starts, classes = _na_group_plan(rows)
    bias = _na_bias_table(rpb, sigs)
    plan = jnp.array([starts, classes], jnp.int32)
    qg = jnp.tile(q_gain.astype(F32), 2)[None, :]
    kg = jnp.tile(k_gain.astype(F32), 2)[None, :]
    n_pairs = NA_HEADS // 2
    blk = (1, s, 2 * NA_HEAD_DIM)
    kern = functools.partial(_na_kernel, rows=rows)
    return pl.pallas_call(
        kern, grid=(n_pairs, b),
        in_specs=[
            pl.BlockSpec(memory_space=pltpu.SMEM),
            pl.BlockSpec(blk, lambda p, i: (i, 0, p)),
            pl.BlockSpec(blk, lambda p, i: (i, 0, n_pairs + p)),
            pl.BlockSpec(blk, lambda p, i: (i, 0, 2 * n_pairs + p)),
            pl.BlockSpec((1, 2 * NA_HEAD_DIM), lambda p, i: (0, 0)),
            pl.BlockSpec((1, 2 * NA_HEAD_DIM), lambda p, i: (0, 0)),
            pl.BlockSpec((1,) + bias.shape[1:], lambda p, i: (p, 0, 0, 0, 0)),
        ],
        out_specs=pl.BlockSpec(blk, lambda p, i: (i, 0, p)),
        out_shape=jax.ShapeDtypeStruct((b, s, NA_WIDTH), BF16),
        scratch_shapes=[pltpu.VMEM((s, 2 * NA_HEAD_DIM), BF16)] * 2,
        compiler_params=_params(("parallel", "parallel")), name="neighbourhood_attention",
    )(plan, proj3d, proj3d, proj3d, qg, kg, bias)


def _chunk_scan(a, reverse):
    n = a.shape[1]
    pos = lax.broadcasted_iota(jnp.int32, a.shape, 1) % SSD_CHUNK
    sh = 1
    while sh < SSD_CHUNK:
        if reverse:
            a = a + jnp.where(pos < SSD_CHUNK - sh, pltpu.roll(a, n - sh, axis=1), 0.0)
        else:
            a = a + jnp.where(pos >= sh, pltpu.roll(a, sh, axis=1), 0.0)
        sh *= 2
    return a


def _head_row(tile, first):
    lo = lax.broadcasted_iota(jnp.int32, (1, V7X_LANES), 1) < SSD_HEAD_DIM
    halves = [jnp.where(lo, tile[first + 2 * i:first + 2 * i + 1, :], tile[first + 2 * i + 1:first + 2 * i + 2, :])
              for i in range(SSD_HEADS_PER_GROUP // 2)]
    return jnp.concatenate(halves, axis=1)


def _ssd_kernel(xs_ref, bt_ref, c_ref, z_ref, dtr_ref, biasr_ref, alogr_ref, dskip_ref, gain_ref, o_ref,
                cumr_s, g2r_s, ld2r_s, wr_s, decr_s, st_s, h_s, *, n_chunks):
    L = SSD_CHUNK
    hpg = SSD_HEADS_PER_GROUP
    gw = SSD_GROUP_WIDTH
    ns = SSD_STATE
    seq = n_chunks * L

    dt = _softplus(dtr_ref[...] + biasr_ref[0])
    a = dt * (-jnp.exp(alogr_ref[0]))
    is_fwd = lax.broadcasted_iota(jnp.int32, a.shape, 0) < hpg
    prefix = _chunk_scan(a, False)
    suffix = _chunk_scan(a, True)
    cum2 = jnp.where(is_fwd, prefix, suffix) * LOG2E
    cumr_s[...] = cum2
    g2r_s[...] = cum2 - jnp.log2(dt)
    ld2r_s[...] = jnp.log2(dt + pltpu.roll(dt, hpg, axis=0))
    wr_s[...] = dt * jnp.exp(jnp.where(is_fwd, suffix, prefix) - a)
    decr_s[...] = jnp.exp(prefix + suffix - a)

    def chunk_slice(c):
        return pl.ds(pl.multiple_of(c * L, L), L)

    lane_head = lax.broadcasted_iota(jnp.int32, (L, gw), 1) // SSD_HEAD_DIM

    def block_diag_x(xs):
        return jnp.concatenate([jnp.where(lane_head == j, xs, jnp.zeros_like(xs)) for j in range(hpg)], axis=0)

    def state_body(c, carry):
        sl = chunk_slice(c)
        bt = bt_ref[:, sl].astype(F32)
        w = wr_s[:, sl]
        lhs = jnp.concatenate(
            [jnp.concatenate([(bt * w[d * hpg + j:d * hpg + j + 1, :]).astype(BF16) for j in range(hpg)], axis=1)
             for d in range(2)], axis=0)
        st_s[c] = _dot(lhs, block_diag_x(xs_ref[0, sl, :]))
        return carry

    lax.fori_loop(0, n_chunks, state_body, 0, unroll=8)

    def fwd_rec(c, h):
        h_s[c, :, 0:gw] = h.astype(BF16)
        return h * _head_row(decr_s[:, chunk_slice(c)], 0) + st_s[c, 0:ns, :]

    def bwd_rec(i, h):
        c = n_chunks - 1 - i
        h_s[c, :, gw:2 * gw] = h.astype(BF16)
        return h * _head_row(decr_s[:, chunk_slice(c)], hpg) + st_s[c, ns:2 * ns, :]

    h0 = jnp.zeros((ns, gw), F32)
    lax.fori_loop(0, n_chunks, fwd_rec, h0)
    lax.fori_loop(0, n_chunks, bwd_rec, h0)

    li = lax.broadcasted_iota(jnp.int32, (L, L), 0)
    si = lax.broadcasted_iota(jnp.int32, (L, L), 1)
    below = si < li
    above = si > li
    lane_lo = lax.broadcasted_iota(jnp.int32, (L, V7X_LANES), 1) < SSD_HEAD_DIM

    def out_body(c, carry):
        sl = chunk_slice(c)
        xs = xs_ref[0, sl, :]
        cm = c_ref[0, sl, :]
        cum_r = cumr_s[:, sl]
        g2 = g2r_s[:, sl]
        ld2 = ld2r_s[:, sl]
        cum_t = [jnp.broadcast_to(cum_r[k:k + 1, :], (L, L)).T for k in range(2 * hpg)]
        cb = _dot(cm, bt_ref[:, sl])
        mats = []
        for j in range(hpg):
            seg_f = cum_t[j] - g2[j:j + 1, :]
            seg_b = cum_t[hpg + j] - g2[hpg + j:hpg + j + 1, :]
            arg = jnp.where(below, seg_f, jnp.where(above, seg_b, ld2[j:j + 1, :]))
            mats.append((cb * jnp.exp2(arg)).astype(BF16))
        y = _dot(jnp.concatenate(mats, axis=1), block_diag_x(xs))
        carried = _dot(cm, h_s[c])
        for d in range(2):
            decay = jnp.exp2(jnp.concatenate(
                [jnp.where(lane_lo, cum_t[d * hpg + 2 * i], cum_t[d * hpg + 2 * i + 1]) for i in range(hpg // 2)],
                axis=1))
            y = y + carried[:, d * gw:(d + 1) * gw] * decay
        y = y + dskip_ref[0] * xs.astype(F32)
        gated = y * _silu(z_ref[0, sl, :].astype(F32))
        ms = jnp.mean(gated * gated, axis=-1, keepdims=True)
        o_ref[0, sl, :] = (gated * lax.rsqrt(ms + NORM_EPS) * gain_ref[0]).astype(BF16)
        return carry

    lax.fori_loop(0, n_chunks, out_body, 0, unroll=8)


def _group_major(v):
    return jnp.transpose(v.astype(F32).reshape(2, SSD_GROUPS, SSD_HEADS_PER_GROUP), (1, 0, 2)).reshape(
        SSD_GROUPS, 2 * SSD_HEADS_PER_GROUP)


def _ssd_mixer(proj3d, xs3d, bt2d, c3d, dt_rows, dt_bias, a_log, d_skip, out_gain):
    b, s, _ = proj3d.shape
    n_chunks = s // SSD_CHUNK
    g, hpg, gw, ns = SSD_GROUPS, SSD_HEADS_PER_GROUP, SSD_GROUP_WIDTH, SSD_STATE
    bias_g = _group_major(dt_bias)[:, :, None]
    alog_g = _group_major(a_log)[:, :, None]
    dskip = jnp.repeat(d_skip.astype(F32), SSD_HEAD_DIM).reshape(g, 1, gw)
    gain = out_gain.astype(F32).reshape(g, 1, gw)
    z_blk = (3 * NA_WIDTH) // gw
    kern = functools.partial(_ssd_kernel, n_chunks=n_chunks)
    small = lambda shape: pl.BlockSpec((1,) + shape, lambda i, k: (k, 0, 0))
    seq_blk = lambda width, blk0: pl.BlockSpec((1, s, width), lambda i, k: (i, 0, blk0 + k))
    row_scratch = pltpu.VMEM((2 * hpg, s), F32)
    return pl.pallas_call(
        kern, grid=(b, g),
        in_specs=[
            seq_blk(gw, 0), pl.BlockSpec((ns, s), lambda i, k: (k, i)), seq_blk(ns, 0), seq_blk(gw, z_blk),
            pl.BlockSpec((2 * hpg, s), lambda i, k: (k, i)),
            small((2 * hpg, 1)), small((2 * hpg, 1)), small((1, gw)), small((1, gw)),
        ],
        out_specs=pl.BlockSpec((1, s, gw), lambda i, k: (i, 0, k)),
        out_shape=jax.ShapeDtypeStruct((b, s, SSD_D_INNER), BF16),
        scratch_shapes=[
            row_scratch, row_scratch, row_scratch, row_scratch, row_scratch,
            pltpu.VMEM((n_chunks, 2 * ns, gw), F32), pltpu.VMEM((n_chunks, ns, 2 * gw), BF16),
        ],
        compiler_params=_params(("parallel", "parallel")), name="ssd_bidirectional",
    )(xs3d, bt2d, c3d, proj3d, dt_rows, bias_g, alog_g, dskip, gain)


FFN_CHUNK = 256


def _mix_ffn_kernel(*refs, n_acts, hidden):
    act_refs = refs[:n_acts]
    wout_refs = refs[n_acts:2 * n_acts]
    x_ref, g_ref, w13_ref, w2_ref, o_ref, hid_ref = refs[2 * n_acts:]
    h = x_ref[...]
    for a_ref, w_ref in zip(act_refs, wout_refs):
        h = h + _dot(a_ref[...], w_ref[...])
    hn = _rms_rows(h, g_ref[...]).astype(BF16)
    for c in range(hidden // FFN_CHUNK):
        gate = slice(c * FFN_CHUNK, (c + 1) * FFN_CHUNK)
        up = slice(hidden + c * FFN_CHUNK, hidden + (c + 1) * FFN_CHUNK)
        hid_ref[:, gate] = (_silu(_dot(hn, w13_ref[:, gate])) * _dot(hn, w13_ref[:, up])).astype(BF16)
    o_ref[...] = h + _dot(hid_ref[...], w2_ref[...])


def _mix_ffn(acts, w_outs, x2d, g, w13, w2, *, tm, name):
    m, d = x2d.shape
    hid = w2.shape[0]
    assert hid % FFN_CHUNK == 0
    row = lambda width: pl.BlockSpec((tm, width), lambda i: (i, 0))
    resident = lambda shape: pl.BlockSpec(shape, lambda i: (0, 0), pipeline_mode=pl.Buffered(1))
    return pl.pallas_call(
        functools.partial(_mix_ffn_kernel, n_acts=len(acts), hidden=hid), grid=(m // tm,),
        in_specs=([row(a.shape[1]) for a in acts] + [resident(w.shape) for w in w_outs]
                  + [row(d), pl.BlockSpec((1, d), lambda i: (0, 0)), resident((d, 2 * hid)), resident((hid, d))]),
        out_specs=row(d),
        out_shape=jax.ShapeDtypeStruct((m, d), F32),
        scratch_shapes=[pltpu.VMEM((tm, hid), BF16)],
        compiler_params=_params(("parallel",)), name=name,
    )(*acts, *w_outs, x2d, g, w13, w2)


def _rope_prep_kernel(p_ref, qg_ref, kg_ref, cos_ref, sin_ref, cost_ref, sint_ref, q_ref, k_ref, v_ref):
    scale = GQA_HEAD_DIM ** -0.5 * LOG2E
    ts = p_ref.shape[1]
    cos_t = cost_ref[...]
    sin_t = sint_ref[...]
    even_row = (lax.broadcasted_iota(jnp.int32, (2 * GQA_HEAD_DIM, ts), 0) % 2) == 0
    q_gain = qg_ref[...] * scale
    hd = GQA_HEAD_DIM
    for pair in range(GQA_HEADS // 2):
        xt = p_ref[0, :, pair * 128:(pair + 1) * 128].astype(F32).T
        x2 = xt * xt
        inv = [lax.rsqrt(jnp.mean(x2[h * hd:(h + 1) * hd], axis=0, keepdims=True) + NORM_EPS) for h in range(2)]
        xn = jnp.concatenate([xt[:hd] * inv[0], xt[hd:] * inv[1]], axis=0) * q_gain
        swapped = jnp.where(even_row, pltpu.roll(xn, 2 * hd - 1, axis=0), pltpu.roll(xn, 1, axis=0))
        out = xn * cos_t + swapped * sin_t
        q_ref[0, 2 * pair] = out[:hd].astype(BF16)
        q_ref[0, 2 * pair + 1] = out[hd:].astype(BF16)

    cos = cos_ref[...]
    sin = sin_ref[...]
    even = (lax.broadcasted_iota(jnp.int32, cos.shape, 1) % 2) == 0
    for pair in range(GQA_KV_HEADS // 2):
        c0 = GQA_Q_WIDTH + pair * 128
        xn = _pair_head_rms(p_ref[0, :, c0:c0 + 128].astype(F32), kg_ref[...])
        swapped = jnp.where(even, pltpu.roll(xn, V7X_LANES - 1, axis=1), pltpu.roll(xn, 1, axis=1))
        blk = xn * cos + swapped * sin
        k_ref[0, 2 * pair] = blk[:, :hd].astype(BF16)
        k_ref[0, 2 * pair + 1] = blk[:, hd:].astype(BF16)
        c1 = GQA_Q_WIDTH + GQA_KV_WIDTH + pair * 128
        vt = p_ref[0, :, c1:c1 + 128].astype(F32).T.astype(BF16)
        v_ref[0, 2 * pair] = vt[:hd]
        v_ref[0, 2 * pair + 1] = vt[hd:]


def _axial_rope_tables(s):
    t = jnp.arange(s)
    row = (t // GRID_W).astype(F32)
    col = (t % GRID_W).astype(F32)
    axis_dims = GQA_HEAD_DIM // 2
    freqs = ROPE_THETA ** (-jnp.arange(0, axis_dims, 2, dtype=F32) / axis_dims)
    ang = jnp.concatenate([row[:, None] * freqs, col[:, None] * freqs], axis=-1)
    cos = jnp.repeat(jnp.cos(ang), 2, axis=-1)
    sin = jnp.stack([-jnp.sin(ang), jnp.sin(ang)], axis=-1).reshape(s, GQA_HEAD_DIM)
    return jnp.tile(cos, (1, 2)), jnp.tile(sin, (1, 2))


def _rope_prep(proj3d, q_gain, k_gain, *, ts):
    b, s, width = proj3d.shape
    cos, sin = _axial_rope_tables(s)
    qg = jnp.tile(q_gain.astype(F32), 2)[:, None]
    kg = jnp.tile(k_gain.astype(F32), 2)[None, :]
    head_out = lambda n: pl.BlockSpec((1, n, ts, GQA_HEAD_DIM), lambda i, t: (i, 0, t, 0))
    shape = lambda n: jax.ShapeDtypeStruct((b, n, s, GQA_HEAD_DIM), BF16)
    t_out = lambda n: pl.BlockSpec((1, n, GQA_HEAD_DIM, ts), lambda i, t: (i, 0, 0, t))
    t_shape = lambda n: jax.ShapeDtypeStruct((b, n, GQA_HEAD_DIM, s), BF16)
    return pl.pallas_call(
        _rope_prep_kernel, grid=(b, s // ts),
        in_specs=[
            pl.BlockSpec((1, ts, width), lambda i, t: (i, t, 0)),
            pl.BlockSpec((128, 1), lambda i, t: (0, 0)),
            pl.BlockSpec((1, 128), lambda i, t: (0, 0)),
            pl.BlockSpec((ts, 128), lambda i, t: (t, 0)),
            pl.BlockSpec((ts, 128), lambda i, t: (t, 0)),
            pl.BlockSpec((128, ts), lambda i, t: (0, t)),
            pl.BlockSpec((128, ts), lambda i, t: (0, t)),
        ],
        out_specs=[t_out(GQA_HEADS), head_out(GQA_KV_HEADS), t_out(GQA_KV_HEADS)],
        out_shape=[t_shape(GQA_HEADS), shape(GQA_KV_HEADS), t_shape(GQA_KV_HEADS)],
        compiler_params=_params(("parallel", "parallel")), name="gqa_norm_rope",
    )(proj3d, qg, kg, cos, sin, cos.T, sin.T)


GQA_KV_CHUNK = 256
GQA_SUM_ROWS = 16


def _gqa_kernel(q_ref, k_ref, vt_ref, o_ref, s_a, s_b, m_a, m_b, *, tq, seq):
    t = pl.program_id(0)
    cols = GQA_REP * tq

    @pl.when(t == 0)
    def _():
        s_b[...] = jnp.zeros(s_b.shape, F32)
        m_b[...] = jnp.zeros(m_b.shape, F32)

    def step(s_cur, m_cur, s_prev, m_prev_ref):
        qt = jnp.concatenate([q_ref[0, r] for r in range(GQA_REP)], axis=1)
        m_prev = m_prev_ref[...]
        m_run = None
        ones = jnp.ones((GQA_SUM_ROWS, GQA_KV_CHUNK), BF16)
        acc = jnp.zeros((GQA_HEAD_DIM + GQA_SUM_ROWS, cols), F32)
        for i in range(seq // GQA_KV_CHUNK):
            rows = slice(i * GQA_KV_CHUNK, (i + 1) * GQA_KV_CHUNK)
            st = _dot(k_ref[0, 0, rows, :], qt)
            s_cur[rows, :] = st
            cm = jnp.max(st, axis=0, keepdims=True)
            m_run = cm if m_run is None else jnp.maximum(m_run, cm)
            p = jnp.exp2(s_prev[rows, :] - m_prev)
            vt_aug = jnp.concatenate([vt_ref[0, 0, :, rows], ones], axis=0)
            acc = acc + _dot(vt_aug, p.astype(BF16))
        m_cur[...] = m_run
        ot = acc[:GQA_HEAD_DIM] * (1.0 / acc[GQA_HEAD_DIM:GQA_HEAD_DIM + 1])
        for r in range(GQA_REP):
            o_ref[0, :, r * GQA_HEAD_DIM:(r + 1) * GQA_HEAD_DIM] = ot[:, r * tq:(r + 1) * tq].T.astype(BF16)

    pl.when(t % 2 == 0)(lambda: step(s_a, m_a, s_b, m_b))
    pl.when(t % 2 == 1)(lambda: step(s_b, m_b, s_a, m_a))


def _gqa_attention(q, k, vt, *, tq):
    b, _, _, s = q.shape
    nq = s // tq
    n_blocks = b * GQA_KV_HEADS * nq
    cols = GQA_REP * tq

    def unravel(u):
        return u // (nq * GQA_KV_HEADS), (u // nq) % GQA_KV_HEADS, u % nq

    def score_block(t):
        return unravel(jnp.minimum(t, n_blocks - 1))

    def finish_block(t):
        return unravel(jnp.maximum(t - 1, 0))

    def q_map(t):
        i, g, j = score_block(t)
        return (i, g, 0, j)

    def k_map(t):
        i, g, _ = score_block(t)
        return (i, g, 0, 0)

    def vt_map(t):
        i, g, _ = finish_block(t)
        return (i, g, 0, 0)

    def o_map(t):
        i, g, j = finish_block(t)
        return (i, j, g)

    kern = functools.partial(_gqa_kernel, tq=tq, seq=s)
    return pl.pallas_call(
        kern, grid=(n_blocks + 1,),
        in_specs=[
            pl.BlockSpec((1, GQA_REP, GQA_HEAD_DIM, tq), q_map),
            pl.BlockSpec((1, 1, s, GQA_HEAD_DIM), k_map),
            pl.BlockSpec((1, 1, GQA_HEAD_DIM, s), vt_map),
        ],
        out_specs=pl.BlockSpec((1, tq, GQA_REP * GQA_HEAD_DIM), o_map),
        out_shape=jax.ShapeDtypeStruct((b, s, GQA_Q_WIDTH), BF16),
        scratch_shapes=[pltpu.VMEM((s, cols), F32), pltpu.VMEM((s, cols), F32),
                        pltpu.VMEM((1, cols), F32), pltpu.VMEM((1, cols), F32)],
        compiler_params=_params(("arbitrary",)), name="gqa_attention",
    )(q, k, vt)


def _even_mixer(x2d, b, s, mix_norm, w_in, q_gain, k_gain, rpb, conv_w, conv_b, dt_bias, a_log, d_skip, out_gain,
                w_out):
    w_main = w_in[:, :EVEN_MAIN_WIDTH].astype(BF16)
    w_dt = jnp.transpose(w_in[:, EVEN_MAIN_WIDTH:].reshape(-1, 2, SSD_GROUPS, SSD_HEADS_PER_GROUP),
                         (2, 1, 3, 0)).reshape(2 * SSD_HEADS, -1).astype(BF16)
    proj, xs, bt, cc, dt_rows = _even_in_proj(x2d, mix_norm.astype(F32)[None, :], w_main, w_dt, conv_w.astype(F32),
                                              conv_b.astype(F32)[None, :], tm=512, seq=s, n_main=EVEN_XBC_OFFSET,
                                              name="even_in_proj")
    proj3d = proj.reshape(b, s, EVEN_XBC_OFFSET)
    na_out = _neighbourhood_attention(proj3d, q_gain, k_gain, rpb)
    ssd_out = _ssd_mixer(proj3d, xs.reshape(b, s, -1), bt, cc.reshape(b, s, -1), dt_rows, dt_bias, a_log, d_skip,
                         out_gain)
    w_out_bf = w_out.astype(BF16)
    return ([na_out.reshape(b * s, NA_WIDTH), ssd_out.reshape(b * s, SSD_D_INNER)],
            [w_out_bf[:NA_WIDTH], w_out_bf[NA_WIDTH:]])


def _odd_mixer(x2d, b, s, mix_norm, w_qkv, q_gain, k_gain, w_out):
    proj = _norm_proj(x2d, mix_norm.astype(F32)[None, :], w_qkv.astype(BF16), tm=512, name="odd_qkv_proj")
    q, k, vt = _rope_prep(proj.reshape(b, s, -1), q_gain, k_gain, ts=512)
    attn = _gqa_attention(q, k, vt, tq=256)
    return [attn.reshape(b * s, GQA_Q_WIDTH)], [w_out.astype(BF16)]


def kernel(x, even_mix_norm, even_w_in, na_q_norm, na_k_norm, na_rel_bias, ssd_conv_w, ssd_conv_b, ssd_dt_bias, ssd_A_log, ssd_D, ssd_out_norm, even_w_out, odd_mix_norm, odd_w_qkv, gqa_q_norm, gqa_k_norm, odd_w_out, ffn_norm, ffn_w13, ffn_w2):
    b, s, d = x.shape
    depth = ffn_norm.shape[0]
    h = x.reshape(b * s, d)
    for layer in range(depth):
        i = layer // 2
        if layer % 2 == 0:
            acts, w_outs = _even_mixer(h, b, s, even_mix_norm[i], even_w_in[i], na_q_norm[i], na_k_norm[i],
                                       na_rel_bias[i], ssd_conv_w[i], ssd_conv_b[i], ssd_dt_bias[i], ssd_A_log[i],
                                       ssd_D[i], ssd_out_norm[i], even_w_out[i])
        else:
            acts, w_outs = _odd_mixer(h, b, s, odd_mix_norm[i], odd_w_qkv[i], gqa_q_norm[i], gqa_k_norm[i],
                                      odd_w_out[i])
        h = _mix_ffn(acts, w_outs, h, ffn_norm[layer].astype(F32)[None, :], ffn_w13[layer].astype(BF16),
                     ffn_w2[layer].astype(BF16), tm=512, name="mix_out_ffn_even" if layer % 2 == 0 else "mix_out_ffn_odd")
    return h.reshape(b, s, d)
```

```python
import functools

import jax
import jax.numpy as jnp
from jax import lax
from jax.experimental import pallas as pl
from jax.experimental.pallas import tpu as pltpu

F32 = jnp.float32
BF16 = jnp.bfloat16

D_MODEL = 1024
GRID_W = 64
NORM_EPS = 1e-6

NA_HEADS = 8
NA_HEAD_DIM = 64
NA_WIDTH = NA_HEADS * NA_HEAD_DIM
NA_KH = 8
NA_KW = 16

SSD_D_INNER = 1024
SSD_HEAD_DIM = 64
SSD_HEADS = 16
SSD_GROUPS = 4
SSD_STATE = 128
SSD_CONV = 4
SSD_CHUNK = 128
SSD_CONV_DIM = SSD_D_INNER + 2 * SSD_GROUPS * SSD_STATE
SSD_GROUP_WIDTH = SSD_D_INNER // SSD_GROUPS
SSD_HEADS_PER_GROUP = SSD_HEADS // SSD_GROUPS

EVEN_MAIN_WIDTH = 3 * NA_WIDTH + SSD_D_INNER + SSD_CONV_DIM
EVEN_XBC_OFFSET = 3 * NA_WIDTH + SSD_D_INNER

GQA_HEADS = 16
GQA_KV_HEADS = 4
GQA_HEAD_DIM = 64
GQA_REP = GQA_HEADS // GQA_KV_HEADS
GQA_Q_WIDTH = GQA_HEADS * GQA_HEAD_DIM
GQA_KV_WIDTH = GQA_KV_HEADS * GQA_HEAD_DIM
ROPE_THETA = 10000.0

FFN_HIDDEN = 2816

V7X_LANES = 128
V7X_VMEM_LIMIT = 56 * 1024 * 1024
MASK_VALUE = -1e30
LOG2E = 1.4426950408889634


def _params(dims):
    return pltpu.CompilerParams(dimension_semantics=dims, vmem_limit_bytes=V7X_VMEM_LIMIT)


def _silu(v):
    return v * (1.0 / (1.0 + jnp.exp(-v)))


def _softplus(v):
    return jnp.maximum(v, 0.0) + jnp.log(1.0 + jnp.exp(-jnp.abs(v)))


def _rms_rows(x, g):
    ms = jnp.mean(x * x, axis=-1, keepdims=True)
    return x * lax.rsqrt(ms + NORM_EPS) * g


def _dot(a, b):
    return jnp.dot(a, b, preferred_element_type=F32)


def _dot_nt(a, b):
    return lax.dot_general(a, b, (((1,), (1,)), ((), ())), preferred_element_type=F32)


PROJ_CHUNK = 512


def _norm_proj_kernel(x_ref, g_ref, w_ref, o_ref):
    xn = _rms_rows(x_ref[...], g_ref[...]).astype(BF16)
    for c in range(o_ref.shape[1] // PROJ_CHUNK):
        cols = slice(c * PROJ_CHUNK, (c + 1) * PROJ_CHUNK)
        o_ref[:, cols] = _dot(xn, w_ref[:, cols]).astype(o_ref.dtype)


CONV_HALO = 8


def _even_in_proj_kernel(x_ref, xp_ref, xn_ref, g_ref, w_ref, wdt_ref, cw_ref, cb_ref, o_ref, ox_ref, obt_ref, oc_ref,
                         odt_ref, *, tiles_per_seq):
    tm = x_ref.shape[0]
    n_main = o_ref.shape[1]
    i = pl.program_id(0) % tiles_per_seq
    gain = g_ref[...]
    xc = _rms_rows(x_ref[...], gain)
    xp = jnp.where(i > 0, _rms_rows(xp_ref[...], gain), 0.0)
    xn = jnp.where(i < tiles_per_seq - 1, _rms_rows(xn_ref[...], gain), 0.0)
    xc_bf = xc.astype(BF16)
    xe_bf = jnp.concatenate([xp, xc, xn], axis=0).astype(BF16)
    odt_ref[...] = _dot_nt(wdt_ref[...], xc_bf)
    n_ext = tm + 2 * CONV_HALO
    left = SSD_CONV // 2
    n_main_chunks = n_main // PROJ_CHUNK
    nx, nb, ncc = (r // PROJ_CHUNK for r in (ox_ref.shape[1], obt_ref.shape[0], oc_ref.shape[1]))
    n_conv_chunks = nx + nb + ncc

    def main_chunk(c):
        cols = slice(c * PROJ_CHUNK, (c + 1) * PROJ_CHUNK)
        o_ref[:, cols] = _dot(xc_bf, w_ref[:, cols]).astype(o_ref.dtype)

    def conv_chunk(c, pr):
        cols = slice(c * PROJ_CHUNK, (c + 1) * PROJ_CHUNK)
        acc = jnp.broadcast_to(cb_ref[:, cols], (tm, PROJ_CHUNK))
        for k in range(SSD_CONV):
            shift = (left - k) % n_ext
            tap = pr if shift == 0 else pltpu.roll(pr, shift, axis=0)
            acc = acc + tap[CONV_HALO:CONV_HALO + tm] * cw_ref[k:k + 1, cols]
        out = _silu(acc)
        if c < nx:
            ox_ref[:, cols] = out.astype(ox_ref.dtype)
        elif c < nx + nb:
            obt_ref[(c - nx) * PROJ_CHUNK:(c - nx + 1) * PROJ_CHUNK, :] = out.T.astype(obt_ref.dtype)
        else:
            oc_ref[:, (c - nx - nb) * PROJ_CHUNK:(c - nx - nb + 1) * PROJ_CHUNK] = out.astype(oc_ref.dtype)

    for c in range(max(n_main_chunks, n_conv_chunks)):
        pr = None
        if c < n_conv_chunks:
            pr = _dot(xe_bf, w_ref[:, n_main + c * PROJ_CHUNK:n_main + (c + 1) * PROJ_CHUNK])
        if c < n_main_chunks:
            main_chunk(c)
        if pr is not None:
            conv_chunk(c, pr)


def _even_in_proj(x2d, g, w, w_dt, conv_w, conv_b, *, tm, seq, n_main, name):
    m, d = x2d.shape
    n = w.shape[1]
    n_x = SSD_D_INNER
    n_bc = SSD_GROUPS * SSD_STATE
    assert n >= n_main + n_x + 2 * n_bc and n_x % PROJ_CHUNK == 0 and n_bc % PROJ_CHUNK == 0
    ns = w_dt.shape[0]
    assert n_main % PROJ_CHUNK == 0 and seq % tm == 0 and tm % CONV_HALO == 0
    halo_per_tile = tm // CONV_HALO
    n_halo_blocks = m // CONV_HALO
    resident = lambda shape: pl.BlockSpec(shape, lambda i: (0, 0), pipeline_mode=pl.Buffered(1))
    kern = functools.partial(_even_in_proj_kernel, tiles_per_seq=seq // tm)
    return pl.pallas_call(
        kern, grid=(m // tm,),
        in_specs=[
            pl.BlockSpec((tm, d), lambda i: (i, 0)),
            pl.BlockSpec((CONV_HALO, d), lambda i: (jnp.maximum(i * halo_per_tile - 1, 0), 0)),
            pl.BlockSpec((CONV_HALO, d), lambda i: (jnp.minimum((i + 1) * halo_per_tile, n_halo_blocks - 1), 0)),
            pl.BlockSpec((1, d), lambda i: (0, 0)),
            resident((d, n)), resident((ns, d)), resident(conv_w.shape), resident(conv_b.shape),
        ],
        out_specs=[pl.BlockSpec((tm, n_main), lambda i: (i, 0)), pl.BlockSpec((tm, n_x), lambda i: (i, 0)),
                   pl.BlockSpec((n_bc, tm), lambda i: (0, i)), pl.BlockSpec((tm, n_bc), lambda i: (i, 0)),
                   pl.BlockSpec((ns, tm), lambda i: (0, i))],
        out_shape=[jax.ShapeDtypeStruct((m, n_main), BF16), jax.ShapeDtypeStruct((m, n_x), BF16),
                   jax.ShapeDtypeStruct((n_bc, m), BF16), jax.ShapeDtypeStruct((m, n_bc), BF16),
                   jax.ShapeDtypeStruct((ns, m), F32)],
        compiler_params=_params(("parallel",)), name=name,
    )(x2d, x2d, x2d, g, w, w_dt, conv_w, conv_b)


def _norm_proj(x2d, g, w, *, tm, name):
    m, d = x2d.shape
    n = w.shape[1]
    assert n % PROJ_CHUNK == 0
    resident = lambda shape: pl.BlockSpec(shape, lambda i: (0, 0), pipeline_mode=pl.Buffered(1))
    return pl.pallas_call(
        _norm_proj_kernel, grid=(m // tm,),
        in_specs=[pl.BlockSpec((tm, d), lambda i: (i, 0)), pl.BlockSpec((1, d), lambda i: (0, 0)), resident((d, n))],
        out_specs=pl.BlockSpec((tm, n), lambda i: (i, 0)), out_shape=jax.ShapeDtypeStruct((m, n), BF16),
        compiler_params=_params(("parallel",)), name=name)(x2d, g, w)


NA_PREP_ROWS = 256
NA_GROUP_ROWS = 4
NA_WIN_ROWS = NA_KH + NA_GROUP_ROWS
NA_DY = 2 * NA_KH - 1
NA_DX = 2 * NA_KW - 1


def _na_group_plan(rows):
    sigs, starts, classes = [], [], []
    for gq in range(rows // NA_GROUP_ROWS):
        ks = min(max(gq * NA_GROUP_ROWS - NA_KH // 2, 0), rows - NA_WIN_ROWS)
        sig = tuple((min(max(r - NA_KH // 2, 0), rows - NA_KH) - ks, r - ks)
                    for r in range(gq * NA_GROUP_ROWS, (gq + 1) * NA_GROUP_ROWS))
        assert all(0 <= first and first + NA_KH <= NA_WIN_ROWS for first, _ in sig)
        if sig not in sigs:
            sigs.append(sig)
        starts.append(ks)
        classes.append(sigs.index(sig))
    return sigs, starts, classes


def _pair_head_rms(x, g):
    lo = lax.broadcasted_iota(jnp.int32, x.shape, 1) < NA_HEAD_DIM
    x2 = x * x
    s_lo = jnp.sum(jnp.where(lo, x2, 0.0), axis=-1, keepdims=True)
    s_hi = jnp.sum(jnp.where(lo, 0.0, x2), axis=-1, keepdims=True)
    ms = jnp.where(lo, s_lo, s_hi) * (1.0 / NA_HEAD_DIM)
    return x * lax.rsqrt(ms + NORM_EPS) * g


def _na_kernel(plan_ref, q_ref, k_ref, v_ref, qg_ref, kg_ref, bias_ref, o_ref, q_s, k_s, *, rows):
    scale = NA_HEAD_DIM ** -0.5 * LOG2E

    same_head = (lax.broadcasted_iota(jnp.int32, (V7X_LANES, V7X_LANES), 0) // NA_HEAD_DIM
                 == lax.broadcasted_iota(jnp.int32, (V7X_LANES, V7X_LANES), 1) // NA_HEAD_DIM)
    head_ones = jnp.where(same_head, 1.0, 0.0).astype(BF16)

    def head_rms(x, g):
        ms = _dot((x * x).astype(BF16), head_ones) * (1.0 / NA_HEAD_DIM)
        return x * lax.rsqrt(ms + NORM_EPS) * g

    def prep(i, carry):
        sl = pl.ds(pl.multiple_of(i * NA_PREP_ROWS, NA_PREP_ROWS), NA_PREP_ROWS)
        q_s[sl, :] = (head_rms(q_ref[0, sl, :].astype(F32), qg_ref[...]) * scale).astype(BF16)
        k_s[sl, :] = head_rms(k_ref[0, sl, :].astype(F32), kg_ref[...]).astype(BF16)
        return carry

    lax.fori_loop(0, (rows * GRID_W) // NA_PREP_ROWS, prep, 0, unroll=4)

    n_q = NA_GROUP_ROWS * GRID_W
    n_keys = NA_WIN_ROWS * GRID_W

    def group_body(gq, carry):
        ks = plan_ref[0, gq]
        cls = plan_ref[1, gq]
        qsl = pl.ds(pl.multiple_of(gq * n_q, n_q), n_q)
        ksl = pl.ds(pl.multiple_of(ks * GRID_W, GRID_W), n_keys)
        q = q_s[qsl, :]
        kk = k_s[ksl, :]
        vv = v_ref[0, ksl, :]
        outs = []
        for h in range(2):
            hs = slice(h * NA_HEAD_DIM, (h + 1) * NA_HEAD_DIM)
            s = _dot_nt(q[:, hs], kk[:, hs]) + bias_ref[0, cls, h]
            m = jnp.max(s, axis=-1, keepdims=True)
            p = jnp.exp2(s - m)
            l = jnp.sum(p, axis=-1, keepdims=True)
            outs.append(_dot(p.astype(BF16), vv[:, hs]) * (1.0 / l))
        o_ref[0, qsl, :] = jnp.concatenate(outs, axis=-1).astype(BF16)
        return carry

    lax.fori_loop(0, rows // NA_GROUP_ROWS, group_body, 0, unroll=4)


def _na_bias_kernel(rpb_ref, o_ref, t_s, *, sigs):
    h = pl.program_id(0)
    q = lax.broadcasted_iota(jnp.int32, (GRID_W, GRID_W), 0)
    k = lax.broadcasted_iota(jnp.int32, (GRID_W, GRID_W), 1)
    dx = jnp.clip(k - q, -(NA_KW - 1), NA_KW - 1) + (NA_KW - 1)
    col_start = jnp.clip(q - NA_KW // 2, 0, GRID_W - NA_KW)
    in_win = (k >= col_start) & (k < col_start + NA_KW)
    masked = jnp.full((GRID_W, GRID_W), MASK_VALUE, F32)
    for dy in range(NA_DY):
        base = (h * NA_DY + dy) * NA_DX
        t = masked
        for d in range(NA_DX):
            t = jnp.where(dx == d, rpb_ref[base + d] * LOG2E, t)
        t_s[dy] = jnp.where(in_win, t, MASK_VALUE)
    for cls, sig in enumerate(sigs):
        for rq, (first, qrow) in enumerate(sig):
            for jk in range(NA_WIN_ROWS):
                attended = first <= jk < first + NA_KH
                tile = t_s[jk - qrow + NA_KH - 1] if attended else masked
                o_ref[0, cls, 0, rq * GRID_W:(rq + 1) * GRID_W, jk * GRID_W:(jk + 1) * GRID_W] = tile


def _na_bias_table(rpb, sigs):
    n_q = NA_GROUP_ROWS * GRID_W
    n_keys = NA_WIN_ROWS * GRID_W
    kern = functools.partial(_na_bias_kernel, sigs=sigs)
    return pl.pallas_call(
        kern, grid=(NA_HEADS,),
        in_specs=[pl.BlockSpec(memory_space=pltpu.SMEM)],
        out_specs=pl.BlockSpec((1, len(sigs), 1, n_q, n_keys), lambda h: (h // 2, 0, h % 2, 0, 0)),
        out_shape=jax.ShapeDtypeStruct((NA_HEADS // 2, len(sigs), 2, n_q, n_keys), F32),
        scratch_shapes=[pltpu.VMEM((NA_DY, GRID_W, GRID_W), F32)],
        compiler_params=_params(("parallel",)), name="na_bias_table",
    )(rpb.astype(F32).reshape(-1))


def _neighbourhood_attention(proj3d, q_gain, k_gain, rpb):
    b, s, _ = proj3d.shape
    rows = s // GRID_W
    assert rows >= NA_WIN_ROWS and rows % NA_GROUP_ROWS == 0
    sigs, starts, classes = _na_group_plan(rows)
    bias = _na_bias_table(rpb, sigs)
    plan = jnp.array([starts, classes], jnp.int32)
    qg = jnp.tile(q_gain.astype(F32), 2)[None, :]
    kg = jnp.tile(k_gain.astype(F32), 2)[None, :]
    n_pairs = NA_HEADS // 2
    blk = (1, s, 2 * NA_HEAD_DIM)
    kern = functools.partial(_na_kernel, rows=rows)
    return pl.pallas_call(
        kern, grid=(n_pairs, b),
        in_specs=[
            pl.BlockSpec(memory_space=pltpu.SMEM),
            pl.BlockSpec(blk, lambda p, i: (i, 0, p)),
            pl.BlockSpec(blk, lambda p, i: (i, 0, n_pairs + p)),
            pl.BlockSpec(blk, lambda p, i: (i, 0, 2 * n_pairs + p)),
            pl.BlockSpec((1, 2 * NA_HEAD_DIM), lambda p, i: (0, 0)),
            pl.BlockSpec((1, 2 * NA_HEAD_DIM), lambda p, i: (0, 0)),
            pl.BlockSpec((1,) + bias.shape[1:], lambda p, i: (p, 0, 0, 0, 0)),
        ],
        out_specs=pl.BlockSpec(blk, lambda p, i: (i, 0, p)),
        out_shape=jax.ShapeDtypeStruct((b, s, NA_WIDTH), BF16),
        scratch_shapes=[pltpu.VMEM((s, 2 * NA_HEAD_DIM), BF16)] * 2,
        compiler_params=_params(("parallel", "parallel")), name="neighbourhood_attention",
    )(plan, proj3d, proj3d, proj3d, qg, kg, bias)


def _chunk_scan(a, reverse):
    n = a.shape[1]
    pos = lax.broadcasted_iota(jnp.int32, a.shape, 1) % SSD_CHUNK
    sh = 1
    while sh < SSD_CHUNK:
        if reverse:
            a = a + jnp.where(pos < SSD_CHUNK - sh, pltpu.roll(a, n - sh, axis=1), 0.0)
        else:
            a = a + jnp.where(pos >= sh, pltpu.roll(a, sh, axis=1), 0.0)
        sh *= 2
    return a


def _head_row(tile, first):
    lo = lax.broadcasted_iota(jnp.int32, (1, V7X_LANES), 1) < SSD_HEAD_DIM
    halves = [jnp.where(lo, tile[first + 2 * i:first + 2 * i + 1, :], tile[first + 2 * i + 1:first + 2 * i + 2, :])
              for i in range(SSD_HEADS_PER_GROUP // 2)]
    return jnp.concatenate(halves, axis=1)


def _ssd_kernel(xs_ref, bt_ref, c_ref, z_ref, dtr_ref, biasr_ref, alogr_ref, dskip_ref, gain_ref, o_ref,
                cumr_s, g2r_s, ld2r_s, wr_s, decr_s, st_s, h_s, *, n_chunks):
    L = SSD_CHUNK
    hpg = SSD_HEADS_PER_GROUP
    gw = SSD_GROUP_WIDTH
    ns = SSD_STATE

    dt = _softplus(dtr_ref[...] + biasr_ref[0])
    a = dt * (-jnp.exp(alogr_ref[0]))
    is_fwd = lax.broadcasted_iota(jnp.int32, a.shape, 0) < hpg
    prefix = _chunk_scan(a, False)
    suffix = _chunk_scan(a, True)
    cum2 = jnp.where(is_fwd, prefix, suffix) * LOG2E
    cumr_s[...] = cum2
    g2r_s[...] = cum2 - jnp.log2(dt)
    ld2r_s[...] = jnp.log2(dt + pltpu.roll(dt, hpg, axis=0))
    wr_s[...] = dt * jnp.exp(jnp.where(is_fwd, suffix, prefix) - a)
    decr_s[...] = jnp.exp(prefix + suffix - a)

    def chunk_slice(c):
        return pl.ds(pl.multiple_of(c * L, L), L)

    lane_head = lax.broadcasted_iota(jnp.int32, (L, gw), 1) // SSD_HEAD_DIM

    def block_diag_x(xs):
        return jnp.concatenate([jnp.where(lane_head == j, xs, jnp.zeros_like(xs)) for j in range(hpg)], axis=0)

    def state_body(c, carry):
        sl = chunk_slice(c)
        bt = bt_ref[:, sl].astype(F32)
        w = wr_s[:, sl]
        lhs = jnp.concatenate(
            [jnp.concatenate([(bt * w[d * hpg + j:d * hpg + j + 1, :]).astype(BF16) for j in range(hpg)], axis=1)
             for d in range(2)], axis=0)
        st_s[c] = _dot(lhs, block_diag_x(xs_ref[0, sl, :]))
        return carry

    lax.fori_loop(0, n_chunks, state_body, 0, unroll=8)

    def fwd_rec(c, h):
        h_s[c, :, 0:gw] = h.astype(BF16)
        return h * _head_row(decr_s[:, chunk_slice(c)], 0) + st_s[c, 0:ns, :]

    def bwd_rec(i, h):
        c = n_chunks - 1 - i
        h_s[c, :, gw:2 * gw] = h.astype(BF16)
        return h * _head_row(decr_s[:, chunk_slice(c)], hpg) + st_s[c, ns:2 * ns, :]

    h0 = jnp.zeros((ns, gw), F32)
    lax.fori_loop(0, n_chunks, fwd_rec, h0)
    lax.fori_loop(0, n_chunks, bwd_rec, h0)

    li = lax.broadcasted_iota(jnp.int32, (L, L), 0)
    si = lax.broadcasted_iota(jnp.int32, (L, L), 1)
    below = si < li
    above = si > li
    lane_lo = lax.broadcasted_iota(jnp.int32, (L, V7X_LANES), 1) < SSD_HEAD_DIM

    def out_body(c, carry):
        sl = chunk_slice(c)
        xs = xs_ref[0, sl, :]
        cm = c_ref[0, sl, :]
        cum_r = cumr_s[:, sl]
        g2 = g2r_s[:, sl]
        ld2 = ld2r_s[:, sl]
        cum_t = [jnp.broadcast_to(cum_r[k:k + 1, :], (L, L)).T for k in range(2 * hpg)]
        cb = _dot(cm, bt_ref[:, sl])
        mats = []
        for j in range(hpg):
            seg_f = cum_t[j] - g2[j:j + 1, :]
            seg_b = cum_t[hpg + j] - g2[hpg + j:hpg + j + 1, :]
            arg = jnp.where(below, seg_f, jnp.where(above, seg_b, ld2[j:j + 1, :]))
            mats.append((cb * jnp.exp2(arg)).astype(BF16))
        y = _dot(jnp.concatenate(mats, axis=1), block_diag_x(xs))
        carried = _dot(cm, h_s[c])
        for d in range(2):
            decay = jnp.exp2(jnp.concatenate(
                [jnp.where(lane_lo, cum_t[d * hpg + 2 * i], cum_t[d * hpg + 2 * i + 1]) for i in range(hpg // 2)],
                axis=1))
            y = y + carried[:, d * gw:(d + 1) * gw] * decay
        y = y + dskip_ref[0] * xs.astype(F32)
        gated = y * _silu(z_ref[0, sl, :].astype(F32))
        ms = jnp.mean(gated * gated, axis=-1, keepdims=True)
        o_ref[0, sl, :] = (gated * lax.rsqrt(ms + NORM_EPS) * gain_ref[0]).astype(BF16)
        return carry

    lax.fori_loop(0, n_chunks, out_body, 0, unroll=8)


def _group_major(v):
    return jnp.transpose(v.astype(F32).reshape(2, SSD_GROUPS, SSD_HEADS_PER_GROUP), (1, 0, 2)).reshape(
        SSD_GROUPS, 2 * SSD_HEADS_PER_GROUP)


def _ssd_mixer(proj3d, xs3d, bt2d, c3d, dt_rows, dt_bias, a_log, d_skip, out_gain):
    b, s, _ = proj3d.shape
    n_chunks = s // SSD_CHUNK
    g, hpg, gw, ns = SSD_GROUPS, SSD_HEADS_PER_GROUP, SSD_GROUP_WIDTH, SSD_STATE
    bias_g = _group_major(dt_bias)[:, :, None]
    alog_g = _group_major(a_log)[:, :, None]
    dskip = jnp.repeat(d_skip.astype(F32), SSD_HEAD_DIM).reshape(g, 1, gw)
    gain = out_gain.astype(F32).reshape(g, 1, gw)
    z_blk = (3 * NA_WIDTH) // gw
    kern = functools.partial(_ssd_kernel, n_chunks=n_chunks)
    small = lambda shape: pl.BlockSpec((1,) + shape, lambda i, k: (k, 0, 0))
    seq_blk = lambda width, blk0: pl.BlockSpec((1, s, width), lambda i, k: (i, 0, blk0 + k))
    row_scratch = pltpu.VMEM((2 * hpg, s), F32)
    return pl.pallas_call(
        kern, grid=(b, g),
        in_specs=[
            seq_blk(gw, 0), pl.BlockSpec((ns, s), lambda i, k: (k, i)), seq_blk(ns, 0), seq_blk(gw, z_blk),
            pl.BlockSpec((2 * hpg, s), lambda i, k: (k, i)),
            small((2 * hpg, 1)), small((2 * hpg, 1)), small((1, gw)), small((1, gw)),
        ],
        out_specs=pl.BlockSpec((1, s, gw), lambda i, k: (i, 0, k)),
        out_shape=jax.ShapeDtypeStruct((b, s, SSD_D_INNER), BF16),
        scratch_shapes=[
            row_scratch, row_scratch, row_scratch, row_scratch, row_scratch,
            pltpu.VMEM((n_chunks, 2 * ns, gw), F32), pltpu.VMEM((n_chunks, ns, 2 * gw), BF16),
        ],
        compiler_params=_params(("parallel", "parallel")), name="ssd_bidirectional",
    )(xs3d, bt2d, c3d, proj3d, dt_rows, bias_g, alog_g, dskip, gain)


FFN_CHUNK = 256


def _mix_ffn_kernel(*refs, n_acts, hidden):
    act_refs = refs[:n_acts]
    wout_refs = refs[n_acts:2 * n_acts]
    x_ref, g_ref, w13_ref, w2_ref, o_ref, hid_ref = refs[2 * n_acts:]
    h = x_ref[...]
    for a_ref, w_ref in zip(act_refs, wout_refs):
        h = h + _dot(a_ref[...], w_ref[...])
    hn = _rms_rows(h, g_ref[...]).astype(BF16)
    for c in range(hidden // FFN_CHUNK):
        gate = slice(c * FFN_CHUNK, (c + 1) * FFN_CHUNK)
        up = slice(hidden + c * FFN_CHUNK, hidden + (c + 1) * FFN_CHUNK)
        hid_ref[:, gate] = (_silu(_dot(hn, w13_ref[:, gate])) * _dot(hn, w13_ref[:, up])).astype(BF16)
    o_ref[...] = h + _dot(hid_ref[...], w2_ref[...])


def _mix_ffn(acts, w_outs, x2d, g, w13, w2, *, tm, name):
    m, d = x2d.shape
    hid = w2.shape[0]
    assert hid % FFN_CHUNK == 0
    row = lambda width: pl.BlockSpec((tm, width), lambda i: (i, 0))
    resident = lambda shape: pl.BlockSpec(shape, lambda i: (0, 0), pipeline_mode=pl.Buffered(1))
    return pl.pallas_call(
        functools.partial(_mix_ffn_kernel, n_acts=len(acts), hidden=hid), grid=(m // tm,),
        in_specs=([row(a.shape[1]) for a in acts] + [resident(w.shape) for w in w_outs]
                  + [row(d), pl.BlockSpec((1, d), lambda i: (0, 0)), resident((d, 2 * hid)), resident((hid, d))]),
        out_specs=row(d),
        out_shape=jax.ShapeDtypeStruct((m, d), F32),
        scratch_shapes=[pltpu.VMEM((tm, hid), BF16)],
        compiler_params=_params(("parallel",)), name=name,
    )(*acts, *w_outs, x2d, g, w13, w2)


def _rope_prep_kernel(p_ref, qg_ref, kg_ref, cos_ref, sin_ref, cost_ref, sint_ref, q_ref, k_ref, v_ref):
    scale = GQA_HEAD_DIM ** -0.5 * LOG2E
    ts = p_ref.shape[1]
    cos_t = cost_ref[...]
    sin_t = sint_ref[...]
    even_row = (lax.broadcasted_iota(jnp.int32, (2 * GQA_HEAD_DIM, ts), 0) % 2) == 0
    q_gain = qg_ref[...] * scale
    hd = GQA_HEAD_DIM
    for pair in range(GQA_HEADS // 2):
        xt = p_ref[0, :, pair * 128:(pair + 1) * 128].astype(F32).T
        x2 = xt * xt
        inv = [lax.rsqrt(jnp.mean(x2[h * hd:(h + 1) * hd], axis=0, keepdims=True) + NORM_EPS) for h in range(2)]
        xn = jnp.concatenate([xt[:hd] * inv[0], xt[hd:] * inv[1]], axis=0) * q_gain
        swapped = jnp.where(even_row, pltpu.roll(xn, 2 * hd - 1, axis=0), pltpu.roll(xn, 1, axis=0))
        out = xn * cos_t + swapped * sin_t
        q_ref[0, 2 * pair] = out[:hd].astype(BF16)
        q_ref[0, 2 * pair + 1] = out[hd:].astype(BF16)

    cos = cos_ref[...]
    sin = sin_ref[...]
    even = (lax.broadcasted_iota(jnp.int32, cos.shape, 1) % 2) == 0
    for pair in range(GQA_KV_HEADS // 2):
        c0 = GQA_Q_WIDTH + pair * 128
        xn = _pair_head_rms(p_ref[0, :, c0:c0 + 128].astype(F32), kg_ref[...])
        swapped = jnp.where(even, pltpu.roll(xn, V7X_LANES - 1, axis=1), pltpu.roll(xn, 1, axis=1))
        blk = xn * cos + swapped * sin
        k_ref[0, 2 * pair] = blk[:, :hd].astype(BF16)
        k_ref[0, 2 * pair + 1] = blk[:, hd:].astype(BF16)
        c1 = GQA_Q_WIDTH + GQA_KV_WIDTH + pair * 128
        vt = p_ref[0, :, c1:c1 + 128].astype(F32).T.astype(BF16)
        v_ref[0, 2 * pair] = vt[:hd]
        v_ref[0, 2 * pair + 1] = vt[hd:]


def _axial_rope_tables(s):
    t = jnp.arange(s)
    row = (t // GRID_W).astype(F32)
    col = (t % GRID_W).astype(F32)
    axis_dims = GQA_HEAD_DIM // 2
    freqs = ROPE_THETA ** (-jnp.arange(0, axis_dims, 2, dtype=F32) / axis_dims)
    ang = jnp.concatenate([row[:, None] * freqs, col[:, None] * freqs], axis=-1)
    cos = jnp.repeat(jnp.cos(ang), 2, axis=-1)
    sin = jnp.stack([-jnp.sin(ang), jnp.sin(ang)], axis=-1).reshape(s, GQA_HEAD_DIM)
    return jnp.tile(cos, (1, 2)), jnp.tile(sin, (1, 2))


def _rope_prep(proj3d, q_gain, k_gain, *, ts):
    b, s, width = proj3d.shape
    cos, sin = _axial_rope_tables(s)
    qg = jnp.tile(q_gain.astype(F32), 2)[:, None]
    kg = jnp.tile(k_gain.astype(F32), 2)[None, :]
    head_out = lambda n: pl.BlockSpec((1, n, ts, GQA_HEAD_DIM), lambda i, t: (i, 0, t, 0))
    shape = lambda n: jax.ShapeDtypeStruct((b, n, s, GQA_HEAD_DIM), BF16)
    t_out = lambda n: pl.BlockSpec((1, n, GQA_HEAD_DIM, ts), lambda i, t: (i, 0, 0, t))
    t_shape = lambda n: jax.ShapeDtypeStruct((b, n, GQA_HEAD_DIM, s), BF16)
    return pl.pallas_call(
        _rope_prep_kernel, grid=(b, s // ts),
        in_specs=[
            pl.BlockSpec((1, ts, width), lambda i, t: (i, t, 0)),
            pl.BlockSpec((128, 1), lambda i, t: (0, 0)),
            pl.BlockSpec((1, 128), lambda i, t: (0, 0)),
            pl.BlockSpec((ts, 128), lambda i, t: (t, 0)),
            pl.BlockSpec((ts, 128), lambda i, t: (t, 0)),
            pl.BlockSpec((128, ts), lambda i, t: (0, t)),
            pl.BlockSpec((128, ts), lambda i, t: (0, t)),
        ],
        out_specs=[t_out(GQA_HEADS), head_out(GQA_KV_HEADS), t_out(GQA_KV_HEADS)],
        out_shape=[t_shape(GQA_HEADS), shape(GQA_KV_HEADS), t_shape(GQA_KV_HEADS)],
        compiler_params=_params(("parallel", "parallel")), name="gqa_norm_rope",
    )(proj3d, qg, kg, cos, sin, cos.T, sin.T)


GQA_KV_CHUNK = 256
GQA_SUM_ROWS = 16


def _gqa_kernel(q_ref, k_ref, vt_ref, o_ref, s_a, s_b, m_a, m_b, *, tq, seq):
    t = pl.program_id(0)
    cols = GQA_REP * tq

    @pl.when(t == 0)
    def _():
        s_b[...] = jnp.zeros(s_b.shape, F32)
        m_b[...] = jnp.zeros(m_b.shape, F32)

    def step(s_cur, m_cur, s_prev, m_prev_ref):
        qt = jnp.concatenate([q_ref[0, r] for r in range(GQA_REP)], axis=1)
        m_prev = m_prev_ref[...]
        m_run = None
        ones = jnp.ones((GQA_SUM_ROWS, GQA_KV_CHUNK), BF16)
        acc = jnp.zeros((GQA_HEAD_DIM + GQA_SUM_ROWS, cols), F32)
        for i in range(seq // GQA_KV_CHUNK):
            rows = slice(i * GQA_KV_CHUNK, (i + 1) * GQA_KV_CHUNK)
            st = _dot(k_ref[0, 0, rows, :], qt)
            s_cur[rows, :] = st
            cm = jnp.max(st, axis=0, keepdims=True)
            m_run = cm if m_run is None else jnp.maximum(m_run, cm)
            p = jnp.exp2(s_prev[rows, :] - m_prev)
            vt_aug = jnp.concatenate([vt_ref[0, 0, :, rows], ones], axis=0)
            acc = acc + _dot(vt_aug, p.astype(BF16))
        m_cur[...] = m_run
        ot = acc[:GQA_HEAD_DIM] * (1.0 / acc[GQA_HEAD_DIM:GQA_HEAD_DIM + 1])
        for r in range(GQA_REP):
            o_ref[0, :, r * GQA_HEAD_DIM:(r + 1) * GQA_HEAD_DIM] = ot[:, r * tq:(r + 1) * tq].T.astype(BF16)

    pl.when(t % 2 == 0)(lambda: step(s_a, m_a, s_b, m_b))
    pl.when(t % 2 == 1)(lambda: step(s_b, m_b, s_a, m_a))


def _gqa_attention(q, k, vt, *, tq):
    b, _, _, s = q.shape
    nq = s // tq
    n_blocks = b * GQA_KV_HEADS * nq
    cols = GQA_REP * tq

    def unravel(u):
        return u // (nq * GQA_KV_HEADS), (u // nq) % GQA_KV_HEADS, u % nq

    def score_block(t):
        return unravel(jnp.minimum(t, n_blocks - 1))

    def finish_block(t):
        return unravel(jnp.maximum(t - 1, 0))

    def q_map(t):
        i, g, j = score_block(t)
        return (i, g, 0, j)

    def k_map(t):
        i, g, _ = score_block(t)
        return (i, g, 0, 0)

    def vt_map(t):
        i, g, _ = finish_block(t)
        return (i, g, 0, 0)

    def o_map(t):
        i, g, j = finish_block(t)
        return (i, j, g)

    kern = functools.partial(_gqa_kernel, tq=tq, seq=s)
    return pl.pallas_call(
        kern, grid=(n_blocks + 1,),
        in_specs=[
            pl.BlockSpec((1, GQA_REP, GQA_HEAD_DIM, tq), q_map),
            pl.BlockSpec((1, 1, s, GQA_HEAD_DIM), k_map),
            pl.BlockSpec((1, 1, GQA_HEAD_DIM, s), vt_map),
        ],
        out_specs=pl.BlockSpec((1, tq, GQA_REP * GQA_HEAD_DIM), o_map),
        out_shape=jax.ShapeDtypeStruct((b, s, GQA_Q_WIDTH), BF16),
        scratch_shapes=[pltpu.VMEM((s, cols), F32), pltpu.VMEM((s, cols), F32),
                        pltpu.VMEM((1, cols), F32), pltpu.VMEM((1, cols), F32)],
        compiler_params=_params(("arbitrary",)), name="gqa_attention",
    )(q, k, vt)


def _even_mixer(x2d, b, s, mix_norm, w_in, q_gain, k_gain, rpb, conv_w, conv_b, dt_bias, a_log, d_skip, out_gain,
                w_out):
    w_main = w_in.astype(BF16)
    w_dt = jnp.transpose(w_in[:, EVEN_MAIN_WIDTH:].reshape(-1, 2, SSD_GROUPS, SSD_HEADS_PER_GROUP),
                         (2, 1, 3, 0)).reshape(2 * SSD_HEADS, -1).astype(BF16)
    proj, xs, bt, cc, dt_rows = _even_in_proj(x2d, mix_norm.astype(F32)[None, :], w_main, w_dt, conv_w.astype(F32),
                                              conv_b.astype(F32)[None, :], tm=1024, seq=s, n_main=EVEN_XBC_OFFSET,
                                              name="even_in_proj")
    proj3d = proj.reshape(b, s, EVEN_XBC_OFFSET)
    na_out = _neighbourhood_attention(proj3d, q_gain, k_gain, rpb)
    ssd_out = _ssd_mixer(proj3d, xs.reshape(b, s, -1), bt, cc.reshape(b, s, -1), dt_rows, dt_bias, a_log, d_skip,
                         out_gain)
    w_out_bf = w_out.astype(BF16)
    return ([na_out.reshape(b * s, NA_WIDTH), ssd_out.reshape(b * s, SSD_D_INNER)],
            [w_out_bf[:NA_WIDTH], w_out_bf[NA_WIDTH:]])


def _odd_mixer(x2d, b, s, mix_norm, w_qkv, q_gain, k_gain, w_out):
    proj = _norm_proj(x2d, mix_norm.astype(F32)[None, :], w_qkv.astype(BF16), tm=512, name="odd_qkv_proj")
    q, k, vt = _rope_prep(proj.reshape(b, s, -1), q_gain, k_gain, ts=512)
    attn = _gqa_attention(q, k, vt, tq=256)
    return [attn.reshape(b * s, GQA_Q_WIDTH)], [w_out.astype(BF16)]


def kernel(x, even_mix_norm, even_w_in, na_q_norm, na_k_norm, na_rel_bias, ssd_conv_w, ssd_conv_b, ssd_dt_bias, ssd_A_log, ssd_D, ssd_out_norm, even_w_out, odd_mix_norm, odd_w_qkv, gqa_q_norm, gqa_k_norm, odd_w_out, ffn_norm, ffn_w13, ffn_w2):
    b, s, d = x.shape
    depth = ffn_norm.shape[0]
    h = x.reshape(b * s, d)
    for layer in range(depth):
        i = layer // 2
        if layer % 2 == 0:
            acts, w_outs = _even_mixer(h, b, s, even_mix_norm[i], even_w_in[i], na_q_norm[i], na_k_norm[i],
                                       na_rel_bias[i], ssd_conv_w[i], ssd_conv_b[i], ssd_dt_bias[i], ssd_A_log[i],
                                       ssd_D[i], ssd_out_norm[i], even_w_out[i])
        else:
            acts, w_outs = _odd_mixer(h, b, s, odd_mix_norm[i], odd_w_qkv[i], gqa_q_norm[i], gqa_k_norm[i],
                                      odd_w_out[i])
        h = _mix_ffn(acts, w_outs, h, ffn_norm[layer].astype(F32)[None, :], ffn_w13[layer].astype(BF16),
                     ffn_w2[layer].astype(BF16), tm=512, name="mix_out_ffn_even" if layer % 2 == 0 else "mix_out_ffn_odd")
    return h.reshape(b, s, d)
```

```python
import functools

import jax
import jax.numpy as jnp
from jax import lax
from jax.experimental import pallas as pl
from jax.experimental.pallas import tpu as pltpu

F32 = jnp.float32
BF16 = jnp.bfloat16

D_MODEL = 1024
GRID_W = 64
NORM_EPS = 1e-6

NA_HEADS = 8
NA_HEAD_DIM = 64
NA_WIDTH = NA_HEADS * NA_HEAD_DIM
NA_KH = 8
NA_KW = 16

SSD_D_INNER = 1024
SSD_HEAD_DIM = 64
SSD_HEADS = 16
SSD_GROUPS = 4
SSD_STATE = 128
SSD_CONV = 4
SSD_CHUNK = 128
SSD_CONV_DIM = SSD_D_INNER + 2 * SSD_GROUPS * SSD_STATE
SSD_GROUP_WIDTH = SSD_D_INNER // SSD_GROUPS
SSD_HEADS_PER_GROUP = SSD_HEADS // SSD_GROUPS

EVEN_MAIN_WIDTH = 3 * NA_WIDTH + SSD_D_INNER + SSD_CONV_DIM
EVEN_XBC_OFFSET = 3 * NA_WIDTH + SSD_D_INNER

GQA_HEADS = 16
GQA_KV_HEADS = 4
GQA_HEAD_DIM = 64
GQA_REP = GQA_HEADS // GQA_KV_HEADS
GQA_Q_WIDTH = GQA_HEADS * GQA_HEAD_DIM
GQA_KV_WIDTH = GQA_KV_HEADS * GQA_HEAD_DIM
ROPE_THETA = 10000.0

FFN_HIDDEN = 2816

V7X_LANES = 128
V7X_VMEM_LIMIT = 56 * 1024 * 1024
MASK_VALUE = -1e30
LOG2E = 1.4426950408889634


def _params(dims):
    return pltpu.CompilerParams(dimension_semantics=dims, vmem_limit_bytes=V7X_VMEM_LIMIT)


def _silu(v):
    return v * (1.0 / (1.0 + jnp.exp(-v)))


def _softplus(v):
    return jnp.maximum(v, 0.0) + jnp.log(1.0 + jnp.exp(-jnp.abs(v)))


def _rms_rows(x, g):
    ms = jnp.mean(x * x, axis=-1, keepdims=True)
    return x * lax.rsqrt(ms + NORM_EPS) * g


def _dot(a, b):
    return jnp.dot(a, b, preferred_element_type=F32)


def _dot_nt(a, b):
    return lax.dot_general(a, b, (((1,), (1,)), ((), ())), preferred_element_type=F32)


PROJ_CHUNK = 512


def _norm_proj_kernel(x_ref, g_ref, w_ref, o_ref):
    xn = _rms_rows(x_ref[...], g_ref[...]).astype(BF16)
    for c in range(o_ref.shape[1] // PROJ_CHUNK):
        cols = slice(c * PROJ_CHUNK, (c + 1) * PROJ_CHUNK)
        o_ref[:, cols] = _dot(xn, w_ref[:, cols]).astype(o_ref.dtype)


CONV_HALO = 8


def _even_in_proj_kernel(x_ref, xp_ref, xn_ref, g_ref, w_ref, wdt_ref, cw_ref, cb_ref, o_ref, ox_ref, obt_ref, oc_ref,
                         odt_ref, *, tiles_per_seq):
    tm = x_ref.shape[0]
    n_main = o_ref.shape[1]
    i = pl.program_id(0) % tiles_per_seq
    gain = g_ref[...]
    xc = _rms_rows(x_ref[...], gain)
    xp = jnp.where(i > 0, _rms_rows(xp_ref[...], gain), 0.0)
    xn = jnp.where(i < tiles_per_seq - 1, _rms_rows(xn_ref[...], gain), 0.0)
    xc_bf = xc.astype(BF16)
    xe_bf = jnp.concatenate([xp, xc, xn], axis=0).astype(BF16)
    odt_ref[...] = _dot_nt(wdt_ref[...], xc_bf)
    n_ext = tm + 2 * CONV_HALO
    left = SSD_CONV // 2
    n_main_chunks = n_main // PROJ_CHUNK
    nx, nb, ncc = (r // PROJ_CHUNK for r in (ox_ref.shape[1], obt_ref.shape[0], oc_ref.shape[1]))
    n_conv_chunks = nx + nb + ncc

    def main_chunk(c):
        cols = slice(c * PROJ_CHUNK, (c + 1) * PROJ_CHUNK)
        o_ref[:, cols] = _dot(xc_bf, w_ref[:, cols]).astype(o_ref.dtype)

    def conv_chunk(c, pr):
        cols = slice(c * PROJ_CHUNK, (c + 1) * PROJ_CHUNK)
        acc = jnp.broadcast_to(cb_ref[:, cols], (tm, PROJ_CHUNK))
        for k in range(SSD_CONV):
            shift = (left - k) % n_ext
            tap = pr if shift == 0 else pltpu.roll(pr, shift, axis=0)
            acc = acc + tap[CONV_HALO:CONV_HALO + tm] * cw_ref[k:k + 1, cols]
        out = _silu(acc)
        if c < nx:
            ox_ref[:, cols] = out.astype(ox_ref.dtype)
        elif c < nx + nb:
            obt_ref[(c - nx) * PROJ_CHUNK:(c - nx + 1) * PROJ_CHUNK, :] = out.T.astype(obt_ref.dtype)
        else:
            oc_ref[:, (c - nx - nb) * PROJ_CHUNK:(c - nx - nb + 1) * PROJ_CHUNK] = out.astype(oc_ref.dtype)

    for c in range(max(n_main_chunks, n_conv_chunks)):
        pr = None
        if c < n_conv_chunks:
            pr = _dot(xe_bf, w_ref[:, n_main + c * PROJ_CHUNK:n_main + (c + 1) * PROJ_CHUNK])
        if c < n_main_chunks:
            main_chunk(c)
        if pr is not None:
            conv_chunk(c, pr)


def _even_in_proj(x2d, g, w, w_dt, conv_w, conv_b, *, tm, seq, n_main, name):
    m, d = x2d.shape
    n = w.shape[1]
    n_x = SSD_D_INNER
    n_bc = SSD_GROUPS * SSD_STATE
    assert n >= n_main + n_x + 2 * n_bc and n_x % PROJ_CHUNK == 0 and n_bc % PROJ_CHUNK == 0
    ns = w_dt.shape[0]
    assert n_main % PROJ_CHUNK == 0 and seq % tm == 0 and tm % CONV_HALO == 0
    halo_per_tile = tm // CONV_HALO
    n_halo_blocks = m // CONV_HALO
    resident = lambda shape: pl.BlockSpec(shape, lambda i: (0, 0), pipeline_mode=pl.Buffered(1))
    kern = functools.partial(_even_in_proj_kernel, tiles_per_seq=seq // tm)
    return pl.pallas_call(
        kern, grid=(m // tm,),
        in_specs=[
            pl.BlockSpec((tm, d), lambda i: (i, 0)),
            pl.BlockSpec((CONV_HALO, d), lambda i: (jnp.maximum(i * halo_per_tile - 1, 0), 0)),
            pl.BlockSpec((CONV_HALO, d), lambda i: (jnp.minimum((i + 1) * halo_per_tile, n_halo_blocks - 1), 0)),
            pl.BlockSpec((1, d), lambda i: (0, 0)),
            resident((d, n)), resident((ns, d)), resident(conv_w.shape), resident(conv_b.shape),
        ],
        out_specs=[pl.BlockSpec((tm, n_main), lambda i: (i, 0)), pl.BlockSpec((tm, n_x), lambda i: (i, 0)),
                   pl.BlockSpec((n_bc, tm), lambda i: (0, i)), pl.BlockSpec((tm, n_bc), lambda i: (i, 0)),
                   pl.BlockSpec((ns, tm), lambda i: (0, i))],
        out_shape=[jax.ShapeDtypeStruct((m, n_main), BF16), jax.ShapeDtypeStruct((m, n_x), BF16),
                   jax.ShapeDtypeStruct((n_bc, m), BF16), jax.ShapeDtypeStruct((m, n_bc), BF16),
                   jax.ShapeDtypeStruct((ns, m), F32)],
        compiler_params=_params(("parallel",)), name=name,
    )(x2d, x2d, x2d, g, w, w_dt, conv_w, conv_b)


def _norm_proj(x2d, g, w, *, tm, name):
    m, d = x2d.shape
    n = w.shape[1]
    assert n % PROJ_CHUNK == 0
    resident = lambda shape: pl.BlockSpec(shape, lambda i: (0, 0), pipeline_mode=pl.Buffered(1))
    return pl.pallas_call(
        _norm_proj_kernel, grid=(m // tm,),
        in_specs=[pl.BlockSpec((tm, d), lambda i: (i, 0)), pl.BlockSpec((1, d), lambda i: (0, 0)), resident((d, n))],
        out_specs=pl.BlockSpec((tm, n), lambda i: (i, 0)), out_shape=jax.ShapeDtypeStruct((m, n), BF16),
        compiler_params=_params(("parallel",)), name=name)(x2d, g, w)


NA_PREP_ROWS = 256
NA_GROUP_ROWS = 4
NA_WIN_ROWS = NA_KH + NA_GROUP_ROWS
NA_DY = 2 * NA_KH - 1
NA_DX = 2 * NA_KW - 1


def _na_group_plan(rows):
    sigs, starts, classes = [], [], []
    for gq in range(rows // NA_GROUP_ROWS):
        ks = min(max(gq * NA_GROUP_ROWS - NA_KH // 2, 0), rows - NA_WIN_ROWS)
        sig = tuple((min(max(r - NA_KH // 2, 0), rows - NA_KH) - ks, r - ks)
                    for r in range(gq * NA_GROUP_ROWS, (gq + 1) * NA_GROUP_ROWS))
        assert all(0 <= first and first + NA_KH <= NA_WIN_ROWS for first, _ in sig)
        if sig not in sigs:
            sigs.append(sig)
        starts.append(ks)
        classes.append(sigs.index(sig))
    return sigs, starts, classes


def _pair_head_rms(x, g):
    lo = lax.broadcasted_iota(jnp.int32, x.shape, 1) < NA_HEAD_DIM
    x2 = x * x
    s_lo = jnp.sum(jnp.where(lo, x2, 0.0), axis=-1, keepdims=True)
    s_hi = jnp.sum(jnp.where(lo, 0.0, x2), axis=-1, keepdims=True)
    ms = jnp.where(lo, s_lo, s_hi) * (1.0 / NA_HEAD_DIM)
    return x * lax.rsqrt(ms + NORM_EPS) * g


def _na_kernel(plan_ref, q_ref, k_ref, v_ref, qg_ref, kg_ref, bias_ref, o_ref, q_s, k_s, *, rows):
    scale = NA_HEAD_DIM ** -0.5 * LOG2E

    same_head = (lax.broadcasted_iota(jnp.int32, (V7X_LANES, V7X_LANES), 0) // NA_HEAD_DIM
                 == lax.broadcasted_iota(jnp.int32, (V7X_LANES, V7X_LANES), 1) // NA_HEAD_DIM)
    head_ones = jnp.where(same_head, 1.0, 0.0).astype(BF16)

    def head_rms(x, g):
        ms = _dot((x * x).astype(BF16), head_ones) * (1.0 / NA_HEAD_DIM)
        return x * lax.rsqrt(ms + NORM_EPS) * g

    def prep(i, carry):
        sl = pl.ds(pl.multiple_of(i * NA_PREP_ROWS, NA_PREP_ROWS), NA_PREP_ROWS)
        q_s[sl, :] = (head_rms(q_ref[0, sl, :].astype(F32), qg_ref[...]) * scale).astype(BF16)
        k_s[sl, :] = head_rms(k_ref[0, sl, :].astype(F32), kg_ref[...]).astype(BF16)
        return carry

    lax.fori_loop(0, (rows * GRID_W) // NA_PREP_ROWS, prep, 0, unroll=4)

    n_q = NA_GROUP_ROWS * GRID_W
    n_keys = NA_WIN_ROWS * GRID_W

    def group_body(gq, carry):
        ks = plan_ref[0, gq]
        cls = plan_ref[1, gq]
        qsl = pl.ds(pl.multiple_of(gq * n_q, n_q), n_q)
        ksl = pl.ds(pl.multiple_of(ks * GRID_W, GRID_W), n_keys)
        q = q_s[qsl, :]
        kk = k_s[ksl, :]
        vv = v_ref[0, ksl, :]
        outs = []
        for h in range(2):
            hs = slice(h * NA_HEAD_DIM, (h + 1) * NA_HEAD_DIM)
            s = _dot_nt(q[:, hs], kk[:, hs]) + bias_ref[0, cls, h]
            m = jnp.max(s, axis=-1, keepdims=True)
            p = jnp.exp2(s - m)
            l = jnp.sum(p, axis=-1, keepdims=True)
            outs.append(_dot(p.astype(BF16), vv[:, hs]) * (1.0 / l))
        o_ref[0, qsl, :] = jnp.concatenate(outs, axis=-1).astype(BF16)
        return carry

    lax.fori_loop(0, rows // NA_GROUP_ROWS, group_body, 0, unroll=4)


def _na_bias_kernel(rpb_ref, o_ref, t_s, *, sigs):
    h = pl.program_id(0)
    q = lax.broadcasted_iota(jnp.int32, (GRID_W, GRID_W), 0)
    k = lax.broadcasted_iota(jnp.int32, (GRID_W, GRID_W), 1)
    dx = jnp.clip(k - q, -(NA_KW - 1), NA_KW - 1) + (NA_KW - 1)
    col_start = jnp.clip(q - NA_KW // 2, 0, GRID_W - NA_KW)
    in_win = (k >= col_start) & (k < col_start + NA_KW)
    masked = jnp.full((GRID_W, GRID_W), MASK_VALUE, F32)
    for dy in range(NA_DY):
        base = (h * NA_DY + dy) * NA_DX
        t = masked
        for d in range(NA_DX):
            t = jnp.where(dx == d, rpb_ref[base + d] * LOG2E, t)
        t_s[dy] = jnp.where(in_win, t, MASK_VALUE)
    for cls, sig in enumerate(sigs):
        for rq, (first, qrow) in enumerate(sig):
            for jk in range(NA_WIN_ROWS):
                attended = first <= jk < first + NA_KH
                tile = t_s[jk - qrow + NA_KH - 1] if attended else masked
                o_ref[0, cls, 0, rq * GRID_W:(rq + 1) * GRID_W, jk * GRID_W:(jk + 1) * GRID_W] = tile


def _na_bias_table(rpb, sigs):
    n_q = NA_GROUP_ROWS * GRID_W
    n_keys = NA_WIN_ROWS * GRID_W
    kern = functools.partial(_na_bias_kernel, sigs=sigs)
    return pl.pallas_call(
        kern, grid=(NA_HEADS,),
        in_specs=[pl.BlockSpec(memory_space=pltpu.SMEM)],
        out_specs=pl.BlockSpec((1, len(sigs), 1, n_q, n_keys), lambda h: (h // 2, 0, h % 2, 0, 0)),
        out_shape=jax.ShapeDtypeStruct((NA_HEADS // 2, len(sigs), 2, n_q, n_keys), F32),
        scratch_shapes=[pltpu.VMEM((NA_DY, GRID_W, GRID_W), F32)],
        compiler_params=_params(("parallel",)), name="na_bias_table",
    )(rpb.astype(F32).reshape(-1))


def _neighbourhood_attention(proj3d, q_gain, k_gain, rpb):
    b, s, _ = proj3d.shape
    rows = s // GRID_W
    assert rows >= NA_WIN_ROWS and rows % NA_GROUP_ROWS == 0
    sigs, starts, classes = _na_group_plan(rows)
    bias = _na_bias_table(rpb, sigs)
    plan = jnp.array([starts, classes], jnp.int32)
    qg = jnp.tile(q_gain.astype(F32), 2)[None, :]
    kg = jnp.tile(k_gain.astype(F32), 2)[None, :]
    n_pairs = NA_HEADS // 2
    blk = (1, s, 2 * NA_HEAD_DIM)
    kern = functools.partial(_na_kernel, rows=rows)
    return pl.pallas_call(
        kern, grid=(n_pairs, b),
        in_specs=[
            pl.BlockSpec(memory_space=pltpu.SMEM),
            pl.BlockSpec(blk, lambda p, i: (i, 0, p)),
            pl.BlockSpec(blk, lambda p, i: (i, 0, n_pairs + p)),
            pl.BlockSpec(blk, lambda p, i: (i, 0, 2 * n_pairs + p)),
            pl.BlockSpec((1, 2 * NA_HEAD_DIM), lambda p, i: (0, 0)),
            pl.BlockSpec((1, 2 * NA_HEAD_DIM), lambda p, i: (0, 0)),
            pl.BlockSpec((1,) + bias.shape[1:], lambda p, i: (p, 0, 0, 0, 0)),
        ],
        out_specs=pl.BlockSpec(blk, lambda p, i: (i, 0, p)),
        out_shape=jax.ShapeDtypeStruct((b, s, NA_WIDTH), BF16),
        scratch_shapes=[pltpu.VMEM((s, 2 * NA_HEAD_DIM), BF16)] * 2,
        compiler_params=_params(("parallel", "parallel")), name="neighbourhood_attention",
    )(plan, proj3d, proj3d, proj3d, qg, kg, bias)


def _chunk_scan(a, reverse):
    n = a.shape[1]
    pos = lax.broadcasted_iota(jnp.int32, a.shape, 1) % SSD_CHUNK
    sh = 1
    while sh < SSD_CHUNK:
        if reverse:
            a = a + jnp.where(pos < SSD_CHUNK - sh, pltpu.roll(a, n - sh, axis=1), 0.0)
        else:
            a = a + jnp.where(pos >= sh, pltpu.roll(a, sh, axis=1), 0.0)
        sh *= 2
    return a


def _head_row(tile, first):
    lo = lax.broadcasted_iota(jnp.int32, (1, V7X_LANES), 1) < SSD_HEAD_DIM
    halves = [jnp.where(lo, tile[first + 2 * i:first + 2 * i + 1, :], tile[first + 2 * i + 1:first + 2 * i + 2, :])
              for i in range(SSD_HEADS_PER_GROUP // 2)]
    return jnp.concatenate(halves, axis=1)


def _ssd_kernel(xs_ref, bt_ref, c_ref, z_ref, dtr_ref, biasr_ref, alogr_ref, dskip_ref, gain_ref, o_ref,
                cumr_s, g2r_s, ld2r_s, wr_s, decr_s, st_s, h_s, *, n_chunks):
    L = SSD_CHUNK
    hpg = SSD_HEADS_PER_GROUP
    gw = SSD_GROUP_WIDTH
    ns = SSD_STATE

    dt = _softplus(dtr_ref[...] + biasr_ref[0])
    a = dt * (-jnp.exp(alogr_ref[0]))
    is_fwd = lax.broadcasted_iota(jnp.int32, a.shape, 0) < hpg
    prefix = _chunk_scan(a, False)
    suffix = _chunk_scan(a, True)
    cum2 = jnp.where(is_fwd, prefix, suffix) * LOG2E
    cumr_s[...] = cum2
    g2r_s[...] = cum2 - jnp.log2(dt)
    ld2r_s[...] = jnp.log2(dt + pltpu.roll(dt, hpg, axis=0))
    wr_s[...] = dt * jnp.exp(jnp.where(is_fwd, suffix, prefix) - a)
    decr_s[...] = jnp.exp(prefix + suffix - a)

    def chunk_slice(c):
        return pl.ds(pl.multiple_of(c * L, L), L)

    lane_head = lax.broadcasted_iota(jnp.int32, (L, gw), 1) // SSD_HEAD_DIM

    def block_diag_x(xs):
        return jnp.concatenate([jnp.where(lane_head == j, xs, jnp.zeros_like(xs)) for j in range(hpg)], axis=0)

    def state_body(c, carry):
        sl = chunk_slice(c)
        bt = bt_ref[:, sl].astype(F32)
        w = wr_s[:, sl]
        lhs = jnp.concatenate(
            [jnp.concatenate([(bt * w[d * hpg + j:d * hpg + j + 1, :]).astype(BF16) for j in range(hpg)], axis=1)
             for d in range(2)], axis=0)
        st_s[c] = _dot(lhs, block_diag_x(xs_ref[0, sl, :]))
        return carry

    lax.fori_loop(0, n_chunks, state_body, 0, unroll=8)

    def fwd_rec(c, h):
        h_s[c, :, 0:gw] = h.astype(BF16)
        return h * _head_row(decr_s[:, chunk_slice(c)], 0) + st_s[c, 0:ns, :]

    def bwd_rec(i, h):
        c = n_chunks - 1 - i
        h_s[c, :, gw:2 * gw] = h.astype(BF16)
        return h * _head_row(decr_s[:, chunk_slice(c)], hpg) + st_s[c, ns:2 * ns, :]

    h0 = jnp.zeros((ns, gw), F32)
    lax.fori_loop(0, n_chunks, fwd_rec, h0)
    lax.fori_loop(0, n_chunks, bwd_rec, h0)

    li = lax.broadcasted_iota(jnp.int32, (L, L), 0)
    si = lax.broadcasted_iota(jnp.int32, (L, L), 1)
    below = si < li
    above = si > li
    lane_lo = lax.broadcasted_iota(jnp.int32, (L, V7X_LANES), 1) < SSD_HEAD_DIM

    def out_body(c, carry):
        sl = chunk_slice(c)
        xs = xs_ref[0, sl, :]
        cm = c_ref[0, sl, :]
        cum_r = cumr_s[:, sl]
        g2 = g2r_s[:, sl]
        ld2 = ld2r_s[:, sl]
        cum_t = [jnp.broadcast_to(cum_r[k:k + 1, :], (L, L)).T for k in range(2 * hpg)]
        cb = _dot(cm, bt_ref[:, sl])
        mats = []
        for j in range(hpg):
            seg_f = cum_t[j] - g2[j:j + 1, :]
            seg_b = cum_t[hpg + j] - g2[hpg + j:hpg + j + 1, :]
            arg = jnp.where(below, seg_f, jnp.where(above, seg_b, ld2[j:j + 1, :]))
            mats.append((cb * jnp.exp2(arg)).astype(BF16))
        y = _dot(jnp.concatenate(mats, axis=1), block_diag_x(xs))
        carried = _dot(cm, h_s[c])
        for d in range(2):
            decay = jnp.exp2(jnp.concatenate(
                [jnp.where(lane_lo, cum_t[d * hpg + 2 * i], cum_t[d * hpg + 2 * i + 1]) for i in range(hpg // 2)],
                axis=1))
            y = y + carried[:, d * gw:(d + 1) * gw] * decay
        y = y + dskip_ref[0] * xs.astype(F32)
        gated = y * _silu(z_ref[0, sl, :].astype(F32))
        ms = jnp.mean(gated * gated, axis=-1, keepdims=True)
        o_ref[0, sl, :] = (gated * lax.rsqrt(ms + NORM_EPS) * gain_ref[0]).astype(BF16)
        return carry

    lax.fori_loop(0, n_chunks, out_body, 0, unroll=8)


def _group_major(v):
    return jnp.transpose(v.astype(F32).reshape(2, SSD_GROUPS, SSD_HEADS_PER_GROUP), (1, 0, 2)).reshape(
        SSD_GROUPS, 2 * SSD_HEADS_PER_GROUP)


def _ssd_mixer(proj3d, xs3d, bt2d, c3d, dt_rows, dt_bias, a_log, d_skip, out_gain):
    b, s, _ = proj3d.shape
    n_chunks = s // SSD_CHUNK
    g, hpg, gw, ns = SSD_GROUPS, SSD_HEADS_PER_GROUP, SSD_GROUP_WIDTH, SSD_STATE
    bias_g = _group_major(dt_bias)[:, :, None]
    alog_g = _group_major(a_log)[:, :, None]
    dskip = jnp.repeat(d_skip.astype(F32), SSD_HEAD_DIM).reshape(g, 1, gw)
    gain = out_gain.astype(F32).reshape(g, 1, gw)
    z_blk = (3 * NA_WIDTH) // gw
    kern = functools.partial(_ssd_kernel, n_chunks=n_chunks)
    small = lambda shape: pl.BlockSpec((1,) + shape, lambda i, k: (k, 0, 0))
    seq_blk = lambda width, blk0: pl.BlockSpec((1, s, width), lambda i, k: (i, 0, blk0 + k))
    row_scratch = pltpu.VMEM((2 * hpg, s), F32)
    return pl.pallas_call(
        kern, grid=(b, g),
        in_specs=[
            seq_blk(gw, 0), pl.BlockSpec((ns, s), lambda i, k: (k, i)), seq_blk(ns, 0), seq_blk(gw, z_blk),
            pl.BlockSpec((2 * hpg, s), lambda i, k: (k, i)),
            small((2 * hpg, 1)), small((2 * hpg, 1)), small((1, gw)), small((1, gw)),
        ],
        out_specs=pl.BlockSpec((1, s, gw), lambda i, k: (i, 0, k)),
        out_shape=jax.ShapeDtypeStruct((b, s, SSD_D_INNER), BF16),
        scratch_shapes=[
            row_scratch, row_scratch, row_scratch, row_scratch, row_scratch,
            pltpu.VMEM((n_chunks, 2 * ns, gw), F32), pltpu.VMEM((n_chunks, ns, 2 * gw), BF16),
        ],
        compiler_params=_params(("parallel", "parallel")), name="ssd_bidirectional",
    )(xs3d, bt2d, c3d, proj3d, dt_rows, bias_g, alog_g, dskip, gain)


FFN_CHUNK = 256


def _mix_ffn_kernel(*refs, n_acts, hidden):
    act_refs = refs[:n_acts]
    wout_refs = refs[n_acts:2 * n_acts]
    x_ref, g_ref, w13_ref, w2_ref, o_ref, hid_ref = refs[2 * n_acts:]
    h = x_ref[...]
    for a_ref, w_ref in zip(act_refs, wout_refs):
        h = h + _dot(a_ref[...], w_ref[...])
    hn = _rms_rows(h, g_ref[...]).astype(BF16)
    for c in range(hidden // FFN_CHUNK):
        gate = slice(c * FFN_CHUNK, (c + 1) * FFN_CHUNK)
        up = slice(hidden + c * FFN_CHUNK, hidden + (c + 1) * FFN_CHUNK)
        hid_ref[:, gate] = (_silu(_dot(hn, w13_ref[:, gate])) * _dot(hn, w13_ref[:, up])).astype(BF16)
    o_ref[...] = h + _dot(hid_ref[...], w2_ref[...])


def _mix_ffn(acts, w_outs, x2d, g, w13, w2, *, tm, name):
    m, d = x2d.shape
    hid = w2.shape[0]
    assert hid % FFN_CHUNK == 0
    row = lambda width: pl.BlockSpec((tm, width), lambda i: (i, 0))
    resident = lambda shape: pl.BlockSpec(shape, lambda i: (0, 0), pipeline_mode=pl.Buffered(1))
    return pl.pallas_call(
        functools.partial(_mix_ffn_kernel, n_acts=len(acts), hidden=hid), grid=(m // tm,),
        in_specs=([row(a.shape[1]) for a in acts] + [resident(w.shape) for w in w_outs]
                  + [row(d), pl.BlockSpec((1, d), lambda i: (0, 0)), resident((d, 2 * hid)), resident((hid, d))]),
        out_specs=row(d),
        out_shape=jax.ShapeDtypeStruct((m, d), F32),
        scratch_shapes=[pltpu.VMEM((tm, hid), BF16)],
        compiler_params=_params(("parallel",)), name=name,
    )(*acts, *w_outs, x2d, g, w13, w2)


def _rope_prep_kernel(p_ref, qg_ref, kg_ref, cos_ref, sin_ref, cost_ref, sint_ref, q_ref, k_ref, v_ref):
    scale = GQA_HEAD_DIM ** -0.5 * LOG2E
    ts = p_ref.shape[1]
    cos_t = cost_ref[...]
    sin_t = sint_ref[...]
    even_row3 = (lax.broadcasted_iota(jnp.int32, (2 * GQA_HEAD_DIM // 8, 8, ts), 1) % 2) == 0
    q_gain = qg_ref[...] * scale
    hd = GQA_HEAD_DIM
    for pair in range(GQA_HEADS // 2):
        xt = p_ref[0, :, pair * 128:(pair + 1) * 128].astype(F32).T
        x2 = xt * xt
        inv = [lax.rsqrt(jnp.mean(x2[h * hd:(h + 1) * hd], axis=0, keepdims=True) + NORM_EPS) for h in range(2)]
        xn = jnp.concatenate([xt[:hd] * inv[0], xt[hd:] * inv[1]], axis=0) * q_gain
        x3 = xn.reshape(2 * hd // 8, 8, ts)
        swapped = jnp.where(even_row3, pltpu.roll(x3, 7, axis=1), pltpu.roll(x3, 1, axis=1)).reshape(2 * hd, ts)
        out = xn * cos_t + swapped * sin_t
        q_ref[0, 2 * pair] = out[:hd].astype(BF16)
        q_ref[0, 2 * pair + 1] = out[hd:].astype(BF16)

    cos = cos_ref[...]
    sin = sin_ref[...]
    even = (lax.broadcasted_iota(jnp.int32, cos.shape, 1) % 2) == 0
    for pair in range(GQA_KV_HEADS // 2):
        c0 = GQA_Q_WIDTH + pair * 128
        xn = _pair_head_rms(p_ref[0, :, c0:c0 + 128].astype(F32), kg_ref[...])
        swapped = jnp.where(even, pltpu.roll(xn, V7X_LANES - 1, axis=1), pltpu.roll(xn, 1, axis=1))
        blk = xn * cos + swapped * sin
        k_ref[0, 2 * pair] = blk[:, :hd].astype(BF16)
        k_ref[0, 2 * pair + 1] = blk[:, hd:].astype(BF16)
        c1 = GQA_Q_WIDTH + GQA_KV_WIDTH + pair * 128
        vt = p_ref[0, :, c1:c1 + 128].astype(F32).T.astype(BF16)
        v_ref[0, 2 * pair] = vt[:hd]
        v_ref[0, 2 * pair + 1] = vt[hd:]


def _axial_rope_tables(s):
    t = jnp.arange(s)
    row = (t // GRID_W).astype(F32)
    col = (t % GRID_W).astype(F32)
    axis_dims = GQA_HEAD_DIM // 2
    freqs = ROPE_THETA ** (-jnp.arange(0, axis_dims, 2, dtype=F32) / axis_dims)
    ang = jnp.concatenate([row[:, None] * freqs, col[:, None] * freqs], axis=-1)
    cos = jnp.repeat(jnp.cos(ang), 2, axis=-1)
    sin = jnp.stack([-jnp.sin(ang), jnp.sin(ang)], axis=-1).reshape(s, GQA_HEAD_DIM)
    return jnp.tile(cos, (1, 2)), jnp.tile(sin, (1, 2))


def _rope_prep(proj3d, q_gain, k_gain, *, ts):
    b, s, width = proj3d.shape
    cos, sin = _axial_rope_tables(s)
    qg = jnp.tile(q_gain.astype(F32), 2)[:, None]
    kg = jnp.tile(k_gain.astype(F32), 2)[None, :]
    head_out = lambda n: pl.BlockSpec((1, n, ts, GQA_HEAD_DIM), lambda i, t: (i, 0, t, 0))
    shape = lambda n: jax.ShapeDtypeStruct((b, n, s, GQA_HEAD_DIM), BF16)
    t_out = lambda n: pl.BlockSpec((1, n, GQA_HEAD_DIM, ts), lambda i, t: (i, 0, 0, t))
    t_shape = lambda n: jax.ShapeDtypeStruct((b, n, GQA_HEAD_DIM, s), BF16)
    return pl.pallas_call(
        _rope_prep_kernel, grid=(b, s // ts),
        in_specs=[
            pl.BlockSpec((1, ts, width), lambda i, t: (i, t, 0)),
            pl.BlockSpec((128, 1), lambda i, t: (0, 0)),
            pl.BlockSpec((1, 128), lambda i, t: (0, 0)),
            pl.BlockSpec((ts, 128), lambda i, t: (t, 0)),
            pl.BlockSpec((ts, 128), lambda i, t: (t, 0)),
            pl.BlockSpec((128, ts), lambda i, t: (0, t)),
            pl.BlockSpec((128, ts), lambda i, t: (0, t)),
        ],
        out_specs=[t_out(GQA_HEADS), head_out(GQA_KV_HEADS), t_out(GQA_KV_HEADS)],
        out_shape=[t_shape(GQA_HEADS), shape(GQA_KV_HEADS), t_shape(GQA_KV_HEADS)],
        compiler_params=_params(("parallel", "parallel")), name="gqa_norm_rope",
    )(proj3d, qg, kg, cos, sin, cos.T, sin.T)


GQA_KV_CHUNK = 256
GQA_SUM_ROWS = 16
GQA_SUB_Q = 128


def _gqa_kernel(q_ref, k_ref, vt_ref, o_ref, s_a, s_b, m_a, m_b, *, tq, seq):
    t = pl.program_id(0)
    n_sub = tq // GQA_SUB_Q
    cols = GQA_REP * GQA_SUB_Q

    @pl.when(t == 0)
    def _():
        s_b[...] = jnp.zeros(s_b.shape, F32)
        m_b[...] = jnp.zeros(m_b.shape, F32)

    def sub_step(sub, s_cur, m_cur, s_prev, m_prev_ref):
        qs = slice(sub * GQA_SUB_Q, (sub + 1) * GQA_SUB_Q)
        qt = jnp.concatenate([q_ref[0, r, :, qs] for r in range(GQA_REP)], axis=1)
        m_prev = m_prev_ref[sub]
        m_run = None
        ones = jnp.ones((GQA_SUM_ROWS, GQA_KV_CHUNK), BF16)
        acc = jnp.zeros((GQA_HEAD_DIM + GQA_SUM_ROWS, cols), F32)
        for i in range(seq // GQA_KV_CHUNK):
            rows = slice(i * GQA_KV_CHUNK, (i + 1) * GQA_KV_CHUNK)
            st = _dot(k_ref[0, 0, rows, :], qt)
            s_cur[sub, rows, :] = st
            cm = jnp.max(st, axis=0, keepdims=True)
            m_run = cm if m_run is None else jnp.maximum(m_run, cm)
            p = jnp.exp2(s_prev[sub, rows, :] - m_prev)
            vt_aug = jnp.concatenate([vt_ref[0, 0, :, rows], ones], axis=0)
            acc = acc + _dot(vt_aug, p.astype(BF16))
        m_cur[sub] = m_run
        ot = acc[:GQA_HEAD_DIM] * (1.0 / acc[GQA_HEAD_DIM:GQA_HEAD_DIM + 1])
        for r in range(GQA_REP):
            o_ref[0, qs, r * GQA_HEAD_DIM:(r + 1) * GQA_HEAD_DIM] = (
                ot[:, r * GQA_SUB_Q:(r + 1) * GQA_SUB_Q].T.astype(BF16))

    def step(*bufs):
        for sub in range(n_sub):
            sub_step(sub, *bufs)

    pl.when(t % 2 == 0)(lambda: step(s_a, m_a, s_b, m_b))
    pl.when(t % 2 == 1)(lambda: step(s_b, m_b, s_a, m_a))


def _gqa_attention(q, k, vt, *, tq):
    b, _, _, s = q.shape
    nq = s // tq
    n_blocks = b * GQA_KV_HEADS * nq
    n_sub = tq // GQA_SUB_Q
    cols = GQA_REP * GQA_SUB_Q

    def unravel(u):
        return u // (nq * GQA_KV_HEADS), (u // nq) % GQA_KV_HEADS, u % nq

    def score_block(t):
        return unravel(jnp.minimum(t, n_blocks - 1))

    def finish_block(t):
        return unravel(jnp.maximum(t - 1, 0))

    def q_map(t):
        i, g, j = score_block(t)
        return (i, g, 0, j)

    def k_map(t):
        i, g, _ = score_block(t)
        return (i, g, 0, 0)

    def vt_map(t):
        i, g, _ = finish_block(t)
        return (i, g, 0, 0)

    def o_map(t):
        i, g, j = finish_block(t)
        return (i, j, g)

    kern = functools.partial(_gqa_kernel, tq=tq, seq=s)
    return pl.pallas_call(
        kern, grid=(n_blocks + 1,),
        in_specs=[
            pl.BlockSpec((1, GQA_REP, GQA_HEAD_DIM, tq), q_map),
            pl.BlockSpec((1, 1, s, GQA_HEAD_DIM), k_map),
            pl.BlockSpec((1, 1, GQA_HEAD_DIM, s), vt_map),
        ],
        out_specs=pl.BlockSpec((1, tq, GQA_REP * GQA_HEAD_DIM), o_map),
        out_shape=jax.ShapeDtypeStruct((b, s, GQA_Q_WIDTH), BF16),
        scratch_shapes=[pltpu.VMEM((n_sub, s, cols), F32), pltpu.VMEM((n_sub, s, cols), F32),
                        pltpu.VMEM((n_sub, 1, cols), F32), pltpu.VMEM((n_sub, 1, cols), F32)],
        compiler_params=_params(("arbitrary",)), name="gqa_attention",
    )(q, k, vt)


def _even_mixer(x2d, b, s, mix_norm, w_in, q_gain, k_gain, rpb, conv_w, conv_b, dt_bias, a_log, d_skip, out_gain,
                w_out):
    w_main = w_in.astype(BF16)
    w_dt = jnp.transpose(w_in[:, EVEN_MAIN_WIDTH:].reshape(-1, 2, SSD_GROUPS, SSD_HEADS_PER_GROUP),
                         (2, 1, 3, 0)).reshape(2 * SSD_HEADS, -1).astype(BF16)
    proj, xs, bt, cc, dt_rows = _even_in_proj(x2d, mix_norm.astype(F32)[None, :], w_main, w_dt, conv_w.astype(F32),
                                              conv_b.astype(F32)[None, :], tm=1024, seq=s, n_main=EVEN_XBC_OFFSET,
                                              name="even_in_proj")
    proj3d = proj.reshape(b, s, EVEN_XBC_OFFSET)
    na_out = _neighbourhood_attention(proj3d, q_gain, k_gain, rpb)
    ssd_out = _ssd_mixer(proj3d, xs.reshape(b, s, -1), bt, cc.reshape(b, s, -1), dt_rows, dt_bias, a_log, d_skip,
                         out_gain)
    w_out_bf = w_out.astype(BF16)
    return ([na_out.reshape(b * s, NA_WIDTH), ssd_out.reshape(b * s, SSD_D_INNER)],
            [w_out_bf[:NA_WIDTH], w_out_bf[NA_WIDTH:]])


def _odd_mixer(x2d, b, s, mix_norm, w_qkv, q_gain, k_gain, w_out):
    proj = _norm_proj(x2d, mix_norm.astype(F32)[None, :], w_qkv.astype(BF16), tm=512, name="odd_qkv_proj")
    q, k, vt = _rope_prep(proj.reshape(b, s, -1), q_gain, k_gain, ts=512)
    attn = _gqa_attention(q, k, vt, tq=512)
    return [attn.reshape(b * s, GQA_Q_WIDTH)], [w_out.astype(BF16)]


def kernel(x, even_mix_norm, even_w_in, na_q_norm, na_k_norm, na_rel_bias, ssd_conv_w, ssd_conv_b, ssd_dt_bias, ssd_A_log, ssd_D, ssd_out_norm, even_w_out, odd_mix_norm, odd_w_qkv, gqa_q_norm, gqa_k_norm, odd_w_out, ffn_norm, ffn_w13, ffn_w2):
    b, s, d = x.shape
    depth = ffn_norm.shape[0]
    h = x.reshape(b * s, d)
    for layer in range(depth):
        i = layer // 2
        if layer % 2 == 0:
            acts, w_outs = _even_mixer(h, b, s, even_mix_norm[i], even_w_in[i], na_q_norm[i], na_k_norm[i],
                                       na_rel_bias[i], ssd_conv_w[i], ssd_conv_b[i], ssd_dt_bias[i], ssd_A_log[i],
                                       ssd_D[i], ssd_out_norm[i], even_w_out[i])
        else:
            acts, w_outs = _odd_mixer(h, b, s, odd_mix_norm[i], odd_w_qkv[i], gqa_q_norm[i], gqa_k_norm[i],
                                      odd_w_out[i])
        h = _mix_ffn(acts, w_outs, h, ffn_norm[layer].astype(F32)[None, :], ffn_w13[layer].astype(BF16),
                     ffn_w2[layer].astype(BF16), tm=512, name="mix_out_ffn_even" if layer % 2 == 0 else "mix_out_ffn_odd")
    return h.reshape(b, s, d)
```

```python
import functools

import jax
import jax.numpy as jnp
from jax import lax
from jax.experimental import pallas as pl
from jax.experimental.pallas import tpu as pltpu

F32 = jnp.float32
BF16 = jnp.bfloat16

D_MODEL = 1024
GRID_W = 64
NORM_EPS = 1e-6

NA_HEADS = 8
NA_HEAD_DIM = 64
NA_WIDTH = NA_HEADS * NA_HEAD_DIM
NA_KH = 8
NA_KW = 16

SSD_D_INNER = 1024
SSD_HEAD_DIM = 64
SSD_HEADS = 16
SSD_GROUPS = 4
SSD_STATE = 128
SSD_CONV = 4
SSD_CHUNK = 128
SSD_CONV_DIM = SSD_D_INNER + 2 * SSD_GROUPS * SSD_STATE
SSD_GROUP_WIDTH = SSD_D_INNER // SSD_GROUPS
SSD_HEADS_PER_GROUP = SSD_HEADS // SSD_GROUPS

EVEN_MAIN_WIDTH = 3 * NA_WIDTH + SSD_D_INNER + SSD_CONV_DIM
EVEN_XBC_OFFSET = 3 * NA_WIDTH + SSD_D_INNER

GQA_HEADS = 16
GQA_KV_HEADS = 4
GQA_HEAD_DIM = 64
GQA_REP = GQA_HEADS // GQA_KV_HEADS
GQA_Q_WIDTH = GQA_HEADS * GQA_HEAD_DIM
GQA_KV_WIDTH = GQA_KV_HEADS * GQA_HEAD_DIM
ROPE_THETA = 10000.0

FFN_HIDDEN = 2816

V7X_LANES = 128
V7X_VMEM_LIMIT = 56 * 1024 * 1024
MASK_VALUE = -1e30
LOG2E = 1.4426950408889634


def _params(dims):
    return pltpu.CompilerParams(dimension_semantics=dims, vmem_limit_bytes=V7X_VMEM_LIMIT)


def _silu(v):
    return v * (1.0 / (1.0 + jnp.exp(-v)))


def _softplus(v):
    return jnp.maximum(v, 0.0) + jnp.log(1.0 + jnp.exp(-jnp.abs(v)))


def _rms_rows(x, g):
    ms = jnp.mean(x * x, axis=-1, keepdims=True)
    return x * lax.rsqrt(ms + NORM_EPS) * g


def _dot(a, b):
    return jnp.dot(a, b, preferred_element_type=F32)


def _dot_nt(a, b):
    return lax.dot_general(a, b, (((1,), (1,)), ((), ())), preferred_element_type=F32)


PROJ_CHUNK = 512


def _norm_proj_kernel(x_ref, g_ref, w_ref, o_ref):
    xn = _rms_rows(x_ref[...], g_ref[...]).astype(BF16)
    for c in range(o_ref.shape[1] // PROJ_CHUNK):
        cols = slice(c * PROJ_CHUNK, (c + 1) * PROJ_CHUNK)
        o_ref[:, cols] = _dot(xn, w_ref[:, cols]).astype(o_ref.dtype)


CONV_HALO = 8


def _even_in_proj_kernel(x_ref, xp_ref, xn_ref, g_ref, w_ref, wdt_ref, cw_ref, cb_ref, o_ref, ox_ref, obt_ref, oc_ref,
                         odt_ref, *, tiles_per_seq):
    tm = x_ref.shape[0]
    n_main = o_ref.shape[1]
    i = pl.program_id(0) % tiles_per_seq
    gain = g_ref[...]
    xc = _rms_rows(x_ref[...], gain)
    xp = jnp.where(i > 0, _rms_rows(xp_ref[...], gain), 0.0)
    xn = jnp.where(i < tiles_per_seq - 1, _rms_rows(xn_ref[...], gain), 0.0)
    xc_bf = xc.astype(BF16)
    xe_bf = jnp.concatenate([xp, xc, xn], axis=0).astype(BF16)
    odt_ref[...] = _dot_nt(wdt_ref[...], xc_bf)
    n_ext = tm + 2 * CONV_HALO
    left = SSD_CONV // 2
    n_main_chunks = n_main // PROJ_CHUNK
    nx, nb, ncc = (r // PROJ_CHUNK for r in (ox_ref.shape[1], obt_ref.shape[0], oc_ref.shape[1]))
    n_conv_chunks = nx + nb + ncc

    def main_chunk(c):
        cols = slice(c * PROJ_CHUNK, (c + 1) * PROJ_CHUNK)
        o_ref[:, cols] = _dot(xc_bf, w_ref[:, cols]).astype(o_ref.dtype)

    def conv_chunk(c, pr):
        cols = slice(c * PROJ_CHUNK, (c + 1) * PROJ_CHUNK)
        acc = jnp.broadcast_to(cb_ref[:, cols], (tm, PROJ_CHUNK))
        for k in range(SSD_CONV):
            shift = (left - k) % n_ext
            tap = pr if shift == 0 else pltpu.roll(pr, shift, axis=0)
            acc = acc + tap[CONV_HALO:CONV_HALO + tm] * cw_ref[k:k + 1, cols]
        out = _silu(acc)
        if c < nx:
            ox_ref[:, cols] = out.astype(ox_ref.dtype)
        elif c < nx + nb:
            obt_ref[(c - nx) * PROJ_CHUNK:(c - nx + 1) * PROJ_CHUNK, :] = out.T.astype(obt_ref.dtype)
        else:
            oc_ref[:, (c - nx - nb) * PROJ_CHUNK:(c - nx - nb + 1) * PROJ_CHUNK] = out.astype(oc_ref.dtype)

    for c in range(max(n_main_chunks, n_conv_chunks)):
        pr = None
        if c < n_conv_chunks:
            pr = _dot(xe_bf, w_ref[:, n_main + c * PROJ_CHUNK:n_main + (c + 1) * PROJ_CHUNK])
        if c < n_main_chunks:
            main_chunk(c)
        if pr is not None:
            conv_chunk(c, pr)


def _even_in_proj(x2d, g, w, w_dt, conv_w, conv_b, *, tm, seq, n_main, name):
    m, d = x2d.shape
    n = w.shape[1]
    n_x = SSD_D_INNER
    n_bc = SSD_GROUPS * SSD_STATE
    assert n >= n_main + n_x + 2 * n_bc and n_x % PROJ_CHUNK == 0 and n_bc % PROJ_CHUNK == 0
    ns = w_dt.shape[0]
    assert n_main % PROJ_CHUNK == 0 and seq % tm == 0 and tm % CONV_HALO == 0
    halo_per_tile = tm // CONV_HALO
    n_halo_blocks = m // CONV_HALO
    resident = lambda shape: pl.BlockSpec(shape, lambda i: (0, 0), pipeline_mode=pl.Buffered(1))
    kern = functools.partial(_even_in_proj_kernel, tiles_per_seq=seq // tm)
    return pl.pallas_call(
        kern, grid=(m // tm,),
        in_specs=[
            pl.BlockSpec((tm, d), lambda i: (i, 0)),
            pl.BlockSpec((CONV_HALO, d), lambda i: (jnp.maximum(i * halo_per_tile - 1, 0), 0)),
            pl.BlockSpec((CONV_HALO, d), lambda i: (jnp.minimum((i + 1) * halo_per_tile, n_halo_blocks - 1), 0)),
            pl.BlockSpec((1, d), lambda i: (0, 0)),
            resident((d, n)), resident((ns, d)), resident(conv_w.shape), resident(conv_b.shape),
        ],
        out_specs=[pl.BlockSpec((tm, n_main), lambda i: (i, 0)), pl.BlockSpec((tm, n_x), lambda i: (i, 0)),
                   pl.BlockSpec((n_bc, tm), lambda i: (0, i)), pl.BlockSpec((tm, n_bc), lambda i: (i, 0)),
                   pl.BlockSpec((ns, tm), lambda i: (0, i))],
        out_shape=[jax.ShapeDtypeStruct((m, n_main), BF16), jax.ShapeDtypeStruct((m, n_x), BF16),
                   jax.ShapeDtypeStruct((n_bc, m), BF16), jax.ShapeDtypeStruct((m, n_bc), BF16),
                   jax.ShapeDtypeStruct((ns, m), F32)],
        compiler_params=_params(("parallel",)), name=name,
    )(x2d, x2d, x2d, g, w, w_dt, conv_w, conv_b)


def _norm_proj(x2d, g, w, *, tm, name):
    m, d = x2d.shape
    n = w.shape[1]
    assert n % PROJ_CHUNK == 0
    resident = lambda shape: pl.BlockSpec(shape, lambda i: (0, 0), pipeline_mode=pl.Buffered(1))
    return pl.pallas_call(
        _norm_proj_kernel, grid=(m // tm,),
        in_specs=[pl.BlockSpec((tm, d), lambda i: (i, 0)), pl.BlockSpec((1, d), lambda i: (0, 0)), resident((d, n))],
        out_specs=pl.BlockSpec((tm, n), lambda i: (i, 0)), out_shape=jax.ShapeDtypeStruct((m, n), BF16),
        compiler_params=_params(("parallel",)), name=name)(x2d, g, w)


NA_PREP_ROWS = 256
NA_GROUP_ROWS = 4
NA_WIN_ROWS = NA_KH + NA_GROUP_ROWS
NA_DY = 2 * NA_KH - 1
NA_DX = 2 * NA_KW - 1


def _na_group_plan(rows):
    sigs, starts, classes = [], [], []
    for gq in range(rows // NA_GROUP_ROWS):
        ks = min(max(gq * NA_GROUP_ROWS - NA_KH // 2, 0), rows - NA_WIN_ROWS)
        sig = tuple((min(max(r - NA_KH // 2, 0), rows - NA_KH) - ks, r - ks)
                    for r in range(gq * NA_GROUP_ROWS, (gq + 1) * NA_GROUP_ROWS))
        assert all(0 <= first and first + NA_KH <= NA_WIN_ROWS for first, _ in sig)
        if sig not in sigs:
            sigs.append(sig)
        starts.append(ks)
        classes.append(sigs.index(sig))
    return sigs, starts, classes


def _pair_head_rms(x, g):
    lo = lax.broadcasted_iota(jnp.int32, x.shape, 1) < NA_HEAD_DIM
    x2 = x * x
    s_lo = jnp.sum(jnp.where(lo, x2, 0.0), axis=-1, keepdims=True)
    s_hi = jnp.sum(jnp.where(lo, 0.0, x2), axis=-1, keepdims=True)
    ms = jnp.where(lo, s_lo, s_hi) * (1.0 / NA_HEAD_DIM)
    return x * lax.rsqrt(ms + NORM_EPS) * g


def _na_kernel(plan_ref, q_ref, k_ref, v_ref, qg_ref, kg_ref, bias_ref, o_ref, q_s, k_s, s_a, s_b, m_a, m_b, *, rows):
    scale = NA_HEAD_DIM ** -0.5 * LOG2E

    same_head = (lax.broadcasted_iota(jnp.int32, (V7X_LANES, V7X_LANES), 0) // NA_HEAD_DIM
                 == lax.broadcasted_iota(jnp.int32, (V7X_LANES, V7X_LANES), 1) // NA_HEAD_DIM)
    head_ones = jnp.where(same_head, 1.0, 0.0).astype(BF16)

    def head_rms(x, g):
        ms = _dot((x * x).astype(BF16), head_ones) * (1.0 / NA_HEAD_DIM)
        return x * lax.rsqrt(ms + NORM_EPS) * g

    def prep(i, carry):
        sl = pl.ds(pl.multiple_of(i * NA_PREP_ROWS, NA_PREP_ROWS), NA_PREP_ROWS)
        q_s[sl, :] = (head_rms(q_ref[0, sl, :].astype(F32), qg_ref[...]) * scale).astype(BF16)
        k_s[sl, :] = head_rms(k_ref[0, sl, :].astype(F32), kg_ref[...]).astype(BF16)
        return carry

    lax.fori_loop(0, (rows * GRID_W) // NA_PREP_ROWS, prep, 0, unroll=4)

    n_q = NA_GROUP_ROWS * GRID_W
    n_keys = NA_WIN_ROWS * GRID_W
    n_groups = rows // NA_GROUP_ROWS
    heads = [slice(h * NA_HEAD_DIM, (h + 1) * NA_HEAD_DIM) for h in range(2)]

    def key_rows(g):
        return pl.ds(pl.multiple_of(plan_ref[0, g] * GRID_W, GRID_W), n_keys)

    def query_rows(g):
        return pl.ds(pl.multiple_of(g * n_q, n_q), n_q)

    def scores(g, s_buf, m_buf):
        q = q_s[query_rows(g), :]
        kk = k_s[key_rows(g), :]
        cls = plan_ref[1, g]
        for h, hs in enumerate(heads):
            s = _dot_nt(q[:, hs], kk[:, hs]) + bias_ref[0, cls, h]
            s_buf[h] = s
            m_buf[h] = jnp.max(s, axis=-1, keepdims=True)

    def finish(g, s_buf, m_buf):
        vv = v_ref[0, key_rows(g), :]
        outs = []
        for h, hs in enumerate(heads):
            p = jnp.exp2(s_buf[h] - m_buf[h])
            l = jnp.sum(p, axis=-1, keepdims=True)
            outs.append(_dot(p.astype(BF16), vv[:, hs]) * (1.0 / l))
        o_ref[0, query_rows(g), :] = jnp.concatenate(outs, axis=-1).astype(BF16)

    scores(0, s_a, m_a)

    def pair_body(i, carry):
        g = 2 * i
        scores(g + 1, s_b, m_b)
        finish(g, s_a, m_a)
        scores(jnp.minimum(g + 2, n_groups - 1), s_a, m_a)
        finish(g + 1, s_b, m_b)
        return carry

    lax.fori_loop(0, n_groups // 2, pair_body, 0)


def _na_bias_kernel(rpb_ref, o_ref, t_s, *, sigs):
    h = pl.program_id(0)
    q = lax.broadcasted_iota(jnp.int32, (GRID_W, GRID_W), 0)
    k = lax.broadcasted_iota(jnp.int32, (GRID_W, GRID_W), 1)
    dx = jnp.clip(k - q, -(NA_KW - 1), NA_KW - 1) + (NA_KW - 1)
    col_start = jnp.clip(q - NA_KW // 2, 0, GRID_W - NA_KW)
    in_win = (k >= col_start) & (k < col_start + NA_KW)
    masked = jnp.full((GRID_W, GRID_W), MASK_VALUE, F32)
    for dy in range(NA_DY):
        base = (h * NA_DY + dy) * NA_DX
        t = masked
        for d in range(NA_DX):
            t = jnp.where(dx == d, rpb_ref[base + d] * LOG2E, t)
        t_s[dy] = jnp.where(in_win, t, MASK_VALUE)
    for cls, sig in enumerate(sigs):
        for rq, (first, qrow) in enumerate(sig):
            for jk in range(NA_WIN_ROWS):
                attended = first <= jk < first + NA_KH
                tile = t_s[jk - qrow + NA_KH - 1] if attended else masked
                o_ref[0, cls, 0, rq * GRID_W:(rq + 1) * GRID_W, jk * GRID_W:(jk + 1) * GRID_W] = tile


def _na_bias_table(rpb, sigs):
    n_q = NA_GROUP_ROWS * GRID_W
    n_keys = NA_WIN_ROWS * GRID_W
    kern = functools.partial(_na_bias_kernel, sigs=sigs)
    return pl.pallas_call(
        kern, grid=(NA_HEADS,),
        in_specs=[pl.BlockSpec(memory_space=pltpu.SMEM)],
        out_specs=pl.BlockSpec((1, len(sigs), 1, n_q, n_keys), lambda h: (h // 2, 0, h % 2, 0, 0)),
        out_shape=jax.ShapeDtypeStruct((NA_HEADS // 2, len(sigs), 2, n_q, n_keys), F32),
        scratch_shapes=[pltpu.VMEM((NA_DY, GRID_W, GRID_W), F32)],
        compiler_params=_params(("parallel",)), name="na_bias_table",
    )(rpb.astype(F32).reshape(-1))


def _neighbourhood_attention(proj3d, q_gain, k_gain, rpb):
    b, s, _ = proj3d.shape
    rows = s // GRID_W
    assert rows >= NA_WIN_ROWS and rows % (2 * NA_GROUP_ROWS) == 0
    sigs, starts, classes = _na_group_plan(rows)
    bias = _na_bias_table(rpb, sigs)
    plan = jnp.array([starts, classes], jnp.int32)
    qg = jnp.tile(q_gain.astype(F32), 2)[None, :]
    kg = jnp.tile(k_gain.astype(F32), 2)[None, :]
    n_pairs = NA_HEADS // 2
    blk = (1, s, 2 * NA_HEAD_DIM)
    kern = functools.partial(_na_kernel, rows=rows)
    return pl.pallas_call(
        kern, grid=(n_pairs, b),
        in_specs=[
            pl.BlockSpec(memory_space=pltpu.SMEM),
            pl.BlockSpec(blk, lambda p, i: (i, 0, p)),
            pl.BlockSpec(blk, lambda p, i: (i, 0, n_pairs + p)),
            pl.BlockSpec(blk, lambda p, i: (i, 0, 2 * n_pairs + p)),
            pl.BlockSpec((1, 2 * NA_HEAD_DIM), lambda p, i: (0, 0)),
            pl.BlockSpec((1, 2 * NA_HEAD_DIM), lambda p, i: (0, 0)),
            pl.BlockSpec((1,) + bias.shape[1:], lambda p, i: (p, 0, 0, 0, 0)),
        ],
        out_specs=pl.BlockSpec(blk, lambda p, i: (i, 0, p)),
        out_shape=jax.ShapeDtypeStruct((b, s, NA_WIDTH), BF16),
        scratch_shapes=([pltpu.VMEM((s, 2 * NA_HEAD_DIM), BF16)] * 2
                        + [pltpu.VMEM((2, NA_GROUP_ROWS * GRID_W, NA_WIN_ROWS * GRID_W), F32)] * 2
                        + [pltpu.VMEM((2, NA_GROUP_ROWS * GRID_W, 1), F32)] * 2),
        compiler_params=_params(("parallel", "parallel")), name="neighbourhood_attention",
    )(plan, proj3d, proj3d, proj3d, qg, kg, bias)


def _chunk_scan(a, reverse):
    n = a.shape[1]
    pos = lax.broadcasted_iota(jnp.int32, a.shape, 1) % SSD_CHUNK
    sh = 1
    while sh < SSD_CHUNK:
        if reverse:
            a = a + jnp.where(pos < SSD_CHUNK - sh, pltpu.roll(a, n - sh, axis=1), 0.0)
        else:
            a = a + jnp.where(pos >= sh, pltpu.roll(a, sh, axis=1), 0.0)
        sh *= 2
    return a


def _head_row(tile, first):
    lo = lax.broadcasted_iota(jnp.int32, (1, V7X_LANES), 1) < SSD_HEAD_DIM
    halves = [jnp.where(lo, tile[first + 2 * i:first + 2 * i + 1, :], tile[first + 2 * i + 1:first + 2 * i + 2, :])
              for i in range(SSD_HEADS_PER_GROUP // 2)]
    return jnp.concatenate(halves, axis=1)


def _ssd_kernel(xs_ref, bt_ref, c_ref, z_ref, dtr_ref, biasr_ref, alogr_ref, dskip_ref, gain_ref, o_ref,
                cumr_s, g2r_s, ld2r_s, wr_s, decr_s, st_s, h_s, *, n_chunks):
    L = SSD_CHUNK
    hpg = SSD_HEADS_PER_GROUP
    gw = SSD_GROUP_WIDTH
    ns = SSD_STATE

    dt = _softplus(dtr_ref[...] + biasr_ref[0])
    a = dt * (-jnp.exp(alogr_ref[0]))
    is_fwd = lax.broadcasted_iota(jnp.int32, a.shape, 0) < hpg
    prefix = _chunk_scan(a, False)
    suffix = _chunk_scan(a, True)
    cum2 = jnp.where(is_fwd, prefix, suffix) * LOG2E
    cumr_s[...] = cum2
    g2r_s[...] = cum2 - jnp.log2(dt)
    ld2r_s[...] = jnp.log2(dt + pltpu.roll(dt, hpg, axis=0))
    wr_s[...] = dt * jnp.exp(jnp.where(is_fwd, suffix, prefix) - a)
    decr_s[...] = jnp.exp(prefix + suffix - a)

    def chunk_slice(c):
        return pl.ds(pl.multiple_of(c * L, L), L)

    lane_head = lax.broadcasted_iota(jnp.int32, (L, gw), 1) // SSD_HEAD_DIM

    def block_diag_x(xs):
        return jnp.concatenate([jnp.where(lane_head == j, xs, jnp.zeros_like(xs)) for j in range(hpg)], axis=0)

    def state_body(c, carry):
        sl = chunk_slice(c)
        bt = bt_ref[:, sl].astype(F32)
        w = wr_s[:, sl]
        lhs = jnp.concatenate(
            [jnp.concatenate([(bt * w[d * hpg + j:d * hpg + j + 1, :]).astype(BF16) for j in range(hpg)], axis=1)
             for d in range(2)], axis=0)
        st_s[c] = _dot(lhs, block_diag_x(xs_ref[0, sl, :]))
        return carry

    lax.fori_loop(0, n_chunks, state_body, 0, unroll=8)

    def fwd_rec(c, h):
        h_s[c, :, 0:gw] = h.astype(BF16)
        return h * _head_row(decr_s[:, chunk_slice(c)], 0) + st_s[c, 0:ns, :]

    def bwd_rec(i, h):
        c = n_chunks - 1 - i
        h_s[c, :, gw:2 * gw] = h.astype(BF16)
        return h * _head_row(decr_s[:, chunk_slice(c)], hpg) + st_s[c, ns:2 * ns, :]

    h0 = jnp.zeros((ns, gw), F32)
    lax.fori_loop(0, n_chunks, fwd_rec, h0)
    lax.fori_loop(0, n_chunks, bwd_rec, h0)

    li = lax.broadcasted_iota(jnp.int32, (L, L), 0)
    si = lax.broadcasted_iota(jnp.int32, (L, L), 1)
    below = si < li
    above = si > li
    lane_lo = lax.broadcasted_iota(jnp.int32, (L, V7X_LANES), 1) < SSD_HEAD_DIM

    def out_body(c, carry):
        sl = chunk_slice(c)
        xs = xs_ref[0, sl, :]
        cm = c_ref[0, sl, :]
        cum_r = cumr_s[:, sl]
        g2 = g2r_s[:, sl]
        ld2 = ld2r_s[:, sl]
        cum_t = [jnp.broadcast_to(cum_r[k:k + 1, :], (L, L)).T for k in range(2 * hpg)]
        cb = _dot(cm, bt_ref[:, sl])
        mats = []
        for j in range(hpg):
            seg_f = cum_t[j] - g2[j:j + 1, :]
            seg_b = cum_t[hpg + j] - g2[hpg + j:hpg + j + 1, :]
            arg = jnp.where(below, seg_f, jnp.where(above, seg_b, ld2[j:j + 1, :]))
            mats.append((cb * jnp.exp2(arg)).astype(BF16))
        y = _dot(jnp.concatenate(mats, axis=1), block_diag_x(xs))
        carried = _dot(cm, h_s[c])
        for d in range(2):
            decay = jnp.exp2(jnp.concatenate(
                [jnp.where(lane_lo, cum_t[d * hpg + 2 * i], cum_t[d * hpg + 2 * i + 1]) for i in range(hpg // 2)],
                axis=1))
            y = y + carried[:, d * gw:(d + 1) * gw] * decay
        y = y + dskip_ref[0] * xs.astype(F32)
        gated = y * _silu(z_ref[0, sl, :].astype(F32))
        ms = jnp.mean(gated * gated, axis=-1, keepdims=True)
        o_ref[0, sl, :] = (gated * lax.rsqrt(ms + NORM_EPS) * gain_ref[0]).astype(BF16)
        return carry

    lax.fori_loop(0, n_chunks, out_body, 0, unroll=8)


def _group_major(v):
    return jnp.transpose(v.astype(F32).reshape(2, SSD_GROUPS, SSD_HEADS_PER_GROUP), (1, 0, 2)).reshape(
        SSD_GROUPS, 2 * SSD_HEADS_PER_GROUP)


def _ssd_mixer(proj3d, xs3d, bt2d, c3d, dt_rows, dt_bias, a_log, d_skip, out_gain):
    b, s, _ = proj3d.shape
    n_chunks = s // SSD_CHUNK
    g, hpg, gw, ns = SSD_GROUPS, SSD_HEADS_PER_GROUP, SSD_GROUP_WIDTH, SSD_STATE
    bias_g = _group_major(dt_bias)[:, :, None]
    alog_g = _group_major(a_log)[:, :, None]
    dskip = jnp.repeat(d_skip.astype(F32), SSD_HEAD_DIM).reshape(g, 1, gw)
    gain = out_gain.astype(F32).reshape(g, 1, gw)
    z_blk = (3 * NA_WIDTH) // gw
    kern = functools.partial(_ssd_kernel, n_chunks=n_chunks)
    small = lambda shape: pl.BlockSpec((1,) + shape, lambda i, k: (k, 0, 0))
    seq_blk = lambda width, blk0: pl.BlockSpec((1, s, width), lambda i, k: (i, 0, blk0 + k))
    row_scratch = pltpu.VMEM((2 * hpg, s), F32)
    return pl.pallas_call(
        kern, grid=(b, g),
        in_specs=[
            seq_blk(gw, 0), pl.BlockSpec((ns, s), lambda i, k: (k, i)), seq_blk(ns, 0), seq_blk(gw, z_blk),
            pl.BlockSpec((2 * hpg, s), lambda i, k: (k, i)),
            small((2 * hpg, 1)), small((2 * hpg, 1)), small((1, gw)), small((1, gw)),
        ],
        out_specs=pl.BlockSpec((1, s, gw), lambda i, k: (i, 0, k)),
        out_shape=jax.ShapeDtypeStruct((b, s, SSD_D_INNER), BF16),
        scratch_shapes=[
            row_scratch, row_scratch, row_scratch, row_scratch, row_scratch,
            pltpu.VMEM((n_chunks, 2 * ns, gw), F32), pltpu.VMEM((n_chunks, ns, 2 * gw), BF16),
        ],
        compiler_params=_params(("parallel", "parallel")), name="ssd_bidirectional",
    )(xs3d, bt2d, c3d, proj3d, dt_rows, bias_g, alog_g, dskip, gain)


FFN_CHUNK = 256


def _mix_ffn_kernel(*refs, n_acts, hidden):
    act_refs = refs[:n_acts]
    wout_refs = refs[n_acts:2 * n_acts]
    x_ref, g_ref, w13_ref, w2_ref, o_ref, hid_ref = refs[2 * n_acts:]
    h = x_ref[...]
    for a_ref, w_ref in zip(act_refs, wout_refs):
        h = h + _dot(a_ref[...], w_ref[...])
    hn = _rms_rows(h, g_ref[...]).astype(BF16)
    for c in range(hidden // FFN_CHUNK):
        gate = slice(c * FFN_CHUNK, (c + 1) * FFN_CHUNK)
        up = slice(hidden + c * FFN_CHUNK, hidden + (c + 1) * FFN_CHUNK)
        hid_ref[:, gate] = (_silu(_dot(hn, w13_ref[:, gate])) * _dot(hn, w13_ref[:, up])).astype(BF16)
    o_ref[...] = h + _dot(hid_ref[...], w2_ref[...])


def _mix_ffn(acts, w_outs, x2d, g, w13, w2, *, tm, name):
    m, d = x2d.shape
    hid = w2.shape[0]
    assert hid % FFN_CHUNK == 0
    row = lambda width: pl.BlockSpec((tm, width), lambda i: (i, 0))
    resident = lambda shape: pl.BlockSpec(shape, lambda i: (0, 0), pipeline_mode=pl.Buffered(1))
    return pl.pallas_call(
        functools.partial(_mix_ffn_kernel, n_acts=len(acts), hidden=hid), grid=(m // tm,),
        in_specs=([row(a.shape[1]) for a in acts] + [resident(w.shape) for w in w_outs]
                  + [row(d), pl.BlockSpec((1, d), lambda i: (0, 0)), resident((d, 2 * hid)), resident((hid, d))]),
        out_specs=row(d),
        out_shape=jax.ShapeDtypeStruct((m, d), F32),
        scratch_shapes=[pltpu.VMEM((tm, hid), BF16)],
        compiler_params=_params(("parallel",)), name=name,
    )(*acts, *w_outs, x2d, g, w13, w2)


def _rope_prep_kernel(p_ref, qg_ref, kg_ref, cos_ref, sin_ref, cost_ref, sint_ref, q_ref, k_ref, v_ref):
    scale = GQA_HEAD_DIM ** -0.5 * LOG2E
    ts = p_ref.shape[1]
    cos_t = cost_ref[...]
    sin_t = sint_ref[...]
    even_row3 = (lax.broadcasted_iota(jnp.int32, (2 * GQA_HEAD_DIM // 8, 8, ts), 1) % 2) == 0
    q_gain = qg_ref[...] * scale
    hd = GQA_HEAD_DIM
    for pair in range(GQA_HEADS // 2):
        xt = p_ref[0, :, pair * 128:(pair + 1) * 128].astype(F32).T
        x2 = xt * xt
        inv = [lax.rsqrt(jnp.mean(x2[h * hd:(h + 1) * hd], axis=0, keepdims=True) + NORM_EPS) for h in range(2)]
        xn = jnp.concatenate([xt[:hd] * inv[0], xt[hd:] * inv[1]], axis=0) * q_gain
        x3 = xn.reshape(2 * hd // 8, 8, ts)
        swapped = jnp.where(even_row3, pltpu.roll(x3, 7, axis=1), pltpu.roll(x3, 1, axis=1)).reshape(2 * hd, ts)
        out = xn * cos_t + swapped * sin_t
        q_ref[0, 2 * pair] = out[:hd].astype(BF16)
        q_ref[0, 2 * pair + 1] = out[hd:].astype(BF16)

    cos = cos_ref[...]
    sin = sin_ref[...]
    even = (lax.broadcasted_iota(jnp.int32, cos.shape, 1) % 2) == 0
    for pair in range(GQA_KV_HEADS // 2):
        c0 = GQA_Q_WIDTH + pair * 128
        xn = _pair_head_rms(p_ref[0, :, c0:c0 + 128].astype(F32), kg_ref[...])
        swapped = jnp.where(even, pltpu.roll(xn, V7X_LANES - 1, axis=1), pltpu.roll(xn, 1, axis=1))
        blk = xn * cos + swapped * sin
        k_ref[0, 2 * pair] = blk[:, :hd].astype(BF16)
        k_ref[0, 2 * pair + 1] = blk[:, hd:].astype(BF16)
        c1 = GQA_Q_WIDTH + GQA_KV_WIDTH + pair * 128
        vt = p_ref[0, :, c1:c1 + 128].astype(F32).T.astype(BF16)
        v_ref[0, 2 * pair] = vt[:hd]
        v_ref[0, 2 * pair + 1] = vt[hd:]


def _axial_rope_tables(s):
    t = jnp.arange(s)
    row = (t // GRID_W).astype(F32)
    col = (t % GRID_W).astype(F32)
    axis_dims = GQA_HEAD_DIM // 2
    freqs = ROPE_THETA ** (-jnp.arange(0, axis_dims, 2, dtype=F32) / axis_dims)
    ang = jnp.concatenate([row[:, None] * freqs, col[:, None] * freqs], axis=-1)
    cos = jnp.repeat(jnp.cos(ang), 2, axis=-1)
    sin = jnp.stack([-jnp.sin(ang), jnp.sin(ang)], axis=-1).reshape(s, GQA_HEAD_DIM)
    return jnp.tile(cos, (1, 2)), jnp.tile(sin, (1, 2))


def _rope_prep(proj3d, q_gain, k_gain, *, ts):
    b, s, width = proj3d.shape
    cos, sin = _axial_rope_tables(s)
    qg = jnp.tile(q_gain.astype(F32), 2)[:, None]
    kg = jnp.tile(k_gain.astype(F32), 2)[None, :]
    head_out = lambda n: pl.BlockSpec((1, n, ts, GQA_HEAD_DIM), lambda i, t: (i, 0, t, 0))
    shape = lambda n: jax.ShapeDtypeStruct((b, n, s, GQA_HEAD_DIM), BF16)
    t_out = lambda n: pl.BlockSpec((1, n, GQA_HEAD_DIM, ts), lambda i, t: (i, 0, 0, t))
    t_shape = lambda n: jax.ShapeDtypeStruct((b, n, GQA_HEAD_DIM, s), BF16)
    return pl.pallas_call(
        _rope_prep_kernel, grid=(b, s // ts),
        in_specs=[
            pl.BlockSpec((1, ts, width), lambda i, t: (i, t, 0)),
            pl.BlockSpec((128, 1), lambda i, t: (0, 0)),
            pl.BlockSpec((1, 128), lambda i, t: (0, 0)),
            pl.BlockSpec((ts, 128), lambda i, t: (t, 0)),
            pl.BlockSpec((ts, 128), lambda i, t: (t, 0)),
            pl.BlockSpec((128, ts), lambda i, t: (0, t)),
            pl.BlockSpec((128, ts), lambda i, t: (0, t)),
        ],
        out_specs=[t_out(GQA_HEADS), head_out(GQA_KV_HEADS), t_out(GQA_KV_HEADS)],
        out_shape=[t_shape(GQA_HEADS), shape(GQA_KV_HEADS), t_shape(GQA_KV_HEADS)],
        compiler_params=_params(("parallel", "parallel")), name="gqa_norm_rope",
    )(proj3d, qg, kg, cos, sin, cos.T, sin.T)


GQA_KV_CHUNK = 256
GQA_SUM_ROWS = 16
GQA_SUB_Q = 128


def _gqa_kernel(q_ref, k_ref, vt_ref, o_ref, s_a, s_b, m_a, m_b, *, tq, seq):
    t = pl.program_id(0)
    n_sub = tq // GQA_SUB_Q
    cols = GQA_REP * GQA_SUB_Q

    @pl.when(t == 0)
    def _():
        s_b[...] = jnp.zeros(s_b.shape, F32)
        m_b[...] = jnp.zeros(m_b.shape, F32)

    def sub_step(sub, s_cur, m_cur, s_prev, m_prev_ref):
        qs = slice(sub * GQA_SUB_Q, (sub + 1) * GQA_SUB_Q)
        qt = jnp.concatenate([q_ref[0, r, :, qs] for r in range(GQA_REP)], axis=1)
        m_prev = m_prev_ref[sub]
        m_run = None
        ones = jnp.ones((GQA_SUM_ROWS, GQA_KV_CHUNK), BF16)
        acc = jnp.zeros((GQA_HEAD_DIM + GQA_SUM_ROWS, cols), F32)
        for i in range(seq // GQA_KV_CHUNK):
            rows = slice(i * GQA_KV_CHUNK, (i + 1) * GQA_KV_CHUNK)
            st = _dot(k_ref[0, 0, rows, :], qt)
            s_cur[sub, rows, :] = st
            cm = jnp.max(st, axis=0, keepdims=True)
            m_run = cm if m_run is None else jnp.maximum(m_run, cm)
            p = jnp.exp2(s_prev[sub, rows, :] - m_prev)
            vt_aug = jnp.concatenate([vt_ref[0, 0, :, rows], ones], axis=0)
            acc = acc + _dot(vt_aug, p.astype(BF16))
        m_cur[sub] = m_run
        ot = acc[:GQA_HEAD_DIM] * (1.0 / acc[GQA_HEAD_DIM:GQA_HEAD_DIM + 1])
        for r in range(GQA_REP):
            o_ref[0, qs, r * GQA_HEAD_DIM:(r + 1) * GQA_HEAD_DIM] = (
                ot[:, r * GQA_SUB_Q:(r + 1) * GQA_SUB_Q].T.astype(BF16))

    def step(*bufs):
        for sub in range(n_sub):
            sub_step(sub, *bufs)

    pl.when(t % 2 == 0)(lambda: step(s_a, m_a, s_b, m_b))
    pl.when(t % 2 == 1)(lambda: step(s_b, m_b, s_a, m_a))


def _gqa_attention(q, k, vt, *, tq):
    b, _, _, s = q.shape
    nq = s // tq
    n_blocks = b * GQA_KV_HEADS * nq
    n_sub = tq // GQA_SUB_Q
    cols = GQA_REP * GQA_SUB_Q

    def unravel(u):
        return u // (nq * GQA_KV_HEADS), (u // nq) % GQA_KV_HEADS, u % nq

    def score_block(t):
        return unravel(jnp.minimum(t, n_blocks - 1))

    def finish_block(t):
        return unravel(jnp.maximum(t - 1, 0))

    def q_map(t):
        i, g, j = score_block(t)
        return (i, g, 0, j)

    def k_map(t):
        i, g, _ = score_block(t)
        return (i, g, 0, 0)

    def vt_map(t):
        i, g, _ = finish_block(t)
        return (i, g, 0, 0)

    def o_map(t):
        i, g, j = finish_block(t)
        return (i, j, g)

    kern = functools.partial(_gqa_kernel, tq=tq, seq=s)
    return pl.pallas_call(
        kern, grid=(n_blocks + 1,),
        in_specs=[
            pl.BlockSpec((1, GQA_REP, GQA_HEAD_DIM, tq), q_map),
            pl.BlockSpec((1, 1, s, GQA_HEAD_DIM), k_map),
            pl.BlockSpec((1, 1, GQA_HEAD_DIM, s), vt_map),
        ],
        out_specs=pl.BlockSpec((1, tq, GQA_REP * GQA_HEAD_DIM), o_map),
        out_shape=jax.ShapeDtypeStruct((b, s, GQA_Q_WIDTH), BF16),
        scratch_shapes=[pltpu.VMEM((n_sub, s, cols), F32), pltpu.VMEM((n_sub, s, cols), F32),
                        pltpu.VMEM((n_sub, 1, cols), F32), pltpu.VMEM((n_sub, 1, cols), F32)],
        compiler_params=_params(("arbitrary",)), name="gqa_attention",
    )(q, k, vt)


def _even_mixer(x2d, b, s, mix_norm, w_in, q_gain, k_gain, rpb, conv_w, conv_b, dt_bias, a_log, d_skip, out_gain,
                w_out):
    w_main = w_in.astype(BF16)
    w_dt = jnp.transpose(w_in[:, EVEN_MAIN_WIDTH:].reshape(-1, 2, SSD_GROUPS, SSD_HEADS_PER_GROUP),
                         (2, 1, 3, 0)).reshape(2 * SSD_HEADS, -1).astype(BF16)
    proj, xs, bt, cc, dt_rows = _even_in_proj(x2d, mix_norm.astype(F32)[None, :], w_main, w_dt, conv_w.astype(F32),
                                              conv_b.astype(F32)[None, :], tm=1024, seq=s, n_main=EVEN_XBC_OFFSET,
                                              name="even_in_proj")
    proj3d = proj.reshape(b, s, EVEN_XBC_OFFSET)
    na_out = _neighbourhood_attention(proj3d, q_gain, k_gain, rpb)
    ssd_out = _ssd_mixer(proj3d, xs.reshape(b, s, -1), bt, cc.reshape(b, s, -1), dt_rows, dt_bias, a_log, d_skip,
                         out_gain)
    w_out_bf = w_out.astype(BF16)
    return ([na_out.reshape(b * s, NA_WIDTH), ssd_out.reshape(b * s, SSD_D_INNER)],
            [w_out_bf[:NA_WIDTH], w_out_bf[NA_WIDTH:]])


def _odd_mixer(x2d, b, s, mix_norm, w_qkv, q_gain, k_gain, w_out):
    proj = _norm_proj(x2d, mix_norm.astype(F32)[None, :], w_qkv.astype(BF16), tm=512, name="odd_qkv_proj")
    q, k, vt = _rope_prep(proj.reshape(b, s, -1), q_gain, k_gain, ts=512)
    attn = _gqa_attention(q, k, vt, tq=512)
    return [attn.reshape(b * s, GQA_Q_WIDTH)], [w_out.astype(BF16)]


def kernel(x, even_mix_norm, even_w_in, na_q_norm, na_k_norm, na_rel_bias, ssd_conv_w, ssd_conv_b, ssd_dt_bias, ssd_A_log, ssd_D, ssd_out_norm, even_w_out, odd_mix_norm, odd_w_qkv, gqa_q_norm, gqa_k_norm, odd_w_out, ffn_norm, ffn_w13, ffn_w2):
    b, s, d = x.shape
    depth = ffn_norm.shape[0]
    h = x.reshape(b * s, d)
    for layer in range(depth):
        i = layer // 2
        if layer % 2 == 0:
            acts, w_outs = _even_mixer(h, b, s, even_mix_norm[i], even_w_in[i], na_q_norm[i], na_k_norm[i],
                                       na_rel_bias[i], ssd_conv_w[i], ssd_conv_b[i], ssd_dt_bias[i], ssd_A_log[i],
                                       ssd_D[i], ssd_out_norm[i], even_w_out[i])
        else:
            acts, w_outs = _odd_mixer(h, b, s, odd_mix_norm[i], odd_w_qkv[i], gqa_q_norm[i], gqa_k_norm[i],
                                      odd_w_out[i])
        h = _mix_ffn(acts, w_outs, h, ffn_norm[layer].astype(F32)[None, :], ffn_w13[layer].astype(BF16),
                     ffn_w2[layer].astype(BF16), tm=512, name="mix_out_ffn_even" if layer % 2 == 0 else "mix_out_ffn_odd")
    return h.reshape(b, s, d)
```

```python
import functools

import jax
import jax.numpy as jnp
from jax import lax
from jax.experimental import pallas as pl
from jax.experimental.pallas import tpu as pltpu

F32 = jnp.float32
BF16 = jnp.bfloat16

D_MODEL = 1024
GRID_W = 64
NORM_EPS = 1e-6

NA_HEADS = 8
NA_HEAD_DIM = 64
NA_WIDTH = NA_HEADS * NA_HEAD_DIM
NA_KH = 8
NA_KW = 16

SSD_D_INNER = 1024
SSD_HEAD_DIM = 64
SSD_HEADS = 16
SSD_GROUPS = 4
SSD_STATE = 128
SSD_CONV = 4
SSD_CHUNK = 128
SSD_CONV_DIM = SSD_D_INNER + 2 * SSD_GROUPS * SSD_STATE
SSD_GROUP_WIDTH = SSD_D_INNER // SSD_GROUPS
SSD_HEADS_PER_GROUP = SSD_HEADS // SSD_GROUPS

EVEN_MAIN_WIDTH = 3 * NA_WIDTH + SSD_D_INNER + SSD_CONV_DIM
EVEN_XBC_OFFSET = 3 * NA_WIDTH + SSD_D_INNER

GQA_HEADS = 16
GQA_KV_HEADS = 4
GQA_HEAD_DIM = 64
GQA_REP = GQA_HEADS // GQA_KV_HEADS
GQA_Q_WIDTH = GQA_HEADS * GQA_HEAD_DIM
GQA_KV_WIDTH = GQA_KV_HEADS * GQA_HEAD_DIM
ROPE_THETA = 10000.0

FFN_HIDDEN = 2816

V7X_LANES = 128
V7X_VMEM_LIMIT = 56 * 1024 * 1024
MASK_VALUE = -1e30
LOG2E = 1.4426950408889634


def _params(dims):
    return pltpu.CompilerParams(dimension_semantics=dims, vmem_limit_bytes=V7X_VMEM_LIMIT)


def _silu(v):
    return v * (1.0 / (1.0 + jnp.exp(-v)))


def _softplus(v):
    return jnp.maximum(v, 0.0) + jnp.log(1.0 + jnp.exp(-jnp.abs(v)))


def _rms_rows(x, g):
    ms = jnp.mean(x * x, axis=-1, keepdims=True)
    return x * lax.rsqrt(ms + NORM_EPS) * g


def _dot(a, b):
    return jnp.dot(a, b, preferred_element_type=F32)


def _dot_nt(a, b):
    return lax.dot_general(a, b, (((1,), (1,)), ((), ())), preferred_element_type=F32)


PROJ_CHUNK = 512


def _norm_proj_kernel(x_ref, g_ref, w_ref, o_ref):
    xn = _rms_rows(x_ref[...], g_ref[...]).astype(BF16)
    for c in range(o_ref.shape[1] // PROJ_CHUNK):
        cols = slice(c * PROJ_CHUNK, (c + 1) * PROJ_CHUNK)
        o_ref[:, cols] = _dot(xn, w_ref[:, cols]).astype(o_ref.dtype)


CONV_HALO = 8


def _even_in_proj_kernel(x_ref, xp_ref, xn_ref, g_ref, w_ref, wdt_ref, cw_ref, cb_ref, o_ref, ox_ref, obt_ref, oc_ref,
                         odt_ref, *, tiles_per_seq):
    tm = x_ref.shape[0]
    n_main = o_ref.shape[1]
    i = pl.program_id(0) % tiles_per_seq
    gain = g_ref[...]
    xc = _rms_rows(x_ref[...], gain)
    xp = jnp.where(i > 0, _rms_rows(xp_ref[...], gain), 0.0)
    xn = jnp.where(i < tiles_per_seq - 1, _rms_rows(xn_ref[...], gain), 0.0)
    xc_bf = xc.astype(BF16)
    xe_bf = jnp.concatenate([xp, xc, xn], axis=0).astype(BF16)
    odt_ref[...] = _dot_nt(wdt_ref[...], xc_bf)
    n_ext = tm + 2 * CONV_HALO
    left = SSD_CONV // 2
    n_main_chunks = n_main // PROJ_CHUNK
    nx, nb, ncc = (r // PROJ_CHUNK for r in (ox_ref.shape[1], obt_ref.shape[0], oc_ref.shape[1]))
    n_conv_chunks = nx + nb + ncc

    def main_chunk(c):
        cols = slice(c * PROJ_CHUNK, (c + 1) * PROJ_CHUNK)
        o_ref[:, cols] = _dot(xc_bf, w_ref[:, cols]).astype(o_ref.dtype)

    def conv_chunk(c, pr):
        cols = slice(c * PROJ_CHUNK, (c + 1) * PROJ_CHUNK)
        acc = jnp.broadcast_to(cb_ref[:, cols], (tm, PROJ_CHUNK))
        for k in range(SSD_CONV):
            shift = (left - k) % n_ext
            tap = pr if shift == 0 else pltpu.roll(pr, shift, axis=0)
            acc = acc + tap[CONV_HALO:CONV_HALO + tm] * cw_ref[k:k + 1, cols]
        out = _silu(acc)
        if c < nx:
            ox_ref[:, cols] = out.astype(ox_ref.dtype)
        elif c < nx + nb:
            obt_ref[(c - nx) * PROJ_CHUNK:(c - nx + 1) * PROJ_CHUNK, :] = out.T.astype(obt_ref.dtype)
        else:
            oc_ref[:, (c - nx - nb) * PROJ_CHUNK:(c - nx - nb + 1) * PROJ_CHUNK] = out.astype(oc_ref.dtype)

    for c in range(max(n_main_chunks, n_conv_chunks)):
        pr = None
        if c < n_conv_chunks:
            pr = _dot(xe_bf, w_ref[:, n_main + c * PROJ_CHUNK:n_main + (c + 1) * PROJ_CHUNK])
        if c < n_main_chunks:
            main_chunk(c)
        if pr is not None:
            conv_chunk(c, pr)


def _even_in_proj(x2d, g, w, w_dt, conv_w, conv_b, *, tm, seq, n_main, name):
    m, d = x2d.shape
    n = w.shape[1]
    n_x = SSD_D_INNER
    n_bc = SSD_GROUPS * SSD_STATE
    assert n >= n_main + n_x + 2 * n_bc and n_x % PROJ_CHUNK == 0 and n_bc % PROJ_CHUNK == 0
    ns = w_dt.shape[0]
    assert n_main % PROJ_CHUNK == 0 and seq % tm == 0 and tm % CONV_HALO == 0
    halo_per_tile = tm // CONV_HALO
    n_halo_blocks = m // CONV_HALO
    resident = lambda shape: pl.BlockSpec(shape, lambda i: (0, 0), pipeline_mode=pl.Buffered(1))
    kern = functools.partial(_even_in_proj_kernel, tiles_per_seq=seq // tm)
    return pl.pallas_call(
        kern, grid=(m // tm,),
        in_specs=[
            pl.BlockSpec((tm, d), lambda i: (i, 0)),
            pl.BlockSpec((CONV_HALO, d), lambda i: (jnp.maximum(i * halo_per_tile - 1, 0), 0)),
            pl.BlockSpec((CONV_HALO, d), lambda i: (jnp.minimum((i + 1) * halo_per_tile, n_halo_blocks - 1), 0)),
            pl.BlockSpec((1, d), lambda i: (0, 0)),
            resident((d, n)), resident((ns, d)), resident(conv_w.shape), resident(conv_b.shape),
        ],
        out_specs=[pl.BlockSpec((tm, n_main), lambda i: (i, 0)), pl.BlockSpec((tm, n_x), lambda i: (i, 0)),
                   pl.BlockSpec((n_bc, tm), lambda i: (0, i)), pl.BlockSpec((tm, n_bc), lambda i: (i, 0)),
                   pl.BlockSpec((ns, tm), lambda i: (0, i))],
        out_shape=[jax.ShapeDtypeStruct((m, n_main), BF16), jax.ShapeDtypeStruct((m, n_x), BF16),
                   jax.ShapeDtypeStruct((n_bc, m), BF16), jax.ShapeDtypeStruct((m, n_bc), BF16),
                   jax.ShapeDtypeStruct((ns, m), F32)],
        compiler_params=_params(("parallel",)), name=name,
    )(x2d, x2d, x2d, g, w, w_dt, conv_w, conv_b)


def _norm_proj(x2d, g, w, *, tm, name):
    m, d = x2d.shape
    n = w.shape[1]
    assert n % PROJ_CHUNK == 0
    resident = lambda shape: pl.BlockSpec(shape, lambda i: (0, 0), pipeline_mode=pl.Buffered(1))
    return pl.pallas_call(
        _norm_proj_kernel, grid=(m // tm,),
        in_specs=[pl.BlockSpec((tm, d), lambda i: (i, 0)), pl.BlockSpec((1, d), lambda i: (0, 0)), resident((d, n))],
        out_specs=pl.BlockSpec((tm, n), lambda i: (i, 0)), out_shape=jax.ShapeDtypeStruct((m, n), BF16),
        compiler_params=_params(("parallel",)), name=name)(x2d, g, w)


NA_PREP_ROWS = 256
NA_GROUP_ROWS = 4
NA_WIN_ROWS = NA_KH + NA_GROUP_ROWS
NA_DY = 2 * NA_KH - 1
NA_DX = 2 * NA_KW - 1


def _na_group_plan(rows):
    sigs, starts, classes = [], [], []
    for gq in range(rows // NA_GROUP_ROWS):
        ks = min(max(gq * NA_GROUP_ROWS - NA_KH // 2, 0), rows - NA_WIN_ROWS)
        sig = tuple((min(max(r - NA_KH // 2, 0), rows - NA_KH) - ks, r - ks)
                    for r in range(gq * NA_GROUP_ROWS, (gq + 1) * NA_GROUP_ROWS))
        assert all(0 <= first and first + NA_KH <= NA_WIN_ROWS for first, _ in sig)
        if sig not in sigs:
            sigs.append(sig)
        starts.append(ks)
        classes.append(sigs.index(sig))
    return sigs, starts, classes


def _pair_head_rms(x, g):
    lo = lax.broadcasted_iota(jnp.int32, x.shape, 1) < NA_HEAD_DIM
    x2 = x * x
    s_lo = jnp.sum(jnp.where(lo, x2, 0.0), axis=-1, keepdims=True)
    s_hi = jnp.sum(jnp.where(lo, 0.0, x2), axis=-1, keepdims=True)
    ms = jnp.where(lo, s_lo, s_hi) * (1.0 / NA_HEAD_DIM)
    return x * lax.rsqrt(ms + NORM_EPS) * g


def _na_kernel(plan_ref, q_ref, k_ref, v_ref, qg_ref, kg_ref, bias_ref, o_ref, q_s, k_s, s_a, s_b, m_a, m_b, *, rows):
    scale = NA_HEAD_DIM ** -0.5 * LOG2E

    same_head = (lax.broadcasted_iota(jnp.int32, (V7X_LANES, V7X_LANES), 0) // NA_HEAD_DIM
                 == lax.broadcasted_iota(jnp.int32, (V7X_LANES, V7X_LANES), 1) // NA_HEAD_DIM)
    head_ones = jnp.where(same_head, 1.0, 0.0).astype(BF16)

    def head_rms(x, g):
        ms = _dot((x * x).astype(BF16), head_ones) * (1.0 / NA_HEAD_DIM)
        return x * lax.rsqrt(ms + NORM_EPS) * g

    def prep(i, carry):
        sl = pl.ds(pl.multiple_of(i * NA_PREP_ROWS, NA_PREP_ROWS), NA_PREP_ROWS)
        q_s[sl, :] = (head_rms(q_ref[0, sl, :].astype(F32), qg_ref[...]) * scale).astype(BF16)
        k_s[sl, :] = head_rms(k_ref[0, sl, :].astype(F32), kg_ref[...]).astype(BF16)
        return carry

    lax.fori_loop(0, (rows * GRID_W) // NA_PREP_ROWS, prep, 0, unroll=4)

    n_q = NA_GROUP_ROWS * GRID_W
    n_keys = NA_WIN_ROWS * GRID_W
    n_groups = rows // NA_GROUP_ROWS
    heads = [slice(h * NA_HEAD_DIM, (h + 1) * NA_HEAD_DIM) for h in range(2)]

    def key_rows(g):
        return pl.ds(pl.multiple_of(plan_ref[0, g] * GRID_W, GRID_W), n_keys)

    def query_rows(g):
        return pl.ds(pl.multiple_of(g * n_q, n_q), n_q)

    def scores(g, s_buf, m_buf):
        q = q_s[query_rows(g), :]
        kk = k_s[key_rows(g), :]
        cls = plan_ref[1, g]
        for h, hs in enumerate(heads):
            s = _dot_nt(q[:, hs], kk[:, hs]) + bias_ref[0, cls, h]
            s_buf[h] = s
            m_buf[h] = jnp.max(s, axis=-1, keepdims=True)

    def finish(g, s_buf, m_buf):
        vv = v_ref[0, key_rows(g), :]
        outs = []
        for h, hs in enumerate(heads):
            p = jnp.exp2(s_buf[h] - m_buf[h])
            l = jnp.sum(p, axis=-1, keepdims=True)
            outs.append(_dot(p.astype(BF16), vv[:, hs]) * (1.0 / l))
        o_ref[0, query_rows(g), :] = jnp.concatenate(outs, axis=-1).astype(BF16)

    scores(0, s_a, m_a)

    def pair_body(i, carry):
        g = 2 * i
        scores(g + 1, s_b, m_b)
        finish(g, s_a, m_a)
        scores(jnp.minimum(g + 2, n_groups - 1), s_a, m_a)
        finish(g + 1, s_b, m_b)
        return carry

    lax.fori_loop(0, n_groups // 2, pair_body, 0)


def _na_bias_kernel(rpb_ref, o_ref, t_s, *, sigs):
    h = pl.program_id(0)
    q = lax.broadcasted_iota(jnp.int32, (GRID_W, GRID_W), 0)
    k = lax.broadcasted_iota(jnp.int32, (GRID_W, GRID_W), 1)
    dx = jnp.clip(k - q, -(NA_KW - 1), NA_KW - 1) + (NA_KW - 1)
    col_start = jnp.clip(q - NA_KW // 2, 0, GRID_W - NA_KW)
    in_win = (k >= col_start) & (k < col_start + NA_KW)
    masked = jnp.full((GRID_W, GRID_W), MASK_VALUE, F32)
    for dy in range(NA_DY):
        base = (h * NA_DY + dy) * NA_DX
        t = masked
        for d in range(NA_DX):
            t = jnp.where(dx == d, rpb_ref[base + d] * LOG2E, t)
        t_s[dy] = jnp.where(in_win, t, MASK_VALUE)
    for cls, sig in enumerate(sigs):
        for rq, (first, qrow) in enumerate(sig):
            for jk in range(NA_WIN_ROWS):
                attended = first <= jk < first + NA_KH
                tile = t_s[jk - qrow + NA_KH - 1] if attended else masked
                o_ref[0, cls, 0, rq * GRID_W:(rq + 1) * GRID_W, jk * GRID_W:(jk + 1) * GRID_W] = tile


def _na_bias_table(rpb, sigs):
    n_q = NA_GROUP_ROWS * GRID_W
    n_keys = NA_WIN_ROWS * GRID_W
    kern = functools.partial(_na_bias_kernel, sigs=sigs)
    return pl.pallas_call(
        kern, grid=(NA_HEADS,),
        in_specs=[pl.BlockSpec(memory_space=pltpu.SMEM)],
        out_specs=pl.BlockSpec((1, len(sigs), 1, n_q, n_keys), lambda h: (h // 2, 0, h % 2, 0, 0)),
        out_shape=jax.ShapeDtypeStruct((NA_HEADS // 2, len(sigs), 2, n_q, n_keys), F32),
        scratch_shapes=[pltpu.VMEM((NA_DY, GRID_W, GRID_W), F32)],
        compiler_params=_params(("parallel",)), name="na_bias_table",
    )(rpb.astype(F32).reshape(-1))


def _neighbourhood_attention(proj3d, q_gain, k_gain, rpb):
    b, s, _ = proj3d.shape
    rows = s // GRID_W
    assert rows >= NA_WIN_ROWS and rows % (2 * NA_GROUP_ROWS) == 0
    sigs, starts, classes = _na_group_plan(rows)
    bias = _na_bias_table(rpb, sigs)
    plan = jnp.array([starts, classes], jnp.int32)
    qg = jnp.tile(q_gain.astype(F32), 2)[None, :]
    kg = jnp.tile(k_gain.astype(F32), 2)[None, :]
    n_pairs = NA_HEADS // 2
    blk = (1, s, 2 * NA_HEAD_DIM)
    kern = functools.partial(_na_kernel, rows=rows)
    return pl.pallas_call(
        kern, grid=(n_pairs, b),
        in_specs=[
            pl.BlockSpec(memory_space=pltpu.SMEM),
            pl.BlockSpec(blk, lambda p, i: (i, 0, p)),
            pl.BlockSpec(blk, lambda p, i: (i, 0, n_pairs + p)),
            pl.BlockSpec(blk, lambda p, i: (i, 0, 2 * n_pairs + p)),
            pl.BlockSpec((1, 2 * NA_HEAD_DIM), lambda p, i: (0, 0)),
            pl.BlockSpec((1, 2 * NA_HEAD_DIM), lambda p, i: (0, 0)),
            pl.BlockSpec((1,) + bias.shape[1:], lambda p, i: (p, 0, 0, 0, 0)),
        ],
        out_specs=pl.BlockSpec(blk, lambda p, i: (i, 0, p)),
        out_shape=jax.ShapeDtypeStruct((b, s, NA_WIDTH), BF16),
        scratch_shapes=([pltpu.VMEM((s, 2 * NA_HEAD_DIM), BF16)] * 2
                        + [pltpu.VMEM((2, NA_GROUP_ROWS * GRID_W, NA_WIN_ROWS * GRID_W), F32)] * 2
                        + [pltpu.VMEM((2, NA_GROUP_ROWS * GRID_W, 1), F32)] * 2),
        compiler_params=_params(("parallel", "parallel")), name="neighbourhood_attention",
    )(plan, proj3d, proj3d, proj3d, qg, kg, bias)


def _chunk_scan(a, reverse):
    n = a.shape[1]
    pos = lax.broadcasted_iota(jnp.int32, a.shape, 1) % SSD_CHUNK
    sh = 1
    while sh < SSD_CHUNK:
        if reverse:
            a = a + jnp.where(pos < SSD_CHUNK - sh, pltpu.roll(a, n - sh, axis=1), 0.0)
        else:
            a = a + jnp.where(pos >= sh, pltpu.roll(a, sh, axis=1), 0.0)
        sh *= 2
    return a


def _head_row(tile, first):
    lo = lax.broadcasted_iota(jnp.int32, (1, V7X_LANES), 1) < SSD_HEAD_DIM
    halves = [jnp.where(lo, tile[first + 2 * i:first + 2 * i + 1, :], tile[first + 2 * i + 1:first + 2 * i + 2, :])
              for i in range(SSD_HEADS_PER_GROUP // 2)]
    return jnp.concatenate(halves, axis=1)


def _ssd_kernel(xs_ref, bt_ref, c_ref, z_ref, dtr_ref, biasr_ref, alogr_ref, dskip_ref, gain_ref, o_ref,
                cumr_s, g2r_s, ld2r_s, wr_s, decr_s, st_s, h_s, *, n_chunks):
    L = SSD_CHUNK
    hpg = SSD_HEADS_PER_GROUP
    gw = SSD_GROUP_WIDTH
    ns = SSD_STATE

    dt = _softplus(dtr_ref[...] + biasr_ref[0])
    a = dt * (-jnp.exp(alogr_ref[0]))
    is_fwd = lax.broadcasted_iota(jnp.int32, a.shape, 0) < hpg
    prefix = _chunk_scan(a, False)
    suffix = _chunk_scan(a, True)
    cum2 = jnp.where(is_fwd, prefix, suffix) * LOG2E
    cumr_s[...] = cum2
    g2r_s[...] = cum2 - jnp.log2(dt)
    ld2r_s[...] = jnp.log2(dt + pltpu.roll(dt, hpg, axis=0))
    wr_s[...] = dt * jnp.exp(jnp.where(is_fwd, suffix, prefix) - a)
    decr_s[...] = jnp.exp(prefix + suffix - a)

    def chunk_slice(c):
        return pl.ds(pl.multiple_of(c * L, L), L)

    lane_head = lax.broadcasted_iota(jnp.int32, (L, gw), 1) // SSD_HEAD_DIM

    def block_diag_x(xs):
        return jnp.concatenate([jnp.where(lane_head == j, xs, jnp.zeros_like(xs)) for j in range(hpg)], axis=0)

    def state_body(c, carry):
        sl = chunk_slice(c)
        bt = bt_ref[:, sl].astype(F32)
        w = wr_s[:, sl]
        lhs = jnp.concatenate(
            [jnp.concatenate([(bt * w[d * hpg + j:d * hpg + j + 1, :]).astype(BF16) for j in range(hpg)], axis=1)
             for d in range(2)], axis=0)
        st_s[c] = _dot(lhs, block_diag_x(xs_ref[0, sl, :]))
        return carry

    lax.fori_loop(0, n_chunks, state_body, 0, unroll=8)

    def fwd_rec(c, h):
        h_s[c, :, 0:gw] = h.astype(BF16)
        return h * _head_row(decr_s[:, chunk_slice(c)], 0) + st_s[c, 0:ns, :]

    def bwd_rec(i, h):
        c = n_chunks - 1 - i
        h_s[c, :, gw:2 * gw] = h.astype(BF16)
        return h * _head_row(decr_s[:, chunk_slice(c)], hpg) + st_s[c, ns:2 * ns, :]

    h0 = jnp.zeros((ns, gw), F32)
    lax.fori_loop(0, n_chunks, fwd_rec, h0)
    lax.fori_loop(0, n_chunks, bwd_rec, h0)

    li = lax.broadcasted_iota(jnp.int32, (L, L), 0)
    si = lax.broadcasted_iota(jnp.int32, (L, L), 1)
    below = si < li
    above = si > li
    lane_lo = lax.broadcasted_iota(jnp.int32, (L, V7X_LANES), 1) < SSD_HEAD_DIM

    def out_body(c, carry):
        sl = chunk_slice(c)
        xs = xs_ref[0, sl, :]
        cm = c_ref[0, sl, :]
        cum_r = cumr_s[:, sl]
        g2 = g2r_s[:, sl]
        ld2 = ld2r_s[:, sl]
        cum_t = [jnp.broadcast_to(cum_r[k:k + 1, :], (L, L)).T for k in range(2 * hpg)]
        cb = _dot(cm, bt_ref[:, sl])
        mats = []
        for j in range(hpg):
            seg_f = cum_t[j] - g2[j:j + 1, :]
            seg_b = cum_t[hpg + j] - g2[hpg + j:hpg + j + 1, :]
            arg = jnp.where(below, seg_f, jnp.where(above, seg_b, ld2[j:j + 1, :]))
            mats.append((cb * jnp.exp2(arg)).astype(BF16))
        y = _dot(jnp.concatenate(mats, axis=1), block_diag_x(xs))
        carried = _dot(cm, h_s[c])
        for d in range(2):
            decay = jnp.exp2(jnp.concatenate(
                [jnp.where(lane_lo, cum_t[d * hpg + 2 * i], cum_t[d * hpg + 2 * i + 1]) for i in range(hpg // 2)],
                axis=1))
            y = y + carried[:, d * gw:(d + 1) * gw] * decay
        y = y + dskip_ref[0] * xs.astype(F32)
        gated = y * _silu(z_ref[0, sl, :].astype(F32))
        ms = jnp.mean(gated * gated, axis=-1, keepdims=True)
        o_ref[0, sl, :] = (gated * lax.rsqrt(ms + NORM_EPS) * gain_ref[0]).astype(BF16)
        return carry

    lax.fori_loop(0, n_chunks, out_body, 0, unroll=8)


def _group_major(v):
    return jnp.transpose(v.astype(F32).reshape(2, SSD_GROUPS, SSD_HEADS_PER_GROUP), (1, 0, 2)).reshape(
        SSD_GROUPS, 2 * SSD_HEADS_PER_GROUP)


def _ssd_mixer(proj3d, xs3d, bt2d, c3d, dt_rows, dt_bias, a_log, d_skip, out_gain):
    b, s, _ = proj3d.shape
    n_chunks = s // SSD_CHUNK
    g, hpg, gw, ns = SSD_GROUPS, SSD_HEADS_PER_GROUP, SSD_GROUP_WIDTH, SSD_STATE
    bias_g = _group_major(dt_bias)[:, :, None]
    alog_g = _group_major(a_log)[:, :, None]
    dskip = jnp.repeat(d_skip.astype(F32), SSD_HEAD_DIM).reshape(g, 1, gw)
    gain = out_gain.astype(F32).reshape(g, 1, gw)
    z_blk = (3 * NA_WIDTH) // gw
    kern = functools.partial(_ssd_kernel, n_chunks=n_chunks)
    small = lambda shape: pl.BlockSpec((1,) + shape, lambda i, k: (k, 0, 0))
    seq_blk = lambda width, blk0: pl.BlockSpec((1, s, width), lambda i, k: (i, 0, blk0 + k))
    row_scratch = pltpu.VMEM((2 * hpg, s), F32)
    return pl.pallas_call(
        kern, grid=(b, g),
        in_specs=[
            seq_blk(gw, 0), pl.BlockSpec((ns, s), lambda i, k: (k, i)), seq_blk(ns, 0), seq_blk(gw, z_blk),
            pl.BlockSpec((2 * hpg, s), lambda i, k: (k, i)),
            small((2 * hpg, 1)), small((2 * hpg, 1)), small((1, gw)), small((1, gw)),
        ],
        out_specs=pl.BlockSpec((1, s, gw), lambda i, k: (i, 0, k)),
        out_shape=jax.ShapeDtypeStruct((b, s, SSD_D_INNER), BF16),
        scratch_shapes=[
            row_scratch, row_scratch, row_scratch, row_scratch, row_scratch,
            pltpu.VMEM((n_chunks, 2 * ns, gw), F32), pltpu.VMEM((n_chunks, ns, 2 * gw), BF16),
        ],
        compiler_params=_params(("parallel", "parallel")), name="ssd_bidirectional",
    )(xs3d, bt2d, c3d, proj3d, dt_rows, bias_g, alog_g, dskip, gain)


FFN_CHUNK = 256


def _mix_ffn_kernel(*refs, n_acts, hidden):
    act_refs = refs[:n_acts]
    wout_refs = refs[n_acts:2 * n_acts]
    x_ref, g_ref, w13_ref, w2_ref, o_ref, hid_ref = refs[2 * n_acts:]
    h = x_ref[...]
    for a_ref, w_ref in zip(act_refs, wout_refs):
        h = h + _dot(a_ref[...], w_ref[...])
    hn = _rms_rows(h, g_ref[...]).astype(BF16)
    for c in range(hidden // FFN_CHUNK):
        gate = slice(c * FFN_CHUNK, (c + 1) * FFN_CHUNK)
        up = slice(hidden + c * FFN_CHUNK, hidden + (c + 1) * FFN_CHUNK)
        hid_ref[:, gate] = (_silu(_dot(hn, w13_ref[:, gate])) * _dot(hn, w13_ref[:, up])).astype(BF16)
    o_ref[...] = h + _dot(hid_ref[...], w2_ref[...])


def _mix_ffn(acts, w_outs, x2d, g, w13, w2, *, tm, name):
    m, d = x2d.shape
    hid = w2.shape[0]
    assert hid % FFN_CHUNK == 0
    row = lambda width: pl.BlockSpec((tm, width), lambda i: (i, 0))
    resident = lambda shape: pl.BlockSpec(shape, lambda i: (0, 0), pipeline_mode=pl.Buffered(1))
    return pl.pallas_call(
        functools.partial(_mix_ffn_kernel, n_acts=len(acts), hidden=hid), grid=(m // tm,),
        in_specs=([row(a.shape[1]) for a in acts] + [resident(w.shape) for w in w_outs]
                  + [row(d), pl.BlockSpec((1, d), lambda i: (0, 0)), resident((d, 2 * hid)), resident((hid, d))]),
        out_specs=row(d),
        out_shape=jax.ShapeDtypeStruct((m, d), F32),
        scratch_shapes=[pltpu.VMEM((tm, hid), BF16)],
        compiler_params=_params(("parallel",)), name=name,
    )(*acts, *w_outs, x2d, g, w13, w2)


def _rope_prep_kernel(p_ref, qg_ref, kg_ref, cos_ref, sin_ref, cost_ref, sint_ref, q_ref, k_ref, v_ref):
    scale = GQA_HEAD_DIM ** -0.5 * LOG2E
    ts = p_ref.shape[1]
    cos_t = cost_ref[...]
    sin_t = sint_ref[...]
    even_row3 = (lax.broadcasted_iota(jnp.int32, (2 * GQA_HEAD_DIM // 8, 8, ts), 1) % 2) == 0
    q_gain = qg_ref[...] * scale
    hd = GQA_HEAD_DIM
    for pair in range(GQA_HEADS // 2):
        xt = p_ref[0, :, pair * 128:(pair + 1) * 128].astype(F32).T
        x2 = xt * xt
        inv = [lax.rsqrt(jnp.mean(x2[h * hd:(h + 1) * hd], axis=0, keepdims=True) + NORM_EPS) for h in range(2)]
        xn = jnp.concatenate([xt[:hd] * inv[0], xt[hd:] * inv[1]], axis=0) * q_gain
        x3 = xn.reshape(2 * hd // 8, 8, ts)
        swapped = jnp.where(even_row3, pltpu.roll(x3, 7, axis=1), pltpu.roll(x3, 1, axis=1)).reshape(2 * hd, ts)
        out = xn * cos_t + swapped * sin_t
        q_ref[0, 2 * pair] = out[:hd].astype(BF16)
        q_ref[0, 2 * pair + 1] = out[hd:].astype(BF16)

    cos = cos_ref[...]
    sin = sin_ref[...]
    even = (lax.broadcasted_iota(jnp.int32, cos.shape, 1) % 2) == 0
    for pair in range(GQA_KV_HEADS // 2):
        c0 = GQA_Q_WIDTH + pair * 128
        xn = _pair_head_rms(p_ref[0, :, c0:c0 + 128].astype(F32), kg_ref[...])
        swapped = jnp.where(even, pltpu.roll(xn, V7X_LANES - 1, axis=1), pltpu.roll(xn, 1, axis=1))
        blk = xn * cos + swapped * sin
        k_ref[0, 2 * pair] = blk[:, :hd].astype(BF16)
        k_ref[0, 2 * pair + 1] = blk[:, hd:].astype(BF16)
        c1 = GQA_Q_WIDTH + GQA_KV_WIDTH + pair * 128
        vt = p_ref[0, :, c1:c1 + 128].astype(F32).T.astype(BF16)
        v_ref[0, 2 * pair] = vt[:hd]
        v_ref[0, 2 * pair + 1] = vt[hd:]


def _axial_rope_tables(s):
    t = jnp.arange(s)
    row = (t // GRID_W).astype(F32)
    col = (t % GRID_W).astype(F32)
    axis_dims = GQA_HEAD_DIM // 2
    freqs = ROPE_THETA ** (-jnp.arange(0, axis_dims, 2, dtype=F32) / axis_dims)
    ang = jnp.concatenate([row[:, None] * freqs, col[:, None] * freqs], axis=-1)
    cos = jnp.repeat(jnp.cos(ang), 2, axis=-1)
    sin = jnp.stack([-jnp.sin(ang), jnp.sin(ang)], axis=-1).reshape(s, GQA_HEAD_DIM)
    return jnp.tile(cos, (1, 2)), jnp.tile(sin, (1, 2))


def _rope_prep(proj3d, q_gain, k_gain, *, ts):
    b, s, width = proj3d.shape
    cos, sin = _axial_rope_tables(s)
    qg = jnp.tile(q_gain.astype(F32), 2)[:, None]
    kg = jnp.tile(k_gain.astype(F32), 2)[None, :]
    head_out = lambda n: pl.BlockSpec((1, n, ts, GQA_HEAD_DIM), lambda i, t: (i, 0, t, 0))
    shape = lambda n: jax.ShapeDtypeStruct((b, n, s, GQA_HEAD_DIM), BF16)
    t_out = lambda n: pl.BlockSpec((1, n, GQA_HEAD_DIM, ts), lambda i, t: (i, 0, 0, t))
    t_shape = lambda n: jax.ShapeDtypeStruct((b, n, GQA_HEAD_DIM, s), BF16)
    return pl.pallas_call(
        _rope_prep_kernel, grid=(b, s // ts),
        in_specs=[
            pl.BlockSpec((1, ts, width), lambda i, t: (i, t, 0)),
            pl.BlockSpec((128, 1), lambda i, t: (0, 0)),
            pl.BlockSpec((1, 128), lambda i, t: (0, 0)),
            pl.BlockSpec((ts, 128), lambda i, t: (t, 0)),
            pl.BlockSpec((ts, 128), lambda i, t: (t, 0)),
            pl.BlockSpec((128, ts), lambda i, t: (0, t)),
            pl.BlockSpec((128, ts), lambda i, t: (0, t)),
        ],
        out_specs=[t_out(GQA_HEADS), head_out(GQA_KV_HEADS), t_out(GQA_KV_HEADS)],
        out_shape=[t_shape(GQA_HEADS), shape(GQA_KV_HEADS), t_shape(GQA_KV_HEADS)],
        compiler_params=_params(("parallel", "parallel")), name="gqa_norm_rope",
    )(proj3d, qg, kg, cos, sin, cos.T, sin.T)


GQA_KV_CHUNK = 256
GQA_SUM_ROWS = 16
GQA_SUB_Q = 128


def _gqa_kernel(q_ref, k_ref, vt_ref, o_ref, s_a, s_b, m_a, m_b, *, tq, seq):
    t = pl.program_id(0)
    n_sub = tq // GQA_SUB_Q
    cols = GQA_REP * GQA_SUB_Q

    @pl.when(t == 0)
    def _():
        s_b[...] = jnp.zeros(s_b.shape, F32)
        m_b[...] = jnp.zeros(m_b.shape, F32)

    def sub_step(sub, s_cur, m_cur, s_prev, m_prev_ref):
        qs = slice(sub * GQA_SUB_Q, (sub + 1) * GQA_SUB_Q)
        qt = jnp.concatenate([q_ref[0, r, :, qs] for r in range(GQA_REP)], axis=1)
        m_prev = m_prev_ref[sub]
        m_run = None
        ones = jnp.ones((GQA_SUM_ROWS, GQA_KV_CHUNK), BF16)
        acc = jnp.zeros((GQA_HEAD_DIM + GQA_SUM_ROWS, cols), F32)
        for i in range(seq // GQA_KV_CHUNK):
            rows = slice(i * GQA_KV_CHUNK, (i + 1) * GQA_KV_CHUNK)
            st = _dot(k_ref[0, 0, rows, :], qt)
            s_cur[sub, rows, :] = st
            cm = jnp.max(st, axis=0, keepdims=True)
            m_run = cm if m_run is None else jnp.maximum(m_run, cm)
            p = jnp.exp2(s_prev[sub, rows, :] - m_prev)
            vt_aug = jnp.concatenate([vt_ref[0, 0, :, rows], ones], axis=0)
            acc = acc + _dot(vt_aug, p.astype(BF16))
        m_cur[sub] = m_run
        ot = acc[:GQA_HEAD_DIM] * (1.0 / acc[GQA_HEAD_DIM:GQA_HEAD_DIM + 1])
        for r in range(GQA_REP):
            o_ref[0, qs, r * GQA_HEAD_DIM:(r + 1) * GQA_HEAD_DIM] = (
                ot[:, r * GQA_SUB_Q:(r + 1) * GQA_SUB_Q].T.astype(BF16))

    def step(*bufs):
        for sub in range(n_sub):
            sub_step(sub, *bufs)

    pl.when(t % 2 == 0)(lambda: step(s_a, m_a, s_b, m_b))
    pl.when(t % 2 == 1)(lambda: step(s_b, m_b, s_a, m_a))


def _gqa_attention(q, k, vt, *, tq):
    b, _, _, s = q.shape
    nq = s // tq
    n_blocks = b * GQA_KV_HEADS * nq
    n_sub = tq // GQA_SUB_Q
    cols = GQA_REP * GQA_SUB_Q

    def unravel(u):
        return u // (nq * GQA_KV_HEADS), (u // nq) % GQA_KV_HEADS, u % nq

    def score_block(t):
        return unravel(jnp.minimum(t, n_blocks - 1))

    def finish_block(t):
        return unravel(jnp.maximum(t - 1, 0))

    def q_map(t):
        i, g, j = score_block(t)
        return (i, g, 0, j)

    def k_map(t):
        i, g, _ = score_block(t)
        return (i, g, 0, 0)

    def vt_map(t):
        i, g, _ = finish_block(t)
        return (i, g, 0, 0)

    def o_map(t):
        i, g, j = finish_block(t)
        return (i, j, g)

    kern = functools.partial(_gqa_kernel, tq=tq, seq=s)
    return pl.pallas_call(
        kern, grid=(n_blocks + 1,),
        in_specs=[
            pl.BlockSpec((1, GQA_REP, GQA_HEAD_DIM, tq), q_map),
            pl.BlockSpec((1, 1, s, GQA_HEAD_DIM), k_map),
            pl.BlockSpec((1, 1, GQA_HEAD_DIM, s), vt_map),
        ],
        out_specs=pl.BlockSpec((1, tq, GQA_REP * GQA_HEAD_DIM), o_map),
        out_shape=jax.ShapeDtypeStruct((b, s, GQA_Q_WIDTH), BF16),
        scratch_shapes=[pltpu.VMEM((n_sub, s, cols), F32), pltpu.VMEM((n_sub, s, cols), F32),
                        pltpu.VMEM((n_sub, 1, cols), F32), pltpu.VMEM((n_sub, 1, cols), F32)],
        compiler_params=_params(("arbitrary",)), name="gqa_attention",
    )(q, k, vt)


def _even_mixer(x2d, b, s, mix_norm, w_in, q_gain, k_gain, rpb, conv_w, conv_b, dt_bias, a_log, d_skip, out_gain,
                w_out):
    w_main = w_in.astype(BF16)
    w_dt = jnp.transpose(w_in[:, EVEN_MAIN_WIDTH:].reshape(-1, 2, SSD_GROUPS, SSD_HEADS_PER_GROUP),
                         (2, 1, 3, 0)).reshape(2 * SSD_HEADS, -1).astype(BF16)
    proj, xs, bt, cc, dt_rows = _even_in_proj(x2d, mix_norm.astype(F32)[None, :], w_main, w_dt, conv_w.astype(F32),
                                              conv_b.astype(F32)[None, :], tm=1024, seq=s, n_main=EVEN_XBC_OFFSET,
                                              name="even_in_proj")
    proj3d = proj.reshape(b, s, EVEN_XBC_OFFSET)
    na_out = _neighbourhood_attention(proj3d, q_gain, k_gain, rpb)
    ssd_out = _ssd_mixer(proj3d, xs.reshape(b, s, -1), bt, cc.reshape(b, s, -1), dt_rows, dt_bias, a_log, d_skip,
                         out_gain)
    w_out_bf = w_out.astype(BF16)
    return ([na_out.reshape(b * s, NA_WIDTH), ssd_out.reshape(b * s, SSD_D_INNER)],
            [w_out_bf[:NA_WIDTH], w_out_bf[NA_WIDTH:]])


def _odd_mixer(x2d, b, s, mix_norm, w_qkv, q_gain, k_gain, w_out):
    proj = _norm_proj(x2d, mix_norm.astype(F32)[None, :], w_qkv.astype(BF16), tm=1024, name="odd_qkv_proj")
    q, k, vt = _rope_prep(proj.reshape(b, s, -1), q_gain, k_gain, ts=1024)
    attn = _gqa_attention(q, k, vt, tq=512)
    return [attn.reshape(b * s, GQA_Q_WIDTH)], [w_out.astype(BF16)]


def kernel(x, even_mix_norm, even_w_in, na_q_norm, na_k_norm, na_rel_bias, ssd_conv_w, ssd_conv_b, ssd_dt_bias, ssd_A_log, ssd_D, ssd_out_norm, even_w_out, odd_mix_norm, odd_w_qkv, gqa_q_norm, gqa_k_norm, odd_w_out, ffn_norm, ffn_w13, ffn_w2):
    b, s, d = x.shape
    depth = ffn_norm.shape[0]
    h = x.reshape(b * s, d)
    for layer in range(depth):
        i = layer // 2
        if layer % 2 == 0:
            acts, w_outs = _even_mixer(h, b, s, even_mix_norm[i], even_w_in[i], na_q_norm[i], na_k_norm[i],
                                       na_rel_bias[i], ssd_conv_w[i], ssd_conv_b[i], ssd_dt_bias[i], ssd_A_log[i],
                                       ssd_D[i], ssd_out_norm[i], even_w_out[i])
        else:
            acts, w_outs = _odd_mixer(h, b, s, odd_mix_norm[i], odd_w_qkv[i], gqa_q_norm[i], gqa_k_norm[i],
                                      odd_w_out[i])
        h = _mix_ffn(acts, w_outs, h, ffn_norm[layer].astype(F32)[None, :], ffn_w13[layer].astype(BF16),
                     ffn_w2[layer].astype(BF16), tm=1024, name="mix_out_ffn_even" if layer % 2 == 0 else "mix_out_ffn_odd")
    return h.reshape(b, s, d)
```

```python
import functools

import jax
import jax.numpy as jnp
from jax import lax
from jax.experimental import pallas as pl
from jax.experimental.pallas import tpu as pltpu

F32 = jnp.float32
BF16 = jnp.bfloat16

D_MODEL = 1024
GRID_W = 64
NORM_EPS = 1e-6

NA_HEADS = 8
NA_HEAD_DIM = 64
NA_WIDTH = NA_HEADS * NA_HEAD_DIM
NA_KH = 8
NA_KW = 16

SSD_D_INNER = 1024
SSD_HEAD_DIM = 64
SSD_HEADS = 16
SSD_GROUPS = 4
SSD_STATE = 128
SSD_CONV = 4
SSD_CHUNK = 128
SSD_CONV_DIM = SSD_D_INNER + 2 * SSD_GROUPS * SSD_STATE
SSD_GROUP_WIDTH = SSD_D_INNER // SSD_GROUPS
SSD_HEADS_PER_GROUP = SSD_HEADS // SSD_GROUPS

EVEN_MAIN_WIDTH = 3 * NA_WIDTH + SSD_D_INNER + SSD_CONV_DIM
EVEN_XBC_OFFSET = 3 * NA_WIDTH + SSD_D_INNER

GQA_HEADS = 16
GQA_KV_HEADS = 4
GQA_HEAD_DIM = 64
GQA_REP = GQA_HEADS // GQA_KV_HEADS
GQA_Q_WIDTH = GQA_HEADS * GQA_HEAD_DIM
GQA_KV_WIDTH = GQA_KV_HEADS * GQA_HEAD_DIM
ROPE_THETA = 10000.0

FFN_HIDDEN = 2816

V7X_LANES = 128
V7X_VMEM_LIMIT = 56 * 1024 * 1024
MASK_VALUE = -1e30
LOG2E = 1.4426950408889634


def _params(dims):
    return pltpu.CompilerParams(dimension_semantics=dims, vmem_limit_bytes=V7X_VMEM_LIMIT)


def _silu(v):
    return v * (1.0 / (1.0 + jnp.exp(-v)))


def _softplus(v):
    return jnp.maximum(v, 0.0) + jnp.log(1.0 + jnp.exp(-jnp.abs(v)))


def _rms_rows(x, g):
    ms = jnp.mean(x * x, axis=-1, keepdims=True)
    return x * lax.rsqrt(ms + NORM_EPS) * g


def _dot(a, b):
    return jnp.dot(a, b, preferred_element_type=F32)


def _dot_nt(a, b):
    return lax.dot_general(a, b, (((1,), (1,)), ((), ())), preferred_element_type=F32)


PROJ_CHUNK = 512


def _norm_proj_kernel(x_ref, g_ref, w_ref, o_ref):
    xn = _rms_rows(x_ref[...], g_ref[...]).astype(BF16)
    for c in range(o_ref.shape[1] // PROJ_CHUNK):
        cols = slice(c * PROJ_CHUNK, (c + 1) * PROJ_CHUNK)
        o_ref[:, cols] = _dot(xn, w_ref[:, cols]).astype(o_ref.dtype)


CONV_HALO = 8


def _even_in_proj_kernel(x_ref, xp_ref, xn_ref, g_ref, w_ref, wdt_ref, cw_ref, cb_ref, o_ref, ox_ref, obt_ref, oc_ref,
                         odt_ref, *, tiles_per_seq):
    tm = x_ref.shape[0]
    n_main = o_ref.shape[1]
    i = pl.program_id(0) % tiles_per_seq
    gain = g_ref[...]
    xc = _rms_rows(x_ref[...], gain)
    xp = jnp.where(i > 0, _rms_rows(xp_ref[...], gain), 0.0)
    xn = jnp.where(i < tiles_per_seq - 1, _rms_rows(xn_ref[...], gain), 0.0)
    xc_bf = xc.astype(BF16)
    xe_bf = jnp.concatenate([xp, xc, xn], axis=0).astype(BF16)
    odt_ref[...] = _dot_nt(wdt_ref[...], xc_bf)
    n_ext = tm + 2 * CONV_HALO
    left = SSD_CONV // 2
    n_main_chunks = n_main // PROJ_CHUNK
    nx, nb, ncc = (r // PROJ_CHUNK for r in (ox_ref.shape[1], obt_ref.shape[0], oc_ref.shape[1]))
    n_conv_chunks = nx + nb + ncc

    def main_chunk(c):
        cols = slice(c * PROJ_CHUNK, (c + 1) * PROJ_CHUNK)
        o_ref[:, cols] = _dot(xc_bf, w_ref[:, cols]).astype(o_ref.dtype)

    def conv_chunk(c, pr):
        cols = slice(c * PROJ_CHUNK, (c + 1) * PROJ_CHUNK)
        acc = jnp.broadcast_to(cb_ref[:, cols], (tm, PROJ_CHUNK))
        for k in range(SSD_CONV):
            shift = (left - k) % n_ext
            tap = pr if shift == 0 else pltpu.roll(pr, shift, axis=0)
            acc = acc + tap[CONV_HALO:CONV_HALO + tm] * cw_ref[k:k + 1, cols]
        out = _silu(acc)
        if c < nx:
            ox_ref[:, cols] = out.astype(ox_ref.dtype)
        elif c < nx + nb:
            obt_ref[(c - nx) * PROJ_CHUNK:(c - nx + 1) * PROJ_CHUNK, :] = out.T.astype(obt_ref.dtype)
        else:
            oc_ref[:, (c - nx - nb) * PROJ_CHUNK:(c - nx - nb + 1) * PROJ_CHUNK] = out.astype(oc_ref.dtype)

    for c in range(max(n_main_chunks, n_conv_chunks)):
        pr = None
        if c < n_conv_chunks:
            pr = _dot(xe_bf, w_ref[:, n_main + c * PROJ_CHUNK:n_main + (c + 1) * PROJ_CHUNK])
        if c < n_main_chunks:
            main_chunk(c)
        if pr is not None:
            conv_chunk(c, pr)


def _even_in_proj(x2d, g, w, w_dt, conv_w, conv_b, *, tm, seq, n_main, name):
    m, d = x2d.shape
    n = w.shape[1]
    n_x = SSD_D_INNER
    n_bc = SSD_GROUPS * SSD_STATE
    assert n >= n_main + n_x + 2 * n_bc and n_x % PROJ_CHUNK == 0 and n_bc % PROJ_CHUNK == 0
    ns = w_dt.shape[0]
    assert n_main % PROJ_CHUNK == 0 and seq % tm == 0 and tm % CONV_HALO == 0
    halo_per_tile = tm // CONV_HALO
    n_halo_blocks = m // CONV_HALO
    resident = lambda shape: pl.BlockSpec(shape, lambda i: (0, 0), pipeline_mode=pl.Buffered(1))
    kern = functools.partial(_even_in_proj_kernel, tiles_per_seq=seq // tm)
    return pl.pallas_call(
        kern, grid=(m // tm,),
        in_specs=[
            pl.BlockSpec((tm, d), lambda i: (i, 0)),
            pl.BlockSpec((CONV_HALO, d), lambda i: (jnp.maximum(i * halo_per_tile - 1, 0), 0)),
            pl.BlockSpec((CONV_HALO, d), lambda i: (jnp.minimum((i + 1) * halo_per_tile, n_halo_blocks - 1), 0)),
            pl.BlockSpec((1, d), lambda i: (0, 0)),
            resident((d, n)), resident((ns, d)), resident(conv_w.shape), resident(conv_b.shape),
        ],
        out_specs=[pl.BlockSpec((tm, n_main), lambda i: (i, 0)), pl.BlockSpec((tm, n_x), lambda i: (i, 0)),
                   pl.BlockSpec((n_bc, tm), lambda i: (0, i)), pl.BlockSpec((tm, n_bc), lambda i: (i, 0)),
                   pl.BlockSpec((ns, tm), lambda i: (0, i))],
        out_shape=[jax.ShapeDtypeStruct((m, n_main), BF16), jax.ShapeDtypeStruct((m, n_x), BF16),
                   jax.ShapeDtypeStruct((n_bc, m), BF16), jax.ShapeDtypeStruct((m, n_bc), BF16),
                   jax.ShapeDtypeStruct((ns, m), F32)],
        compiler_params=_params(("parallel",)), name=name,
    )(x2d, x2d, x2d, g, w, w_dt, conv_w, conv_b)


def _norm_proj(x2d, g, w, *, tm, name):
    m, d = x2d.shape
    n = w.shape[1]
    assert n % PROJ_CHUNK == 0
    resident = lambda shape: pl.BlockSpec(shape, lambda i: (0, 0), pipeline_mode=pl.Buffered(1))
    return pl.pallas_call(
        _norm_proj_kernel, grid=(m // tm,),
        in_specs=[pl.BlockSpec((tm, d), lambda i: (i, 0)), pl.BlockSpec((1, d), lambda i: (0, 0)), resident((d, n))],
        out_specs=pl.BlockSpec((tm, n), lambda i: (i, 0)), out_shape=jax.ShapeDtypeStruct((m, n), BF16),
        compiler_params=_params(("parallel",)), name=name)(x2d, g, w)


NA_PREP_ROWS = 256
NA_GROUP_ROWS = 4
NA_WIN_ROWS = NA_KH + NA_GROUP_ROWS
NA_DY = 2 * NA_KH - 1
NA_DX = 2 * NA_KW - 1


def _na_group_plan(rows):
    sigs, starts, classes = [], [], []
    for gq in range(rows // NA_GROUP_ROWS):
        ks = min(max(gq * NA_GROUP_ROWS - NA_KH // 2, 0), rows - NA_WIN_ROWS)
        sig = tuple((min(max(r - NA_KH // 2, 0), rows - NA_KH) - ks, r - ks)
                    for r in range(gq * NA_GROUP_ROWS, (gq + 1) * NA_GROUP_ROWS))
        assert all(0 <= first and first + NA_KH <= NA_WIN_ROWS for first, _ in sig)
        if sig not in sigs:
            sigs.append(sig)
        starts.append(ks)
        classes.append(sigs.index(sig))
    return sigs, starts, classes


def _pair_head_rms(x, g):
    lo = lax.broadcasted_iota(jnp.int32, x.shape, 1) < NA_HEAD_DIM
    x2 = x * x
    s_lo = jnp.sum(jnp.where(lo, x2, 0.0), axis=-1, keepdims=True)
    s_hi = jnp.sum(jnp.where(lo, 0.0, x2), axis=-1, keepdims=True)
    ms = jnp.where(lo, s_lo, s_hi) * (1.0 / NA_HEAD_DIM)
    return x * lax.rsqrt(ms + NORM_EPS) * g


def _na_kernel(plan_ref, q_ref, k_ref, v_ref, qg_ref, kg_ref, bias_ref, o_ref, q_s, k_s, s_a, s_b, m_a, m_b, *, rows):
    scale = NA_HEAD_DIM ** -0.5 * LOG2E

    same_head = (lax.broadcasted_iota(jnp.int32, (V7X_LANES, V7X_LANES), 0) // NA_HEAD_DIM
                 == lax.broadcasted_iota(jnp.int32, (V7X_LANES, V7X_LANES), 1) // NA_HEAD_DIM)
    head_ones = jnp.where(same_head, 1.0, 0.0).astype(BF16)

    def head_rms(x, g):
        ms = _dot((x * x).astype(BF16), head_ones) * (1.0 / NA_HEAD_DIM)
        return x * lax.rsqrt(ms + NORM_EPS) * g

    def prep(i, carry):
        sl = pl.ds(pl.multiple_of(i * NA_PREP_ROWS, NA_PREP_ROWS), NA_PREP_ROWS)
        q_s[sl, :] = (head_rms(q_ref[0, sl, :].astype(F32), qg_ref[...]) * scale).astype(BF16)
        k_s[sl, :] = head_rms(k_ref[0, sl, :].astype(F32), kg_ref[...]).astype(BF16)
        return carry

    lax.fori_loop(0, (rows * GRID_W) // NA_PREP_ROWS, prep, 0, unroll=4)

    n_q = NA_GROUP_ROWS * GRID_W
    n_keys = NA_WIN_ROWS * GRID_W
    n_groups = rows // NA_GROUP_ROWS
    heads = [slice(h * NA_HEAD_DIM, (h + 1) * NA_HEAD_DIM) for h in range(2)]

    def key_rows(g):
        return pl.ds(pl.multiple_of(plan_ref[0, g] * GRID_W, GRID_W), n_keys)

    def query_rows(g):
        return pl.ds(pl.multiple_of(g * n_q, n_q), n_q)

    def scores(g, s_buf, m_buf):
        q = q_s[query_rows(g), :]
        kk = k_s[key_rows(g), :]
        cls = plan_ref[1, g]
        for h, hs in enumerate(heads):
            s = _dot_nt(q[:, hs], kk[:, hs]) + bias_ref[0, cls, h]
            s_buf[h] = s
            m_buf[h] = jnp.max(s, axis=-1, keepdims=True)

    def finish(g, s_buf, m_buf):
        vv = v_ref[0, key_rows(g), :]
        outs = []
        for h, hs in enumerate(heads):
            p = jnp.exp2(s_buf[h] - m_buf[h])
            l = jnp.sum(p, axis=-1, keepdims=True)
            outs.append(_dot(p.astype(BF16), vv[:, hs]) * (1.0 / l))
        o_ref[0, query_rows(g), :] = jnp.concatenate(outs, axis=-1).astype(BF16)

    scores(0, s_a, m_a)

    def pair_body(i, carry):
        g = 2 * i
        scores(g + 1, s_b, m_b)
        finish(g, s_a, m_a)
        scores(jnp.minimum(g + 2, n_groups - 1), s_a, m_a)
        finish(g + 1, s_b, m_b)
        return carry

    lax.fori_loop(0, n_groups // 2, pair_body, 0)


def _na_bias_kernel(rpb_ref, o_ref, t_s, *, sigs):
    h = pl.program_id(0)
    q = lax.broadcasted_iota(jnp.int32, (GRID_W, GRID_W), 0)
    k = lax.broadcasted_iota(jnp.int32, (GRID_W, GRID_W), 1)
    dx = jnp.clip(k - q, -(NA_KW - 1), NA_KW - 1) + (NA_KW - 1)
    col_start = jnp.clip(q - NA_KW // 2, 0, GRID_W - NA_KW)
    in_win = (k >= col_start) & (k < col_start + NA_KW)
    masked = jnp.full((GRID_W, GRID_W), MASK_VALUE, F32)
    for dy in range(NA_DY):
        base = (h * NA_DY + dy) * NA_DX
        t = masked
        for d in range(NA_DX):
            t = jnp.where(dx == d, rpb_ref[base + d] * LOG2E, t)
        t_s[dy] = jnp.where(in_win, t, MASK_VALUE)
    for cls, sig in enumerate(sigs):
        for rq, (first, qrow) in enumerate(sig):
            for jk in range(NA_WIN_ROWS):
                attended = first <= jk < first + NA_KH
                tile = t_s[jk - qrow + NA_KH - 1] if attended else masked
                o_ref[0, cls, 0, rq * GRID_W:(rq + 1) * GRID_W, jk * GRID_W:(jk + 1) * GRID_W] = tile


def _na_bias_table(rpb, sigs):
    n_q = NA_GROUP_ROWS * GRID_W
    n_keys = NA_WIN_ROWS * GRID_W
    kern = functools.partial(_na_bias_kernel, sigs=sigs)
    return pl.pallas_call(
        kern, grid=(NA_HEADS,),
        in_specs=[pl.BlockSpec(memory_space=pltpu.SMEM)],
        out_specs=pl.BlockSpec((1, len(sigs), 1, n_q, n_keys), lambda h: (h // 2, 0, h % 2, 0, 0)),
        out_shape=jax.ShapeDtypeStruct((NA_HEADS // 2, len(sigs), 2, n_q, n_keys), F32),
        scratch_shapes=[pltpu.VMEM((NA_DY, GRID_W, GRID_W), F32)],
        compiler_params=_params(("parallel",)), name="na_bias_table",
    )(rpb.astype(F32).reshape(-1))


def _neighbourhood_attention(proj3d, q_gain, k_gain, rpb):
    b, s, _ = proj3d.shape
    rows = s // GRID_W
    assert rows >= NA_WIN_ROWS and rows % (2 * NA_GROUP_ROWS) == 0
    sigs, starts, classes = _na_group_plan(rows)
    bias = _na_bias_table(rpb, sigs)
    plan = jnp.array([starts, classes], jnp.int32)
    qg = jnp.tile(q_gain.astype(F32), 2)[None, :]
    kg = jnp.tile(k_gain.astype(F32), 2)[None, :]
    n_pairs = NA_HEADS // 2
    blk = (1, s, 2 * NA_HEAD_DIM)
    kern = functools.partial(_na_kernel, rows=rows)
    return pl.pallas_call(
        kern, grid=(n_pairs, b),
        in_specs=[
            pl.BlockSpec(memory_space=pltpu.SMEM),
            pl.BlockSpec(blk, lambda p, i: (i, 0, p)),
            pl.BlockSpec(blk, lambda p, i: (i, 0, n_pairs + p)),
            pl.BlockSpec(blk, lambda p, i: (i, 0, 2 * n_pairs + p)),
            pl.BlockSpec((1, 2 * NA_HEAD_DIM), lambda p, i: (0, 0)),
            pl.BlockSpec((1, 2 * NA_HEAD_DIM), lambda p, i: (0, 0)),
            pl.BlockSpec((1,) + bias.shape[1:], lambda p, i: (p, 0, 0, 0, 0)),
        ],
        out_specs=pl.BlockSpec(blk, lambda p, i: (i, 0, p)),
        out_shape=jax.ShapeDtypeStruct((b, s, NA_WIDTH), BF16),
        scratch_shapes=([pltpu.VMEM((s, 2 * NA_HEAD_DIM), BF16)] * 2
                        + [pltpu.VMEM((2, NA_GROUP_ROWS * GRID_W, NA_WIN_ROWS * GRID_W), F32)] * 2
                        + [pltpu.VMEM((2, NA_GROUP_ROWS * GRID_W, 1), F32)] * 2),
        compiler_params=_params(("parallel", "parallel")), name="neighbourhood_attention",
    )(plan, proj3d, proj3d, proj3d, qg, kg, bias)


def _chunk_scan(a, reverse):
    n = a.shape[1]
    pos = lax.broadcasted_iota(jnp.int32, a.shape, 1) % SSD_CHUNK
    sh = 1
    while sh < SSD_CHUNK:
        if reverse:
            a = a + jnp.where(pos < SSD_CHUNK - sh, pltpu.roll(a, n - sh, axis=1), 0.0)
        else:
            a = a + jnp.where(pos >= sh, pltpu.roll(a, sh, axis=1), 0.0)
        sh *= 2
    return a


def _head_row(tile, first):
    lo = lax.broadcasted_iota(jnp.int32, (1, V7X_LANES), 1) < SSD_HEAD_DIM
    halves = [jnp.where(lo, tile[first + 2 * i:first + 2 * i + 1, :], tile[first + 2 * i + 1:first + 2 * i + 2, :])
              for i in range(SSD_HEADS_PER_GROUP // 2)]
    return jnp.concatenate(halves, axis=1)


def _ssd_kernel(xs_ref, bt_ref, c_ref, z_ref, dtr_ref, biasr_ref, alogr_ref, dskip_ref, gain_ref, o_ref,
                cumr_s, g2r_s, ld2r_s, wr_s, decr_s, st_s, h_s, *, n_chunks):
    L = SSD_CHUNK
    hpg = SSD_HEADS_PER_GROUP
    gw = SSD_GROUP_WIDTH
    ns = SSD_STATE

    dt = _softplus(dtr_ref[...] + biasr_ref[0])
    a = dt * (-jnp.exp(alogr_ref[0]))
    is_fwd = lax.broadcasted_iota(jnp.int32, a.shape, 0) < hpg
    prefix = _chunk_scan(a, False)
    suffix = _chunk_scan(a, True)
    cum2 = jnp.where(is_fwd, prefix, suffix) * LOG2E
    cumr_s[...] = cum2
    g2r_s[...] = cum2 - jnp.log2(dt)
    ld2r_s[...] = jnp.log2(dt + pltpu.roll(dt, hpg, axis=0))
    wr_s[...] = dt * jnp.exp(jnp.where(is_fwd, suffix, prefix) - a)
    decr_s[...] = jnp.exp(prefix + suffix - a)

    def chunk_slice(c):
        return pl.ds(pl.multiple_of(c * L, L), L)

    lane_head = lax.broadcasted_iota(jnp.int32, (L, gw), 1) // SSD_HEAD_DIM

    def block_diag_x(xs):
        return jnp.concatenate([jnp.where(lane_head == j, xs, jnp.zeros_like(xs)) for j in range(hpg)], axis=0)

    def state_body(c, carry):
        sl = chunk_slice(c)
        bt = bt_ref[:, sl].astype(F32)
        w = wr_s[:, sl]
        lhs = jnp.concatenate(
            [jnp.concatenate([(bt * w[d * hpg + j:d * hpg + j + 1, :]).astype(BF16) for j in range(hpg)], axis=1)
             for d in range(2)], axis=0)
        st_s[c] = _dot(lhs, block_diag_x(xs_ref[0, sl, :]))
        return carry

    lax.fori_loop(0, n_chunks, state_body, 0, unroll=8)

    def fwd_rec(c, h):
        h_s[c, :, 0:gw] = h.astype(BF16)
        return h * _head_row(decr_s[:, chunk_slice(c)], 0) + st_s[c, 0:ns, :]

    def bwd_rec(i, h):
        c = n_chunks - 1 - i
        h_s[c, :, gw:2 * gw] = h.astype(BF16)
        return h * _head_row(decr_s[:, chunk_slice(c)], hpg) + st_s[c, ns:2 * ns, :]

    h0 = jnp.zeros((ns, gw), F32)
    lax.fori_loop(0, n_chunks, fwd_rec, h0)
    lax.fori_loop(0, n_chunks, bwd_rec, h0)

    li = lax.broadcasted_iota(jnp.int32, (L, L), 0)
    si = lax.broadcasted_iota(jnp.int32, (L, L), 1)
    below = si < li
    above = si > li
    lane_lo = lax.broadcasted_iota(jnp.int32, (L, V7X_LANES), 1) < SSD_HEAD_DIM

    def out_body(c, carry):
        sl = chunk_slice(c)
        xs = xs_ref[0, sl, :]
        cm = c_ref[0, sl, :]
        cum_r = cumr_s[:, sl]
        g2 = g2r_s[:, sl]
        ld2 = ld2r_s[:, sl]
        cum_t = [jnp.broadcast_to(cum_r[k:k + 1, :], (L, L)).T for k in range(2 * hpg)]
        cb = _dot(cm, bt_ref[:, sl])
        mats = []
        for j in range(hpg):
            seg_f = cum_t[j] - g2[j:j + 1, :]
            seg_b = cum_t[hpg + j] - g2[hpg + j:hpg + j + 1, :]
            arg = jnp.where(below, seg_f, jnp.where(above, seg_b, ld2[j:j + 1, :]))
            mats.append((cb * jnp.exp2(arg)).astype(BF16))
        y = _dot(jnp.concatenate(mats, axis=1), block_diag_x(xs))
        carried = _dot(cm, h_s[c])
        for d in range(2):
            decay = jnp.exp2(jnp.concatenate(
                [jnp.where(lane_lo, cum_t[d * hpg + 2 * i], cum_t[d * hpg + 2 * i + 1]) for i in range(hpg // 2)],
                axis=1))
            y = y + carried[:, d * gw:(d + 1) * gw] * decay
        y = y + dskip_ref[0] * xs.astype(F32)
        gated = y * _silu(z_ref[0, sl, :].astype(F32))
        ms = jnp.mean(gated * gated, axis=-1, keepdims=True)
        o_ref[0, sl, :] = (gated * lax.rsqrt(ms + NORM_EPS) * gain_ref[0]).astype(BF16)
        return carry

    lax.fori_loop(0, n_chunks, out_body, 0, unroll=8)


def _group_major(v):
    return jnp.transpose(v.astype(F32).reshape(2, SSD_GROUPS, SSD_HEADS_PER_GROUP), (1, 0, 2)).reshape(
        SSD_GROUPS, 2 * SSD_HEADS_PER_GROUP)


def _ssd_mixer(proj3d, xs3d, bt2d, c3d, dt_rows, dt_bias, a_log, d_skip, out_gain):
    b, s, _ = proj3d.shape
    n_chunks = s // SSD_CHUNK
    g, hpg, gw, ns = SSD_GROUPS, SSD_HEADS_PER_GROUP, SSD_GROUP_WIDTH, SSD_STATE
    bias_g = _group_major(dt_bias)[:, :, None]
    alog_g = _group_major(a_log)[:, :, None]
    dskip = jnp.repeat(d_skip.astype(F32), SSD_HEAD_DIM).reshape(g, 1, gw)
    gain = out_gain.astype(F32).reshape(g, 1, gw)
    z_blk = (3 * NA_WIDTH) // gw
    kern = functools.partial(_ssd_kernel, n_chunks=n_chunks)
    small = lambda shape: pl.BlockSpec((1,) + shape, lambda i, k: (k, 0, 0))
    seq_blk = lambda width, blk0: pl.BlockSpec((1, s, width), lambda i, k: (i, 0, blk0 + k))
    row_scratch = pltpu.VMEM((2 * hpg, s), F32)
    return pl.pallas_call(
        kern, grid=(b, g),
        in_specs=[
            seq_blk(gw, 0), pl.BlockSpec((ns, s), lambda i, k: (k, i)), seq_blk(ns, 0), seq_blk(gw, z_blk),
            pl.BlockSpec((2 * hpg, s), lambda i, k: (k, i)),
            small((2 * hpg, 1)), small((2 * hpg, 1)), small((1, gw)), small((1, gw)),
        ],
        out_specs=pl.BlockSpec((1, s, gw), lambda i, k: (i, 0, k)),
        out_shape=jax.ShapeDtypeStruct((b, s, SSD_D_INNER), BF16),
        scratch_shapes=[
            row_scratch, row_scratch, row_scratch, row_scratch, row_scratch,
            pltpu.VMEM((n_chunks, 2 * ns, gw), F32), pltpu.VMEM((n_chunks, ns, 2 * gw), BF16),
        ],
        compiler_params=_params(("parallel", "parallel")), name="ssd_bidirectional",
    )(xs3d, bt2d, c3d, proj3d, dt_rows, bias_g, alog_g, dskip, gain)


FFN_CHUNK = 256


def _mix_ffn_kernel(*refs, n_acts, hidden):
    act_refs = refs[:n_acts]
    wout_refs = refs[n_acts:2 * n_acts]
    x_ref, g_ref, w13_ref, w2_ref, o_ref, hid_ref = refs[2 * n_acts:]
    h = x_ref[...]
    for a_ref, w_ref in zip(act_refs, wout_refs):
        h = h + _dot(a_ref[...], w_ref[...])
    hn = _rms_rows(h, g_ref[...]).astype(BF16)
    for c in range(hidden // FFN_CHUNK):
        gate = slice(c * FFN_CHUNK, (c + 1) * FFN_CHUNK)
        up = slice(hidden + c * FFN_CHUNK, hidden + (c + 1) * FFN_CHUNK)
        hid_ref[:, gate] = (_silu(_dot(hn, w13_ref[:, gate])) * _dot(hn, w13_ref[:, up])).astype(BF16)
    o_ref[...] = h + _dot(hid_ref[...], w2_ref[...])


def _mix_ffn(acts, w_outs, x2d, g, w13, w2, layer, *, tm, name):
    m, d = x2d.shape
    hid = w2.shape[1]
    assert hid % FFN_CHUNK == 0
    row = lambda width: pl.BlockSpec((tm, width), lambda i: (i, 0))
    resident = lambda shape: pl.BlockSpec(shape, lambda i: (0, 0), pipeline_mode=pl.Buffered(1))
    stacked = lambda shape: pl.BlockSpec((None,) + shape, lambda i: (layer, 0, 0), pipeline_mode=pl.Buffered(1))
    return pl.pallas_call(
        functools.partial(_mix_ffn_kernel, n_acts=len(acts), hidden=hid), grid=(m // tm,),
        in_specs=([row(a.shape[1]) for a in acts] + [resident(w.shape) for w in w_outs]
                  + [row(d), pl.BlockSpec((1, d), lambda i: (0, 0)), stacked((d, 2 * hid)), stacked((hid, d))]),
        out_specs=row(d),
        out_shape=jax.ShapeDtypeStruct((m, d), F32),
        scratch_shapes=[pltpu.VMEM((tm, hid), BF16)],
        compiler_params=_params(("parallel",)), name=name,
    )(*acts, *w_outs, x2d, g, w13, w2)


def _rope_prep_kernel(p_ref, qg_ref, kg_ref, cos_ref, sin_ref, cost_ref, sint_ref, q_ref, k_ref, v_ref):
    scale = GQA_HEAD_DIM ** -0.5 * LOG2E
    ts = p_ref.shape[1]
    cos_t = cost_ref[...]
    sin_t = sint_ref[...]
    even_row3 = (lax.broadcasted_iota(jnp.int32, (2 * GQA_HEAD_DIM // 8, 8, ts), 1) % 2) == 0
    q_gain = qg_ref[...] * scale
    hd = GQA_HEAD_DIM
    for pair in range(GQA_HEADS // 2):
        xt = p_ref[0, :, pair * 128:(pair + 1) * 128].astype(F32).T
        x2 = xt * xt
        inv = [lax.rsqrt(jnp.mean(x2[h * hd:(h + 1) * hd], axis=0, keepdims=True) + NORM_EPS) for h in range(2)]
        xn = jnp.concatenate([xt[:hd] * inv[0], xt[hd:] * inv[1]], axis=0) * q_gain
        x3 = xn.reshape(2 * hd // 8, 8, ts)
        swapped = jnp.where(even_row3, pltpu.roll(x3, 7, axis=1), pltpu.roll(x3, 1, axis=1)).reshape(2 * hd, ts)
        out = xn * cos_t + swapped * sin_t
        q_ref[0, 2 * pair] = out[:hd].astype(BF16)
        q_ref[0, 2 * pair + 1] = out[hd:].astype(BF16)

    cos = cos_ref[...]
    sin = sin_ref[...]
    even = (lax.broadcasted_iota(jnp.int32, cos.shape, 1) % 2) == 0
    for pair in range(GQA_KV_HEADS // 2):
        c0 = GQA_Q_WIDTH + pair * 128
        xn = _pair_head_rms(p_ref[0, :, c0:c0 + 128].astype(F32), kg_ref[...])
        swapped = jnp.where(even, pltpu.roll(xn, V7X_LANES - 1, axis=1), pltpu.roll(xn, 1, axis=1))
        blk = xn * cos + swapped * sin
        k_ref[0, 2 * pair] = blk[:, :hd].astype(BF16)
        k_ref[0, 2 * pair + 1] = blk[:, hd:].astype(BF16)
        c1 = GQA_Q_WIDTH + GQA_KV_WIDTH + pair * 128
        vt = p_ref[0, :, c1:c1 + 128].astype(F32).T.astype(BF16)
        v_ref[0, 2 * pair] = vt[:hd]
        v_ref[0, 2 * pair + 1] = vt[hd:]


def _axial_rope_tables(s):
    t = jnp.arange(s)
    row = (t // GRID_W).astype(F32)
    col = (t % GRID_W).astype(F32)
    axis_dims = GQA_HEAD_DIM // 2
    freqs = ROPE_THETA ** (-jnp.arange(0, axis_dims, 2, dtype=F32) / axis_dims)
    ang = jnp.concatenate([row[:, None] * freqs, col[:, None] * freqs], axis=-1)
    cos = jnp.repeat(jnp.cos(ang), 2, axis=-1)
    sin = jnp.stack([-jnp.sin(ang), jnp.sin(ang)], axis=-1).reshape(s, GQA_HEAD_DIM)
    return jnp.tile(cos, (1, 2)), jnp.tile(sin, (1, 2))


def _rope_prep(proj3d, q_gain, k_gain, *, ts):
    b, s, width = proj3d.shape
    cos, sin = _axial_rope_tables(s)
    qg = jnp.tile(q_gain.astype(F32), 2)[:, None]
    kg = jnp.tile(k_gain.astype(F32), 2)[None, :]
    head_out = lambda n: pl.BlockSpec((1, n, ts, GQA_HEAD_DIM), lambda i, t: (i, 0, t, 0))
    shape = lambda n: jax.ShapeDtypeStruct((b, n, s, GQA_HEAD_DIM), BF16)
    t_out = lambda n: pl.BlockSpec((1, n, GQA_HEAD_DIM, ts), lambda i, t: (i, 0, 0, t))
    t_shape = lambda n: jax.ShapeDtypeStruct((b, n, GQA_HEAD_DIM, s), BF16)
    return pl.pallas_call(
        _rope_prep_kernel, grid=(b, s // ts),
        in_specs=[
            pl.BlockSpec((1, ts, width), lambda i, t: (i, t, 0)),
            pl.BlockSpec((128, 1), lambda i, t: (0, 0)),
            pl.BlockSpec((1, 128), lambda i, t: (0, 0)),
            pl.BlockSpec((ts, 128), lambda i, t: (t, 0)),
            pl.BlockSpec((ts, 128), lambda i, t: (t, 0)),
            pl.BlockSpec((128, ts), lambda i, t: (0, t)),
            pl.BlockSpec((128, ts), lambda i, t: (0, t)),
        ],
        out_specs=[t_out(GQA_HEADS), head_out(GQA_KV_HEADS), t_out(GQA_KV_HEADS)],
        out_shape=[t_shape(GQA_HEADS), shape(GQA_KV_HEADS), t_shape(GQA_KV_HEADS)],
        compiler_params=_params(("parallel", "parallel")), name="gqa_norm_rope",
    )(proj3d, qg, kg, cos, sin, cos.T, sin.T)


GQA_KV_CHUNK = 256
GQA_SUM_ROWS = 16
GQA_SUB_Q = 128


def _gqa_kernel(q_ref, k_ref, vt_ref, o_ref, s_a, s_b, m_a, m_b, *, tq, seq):
    t = pl.program_id(0)
    n_sub = tq // GQA_SUB_Q
    cols = GQA_REP * GQA_SUB_Q

    @pl.when(t == 0)
    def _():
        s_b[...] = jnp.zeros(s_b.shape, F32)
        m_b[...] = jnp.zeros(m_b.shape, F32)

    def sub_step(sub, s_cur, m_cur, s_prev, m_prev_ref):
        qs = slice(sub * GQA_SUB_Q, (sub + 1) * GQA_SUB_Q)
        qt = jnp.concatenate([q_ref[0, r, :, qs] for r in range(GQA_REP)], axis=1)
        m_prev = m_prev_ref[sub]
        m_run = None
        ones = jnp.ones((GQA_SUM_ROWS, GQA_KV_CHUNK), BF16)
        acc = jnp.zeros((GQA_HEAD_DIM + GQA_SUM_ROWS, cols), F32)
        for i in range(seq // GQA_KV_CHUNK):
            rows = slice(i * GQA_KV_CHUNK, (i + 1) * GQA_KV_CHUNK)
            st = _dot(k_ref[0, 0, rows, :], qt)
            s_cur[sub, rows, :] = st
            cm = jnp.max(st, axis=0, keepdims=True)
            m_run = cm if m_run is None else jnp.maximum(m_run, cm)
            p = jnp.exp2(s_prev[sub, rows, :] - m_prev)
            vt_aug = jnp.concatenate([vt_ref[0, 0, :, rows], ones], axis=0)
            acc = acc + _dot(vt_aug, p.astype(BF16))
        m_cur[sub] = m_run
        ot = acc[:GQA_HEAD_DIM] * (1.0 / acc[GQA_HEAD_DIM:GQA_HEAD_DIM + 1])
        for r in range(GQA_REP):
            o_ref[0, qs, r * GQA_HEAD_DIM:(r + 1) * GQA_HEAD_DIM] = (
                ot[:, r * GQA_SUB_Q:(r + 1) * GQA_SUB_Q].T.astype(BF16))

    def step(*bufs):
        for sub in range(n_sub):
            sub_step(sub, *bufs)

    pl.when(t % 2 == 0)(lambda: step(s_a, m_a, s_b, m_b))
    pl.when(t % 2 == 1)(lambda: step(s_b, m_b, s_a, m_a))


def _gqa_attention(q, k, vt, *, tq):
    b, _, _, s = q.shape
    nq = s // tq
    n_blocks = b * GQA_KV_HEADS * nq
    n_sub = tq // GQA_SUB_Q
    cols = GQA_REP * GQA_SUB_Q

    def unravel(u):
        return u // (nq * GQA_KV_HEADS), (u // nq) % GQA_KV_HEADS, u % nq

    def score_block(t):
        return unravel(jnp.minimum(t, n_blocks - 1))

    def finish_block(t):
        return unravel(jnp.maximum(t - 1, 0))

    def q_map(t):
        i, g, j = score_block(t)
        return (i, g, 0, j)

    def k_map(t):
        i, g, _ = score_block(t)
        return (i, g, 0, 0)

    def vt_map(t):
        i, g, _ = finish_block(t)
        return (i, g, 0, 0)

    def o_map(t):
        i, g, j = finish_block(t)
        return (i, j, g)

    kern = functools.partial(_gqa_kernel, tq=tq, seq=s)
    return pl.pallas_call(
        kern, grid=(n_blocks + 1,),
        in_specs=[
            pl.BlockSpec((1, GQA_REP, GQA_HEAD_DIM, tq), q_map),
            pl.BlockSpec((1, 1, s, GQA_HEAD_DIM), k_map),
            pl.BlockSpec((1, 1, GQA_HEAD_DIM, s), vt_map),
        ],
        out_specs=pl.BlockSpec((1, tq, GQA_REP * GQA_HEAD_DIM), o_map),
        out_shape=jax.ShapeDtypeStruct((b, s, GQA_Q_WIDTH), BF16),
        scratch_shapes=[pltpu.VMEM((n_sub, s, cols), F32), pltpu.VMEM((n_sub, s, cols), F32),
                        pltpu.VMEM((n_sub, 1, cols), F32), pltpu.VMEM((n_sub, 1, cols), F32)],
        compiler_params=_params(("arbitrary",)), name="gqa_attention",
    )(q, k, vt)


def _even_mixer(x2d, b, s, mix_norm, w_in, q_gain, k_gain, rpb, conv_w, conv_b, dt_bias, a_log, d_skip, out_gain,
                w_out):
    w_main = w_in.astype(BF16)
    w_dt = jnp.transpose(w_in[:, EVEN_MAIN_WIDTH:].reshape(-1, 2, SSD_GROUPS, SSD_HEADS_PER_GROUP),
                         (2, 1, 3, 0)).reshape(2 * SSD_HEADS, -1).astype(BF16)
    proj, xs, bt, cc, dt_rows = _even_in_proj(x2d, mix_norm.astype(F32)[None, :], w_main, w_dt, conv_w.astype(F32),
                                              conv_b.astype(F32)[None, :], tm=1024, seq=s, n_main=EVEN_XBC_OFFSET,
                                              name="even_in_proj")
    proj3d = proj.reshape(b, s, EVEN_XBC_OFFSET)
    na_out = _neighbourhood_attention(proj3d, q_gain, k_gain, rpb)
    ssd_out = _ssd_mixer(proj3d, xs.reshape(b, s, -1), bt, cc.reshape(b, s, -1), dt_rows, dt_bias, a_log, d_skip,
                         out_gain)
    w_out_bf = w_out.astype(BF16)
    return ([na_out.reshape(b * s, NA_WIDTH), ssd_out.reshape(b * s, SSD_D_INNER)],
            [w_out_bf[:NA_WIDTH], w_out_bf[NA_WIDTH:]])


def _odd_mixer(x2d, b, s, mix_norm, w_qkv, q_gain, k_gain, w_out):
    proj = _norm_proj(x2d, mix_norm.astype(F32)[None, :], w_qkv.astype(BF16), tm=1024, name="odd_qkv_proj")
    q, k, vt = _rope_prep(proj.reshape(b, s, -1), q_gain, k_gain, ts=1024)
    attn = _gqa_attention(q, k, vt, tq=512)
    return [attn.reshape(b * s, GQA_Q_WIDTH)], [w_out.astype(BF16)]


def kernel(x, even_mix_norm, even_w_in, na_q_norm, na_k_norm, na_rel_bias, ssd_conv_w, ssd_conv_b, ssd_dt_bias, ssd_A_log, ssd_D, ssd_out_norm, even_w_out, odd_mix_norm, odd_w_qkv, gqa_q_norm, gqa_k_norm, odd_w_out, ffn_norm, ffn_w13, ffn_w2):
    b, s, d = x.shape
    depth = ffn_norm.shape[0]
    h = x.reshape(b * s, d)
    w13_bf = ffn_w13.astype(BF16)
    w2_bf = ffn_w2.astype(BF16)
    for layer in range(depth):
        i = layer // 2
        if layer % 2 == 0:
            acts, w_outs = _even_mixer(h, b, s, even_mix_norm[i], even_w_in[i], na_q_norm[i], na_k_norm[i],
                                       na_rel_bias[i], ssd_conv_w[i], ssd_conv_b[i], ssd_dt_bias[i], ssd_A_log[i],
                                       ssd_D[i], ssd_out_norm[i], even_w_out[i])
        else:
            acts, w_outs = _odd_mixer(h, b, s, odd_mix_norm[i], odd_w_qkv[i], gqa_q_norm[i], gqa_k_norm[i],
                                      odd_w_out[i])
        h = _mix_ffn(acts, w_outs, h, ffn_norm[layer].astype(F32)[None, :], w13_bf, w2_bf, layer, tm=1024,
                     name="mix_out_ffn_even" if layer % 2 == 0 else "mix_out_ffn_odd")
    return h.reshape(b, s, d)
```

```python
import functools

import jax
import jax.numpy as jnp
import numpy as np
from jax import lax
from jax.experimental import pallas as pl
from jax.experimental.pallas import tpu as pltpu

F32 = jnp.float32
BF16 = jnp.bfloat16

D_MODEL = 1024
GRID_W = 64
NORM_EPS = 1e-6

NA_HEADS = 8
NA_HEAD_DIM = 64
NA_WIDTH = NA_HEADS * NA_HEAD_DIM
NA_KH = 8
NA_KW = 16

SSD_D_INNER = 1024
SSD_HEAD_DIM = 64
SSD_HEADS = 16
SSD_GROUPS = 4
SSD_STATE = 128
SSD_CONV = 4
SSD_CHUNK = 128
SSD_CONV_DIM = SSD_D_INNER + 2 * SSD_GROUPS * SSD_STATE
SSD_GROUP_WIDTH = SSD_D_INNER // SSD_GROUPS
SSD_HEADS_PER_GROUP = SSD_HEADS // SSD_GROUPS

EVEN_MAIN_WIDTH = 3 * NA_WIDTH + SSD_D_INNER + SSD_CONV_DIM
EVEN_XBC_OFFSET = 3 * NA_WIDTH + SSD_D_INNER

GQA_HEADS = 16
GQA_KV_HEADS = 4
GQA_HEAD_DIM = 64
GQA_REP = GQA_HEADS // GQA_KV_HEADS
GQA_Q_WIDTH = GQA_HEADS * GQA_HEAD_DIM
GQA_KV_WIDTH = GQA_KV_HEADS * GQA_HEAD_DIM
ROPE_THETA = 10000.0

FFN_HIDDEN = 2816

V7X_LANES = 128
V7X_VMEM_LIMIT = 56 * 1024 * 1024
MASK_VALUE = -1e30
LOG2E = 1.4426950408889634


def _params(dims):
    return pltpu.CompilerParams(dimension_semantics=dims, vmem_limit_bytes=V7X_VMEM_LIMIT)


def _silu(v):
    return v * (1.0 / (1.0 + jnp.exp(-v)))


def _softplus(v):
    return jnp.maximum(v, 0.0) + jnp.log(1.0 + jnp.exp(-jnp.abs(v)))


def _rms_rows(x, g):
    ms = jnp.mean(x * x, axis=-1, keepdims=True)
    return x * lax.rsqrt(ms + NORM_EPS) * g


def _dot(a, b):
    return jnp.dot(a, b, preferred_element_type=F32)


def _dot_nt(a, b):
    return lax.dot_general(a, b, (((1,), (1,)), ((), ())), preferred_element_type=F32)


PROJ_CHUNK = 512


def _norm_proj_kernel(x_ref, g_ref, w_ref, o_ref):
    xn = _rms_rows(x_ref[...], g_ref[...]).astype(BF16)
    for c in range(o_ref.shape[1] // PROJ_CHUNK):
        cols = slice(c * PROJ_CHUNK, (c + 1) * PROJ_CHUNK)
        o_ref[:, cols] = _dot(xn, w_ref[:, cols]).astype(o_ref.dtype)


CONV_HALO = 8


def _even_in_proj_kernel(x_ref, xp_ref, xn_ref, g_ref, w_ref, wdt_ref, cw_ref, cb_ref, o_ref, ox_ref, obt_ref, oc_ref,
                         odt_ref, *, tiles_per_seq):
    tm = x_ref.shape[0]
    n_main = o_ref.shape[1]
    i = pl.program_id(0) % tiles_per_seq
    gain = g_ref[...]
    xc = _rms_rows(x_ref[...], gain)
    xp = jnp.where(i > 0, _rms_rows(xp_ref[...], gain), 0.0)
    xn = jnp.where(i < tiles_per_seq - 1, _rms_rows(xn_ref[...], gain), 0.0)
    xc_bf = xc.astype(BF16)
    xe_bf = jnp.concatenate([xp, xc, xn], axis=0).astype(BF16)
    odt_ref[...] = _dot_nt(wdt_ref[...], xc_bf)
    n_ext = tm + 2 * CONV_HALO
    left = SSD_CONV // 2
    n_main_chunks = n_main // PROJ_CHUNK
    nx, nb, ncc = (r // PROJ_CHUNK for r in (ox_ref.shape[1], obt_ref.shape[0], oc_ref.shape[1]))
    n_conv_chunks = nx + nb + ncc

    def main_chunk(c):
        cols = slice(c * PROJ_CHUNK, (c + 1) * PROJ_CHUNK)
        o_ref[:, cols] = _dot(xc_bf, w_ref[:, cols]).astype(o_ref.dtype)

    def conv_chunk(c, pr):
        cols = slice(c * PROJ_CHUNK, (c + 1) * PROJ_CHUNK)
        acc = jnp.broadcast_to(cb_ref[:, cols], (tm, PROJ_CHUNK))
        for k in range(SSD_CONV):
            shift = (left - k) % n_ext
            tap = pr if shift == 0 else pltpu.roll(pr, shift, axis=0)
            acc = acc + tap[CONV_HALO:CONV_HALO + tm] * cw_ref[k:k + 1, cols]
        out = _silu(acc)
        if c < nx:
            ox_ref[:, cols] = out.astype(ox_ref.dtype)
        elif c < nx + nb:
            obt_ref[(c - nx) * PROJ_CHUNK:(c - nx + 1) * PROJ_CHUNK, :] = out.T.astype(obt_ref.dtype)
        else:
            oc_ref[:, (c - nx - nb) * PROJ_CHUNK:(c - nx - nb + 1) * PROJ_CHUNK] = out.astype(oc_ref.dtype)

    for c in range(max(n_main_chunks, n_conv_chunks)):
        pr = None
        if c < n_conv_chunks:
            pr = _dot(xe_bf, w_ref[:, n_main + c * PROJ_CHUNK:n_main + (c + 1) * PROJ_CHUNK])
        if c < n_main_chunks:
            main_chunk(c)
        if pr is not None:
            conv_chunk(c, pr)


def _even_in_proj(x2d, g, w, w_dt, conv_w, conv_b, *, tm, seq, n_main, name):
    m, d = x2d.shape
    n = w.shape[1]
    n_x = SSD_D_INNER
    n_bc = SSD_GROUPS * SSD_STATE
    assert n >= n_main + n_x + 2 * n_bc and n_x % PROJ_CHUNK == 0 and n_bc % PROJ_CHUNK == 0
    ns = w_dt.shape[0]
    assert n_main % PROJ_CHUNK == 0 and seq % tm == 0 and tm % CONV_HALO == 0
    halo_per_tile = tm // CONV_HALO
    n_halo_blocks = m // CONV_HALO
    resident = lambda shape: pl.BlockSpec(shape, lambda i: (0, 0), pipeline_mode=pl.Buffered(1))
    kern = functools.partial(_even_in_proj_kernel, tiles_per_seq=seq // tm)
    return pl.pallas_call(
        kern, grid=(m // tm,),
        in_specs=[
            pl.BlockSpec((tm, d), lambda i: (i, 0)),
            pl.BlockSpec((CONV_HALO, d), lambda i: (jnp.maximum(i * halo_per_tile - 1, 0), 0)),
            pl.BlockSpec((CONV_HALO, d), lambda i: (jnp.minimum((i + 1) * halo_per_tile, n_halo_blocks - 1), 0)),
            pl.BlockSpec((1, d), lambda i: (0, 0)),
            resident((d, n)), resident((ns, d)), resident(conv_w.shape), resident(conv_b.shape),
        ],
        out_specs=[pl.BlockSpec((tm, n_main), lambda i: (i, 0)), pl.BlockSpec((tm, n_x), lambda i: (i, 0)),
                   pl.BlockSpec((n_bc, tm), lambda i: (0, i)), pl.BlockSpec((tm, n_bc), lambda i: (i, 0)),
                   pl.BlockSpec((ns, tm), lambda i: (0, i))],
        out_shape=[jax.ShapeDtypeStruct((m, n_main), BF16), jax.ShapeDtypeStruct((m, n_x), BF16),
                   jax.ShapeDtypeStruct((n_bc, m), BF16), jax.ShapeDtypeStruct((m, n_bc), BF16),
                   jax.ShapeDtypeStruct((ns, m), F32)],
        compiler_params=_params(("parallel",)), name=name,
    )(x2d, x2d, x2d, g, w, w_dt, conv_w, conv_b)


def _norm_proj(x2d, g, w, *, tm, name):
    m, d = x2d.shape
    n = w.shape[1]
    assert n % PROJ_CHUNK == 0
    resident = lambda shape: pl.BlockSpec(shape, lambda i: (0, 0), pipeline_mode=pl.Buffered(1))
    return pl.pallas_call(
        _norm_proj_kernel, grid=(m // tm,),
        in_specs=[pl.BlockSpec((tm, d), lambda i: (i, 0)), pl.BlockSpec((1, d), lambda i: (0, 0)), resident((d, n))],
        out_specs=pl.BlockSpec((tm, n), lambda i: (i, 0)), out_shape=jax.ShapeDtypeStruct((m, n), BF16),
        compiler_params=_params(("parallel",)), name=name)(x2d, g, w)


NA_PREP_ROWS = 256
NA_GROUP_ROWS = 4
NA_WIN_ROWS = NA_KH + NA_GROUP_ROWS
NA_DY = 2 * NA_KH - 1
NA_DX = 2 * NA_KW - 1


def _na_group_plan(rows):
    sigs, starts, classes = [], [], []
    for gq in range(rows // NA_GROUP_ROWS):
        ks = min(max(gq * NA_GROUP_ROWS - NA_KH // 2, 0), rows - NA_WIN_ROWS)
        sig = tuple((min(max(r - NA_KH // 2, 0), rows - NA_KH) - ks, r - ks)
                    for r in range(gq * NA_GROUP_ROWS, (gq + 1) * NA_GROUP_ROWS))
        assert all(0 <= first and first + NA_KH <= NA_WIN_ROWS for first, _ in sig)
        if sig not in sigs:
            sigs.append(sig)
        starts.append(ks)
        classes.append(sigs.index(sig))
    return sigs, starts, classes


def _pair_head_rms(x, g):
    lo = lax.broadcasted_iota(jnp.int32, x.shape, 1) < NA_HEAD_DIM
    x2 = x * x
    s_lo = jnp.sum(jnp.where(lo, x2, 0.0), axis=-1, keepdims=True)
    s_hi = jnp.sum(jnp.where(lo, 0.0, x2), axis=-1, keepdims=True)
    ms = jnp.where(lo, s_lo, s_hi) * (1.0 / NA_HEAD_DIM)
    return x * lax.rsqrt(ms + NORM_EPS) * g


def _na_kernel(plan_ref, q_ref, k_ref, v_ref, qg_ref, kg_ref, bias_ref, o_ref, q_s, k_s, s_a, s_b, m_a, m_b, *, rows):
    scale = NA_HEAD_DIM ** -0.5 * LOG2E

    same_head = (lax.broadcasted_iota(jnp.int32, (V7X_LANES, V7X_LANES), 0) // NA_HEAD_DIM
                 == lax.broadcasted_iota(jnp.int32, (V7X_LANES, V7X_LANES), 1) // NA_HEAD_DIM)
    head_ones = jnp.where(same_head, 1.0, 0.0).astype(BF16)

    def head_rms(x, g):
        ms = _dot((x * x).astype(BF16), head_ones) * (1.0 / NA_HEAD_DIM)
        return x * lax.rsqrt(ms + NORM_EPS) * g

    def prep(i, carry):
        sl = pl.ds(pl.multiple_of(i * NA_PREP_ROWS, NA_PREP_ROWS), NA_PREP_ROWS)
        q_s[sl, :] = (head_rms(q_ref[0, sl, :].astype(F32), qg_ref[...]) * scale).astype(BF16)
        k_s[sl, :] = head_rms(k_ref[0, sl, :].astype(F32), kg_ref[...]).astype(BF16)
        return carry

    lax.fori_loop(0, (rows * GRID_W) // NA_PREP_ROWS, prep, 0, unroll=4)

    n_q = NA_GROUP_ROWS * GRID_W
    n_keys = NA_WIN_ROWS * GRID_W
    n_groups = rows // NA_GROUP_ROWS
    heads = [slice(h * NA_HEAD_DIM, (h + 1) * NA_HEAD_DIM) for h in range(2)]

    def key_rows(g):
        return pl.ds(pl.multiple_of(plan_ref[0, g] * GRID_W, GRID_W), n_keys)

    def query_rows(g):
        return pl.ds(pl.multiple_of(g * n_q, n_q), n_q)

    def scores(g, s_buf, m_buf):
        q = q_s[query_rows(g), :]
        kk = k_s[key_rows(g), :]
        cls = plan_ref[1, g]
        for h, hs in enumerate(heads):
            s = _dot_nt(q[:, hs], kk[:, hs]) + bias_ref[0, cls, h]
            s_buf[h] = s
            m_buf[h] = jnp.max(s, axis=-1, keepdims=True)

    def finish(g, s_buf, m_buf):
        vv = v_ref[0, key_rows(g), :]
        outs = []
        for h, hs in enumerate(heads):
            p = jnp.exp2(s_buf[h] - m_buf[h])
            l = jnp.sum(p, axis=-1, keepdims=True)
            outs.append(_dot(p.astype(BF16), vv[:, hs]) * (1.0 / l))
        o_ref[0, query_rows(g), :] = jnp.concatenate(outs, axis=-1).astype(BF16)

    scores(0, s_a, m_a)

    def pair_body(i, carry):
        g = 2 * i
        scores(g + 1, s_b, m_b)
        finish(g, s_a, m_a)
        scores(g + 2, s_a, m_a)
        finish(g + 1, s_b, m_b)
        return carry

    lax.fori_loop(0, n_groups // 2 - 1, pair_body, 0)
    scores(n_groups - 1, s_b, m_b)
    finish(n_groups - 2, s_a, m_a)
    finish(n_groups - 1, s_b, m_b)


def _na_bias_kernel(rpb_ref, o_ref, t_s, *, sigs):
    h = pl.program_id(0)
    q = lax.broadcasted_iota(jnp.int32, (GRID_W, GRID_W), 0)
    k = lax.broadcasted_iota(jnp.int32, (GRID_W, GRID_W), 1)
    dx = jnp.clip(k - q, -(NA_KW - 1), NA_KW - 1) + (NA_KW - 1)
    col_start = jnp.clip(q - NA_KW // 2, 0, GRID_W - NA_KW)
    in_win = (k >= col_start) & (k < col_start + NA_KW)
    masked = jnp.full((GRID_W, GRID_W), MASK_VALUE, F32)
    for dy in range(NA_DY):
        base = (h * NA_DY + dy) * NA_DX
        t = masked
        for d in range(NA_DX):
            t = jnp.where(dx == d, rpb_ref[base + d] * LOG2E, t)
        t_s[dy] = jnp.where(in_win, t, MASK_VALUE)
    for cls, sig in enumerate(sigs):
        for rq, (first, qrow) in enumerate(sig):
            for jk in range(NA_WIN_ROWS):
                attended = first <= jk < first + NA_KH
                tile = t_s[jk - qrow + NA_KH - 1] if attended else masked
                o_ref[0, cls, 0, rq * GRID_W:(rq + 1) * GRID_W, jk * GRID_W:(jk + 1) * GRID_W] = tile


def _na_bias_table(rpb, sigs):
    n_q = NA_GROUP_ROWS * GRID_W
    n_keys = NA_WIN_ROWS * GRID_W
    kern = functools.partial(_na_bias_kernel, sigs=sigs)
    return pl.pallas_call(
        kern, grid=(NA_HEADS,),
        in_specs=[pl.BlockSpec(memory_space=pltpu.SMEM)],
        out_specs=pl.BlockSpec((1, len(sigs), 1, n_q, n_keys), lambda h: (h // 2, 0, h % 2, 0, 0)),
        out_shape=jax.ShapeDtypeStruct((NA_HEADS // 2, len(sigs), 2, n_q, n_keys), F32),
        scratch_shapes=[pltpu.VMEM((NA_DY, GRID_W, GRID_W), F32)],
        compiler_params=_params(("parallel",)), name="na_bias_table",
    )(rpb.astype(F32).reshape(-1))


def _neighbourhood_attention(proj3d, q_gain, k_gain, rpb):
    b, s, _ = proj3d.shape
    rows = s // GRID_W
    assert rows >= NA_WIN_ROWS and rows % (2 * NA_GROUP_ROWS) == 0
    sigs, starts, classes = _na_group_plan(rows)
    bias = _na_bias_table(rpb, sigs)
    plan = jnp.array([starts, classes], jnp.int32)
    qg = jnp.tile(q_gain.astype(F32), 2)[None, :]
    kg = jnp.tile(k_gain.astype(F32), 2)[None, :]
    n_pairs = NA_HEADS // 2
    blk = (1, s, 2 * NA_HEAD_DIM)
    kern = functools.partial(_na_kernel, rows=rows)
    return pl.pallas_call(
        kern, grid=(n_pairs, b),
        in_specs=[
            pl.BlockSpec(memory_space=pltpu.SMEM),
            pl.BlockSpec(blk, lambda p, i: (i, 0, p)),
            pl.BlockSpec(blk, lambda p, i: (i, 0, n_pairs + p)),
            pl.BlockSpec(blk, lambda p, i: (i, 0, 2 * n_pairs + p)),
            pl.BlockSpec((1, 2 * NA_HEAD_DIM), lambda p, i: (0, 0)),
            pl.BlockSpec((1, 2 * NA_HEAD_DIM), lambda p, i: (0, 0)),
            pl.BlockSpec((1,) + bias.shape[1:], lambda p, i: (p, 0, 0, 0, 0)),
        ],
        out_specs=pl.BlockSpec(blk, lambda p, i: (i, 0, p)),
        out_shape=jax.ShapeDtypeStruct((b, s, NA_WIDTH), BF16),
        scratch_shapes=([pltpu.VMEM((s, 2 * NA_HEAD_DIM), BF16)] * 2
                        + [pltpu.VMEM((2, NA_GROUP_ROWS * GRID_W, NA_WIN_ROWS * GRID_W), F32)] * 2
                        + [pltpu.VMEM((2, NA_GROUP_ROWS * GRID_W, 1), F32)] * 2),
        compiler_params=_params(("parallel", "parallel")), name="neighbourhood_attention",
    )(plan, proj3d, proj3d, proj3d, qg, kg, bias)


def _chunk_scan(a, reverse):
    n = a.shape[1]
    pos = lax.broadcasted_iota(jnp.int32, a.shape, 1) % SSD_CHUNK
    sh = 1
    while sh < SSD_CHUNK:
        if reverse:
            a = a + jnp.where(pos < SSD_CHUNK - sh, pltpu.roll(a, n - sh, axis=1), 0.0)
        else:
            a = a + jnp.where(pos >= sh, pltpu.roll(a, sh, axis=1), 0.0)
        sh *= 2
    return a


def _head_row(tile, first):
    lo = lax.broadcasted_iota(jnp.int32, (1, V7X_LANES), 1) < SSD_HEAD_DIM
    halves = [jnp.where(lo, tile[first + 2 * i:first + 2 * i + 1, :], tile[first + 2 * i + 1:first + 2 * i + 2, :])
              for i in range(SSD_HEADS_PER_GROUP // 2)]
    return jnp.concatenate(halves, axis=1)


def _ssd_kernel(xs_ref, bt_ref, c_ref, z_ref, dtr_ref, biasr_ref, alogr_ref, dskip_ref, gain_ref, o_ref,
                cumr_s, g2r_s, ld2r_s, wr_s, decr_s, st_s, h_s, *, n_chunks):
    L = SSD_CHUNK
    hpg = SSD_HEADS_PER_GROUP
    gw = SSD_GROUP_WIDTH
    ns = SSD_STATE

    dt = _softplus(dtr_ref[...] + biasr_ref[0])
    a = dt * (-jnp.exp(alogr_ref[0]))
    is_fwd = lax.broadcasted_iota(jnp.int32, a.shape, 0) < hpg
    prefix = _chunk_scan(a, False)
    suffix = _chunk_scan(a, True)
    cum2 = jnp.where(is_fwd, prefix, suffix) * LOG2E
    cumr_s[...] = cum2
    g2r_s[...] = cum2 - jnp.log2(dt)
    ld2r_s[...] = jnp.log2(dt + pltpu.roll(dt, hpg, axis=0))
    wr_s[...] = dt * jnp.exp(jnp.where(is_fwd, suffix, prefix) - a)
    decr_s[...] = jnp.exp(prefix + suffix - a)

    def chunk_slice(c):
        return pl.ds(pl.multiple_of(c * L, L), L)

    lane_head = lax.broadcasted_iota(jnp.int32, (L, gw), 1) // SSD_HEAD_DIM

    def block_diag_x(xs):
        return jnp.concatenate([jnp.where(lane_head == j, xs, jnp.zeros_like(xs)) for j in range(hpg)], axis=0)

    def state_body(c, carry):
        sl = chunk_slice(c)
        bt = bt_ref[:, sl].astype(F32)
        w = wr_s[:, sl]
        lhs = jnp.concatenate(
            [jnp.concatenate([(bt * w[d * hpg + j:d * hpg + j + 1, :]).astype(BF16) for j in range(hpg)], axis=1)
             for d in range(2)], axis=0)
        st_s[c] = _dot(lhs, block_diag_x(xs_ref[0, sl, :]))
        return carry

    lax.fori_loop(0, n_chunks, state_body, 0, unroll=8)

    def fwd_rec(c, h):
        h_s[c, :, 0:gw] = h.astype(BF16)
        return h * _head_row(decr_s[:, chunk_slice(c)], 0) + st_s[c, 0:ns, :]

    def bwd_rec(i, h):
        c = n_chunks - 1 - i
        h_s[c, :, gw:2 * gw] = h.astype(BF16)
        return h * _head_row(decr_s[:, chunk_slice(c)], hpg) + st_s[c, ns:2 * ns, :]

    h0 = jnp.zeros((ns, gw), F32)
    lax.fori_loop(0, n_chunks, fwd_rec, h0)
    lax.fori_loop(0, n_chunks, bwd_rec, h0)

    li = lax.broadcasted_iota(jnp.int32, (L, L), 0)
    si = lax.broadcasted_iota(jnp.int32, (L, L), 1)
    below = si < li
    above = si > li
    lane_lo = lax.broadcasted_iota(jnp.int32, (L, V7X_LANES), 1) < SSD_HEAD_DIM

    def out_body(c, carry):
        sl = chunk_slice(c)
        xs = xs_ref[0, sl, :]
        cm = c_ref[0, sl, :]
        cum_r = cumr_s[:, sl]
        g2 = g2r_s[:, sl]
        ld2 = ld2r_s[:, sl]
        cum_t = [jnp.broadcast_to(cum_r[k:k + 1, :], (L, L)).T for k in range(2 * hpg)]
        cb = _dot(cm, bt_ref[:, sl])
        mats = []
        for j in range(hpg):
            seg_f = cum_t[j] - g2[j:j + 1, :]
            seg_b = cum_t[hpg + j] - g2[hpg + j:hpg + j + 1, :]
            arg = jnp.where(below, seg_f, jnp.where(above, seg_b, ld2[j:j + 1, :]))
            mats.append((cb * jnp.exp2(arg)).astype(BF16))
        y = _dot(jnp.concatenate(mats, axis=1), block_diag_x(xs))
        carried = _dot(cm, h_s[c])
        for d in range(2):
            decay = jnp.exp2(jnp.concatenate(
                [jnp.where(lane_lo, cum_t[d * hpg + 2 * i], cum_t[d * hpg + 2 * i + 1]) for i in range(hpg // 2)],
                axis=1))
            y = y + carried[:, d * gw:(d + 1) * gw] * decay
        y = y + dskip_ref[0] * xs.astype(F32)
        gated = y * _silu(z_ref[0, sl, :].astype(F32))
        ms = jnp.mean(gated * gated, axis=-1, keepdims=True)
        o_ref[0, sl, :] = (gated * lax.rsqrt(ms + NORM_EPS) * gain_ref[0]).astype(BF16)
        return carry

    lax.fori_loop(0, n_chunks, out_body, 0, unroll=8)


def _group_major(v):
    return jnp.transpose(v.astype(F32).reshape(2, SSD_GROUPS, SSD_HEADS_PER_GROUP), (1, 0, 2)).reshape(
        SSD_GROUPS, 2 * SSD_HEADS_PER_GROUP)


def _ssd_mixer(proj3d, xs3d, bt2d, c3d, dt_rows, dt_bias, a_log, d_skip, out_gain):
    b, s, _ = proj3d.shape
    n_chunks = s // SSD_CHUNK
    g, hpg, gw, ns = SSD_GROUPS, SSD_HEADS_PER_GROUP, SSD_GROUP_WIDTH, SSD_STATE
    bias_g = _group_major(dt_bias)[:, :, None]
    alog_g = _group_major(a_log)[:, :, None]
    dskip = jnp.repeat(d_skip.astype(F32), SSD_HEAD_DIM).reshape(g, 1, gw)
    gain = out_gain.astype(F32).reshape(g, 1, gw)
    z_blk = (3 * NA_WIDTH) // gw
    kern = functools.partial(_ssd_kernel, n_chunks=n_chunks)
    small = lambda shape: pl.BlockSpec((1,) + shape, lambda i, k: (k, 0, 0))
    seq_blk = lambda width, blk0: pl.BlockSpec((1, s, width), lambda i, k: (i, 0, blk0 + k))
    row_scratch = pltpu.VMEM((2 * hpg, s), F32)
    return pl.pallas_call(
        kern, grid=(b, g),
        in_specs=[
            seq_blk(gw, 0), pl.BlockSpec((ns, s), lambda i, k: (k, i)), seq_blk(ns, 0), seq_blk(gw, z_blk),
            pl.BlockSpec((2 * hpg, s), lambda i, k: (k, i)),
            small((2 * hpg, 1)), small((2 * hpg, 1)), small((1, gw)), small((1, gw)),
        ],
        out_specs=pl.BlockSpec((1, s, gw), lambda i, k: (i, 0, k)),
        out_shape=jax.ShapeDtypeStruct((b, s, SSD_D_INNER), BF16),
        scratch_shapes=[
            row_scratch, row_scratch, row_scratch, row_scratch, row_scratch,
            pltpu.VMEM((n_chunks, 2 * ns, gw), F32), pltpu.VMEM((n_chunks, ns, 2 * gw), BF16),
        ],
        compiler_params=_params(("parallel", "parallel")), name="ssd_bidirectional",
    )(xs3d, bt2d, c3d, proj3d, dt_rows, bias_g, alog_g, dskip, gain)


FFN_CHUNK = 256


def _mix_ffn_kernel(*refs, n_acts, hidden):
    act_refs = refs[:n_acts]
    wout_refs = refs[n_acts:2 * n_acts]
    x_ref, g_ref, w13_ref, w2_ref, o_ref, hid_ref = refs[2 * n_acts:]
    h = x_ref[...]
    for a_ref, w_ref in zip(act_refs, wout_refs):
        h = h + _dot(a_ref[...], w_ref[...])
    hn = _rms_rows(h, g_ref[...]).astype(BF16)
    for c in range(hidden // FFN_CHUNK):
        gate = slice(c * FFN_CHUNK, (c + 1) * FFN_CHUNK)
        up = slice(hidden + c * FFN_CHUNK, hidden + (c + 1) * FFN_CHUNK)
        hid_ref[:, gate] = (_silu(_dot(hn, w13_ref[:, gate])) * _dot(hn, w13_ref[:, up])).astype(BF16)
    o_ref[...] = h + _dot(hid_ref[...], w2_ref[...])


def _mix_ffn(acts, w_outs, x2d, g, w13, w2, layer, *, tm, name):
    m, d = x2d.shape
    hid = w2.shape[1]
    assert hid % FFN_CHUNK == 0
    row = lambda width: pl.BlockSpec((tm, width), lambda i: (i, 0))
    resident = lambda shape: pl.BlockSpec(shape, lambda i: (0, 0), pipeline_mode=pl.Buffered(1))
    stacked = lambda shape: pl.BlockSpec((None,) + shape, lambda i: (layer, 0, 0), pipeline_mode=pl.Buffered(1))
    return pl.pallas_call(
        functools.partial(_mix_ffn_kernel, n_acts=len(acts), hidden=hid), grid=(m // tm,),
        in_specs=([row(a.shape[1]) for a in acts] + [resident(w.shape) for w in w_outs]
                  + [row(d), pl.BlockSpec((1, d), lambda i: (0, 0)), stacked((d, 2 * hid)), stacked((hid, d))]),
        out_specs=row(d),
        out_shape=jax.ShapeDtypeStruct((m, d), F32),
        scratch_shapes=[pltpu.VMEM((tm, hid), BF16)],
        compiler_params=_params(("parallel",)), name=name,
    )(*acts, *w_outs, x2d, g, w13, w2)


def _rope_prep_kernel(p_ref, qg_ref, kg_ref, cos_ref, sin_ref, cost_ref, sint_ref, q_ref, k_ref, v_ref):
    scale = GQA_HEAD_DIM ** -0.5 * LOG2E
    ts = p_ref.shape[1]
    cos_t = cost_ref[...]
    sin_t = sint_ref[...]
    even_row3 = (lax.broadcasted_iota(jnp.int32, (2 * GQA_HEAD_DIM // 8, 8, ts), 1) % 2) == 0
    q_gain = qg_ref[...] * scale
    hd = GQA_HEAD_DIM
    for pair in range(GQA_HEADS // 2):
        xt = p_ref[0, :, pair * 128:(pair + 1) * 128].astype(F32).T
        x2 = xt * xt
        inv = [lax.rsqrt(jnp.mean(x2[h * hd:(h + 1) * hd], axis=0, keepdims=True) + NORM_EPS) for h in range(2)]
        xn = jnp.concatenate([xt[:hd] * inv[0], xt[hd:] * inv[1]], axis=0) * q_gain
        x3 = xn.reshape(2 * hd // 8, 8, ts)
        swapped = jnp.where(even_row3, pltpu.roll(x3, 7, axis=1), pltpu.roll(x3, 1, axis=1)).reshape(2 * hd, ts)
        out = xn * cos_t + swapped * sin_t
        q_ref[0, 2 * pair] = out[:hd].astype(BF16)
        q_ref[0, 2 * pair + 1] = out[hd:].astype(BF16)

    cos = cos_ref[...]
    sin = sin_ref[...]
    even = (lax.broadcasted_iota(jnp.int32, cos.shape, 1) % 2) == 0
    for pair in range(GQA_KV_HEADS // 2):
        c0 = GQA_Q_WIDTH + pair * 128
        xn = _pair_head_rms(p_ref[0, :, c0:c0 + 128].astype(F32), kg_ref[...])
        swapped = jnp.where(even, pltpu.roll(xn, V7X_LANES - 1, axis=1), pltpu.roll(xn, 1, axis=1))
        blk = xn * cos + swapped * sin
        k_ref[0, 2 * pair] = blk[:, :hd].astype(BF16)
        k_ref[0, 2 * pair + 1] = blk[:, hd:].astype(BF16)
        c1 = GQA_Q_WIDTH + GQA_KV_WIDTH + pair * 128
        vt = p_ref[0, :, c1:c1 + 128].astype(F32).T.astype(BF16)
        v_ref[0, 2 * pair] = vt[:hd]
        v_ref[0, 2 * pair + 1] = vt[hd:]


def _axial_rope_tables(s):
    t = np.arange(s)
    row = (t // GRID_W).astype(np.float32)
    col = (t % GRID_W).astype(np.float32)
    axis_dims = GQA_HEAD_DIM // 2
    freqs = np.float32(ROPE_THETA) ** (-np.arange(0, axis_dims, 2, dtype=np.float32) / np.float32(axis_dims))
    ang = np.concatenate([row[:, None] * freqs, col[:, None] * freqs], axis=-1).astype(np.float32)
    cos = np.repeat(np.cos(ang), 2, axis=-1)
    sin = np.stack([-np.sin(ang), np.sin(ang)], axis=-1).reshape(s, GQA_HEAD_DIM)
    return np.tile(cos, (1, 2)).astype(np.float32), np.tile(sin, (1, 2)).astype(np.float32)


def _rope_prep(proj3d, q_gain, k_gain, *, ts):
    b, s, width = proj3d.shape
    cos, sin = _axial_rope_tables(s)
    qg = jnp.tile(q_gain.astype(F32), 2)[:, None]
    kg = jnp.tile(k_gain.astype(F32), 2)[None, :]
    head_out = lambda n: pl.BlockSpec((1, n, ts, GQA_HEAD_DIM), lambda i, t: (i, 0, t, 0))
    shape = lambda n: jax.ShapeDtypeStruct((b, n, s, GQA_HEAD_DIM), BF16)
    t_out = lambda n: pl.BlockSpec((1, n, GQA_HEAD_DIM, ts), lambda i, t: (i, 0, 0, t))
    t_shape = lambda n: jax.ShapeDtypeStruct((b, n, GQA_HEAD_DIM, s), BF16)
    return pl.pallas_call(
        _rope_prep_kernel, grid=(b, s // ts),
        in_specs=[
            pl.BlockSpec((1, ts, width), lambda i, t: (i, t, 0)),
            pl.BlockSpec((128, 1), lambda i, t: (0, 0)),
            pl.BlockSpec((1, 128), lambda i, t: (0, 0)),
            pl.BlockSpec((ts, 128), lambda i, t: (t, 0)),
            pl.BlockSpec((ts, 128), lambda i, t: (t, 0)),
            pl.BlockSpec((128, ts), lambda i, t: (0, t)),
            pl.BlockSpec((128, ts), lambda i, t: (0, t)),
        ],
        out_specs=[t_out(GQA_HEADS), head_out(GQA_KV_HEADS), t_out(GQA_KV_HEADS)],
        out_shape=[t_shape(GQA_HEADS), shape(GQA_KV_HEADS), t_shape(GQA_KV_HEADS)],
        compiler_params=_params(("parallel", "parallel")), name="gqa_norm_rope",
    )(proj3d, qg, kg, jnp.asarray(cos), jnp.asarray(sin), jnp.asarray(cos.T.copy()), jnp.asarray(sin.T.copy()))


GQA_KV_CHUNK = 256
GQA_SUM_ROWS = 16
GQA_SUB_Q = 128


def _gqa_kernel(q_ref, k_ref, vt_ref, o_ref, s_a, s_b, m_a, m_b, *, tq, seq):
    t = pl.program_id(0)
    n_sub = tq // GQA_SUB_Q
    cols = GQA_REP * GQA_SUB_Q

    @pl.when(t == 0)
    def _():
        s_b[...] = jnp.zeros(s_b.shape, F32)
        m_b[...] = jnp.zeros(m_b.shape, F32)

    def sub_step(sub, s_cur, m_cur, s_prev, m_prev_ref):
        qs = slice(sub * GQA_SUB_Q, (sub + 1) * GQA_SUB_Q)
        qt = jnp.concatenate([q_ref[0, r, :, qs] for r in range(GQA_REP)], axis=1)
        m_prev = m_prev_ref[sub]
        m_run = None
        ones = jnp.ones((GQA_SUM_ROWS, GQA_KV_CHUNK), BF16)
        acc = jnp.zeros((GQA_HEAD_DIM + GQA_SUM_ROWS, cols), F32)
        for i in range(seq // GQA_KV_CHUNK):
            rows = slice(i * GQA_KV_CHUNK, (i + 1) * GQA_KV_CHUNK)
            st = _dot(k_ref[0, 0, rows, :], qt)
            s_cur[sub, rows, :] = st
            cm = jnp.max(st, axis=0, keepdims=True)
            m_run = cm if m_run is None else jnp.maximum(m_run, cm)
            p = jnp.exp2(s_prev[sub, rows, :] - m_prev)
            vt_aug = jnp.concatenate([vt_ref[0, 0, :, rows], ones], axis=0)
            acc = acc + _dot(vt_aug, p.astype(BF16))
        m_cur[sub] = m_run
        ot = acc[:GQA_HEAD_DIM] * (1.0 / acc[GQA_HEAD_DIM:GQA_HEAD_DIM + 1])
        for r in range(GQA_REP):
            o_ref[0, qs, r * GQA_HEAD_DIM:(r + 1) * GQA_HEAD_DIM] = (
                ot[:, r * GQA_SUB_Q:(r + 1) * GQA_SUB_Q].T.astype(BF16))

    def step(*bufs):
        for sub in range(n_sub):
            sub_step(sub, *bufs)

    pl.when(t % 2 == 0)(lambda: step(s_a, m_a, s_b, m_b))
    pl.when(t % 2 == 1)(lambda: step(s_b, m_b, s_a, m_a))


def _gqa_attention(q, k, vt, *, tq):
    b, _, _, s = q.shape
    nq = s // tq
    n_blocks = b * GQA_KV_HEADS * nq
    n_sub = tq // GQA_SUB_Q
    cols = GQA_REP * GQA_SUB_Q

    def unravel(u):
        return u // (nq * GQA_KV_HEADS), (u // nq) % GQA_KV_HEADS, u % nq

    def score_block(t):
        return unravel(jnp.minimum(t, n_blocks - 1))

    def finish_block(t):
        return unravel(jnp.maximum(t - 1, 0))

    def q_map(t):
        i, g, j = score_block(t)
        return (i, g, 0, j)

    def k_map(t):
        i, g, _ = score_block(t)
        return (i, g, 0, 0)

    def vt_map(t):
        i, g, _ = finish_block(t)
        return (i, g, 0, 0)

    def o_map(t):
        i, g, j = finish_block(t)
        return (i, j, g)

    kern = functools.partial(_gqa_kernel, tq=tq, seq=s)
    return pl.pallas_call(
        kern, grid=(n_blocks + 1,),
        in_specs=[
            pl.BlockSpec((1, GQA_REP, GQA_HEAD_DIM, tq), q_map),
            pl.BlockSpec((1, 1, s, GQA_HEAD_DIM), k_map),
            pl.BlockSpec((1, 1, GQA_HEAD_DIM, s), vt_map),
        ],
        out_specs=pl.BlockSpec((1, tq, GQA_REP * GQA_HEAD_DIM), o_map),
        out_shape=jax.ShapeDtypeStruct((b, s, GQA_Q_WIDTH), BF16),
        scratch_shapes=[pltpu.VMEM((n_sub, s, cols), F32), pltpu.VMEM((n_sub, s, cols), F32),
                        pltpu.VMEM((n_sub, 1, cols), F32), pltpu.VMEM((n_sub, 1, cols), F32)],
        compiler_params=_params(("arbitrary",)), name="gqa_attention",
    )(q, k, vt)


def _even_mixer(x2d, b, s, mix_norm, w_in, q_gain, k_gain, rpb, conv_w, conv_b, dt_bias, a_log, d_skip, out_gain,
                w_out):
    w_main = w_in.astype(BF16)
    w_dt = jnp.transpose(w_in[:, EVEN_MAIN_WIDTH:].reshape(-1, 2, SSD_GROUPS, SSD_HEADS_PER_GROUP),
                         (2, 1, 3, 0)).reshape(2 * SSD_HEADS, -1).astype(BF16)
    proj, xs, bt, cc, dt_rows = _even_in_proj(x2d, mix_norm.astype(F32)[None, :], w_main, w_dt, conv_w.astype(F32),
                                              conv_b.astype(F32)[None, :], tm=1024, seq=s, n_main=EVEN_XBC_OFFSET,
                                              name="even_in_proj")
    proj3d = proj.reshape(b, s, EVEN_XBC_OFFSET)
    na_out = _neighbourhood_attention(proj3d, q_gain, k_gain, rpb)
    ssd_out = _ssd_mixer(proj3d, xs.reshape(b, s, -1), bt, cc.reshape(b, s, -1), dt_rows, dt_bias, a_log, d_skip,
                         out_gain)
    w_out_bf = w_out.astype(BF16)
    return ([na_out.reshape(b * s, NA_WIDTH), ssd_out.reshape(b * s, SSD_D_INNER)],
            [w_out_bf[:NA_WIDTH], w_out_bf[NA_WIDTH:]])


def _odd_mixer(x2d, b, s, mix_norm, w_qkv, q_gain, k_gain, w_out):
    proj = _norm_proj(x2d, mix_norm.astype(F32)[None, :], w_qkv.astype(BF16), tm=1024, name="odd_qkv_proj")
    q, k, vt = _rope_prep(proj.reshape(b, s, -1), q_gain, k_gain, ts=1024)
    attn = _gqa_attention(q, k, vt, tq=512)
    return [attn.reshape(b * s, GQA_Q_WIDTH)], [w_out.astype(BF16)]


def kernel(x, even_mix_norm, even_w_in, na_q_norm, na_k_norm, na_rel_bias, ssd_conv_w, ssd_conv_b, ssd_dt_bias, ssd_A_log, ssd_D, ssd_out_norm, even_w_out, odd_mix_norm, odd_w_qkv, gqa_q_norm, gqa_k_norm, odd_w_out, ffn_norm, ffn_w13, ffn_w2):
    b, s, d = x.shape
    depth = ffn_norm.shape[0]
    h = x.reshape(b * s, d)
    w13_bf = ffn_w13.astype(BF16)
    w2_bf = ffn_w2.astype(BF16)
    for layer in range(depth):
        i = layer // 2
        if layer % 2 == 0:
            acts, w_outs = _even_mixer(h, b, s, even_mix_norm[i], even_w_in[i], na_q_norm[i], na_k_norm[i],
                                       na_rel_bias[i], ssd_conv_w[i], ssd_conv_b[i], ssd_dt_bias[i], ssd_A_log[i],
                                       ssd_D[i], ssd_out_norm[i], even_w_out[i])
        else:
            acts, w_outs = _odd_mixer(h, b, s, odd_mix_norm[i], odd_w_qkv[i], gqa_q_norm[i], gqa_k_norm[i],
                                      odd_w_out[i])
        h = _mix_ffn(acts, w_outs, h, ffn_norm[layer].astype(F32)[None, :], w13_bf, w2_bf, layer, tm=1024,
                     name="mix_out_ffn_even" if layer % 2 == 0 else "mix_out_ffn_odd")
    return h.reshape(b, s, d)
```

```python
import functools

import jax
import jax.numpy as jnp
import numpy as np
from jax import lax
from jax.experimental import pallas as pl
from jax.experimental.pallas import tpu as pltpu

F32 = jnp.float32
BF16 = jnp.bfloat16

GRID_W = 64
NORM_EPS = 1e-6

NA_HEADS = 8
NA_HEAD_DIM = 64
NA_WIDTH = NA_HEADS * NA_HEAD_DIM
NA_KH = 8
NA_KW = 16

SSD_D_INNER = 1024
SSD_HEAD_DIM = 64
SSD_HEADS = 16
SSD_GROUPS = 4
SSD_STATE = 128
SSD_CONV = 4
SSD_CHUNK = 128
SSD_CONV_DIM = SSD_D_INNER + 2 * SSD_GROUPS * SSD_STATE
SSD_GROUP_WIDTH = SSD_D_INNER // SSD_GROUPS
SSD_HEADS_PER_GROUP = SSD_HEADS // SSD_GROUPS

EVEN_MAIN_WIDTH = 3 * NA_WIDTH + SSD_D_INNER + SSD_CONV_DIM
EVEN_XBC_OFFSET = 3 * NA_WIDTH + SSD_D_INNER

GQA_HEADS = 16
GQA_KV_HEADS = 4
GQA_HEAD_DIM = 64
GQA_REP = GQA_HEADS // GQA_KV_HEADS
GQA_Q_WIDTH = GQA_HEADS * GQA_HEAD_DIM
GQA_KV_WIDTH = GQA_KV_HEADS * GQA_HEAD_DIM
GQA_PAIR = 2 * GQA_HEAD_DIM
ROPE_THETA = 10000.0

V7X_LANES = 128
V7X_VMEM_LIMIT = 56 * 1024 * 1024
MASK_VALUE = -1e30
LOG2E = 1.4426950408889634


def _params(dims):
    return pltpu.CompilerParams(dimension_semantics=dims, vmem_limit_bytes=V7X_VMEM_LIMIT)


def _silu(v):
    return v * (1.0 / (1.0 + jnp.exp(-v)))


def _softplus(v):
    return jnp.maximum(v, 0.0) + jnp.log(1.0 + jnp.exp(-jnp.abs(v)))


def _rms_rows(x, g):
    ms = jnp.mean(x * x, axis=-1, keepdims=True)
    return x * lax.rsqrt(ms + NORM_EPS) * g


def _dot(a, b):
    return jnp.dot(a, b, preferred_element_type=F32)


def _dot_nt(a, b):
    return lax.dot_general(a, b, (((1,), (1,)), ((), ())), preferred_element_type=F32)


PROJ_CHUNK = 512


def _norm_proj_kernel(x_ref, g_ref, w_ref, o_ref):
    xn = _rms_rows(x_ref[...], g_ref[...]).astype(BF16)
    for c in range(o_ref.shape[1] // PROJ_CHUNK):
        cols = slice(c * PROJ_CHUNK, (c + 1) * PROJ_CHUNK)
        o_ref[:, cols] = _dot(xn, w_ref[:, cols]).astype(o_ref.dtype)


CONV_HALO = 8


def _even_in_proj_kernel(x_ref, xp_ref, xn_ref, g_ref, w_ref, wdt_ref, cw_ref, cb_ref, o_ref, ox_ref, obt_ref, oc_ref,
                         odt_ref, *, tiles_per_seq):
    tm = x_ref.shape[0]
    n_main = o_ref.shape[1]
    i = pl.program_id(0) % tiles_per_seq
    gain = g_ref[...]
    xc = _rms_rows(x_ref[...], gain)
    xp = jnp.where(i > 0, _rms_rows(xp_ref[...], gain), 0.0)
    xn = jnp.where(i < tiles_per_seq - 1, _rms_rows(xn_ref[...], gain), 0.0)
    xc_bf = xc.astype(BF16)
    xe_bf = jnp.concatenate([xp, xc, xn], axis=0).astype(BF16)
    odt_ref[...] = _dot_nt(wdt_ref[...], xc_bf)
    n_ext = tm + 2 * CONV_HALO
    left = SSD_CONV // 2
    n_main_chunks = n_main // PROJ_CHUNK
    nx, nb, ncc = (r // PROJ_CHUNK for r in (ox_ref.shape[1], obt_ref.shape[0], oc_ref.shape[1]))
    n_conv_chunks = nx + nb + ncc

    def main_chunk(c):
        cols = slice(c * PROJ_CHUNK, (c + 1) * PROJ_CHUNK)
        o_ref[:, cols] = _dot(xc_bf, w_ref[:, cols]).astype(o_ref.dtype)

    def conv_chunk(c, pr):
        cols = slice(c * PROJ_CHUNK, (c + 1) * PROJ_CHUNK)
        acc = jnp.broadcast_to(cb_ref[:, cols], (tm, PROJ_CHUNK))
        for k in range(SSD_CONV):
            shift = (left - k) % n_ext
            tap = pr if shift == 0 else pltpu.roll(pr, shift, axis=0)
            acc = acc + tap[CONV_HALO:CONV_HALO + tm] * cw_ref[k:k + 1, cols]
        out = _silu(acc)
        if c < nx:
            ox_ref[:, cols] = out.astype(ox_ref.dtype)
        elif c < nx + nb:
            obt_ref[(c - nx) * PROJ_CHUNK:(c - nx + 1) * PROJ_CHUNK, :] = out.T.astype(obt_ref.dtype)
        else:
            oc_ref[:, (c - nx - nb) * PROJ_CHUNK:(c - nx - nb + 1) * PROJ_CHUNK] = out.astype(oc_ref.dtype)

    for c in range(max(n_main_chunks, n_conv_chunks)):
        pr = None
        if c < n_conv_chunks:
            pr = _dot(xe_bf, w_ref[:, n_main + c * PROJ_CHUNK:n_main + (c + 1) * PROJ_CHUNK])
        if c < n_main_chunks:
            main_chunk(c)
        if pr is not None:
            conv_chunk(c, pr)


def _even_in_proj(x2d, g, w, w_dt, conv_w, conv_b, *, tm, seq, n_main, name):
    m, d = x2d.shape
    n = w.shape[1]
    n_x = SSD_D_INNER
    n_bc = SSD_GROUPS * SSD_STATE
    assert n >= n_main + n_x + 2 * n_bc and n_x % PROJ_CHUNK == 0 and n_bc % PROJ_CHUNK == 0
    ns = w_dt.shape[0]
    assert n_main % PROJ_CHUNK == 0 and seq % tm == 0 and tm % CONV_HALO == 0
    halo_per_tile = tm // CONV_HALO
    n_halo_blocks = m // CONV_HALO
    resident = lambda shape: pl.BlockSpec(shape, lambda i: (0, 0), pipeline_mode=pl.Buffered(1))
    kern = functools.partial(_even_in_proj_kernel, tiles_per_seq=seq // tm)
    return pl.pallas_call(
        kern, grid=(m // tm,),
        in_specs=[
            pl.BlockSpec((tm, d), lambda i: (i, 0)),
            pl.BlockSpec((CONV_HALO, d), lambda i: (jnp.maximum(i * halo_per_tile - 1, 0), 0)),
            pl.BlockSpec((CONV_HALO, d), lambda i: (jnp.minimum((i + 1) * halo_per_tile, n_halo_blocks - 1), 0)),
            pl.BlockSpec((1, d), lambda i: (0, 0)),
            resident((d, n)), resident((ns, d)), resident(conv_w.shape), resident(conv_b.shape),
        ],
        out_specs=[pl.BlockSpec((tm, n_main), lambda i: (i, 0)), pl.BlockSpec((tm, n_x), lambda i: (i, 0)),
                   pl.BlockSpec((n_bc, tm), lambda i: (0, i)), pl.BlockSpec((tm, n_bc), lambda i: (i, 0)),
                   pl.BlockSpec((ns, tm), lambda i: (0, i))],
        out_shape=[jax.ShapeDtypeStruct((m, n_main), BF16), jax.ShapeDtypeStruct((m, n_x), BF16),
                   jax.ShapeDtypeStruct((n_bc, m), BF16), jax.ShapeDtypeStruct((m, n_bc), BF16),
                   jax.ShapeDtypeStruct((ns, m), F32)],
        compiler_params=_params(("parallel",)), name=name,
    )(x2d, x2d, x2d, g, w, w_dt, conv_w, conv_b)


def _norm_proj(x2d, g, w, *, tm, name):
    m, d = x2d.shape
    n = w.shape[1]
    assert n % PROJ_CHUNK == 0
    resident = lambda shape: pl.BlockSpec(shape, lambda i: (0, 0), pipeline_mode=pl.Buffered(1))
    return pl.pallas_call(
        _norm_proj_kernel, grid=(m // tm,),
        in_specs=[pl.BlockSpec((tm, d), lambda i: (i, 0)), pl.BlockSpec((1, d), lambda i: (0, 0)), resident((d, n))],
        out_specs=pl.BlockSpec((tm, n), lambda i: (i, 0)), out_shape=jax.ShapeDtypeStruct((m, n), BF16),
        compiler_params=_params(("parallel",)), name=name)(x2d, g, w)


NA_PREP_ROWS = 256
NA_GROUP_ROWS = 4
NA_WIN_ROWS = NA_KH + NA_GROUP_ROWS
NA_DY = 2 * NA_KH - 1
NA_DX = 2 * NA_KW - 1


def _na_group_plan(rows):
    sigs, starts, classes = [], [], []
    for gq in range(rows // NA_GROUP_ROWS):
        ks = min(max(gq * NA_GROUP_ROWS - NA_KH // 2, 0), rows - NA_WIN_ROWS)
        sig = tuple((min(max(r - NA_KH // 2, 0), rows - NA_KH) - ks, r - ks)
                    for r in range(gq * NA_GROUP_ROWS, (gq + 1) * NA_GROUP_ROWS))
        assert all(0 <= first and first + NA_KH <= NA_WIN_ROWS for first, _ in sig)
        if sig not in sigs:
            sigs.append(sig)
        starts.append(ks)
        classes.append(sigs.index(sig))
    return sigs, starts, classes


def _pair_head_rms(x, g):
    lo = lax.broadcasted_iota(jnp.int32, x.shape, 1) < NA_HEAD_DIM
    x2 = x * x
    s_lo = jnp.sum(jnp.where(lo, x2, 0.0), axis=-1, keepdims=True)
    s_hi = jnp.sum(jnp.where(lo, 0.0, x2), axis=-1, keepdims=True)
    ms = jnp.where(lo, s_lo, s_hi) * (1.0 / NA_HEAD_DIM)
    return x * lax.rsqrt(ms + NORM_EPS) * g


def _na_kernel(plan_ref, q_ref, k_ref, v_ref, qg_ref, kg_ref, bias_ref, o_ref, q_s, k_s, s_a, s_b, m_a, m_b, *, rows):
    scale = NA_HEAD_DIM ** -0.5 * LOG2E

    same_head = (lax.broadcasted_iota(jnp.int32, (V7X_LANES, V7X_LANES), 0) // NA_HEAD_DIM
                 == lax.broadcasted_iota(jnp.int32, (V7X_LANES, V7X_LANES), 1) // NA_HEAD_DIM)
    head_ones = jnp.where(same_head, 1.0, 0.0).astype(BF16)

    def head_rms(x, g):
        ms = _dot((x * x).astype(BF16), head_ones) * (1.0 / NA_HEAD_DIM)
        return x * lax.rsqrt(ms + NORM_EPS) * g

    def prep(i, carry):
        sl = pl.ds(pl.multiple_of(i * NA_PREP_ROWS, NA_PREP_ROWS), NA_PREP_ROWS)
        q_s[sl, :] = (head_rms(q_ref[0, sl, :].astype(F32), qg_ref[...]) * scale).astype(BF16)
        k_s[sl, :] = head_rms(k_ref[0, sl, :].astype(F32), kg_ref[...]).astype(BF16)
        return carry

    lax.fori_loop(0, (rows * GRID_W) // NA_PREP_ROWS, prep, 0, unroll=4)

    n_q = NA_GROUP_ROWS * GRID_W
    n_keys = NA_WIN_ROWS * GRID_W
    n_groups = rows // NA_GROUP_ROWS
    heads = [slice(h * NA_HEAD_DIM, (h + 1) * NA_HEAD_DIM) for h in range(2)]

    def key_rows(g):
        return pl.ds(pl.multiple_of(plan_ref[0, g] * GRID_W, GRID_W), n_keys)

    def query_rows(g):
        return pl.ds(pl.multiple_of(g * n_q, n_q), n_q)

    def scores(g, s_buf, m_buf):
        q = q_s[query_rows(g), :]
        kk = k_s[key_rows(g), :]
        cls = plan_ref[1, g]
        for h, hs in enumerate(heads):
            s = _dot_nt(q[:, hs], kk[:, hs]) + bias_ref[0, cls, h]
            s_buf[h] = s
            m_buf[h] = jnp.max(s, axis=-1, keepdims=True)

    def finish(g, s_buf, m_buf):
        vv = v_ref[0, key_rows(g), :]
        outs = []
        for h, hs in enumerate(heads):
            p = jnp.exp2(s_buf[h] - m_buf[h])
            l = jnp.sum(p, axis=-1, keepdims=True)
            outs.append(_dot(p.astype(BF16), vv[:, hs]) * (1.0 / l))
        o_ref[0, query_rows(g), :] = jnp.concatenate(outs, axis=-1).astype(BF16)

    scores(0, s_a, m_a)

    def pair_body(i, carry):
        g = 2 * i
        scores(g + 1, s_b, m_b)
        finish(g, s_a, m_a)
        scores(g + 2, s_a, m_a)
        finish(g + 1, s_b, m_b)
        return carry

    lax.fori_loop(0, n_groups // 2 - 1, pair_body, 0)
    scores(n_groups - 1, s_b, m_b)
    finish(n_groups - 2, s_a, m_a)
    finish(n_groups - 1, s_b, m_b)


def _na_bias_kernel(rpb_ref, o_ref, t_s, *, sigs):
    h = pl.program_id(0)
    q = lax.broadcasted_iota(jnp.int32, (GRID_W, GRID_W), 0)
    k = lax.broadcasted_iota(jnp.int32, (GRID_W, GRID_W), 1)
    dx = jnp.clip(k - q, -(NA_KW - 1), NA_KW - 1) + (NA_KW - 1)
    col_start = jnp.clip(q - NA_KW // 2, 0, GRID_W - NA_KW)
    in_win = (k >= col_start) & (k < col_start + NA_KW)
    masked = jnp.full((GRID_W, GRID_W), MASK_VALUE, F32)
    for dy in range(NA_DY):
        base = (h * NA_DY + dy) * NA_DX
        t = masked
        for d in range(NA_DX):
            t = jnp.where(dx == d, rpb_ref[base + d] * LOG2E, t)
        t_s[dy] = jnp.where(in_win, t, MASK_VALUE)
    for cls, sig in enumerate(sigs):
        for rq, (first, qrow) in enumerate(sig):
            for jk in range(NA_WIN_ROWS):
                attended = first <= jk < first + NA_KH
                tile = t_s[jk - qrow + NA_KH - 1] if attended else masked
                o_ref[0, cls, 0, rq * GRID_W:(rq + 1) * GRID_W, jk * GRID_W:(jk + 1) * GRID_W] = tile


def _na_bias_table(rpb, sigs):
    n_q = NA_GROUP_ROWS * GRID_W
    n_keys = NA_WIN_ROWS * GRID_W
    kern = functools.partial(_na_bias_kernel, sigs=sigs)
    return pl.pallas_call(
        kern, grid=(NA_HEADS,),
        in_specs=[pl.BlockSpec(memory_space=pltpu.SMEM)],
        out_specs=pl.BlockSpec((1, len(sigs), 1, n_q, n_keys), lambda h: (h // 2, 0, h % 2, 0, 0)),
        out_shape=jax.ShapeDtypeStruct((NA_HEADS // 2, len(sigs), 2, n_q, n_keys), F32),
        scratch_shapes=[pltpu.VMEM((NA_DY, GRID_W, GRID_W), F32)],
        compiler_params=_params(("parallel",)), name="na_bias_table",
    )(rpb.astype(F32).reshape(-1))


def _neighbourhood_attention(proj3d, q_gain, k_gain, rpb):
    b, s, _ = proj3d.shape
    rows = s // GRID_W
    assert rows >= NA_WIN_ROWS and rows % (2 * NA_GROUP_ROWS) == 0
    sigs, starts, classes = _na_group_plan(rows)
    bias = _na_bias_table(rpb, sigs)
    plan = jnp.array([starts, classes], jnp.int32)
    qg = jnp.tile(q_gain.astype(F32), 2)[None, :]
    kg = jnp.tile(k_gain.astype(F32), 2)[None, :]
    n_pairs = NA_HEADS // 2
    blk = (1, s, 2 * NA_HEAD_DIM)
    kern = functools.partial(_na_kernel, rows=rows)
    return pl.pallas_call(
        kern, grid=(n_pairs, b),
        in_specs=[
            pl.BlockSpec(memory_space=pltpu.SMEM),
            pl.BlockSpec(blk, lambda p, i: (i, 0, p)),
            pl.BlockSpec(blk, lambda p, i: (i, 0, n_pairs + p)),
            pl.BlockSpec(blk, lambda p, i: (i, 0, 2 * n_pairs + p)),
            pl.BlockSpec((1, 2 * NA_HEAD_DIM), lambda p, i: (0, 0)),
            pl.BlockSpec((1, 2 * NA_HEAD_DIM), lambda p, i: (0, 0)),
            pl.BlockSpec((1,) + bias.shape[1:], lambda p, i: (p, 0, 0, 0, 0)),
        ],
        out_specs=pl.BlockSpec(blk, lambda p, i: (i, 0, p)),
        out_shape=jax.ShapeDtypeStruct((b, s, NA_WIDTH), BF16),
        scratch_shapes=([pltpu.VMEM((s, 2 * NA_HEAD_DIM), BF16)] * 2
                        + [pltpu.VMEM((2, NA_GROUP_ROWS * GRID_W, NA_WIN_ROWS * GRID_W), F32)] * 2
                        + [pltpu.VMEM((2, NA_GROUP_ROWS * GRID_W, 1), F32)] * 2),
        compiler_params=_params(("parallel", "parallel")), name="neighbourhood_attention",
    )(plan, proj3d, proj3d, proj3d, qg, kg, bias)


def _chunk_scan(a, reverse):
    n = a.shape[1]
    pos = lax.broadcasted_iota(jnp.int32, a.shape, 1) % SSD_CHUNK
    sh = 1
    while sh < SSD_CHUNK:
        if reverse:
            a = a + jnp.where(pos < SSD_CHUNK - sh, pltpu.roll(a, n - sh, axis=1), 0.0)
        else:
            a = a + jnp.where(pos >= sh, pltpu.roll(a, sh, axis=1), 0.0)
        sh *= 2
    return a


def _head_row(tile, first):
    lo = lax.broadcasted_iota(jnp.int32, (1, V7X_LANES), 1) < SSD_HEAD_DIM
    halves = [jnp.where(lo, tile[first + 2 * i:first + 2 * i + 1, :], tile[first + 2 * i + 1:first + 2 * i + 2, :])
              for i in range(SSD_HEADS_PER_GROUP // 2)]
    return jnp.concatenate(halves, axis=1)


def _ssd_kernel(xs_ref, bt_ref, c_ref, z_ref, dtr_ref, biasr_ref, alogr_ref, dskip_ref, gain_ref, o_ref,
                cumr_s, g2r_s, ld2r_s, wr_s, decr_s, st_s, h_s, *, n_chunks):
    L = SSD_CHUNK
    hpg = SSD_HEADS_PER_GROUP
    gw = SSD_GROUP_WIDTH
    ns = SSD_STATE

    dt = _softplus(dtr_ref[...] + biasr_ref[0])
    a = dt * (-jnp.exp(alogr_ref[0]))
    is_fwd = lax.broadcasted_iota(jnp.int32, a.shape, 0) < hpg
    prefix = _chunk_scan(a, False)
    suffix = _chunk_scan(a, True)
    cum2 = jnp.where(is_fwd, prefix, suffix) * LOG2E
    cumr_s[...] = cum2
    g2r_s[...] = cum2 - jnp.log2(dt)
    ld2r_s[...] = jnp.log2(dt + pltpu.roll(dt, hpg, axis=0))
    wr_s[...] = dt * jnp.exp(jnp.where(is_fwd, suffix, prefix) - a)
    decr_s[...] = jnp.exp(prefix + suffix - a)

    def chunk_slice(c):
        return pl.ds(pl.multiple_of(c * L, L), L)

    lane_head = lax.broadcasted_iota(jnp.int32, (L, gw), 1) // SSD_HEAD_DIM

    def block_diag_x(xs):
        return jnp.concatenate([jnp.where(lane_head == j, xs, jnp.zeros_like(xs)) for j in range(hpg)], axis=0)

    def state_body(c, carry):
        sl = chunk_slice(c)
        bt = bt_ref[:, sl].astype(F32)
        w = wr_s[:, sl]
        lhs = jnp.concatenate(
            [jnp.concatenate([(bt * w[d * hpg + j:d * hpg + j + 1, :]).astype(BF16) for j in range(hpg)], axis=1)
             for d in range(2)], axis=0)
        st_s[c] = _dot(lhs, block_diag_x(xs_ref[0, sl, :]))
        return carry

    lax.fori_loop(0, n_chunks, state_body, 0, unroll=8)

    def fwd_rec(c, h):
        h_s[c, :, 0:gw] = h.astype(BF16)
        return h * _head_row(decr_s[:, chunk_slice(c)], 0) + st_s[c, 0:ns, :]

    def bwd_rec(i, h):
        c = n_chunks - 1 - i
        h_s[c, :, gw:2 * gw] = h.astype(BF16)
        return h * _head_row(decr_s[:, chunk_slice(c)], hpg) + st_s[c, ns:2 * ns, :]

    h0 = jnp.zeros((ns, gw), F32)
    lax.fori_loop(0, n_chunks, fwd_rec, h0)
    lax.fori_loop(0, n_chunks, bwd_rec, h0)

    li = lax.broadcasted_iota(jnp.int32, (L, L), 0)
    si = lax.broadcasted_iota(jnp.int32, (L, L), 1)
    below = si < li
    above = si > li
    lane_lo = lax.broadcasted_iota(jnp.int32, (L, V7X_LANES), 1) < SSD_HEAD_DIM

    def out_body(c, carry):
        sl = chunk_slice(c)
        xs = xs_ref[0, sl, :]
        cm = c_ref[0, sl, :]
        cum_r = cumr_s[:, sl]
        g2 = g2r_s[:, sl]
        ld2 = ld2r_s[:, sl]
        cum_t = [jnp.broadcast_to(cum_r[k:k + 1, :], (L, L)).T for k in range(2 * hpg)]
        cb = _dot(cm, bt_ref[:, sl])
        mats = []
        for j in range(hpg):
            seg_f = cum_t[j] - g2[j:j + 1, :]
            seg_b = cum_t[hpg + j] - g2[hpg + j:hpg + j + 1, :]
            arg = jnp.where(below, seg_f, jnp.where(above, seg_b, ld2[j:j + 1, :]))
            mats.append((cb * jnp.exp2(arg)).astype(BF16))
        y = _dot(jnp.concatenate(mats, axis=1), block_diag_x(xs))
        carried = _dot(cm, h_s[c])
        for d in range(2):
            decay = jnp.exp2(jnp.concatenate(
                [jnp.where(lane_lo, cum_t[d * hpg + 2 * i], cum_t[d * hpg + 2 * i + 1]) for i in range(hpg // 2)],
                axis=1))
            y = y + carried[:, d * gw:(d + 1) * gw] * decay
        y = y + dskip_ref[0] * xs.astype(F32)
        gated = y * _silu(z_ref[0, sl, :].astype(F32))
        ms = jnp.mean(gated * gated, axis=-1, keepdims=True)
        o_ref[0, sl, :] = (gated * lax.rsqrt(ms + NORM_EPS) * gain_ref[0]).astype(BF16)
        return carry

    lax.fori_loop(0, n_chunks, out_body, 0, unroll=8)


def _group_major(v):
    return jnp.transpose(v.astype(F32).reshape(2, SSD_GROUPS, SSD_HEADS_PER_GROUP), (1, 0, 2)).reshape(
        SSD_GROUPS, 2 * SSD_HEADS_PER_GROUP)


def _ssd_mixer(proj3d, xs3d, bt2d, c3d, dt_rows, dt_bias, a_log, d_skip, out_gain):
    b, s, _ = proj3d.shape
    n_chunks = s // SSD_CHUNK
    g, hpg, gw, ns = SSD_GROUPS, SSD_HEADS_PER_GROUP, SSD_GROUP_WIDTH, SSD_STATE
    bias_g = _group_major(dt_bias)[:, :, None]
    alog_g = _group_major(a_log)[:, :, None]
    dskip = jnp.repeat(d_skip.astype(F32), SSD_HEAD_DIM).reshape(g, 1, gw)
    gain = out_gain.astype(F32).reshape(g, 1, gw)
    z_blk = (3 * NA_WIDTH) // gw
    kern = functools.partial(_ssd_kernel, n_chunks=n_chunks)
    small = lambda shape: pl.BlockSpec((1,) + shape, lambda i, k: (k, 0, 0))
    seq_blk = lambda width, blk0: pl.BlockSpec((1, s, width), lambda i, k: (i, 0, blk0 + k))
    row_scratch = pltpu.VMEM((2 * hpg, s), F32)
    return pl.pallas_call(
        kern, grid=(b, g),
        in_specs=[
            seq_blk(gw, 0), pl.BlockSpec((ns, s), lambda i, k: (k, i)), seq_blk(ns, 0), seq_blk(gw, z_blk),
            pl.BlockSpec((2 * hpg, s), lambda i, k: (k, i)),
            small((2 * hpg, 1)), small((2 * hpg, 1)), small((1, gw)), small((1, gw)),
        ],
        out_specs=pl.BlockSpec((1, s, gw), lambda i, k: (i, 0, k)),
        out_shape=jax.ShapeDtypeStruct((b, s, SSD_D_INNER), BF16),
        scratch_shapes=[
            row_scratch, row_scratch, row_scratch, row_scratch, row_scratch,
            pltpu.VMEM((n_chunks, 2 * ns, gw), F32), pltpu.VMEM((n_chunks, ns, 2 * gw), BF16),
        ],
        compiler_params=_params(("parallel", "parallel")), name="ssd_bidirectional",
    )(xs3d, bt2d, c3d, proj3d, dt_rows, bias_g, alog_g, dskip, gain)


FFN_CHUNK = 256


def _mix_ffn_kernel(*refs, n_acts, hidden):
    act_refs = refs[:n_acts]
    wout_refs = refs[n_acts:2 * n_acts]
    x_ref, g_ref, w13_ref, w2_ref, o_ref, hid_ref = refs[2 * n_acts:]
    h = x_ref[...]
    for a_ref, w_ref in zip(act_refs, wout_refs):
        h = h + _dot(a_ref[...], w_ref[...])
    hn = _rms_rows(h, g_ref[...]).astype(BF16)
    for c in range(hidden // FFN_CHUNK):
        gate = slice(c * FFN_CHUNK, (c + 1) * FFN_CHUNK)
        up = slice(hidden + c * FFN_CHUNK, hidden + (c + 1) * FFN_CHUNK)
        hid_ref[:, gate] = (_silu(_dot(hn, w13_ref[:, gate])) * _dot(hn, w13_ref[:, up])).astype(BF16)
    o_ref[...] = h + _dot(hid_ref[...], w2_ref[...])


def _mix_ffn(acts, w_outs, x2d, g, w13, w2, layer, *, tm, name):
    m, d = x2d.shape
    hid = w2.shape[1]
    assert hid % FFN_CHUNK == 0
    row = lambda width: pl.BlockSpec((tm, width), lambda i: (i, 0))
    resident = lambda shape: pl.BlockSpec(shape, lambda i: (0, 0), pipeline_mode=pl.Buffered(1))
    stacked = lambda shape: pl.BlockSpec((None,) + shape, lambda i: (layer, 0, 0), pipeline_mode=pl.Buffered(1))
    return pl.pallas_call(
        functools.partial(_mix_ffn_kernel, n_acts=len(acts), hidden=hid), grid=(m // tm,),
        in_specs=([row(a.shape[1]) for a in acts] + [resident(w.shape) for w in w_outs]
                  + [row(d), pl.BlockSpec((1, d), lambda i: (0, 0)), stacked((d, 2 * hid)), stacked((hid, d))]),
        out_specs=row(d),
        out_shape=jax.ShapeDtypeStruct((m, d), F32),
        scratch_shapes=[pltpu.VMEM((tm, hid), BF16)],
        compiler_params=_params(("parallel",)), name=name,
    )(*acts, *w_outs, x2d, g, w13, w2)


def _rope_prep_kernel(p_ref, qg_ref, kg_ref, cos_ref, sin_ref, cost_ref, sint_ref, q_ref, k_ref, v_ref):
    scale = GQA_HEAD_DIM ** -0.5 * LOG2E
    ts = p_ref.shape[1]
    cos_t = cost_ref[...]
    sin_t = sint_ref[...]
    even_row3 = (lax.broadcasted_iota(jnp.int32, (2 * GQA_HEAD_DIM // 8, 8, ts), 1) % 2) == 0
    q_gain = qg_ref[...] * scale
    hd = GQA_HEAD_DIM
    for pair in range(GQA_HEADS // 2):
        xt = p_ref[0, :, pair * GQA_PAIR:(pair + 1) * GQA_PAIR].astype(F32).T
        x2 = xt * xt
        inv = [lax.rsqrt(jnp.mean(x2[h * hd:(h + 1) * hd], axis=0, keepdims=True) + NORM_EPS) for h in range(2)]
        xn = jnp.concatenate([xt[:hd] * inv[0], xt[hd:] * inv[1]], axis=0) * q_gain
        x3 = xn.reshape(2 * hd // 8, 8, ts)
        swapped = jnp.where(even_row3, pltpu.roll(x3, 7, axis=1), pltpu.roll(x3, 1, axis=1)).reshape(2 * hd, ts)
        out = xn * cos_t + swapped * sin_t
        q_ref[0, 2 * pair] = out[:hd].astype(BF16)
        q_ref[0, 2 * pair + 1] = out[hd:].astype(BF16)

    cos = cos_ref[...]
    sin = sin_ref[...]
    even = (lax.broadcasted_iota(jnp.int32, cos.shape, 1) % 2) == 0
    for pair in range(GQA_KV_HEADS // 2):
        c0 = GQA_Q_WIDTH + pair * GQA_PAIR
        xn = _pair_head_rms(p_ref[0, :, c0:c0 + GQA_PAIR].astype(F32), kg_ref[...])
        swapped = jnp.where(even, pltpu.roll(xn, V7X_LANES - 1, axis=1), pltpu.roll(xn, 1, axis=1))
        blk = xn * cos + swapped * sin
        k_ref[0, 2 * pair] = blk[:, :hd].astype(BF16)
        k_ref[0, 2 * pair + 1] = blk[:, hd:].astype(BF16)
        c1 = GQA_Q_WIDTH + GQA_KV_WIDTH + pair * GQA_PAIR
        vt = p_ref[0, :, c1:c1 + GQA_PAIR].astype(F32).T.astype(BF16)
        v_ref[0, 2 * pair] = vt[:hd]
        v_ref[0, 2 * pair + 1] = vt[hd:]


def _axial_rope_tables(s):
    t = np.arange(s)
    row = (t // GRID_W).astype(np.float32)
    col = (t % GRID_W).astype(np.float32)
    axis_dims = GQA_HEAD_DIM // 2
    freqs = np.float32(ROPE_THETA) ** (-np.arange(0, axis_dims, 2, dtype=np.float32) / np.float32(axis_dims))
    ang = np.concatenate([row[:, None] * freqs, col[:, None] * freqs], axis=-1).astype(np.float32)
    cos = np.repeat(np.cos(ang), 2, axis=-1)
    sin = np.stack([-np.sin(ang), np.sin(ang)], axis=-1).reshape(s, GQA_HEAD_DIM)
    return np.tile(cos, (1, 2)).astype(np.float32), np.tile(sin, (1, 2)).astype(np.float32)


def _rope_prep(proj3d, q_gain, k_gain, *, ts):
    b, s, width = proj3d.shape
    cos, sin = _axial_rope_tables(s)
    qg = jnp.tile(q_gain.astype(F32), 2)[:, None]
    kg = jnp.tile(k_gain.astype(F32), 2)[None, :]
    head_out = lambda n: pl.BlockSpec((1, n, ts, GQA_HEAD_DIM), lambda i, t: (i, 0, t, 0))
    shape = lambda n: jax.ShapeDtypeStruct((b, n, s, GQA_HEAD_DIM), BF16)
    t_out = lambda n: pl.BlockSpec((1, n, GQA_HEAD_DIM, ts), lambda i, t: (i, 0, 0, t))
    t_shape = lambda n: jax.ShapeDtypeStruct((b, n, GQA_HEAD_DIM, s), BF16)
    return pl.pallas_call(
        _rope_prep_kernel, grid=(b, s // ts),
        in_specs=[
            pl.BlockSpec((1, ts, width), lambda i, t: (i, t, 0)),
            pl.BlockSpec((GQA_PAIR, 1), lambda i, t: (0, 0)),
            pl.BlockSpec((1, GQA_PAIR), lambda i, t: (0, 0)),
            pl.BlockSpec((ts, GQA_PAIR), lambda i, t: (t, 0)),
            pl.BlockSpec((ts, GQA_PAIR), lambda i, t: (t, 0)),
            pl.BlockSpec((GQA_PAIR, ts), lambda i, t: (0, t)),
            pl.BlockSpec((GQA_PAIR, ts), lambda i, t: (0, t)),
        ],
        out_specs=[t_out(GQA_HEADS), head_out(GQA_KV_HEADS), t_out(GQA_KV_HEADS)],
        out_shape=[t_shape(GQA_HEADS), shape(GQA_KV_HEADS), t_shape(GQA_KV_HEADS)],
        compiler_params=_params(("parallel", "parallel")), name="gqa_norm_rope",
    )(proj3d, qg, kg, jnp.asarray(cos), jnp.asarray(sin), jnp.asarray(cos.T.copy()), jnp.asarray(sin.T.copy()))


GQA_KV_CHUNK = 256
GQA_SUM_ROWS = 16
GQA_SUB_Q = 128


def _gqa_kernel(q_ref, k_ref, vt_ref, o_ref, s_a, s_b, m_a, m_b, *, tq, seq):
    t = pl.program_id(0)
    n_sub = tq // GQA_SUB_Q
    cols = GQA_REP * GQA_SUB_Q

    @pl.when(t == 0)
    def _():
        s_b[...] = jnp.zeros(s_b.shape, F32)
        m_b[...] = jnp.zeros(m_b.shape, F32)

    def sub_step(sub, s_cur, m_cur, s_prev, m_prev_ref):
        qs = slice(sub * GQA_SUB_Q, (sub + 1) * GQA_SUB_Q)
        qt = jnp.concatenate([q_ref[0, r, :, qs] for r in range(GQA_REP)], axis=1)
        m_prev = m_prev_ref[sub]
        m_run = None
        ones = jnp.ones((GQA_SUM_ROWS, GQA_KV_CHUNK), BF16)
        acc = jnp.zeros((GQA_HEAD_DIM + GQA_SUM_ROWS, cols), F32)
        for i in range(seq // GQA_KV_CHUNK):
            rows = slice(i * GQA_KV_CHUNK, (i + 1) * GQA_KV_CHUNK)
            st = _dot(k_ref[0, 0, rows, :], qt)
            s_cur[sub, rows, :] = st
            cm = jnp.max(st, axis=0, keepdims=True)
            m_run = cm if m_run is None else jnp.maximum(m_run, cm)
            p = jnp.exp2(s_prev[sub, rows, :] - m_prev)
            vt_aug = jnp.concatenate([vt_ref[0, 0, :, rows], ones], axis=0)
            acc = acc + _dot(vt_aug, p.astype(BF16))
        m_cur[sub] = m_run
        ot = acc[:GQA_HEAD_DIM] * (1.0 / acc[GQA_HEAD_DIM:GQA_HEAD_DIM + 1])
        for r in range(GQA_REP):
            o_ref[0, qs, r * GQA_HEAD_DIM:(r + 1) * GQA_HEAD_DIM] = (
                ot[:, r * GQA_SUB_Q:(r + 1) * GQA_SUB_Q].T.astype(BF16))

    def step(*bufs):
        for sub in range(n_sub):
            sub_step(sub, *bufs)

    pl.when(t % 2 == 0)(lambda: step(s_a, m_a, s_b, m_b))
    pl.when(t % 2 == 1)(lambda: step(s_b, m_b, s_a, m_a))


def _gqa_attention(q, k, vt, *, tq):
    b, _, _, s = q.shape
    nq = s // tq
    n_blocks = b * GQA_KV_HEADS * nq
    n_sub = tq // GQA_SUB_Q
    cols = GQA_REP * GQA_SUB_Q

    def unravel(u):
        return u // (nq * GQA_KV_HEADS), (u // nq) % GQA_KV_HEADS, u % nq

    def score_block(t):
        return unravel(jnp.minimum(t, n_blocks - 1))

    def finish_block(t):
        return unravel(jnp.maximum(t - 1, 0))

    def q_map(t):
        i, g, j = score_block(t)
        return (i, g, 0, j)

    def k_map(t):
        i, g, _ = score_block(t)
        return (i, g, 0, 0)

    def vt_map(t):
        i, g, _ = finish_block(t)
        return (i, g, 0, 0)

    def o_map(t):
        i, g, j = finish_block(t)
        return (i, j, g)

    kern = functools.partial(_gqa_kernel, tq=tq, seq=s)
    return pl.pallas_call(
        kern, grid=(n_blocks + 1,),
        in_specs=[
            pl.BlockSpec((1, GQA_REP, GQA_HEAD_DIM, tq), q_map),
            pl.BlockSpec((1, 1, s, GQA_HEAD_DIM), k_map),
            pl.BlockSpec((1, 1, GQA_HEAD_DIM, s), vt_map),
        ],
        out_specs=pl.BlockSpec((1, tq, GQA_REP * GQA_HEAD_DIM), o_map),
        out_shape=jax.ShapeDtypeStruct((b, s, GQA_Q_WIDTH), BF16),
        scratch_shapes=[pltpu.VMEM((n_sub, s, cols), F32), pltpu.VMEM((n_sub, s, cols), F32),
                        pltpu.VMEM((n_sub, 1, cols), F32), pltpu.VMEM((n_sub, 1, cols), F32)],
        compiler_params=_params(("arbitrary",)), name="gqa_attention",
    )(q, k, vt)


def _even_mixer(x2d, b, s, mix_norm, w_in, q_gain, k_gain, rpb, conv_w, conv_b, dt_bias, a_log, d_skip, out_gain,
                w_out):
    w_main = w_in.astype(BF16)
    w_dt = jnp.transpose(w_in[:, EVEN_MAIN_WIDTH:].reshape(-1, 2, SSD_GROUPS, SSD_HEADS_PER_GROUP),
                         (2, 1, 3, 0)).reshape(2 * SSD_HEADS, -1).astype(BF16)
    proj, xs, bt, cc, dt_rows = _even_in_proj(x2d, mix_norm.astype(F32)[None, :], w_main, w_dt, conv_w.astype(F32),
                                              conv_b.astype(F32)[None, :], tm=1024, seq=s, n_main=EVEN_XBC_OFFSET,
                                              name="even_in_proj")
    proj3d = proj.reshape(b, s, EVEN_XBC_OFFSET)
    na_out = _neighbourhood_attention(proj3d, q_gain, k_gain, rpb)
    ssd_out = _ssd_mixer(proj3d, xs.reshape(b, s, -1), bt, cc.reshape(b, s, -1), dt_rows, dt_bias, a_log, d_skip,
                         out_gain)
    w_out_bf = w_out.astype(BF16)
    return ([na_out.reshape(b * s, NA_WIDTH), ssd_out.reshape(b * s, SSD_D_INNER)],
            [w_out_bf[:NA_WIDTH], w_out_bf[NA_WIDTH:]])


def _odd_mixer(x2d, b, s, mix_norm, w_qkv, q_gain, k_gain, w_out):
    proj = _norm_proj(x2d, mix_norm.astype(F32)[None, :], w_qkv.astype(BF16), tm=1024, name="odd_qkv_proj")
    q, k, vt = _rope_prep(proj.reshape(b, s, -1), q_gain, k_gain, ts=1024)
    attn = _gqa_attention(q, k, vt, tq=512)
    return [attn.reshape(b * s, GQA_Q_WIDTH)], [w_out.astype(BF16)]


def kernel(x, even_mix_norm, even_w_in, na_q_norm, na_k_norm, na_rel_bias, ssd_conv_w, ssd_conv_b, ssd_dt_bias, ssd_A_log, ssd_D, ssd_out_norm, even_w_out, odd_mix_norm, odd_w_qkv, gqa_q_norm, gqa_k_norm, odd_w_out, ffn_norm, ffn_w13, ffn_w2):
    b, s, d = x.shape
    depth = ffn_norm.shape[0]
    h = x.reshape(b * s, d)
    w13_bf = ffn_w13.astype(BF16)
    w2_bf = ffn_w2.astype(BF16)
    for layer in range(depth):
        i = layer // 2
        if layer % 2 == 0:
            acts, w_outs = _even_mixer(h, b, s, even_mix_norm[i], even_w_in[i], na_q_norm[i], na_k_norm[i],
                                       na_rel_bias[i], ssd_conv_w[i], ssd_conv_b[i], ssd_dt_bias[i], ssd_A_log[i],
                                       ssd_D[i], ssd_out_norm[i], even_w_out[i])
        else:
            acts, w_outs = _odd_mixer(h, b, s, odd_mix_norm[i], odd_w_qkv[i], gqa_q_norm[i], gqa_k_norm[i],
                                      odd_w_out[i])
        h = _mix_ffn(acts, w_outs, h, ffn_norm[layer].astype(F32)[None, :], w13_bf, w2_bf, layer, tm=1024,
                     name="mix_out_ffn_even" if layer % 2 == 0 else "mix_out_ffn_odd")
    return h.reshape(b, s, d)
```

```python
import functools

import jax
import jax.numpy as jnp
import numpy as np
from jax import lax
from jax.experimental import pallas as pl
from jax.experimental.pallas import tpu as pltpu

F32 = jnp.float32
BF16 = jnp.bfloat16

GRID_W = 64
NORM_EPS = 1e-6

NA_HEADS = 8
NA_HEAD_DIM = 64
NA_WIDTH = NA_HEADS * NA_HEAD_DIM
NA_KH = 8
NA_KW = 16

SSD_D_INNER = 1024
SSD_HEAD_DIM = 64
SSD_HEADS = 16
SSD_GROUPS = 4
SSD_STATE = 128
SSD_CONV = 4
SSD_CHUNK = 128
SSD_CONV_DIM = SSD_D_INNER + 2 * SSD_GROUPS * SSD_STATE
SSD_GROUP_WIDTH = SSD_D_INNER // SSD_GROUPS
SSD_HEADS_PER_GROUP = SSD_HEADS // SSD_GROUPS

EVEN_MAIN_WIDTH = 3 * NA_WIDTH + SSD_D_INNER + SSD_CONV_DIM
EVEN_XBC_OFFSET = 3 * NA_WIDTH + SSD_D_INNER

GQA_HEADS = 16
GQA_KV_HEADS = 4
GQA_HEAD_DIM = 64
GQA_REP = GQA_HEADS // GQA_KV_HEADS
GQA_Q_WIDTH = GQA_HEADS * GQA_HEAD_DIM
GQA_KV_WIDTH = GQA_KV_HEADS * GQA_HEAD_DIM
GQA_PAIR = 2 * GQA_HEAD_DIM
ROPE_THETA = 10000.0

V7X_LANES = 128
V7X_VMEM_LIMIT = 56 * 1024 * 1024
MASK_VALUE = -1e30
LOG2E = 1.4426950408889634


def _params(dims):
    return pltpu.CompilerParams(dimension_semantics=dims, vmem_limit_bytes=V7X_VMEM_LIMIT)


def _silu(v):
    return v * (1.0 / (1.0 + jnp.exp(-v)))


def _softplus(v):
    return jnp.maximum(v, 0.0) + jnp.log(1.0 + jnp.exp(-jnp.abs(v)))


def _rms_rows(x, g):
    ms = jnp.mean(x * x, axis=-1, keepdims=True)
    return x * lax.rsqrt(ms + NORM_EPS) * g


def _dot(a, b):
    return jnp.dot(a, b, preferred_element_type=F32)


def _dot_nt(a, b):
    return lax.dot_general(a, b, (((1,), (1,)), ((), ())), preferred_element_type=F32)


PROJ_CHUNK = 256


def _norm_proj_kernel(x_ref, g_ref, w_ref, o_ref):
    xn = _rms_rows(x_ref[...], g_ref[...]).astype(BF16)
    for c in range(o_ref.shape[1] // PROJ_CHUNK):
        cols = slice(c * PROJ_CHUNK, (c + 1) * PROJ_CHUNK)
        o_ref[:, cols] = _dot(xn, w_ref[:, cols]).astype(o_ref.dtype)


CONV_HALO = 8


def _even_in_proj_kernel(x_ref, xp_ref, xn_ref, g_ref, w_ref, wdt_ref, cw_ref, cb_ref, o_ref, ox_ref, obt_ref, oc_ref,
                         odt_ref, *, tiles_per_seq):
    tm = x_ref.shape[0]
    n_main = o_ref.shape[1]
    i = pl.program_id(0) % tiles_per_seq
    gain = g_ref[...]
    xc = _rms_rows(x_ref[...], gain)
    xp = jnp.where(i > 0, _rms_rows(xp_ref[...], gain), 0.0)
    xn = jnp.where(i < tiles_per_seq - 1, _rms_rows(xn_ref[...], gain), 0.0)
    xc_bf = xc.astype(BF16)
    xe_bf = jnp.concatenate([xp, xc, xn], axis=0).astype(BF16)
    odt_ref[...] = _dot_nt(wdt_ref[...], xc_bf)
    n_ext = tm + 2 * CONV_HALO
    left = SSD_CONV // 2
    n_main_chunks = n_main // PROJ_CHUNK
    nx, nb, ncc = (r // PROJ_CHUNK for r in (ox_ref.shape[1], obt_ref.shape[0], oc_ref.shape[1]))
    n_conv_chunks = nx + nb + ncc

    def main_chunk(c):
        cols = slice(c * PROJ_CHUNK, (c + 1) * PROJ_CHUNK)
        o_ref[:, cols] = _dot(xc_bf, w_ref[:, cols]).astype(o_ref.dtype)

    def conv_chunk(c, pr):
        cols = slice(c * PROJ_CHUNK, (c + 1) * PROJ_CHUNK)
        acc = jnp.broadcast_to(cb_ref[:, cols], (tm, PROJ_CHUNK))
        for k in range(SSD_CONV):
            shift = (left - k) % n_ext
            tap = pr if shift == 0 else pltpu.roll(pr, shift, axis=0)
            acc = acc + tap[CONV_HALO:CONV_HALO + tm] * cw_ref[k:k + 1, cols]
        out = _silu(acc)
        if c < nx:
            ox_ref[:, cols] = out.astype(ox_ref.dtype)
        elif c < nx + nb:
            obt_ref[(c - nx) * PROJ_CHUNK:(c - nx + 1) * PROJ_CHUNK, :] = out.T.astype(obt_ref.dtype)
        else:
            oc_ref[:, (c - nx - nb) * PROJ_CHUNK:(c - nx - nb + 1) * PROJ_CHUNK] = out.astype(oc_ref.dtype)

    for c in range(max(n_main_chunks, n_conv_chunks)):
        pr = None
        if c < n_conv_chunks:
            pr = _dot(xe_bf, w_ref[:, n_main + c * PROJ_CHUNK:n_main + (c + 1) * PROJ_CHUNK])
        if c < n_main_chunks:
            main_chunk(c)
        if pr is not None:
            conv_chunk(c, pr)


def _even_in_proj(x2d, g, w, w_dt, conv_w, conv_b, *, tm, seq, n_main, name):
    m, d = x2d.shape
    n = w.shape[1]
    n_x = SSD_D_INNER
    n_bc = SSD_GROUPS * SSD_STATE
    assert n >= n_main + n_x + 2 * n_bc and n_x % PROJ_CHUNK == 0 and n_bc % PROJ_CHUNK == 0
    ns = w_dt.shape[0]
    assert n_main % PROJ_CHUNK == 0 and seq % tm == 0 and tm % CONV_HALO == 0
    halo_per_tile = tm // CONV_HALO
    n_halo_blocks = m // CONV_HALO
    resident = lambda shape: pl.BlockSpec(shape, lambda i: (0, 0), pipeline_mode=pl.Buffered(1))
    kern = functools.partial(_even_in_proj_kernel, tiles_per_seq=seq // tm)
    return pl.pallas_call(
        kern, grid=(m // tm,),
        in_specs=[
            pl.BlockSpec((tm, d), lambda i: (i, 0)),
            pl.BlockSpec((CONV_HALO, d), lambda i: (jnp.maximum(i * halo_per_tile - 1, 0), 0)),
            pl.BlockSpec((CONV_HALO, d), lambda i: (jnp.minimum((i + 1) * halo_per_tile, n_halo_blocks - 1), 0)),
            pl.BlockSpec((1, d), lambda i: (0, 0)),
            resident((d, n)), resident((ns, d)), resident(conv_w.shape), resident(conv_b.shape),
        ],
        out_specs=[pl.BlockSpec((tm, n_main), lambda i: (i, 0)), pl.BlockSpec((tm, n_x), lambda i: (i, 0)),
                   pl.BlockSpec((n_bc, tm), lambda i: (0, i)), pl.BlockSpec((tm, n_bc), lambda i: (i, 0)),
                   pl.BlockSpec((ns, tm), lambda i: (0, i))],
        out_shape=[jax.ShapeDtypeStruct((m, n_main), BF16), jax.ShapeDtypeStruct((m, n_x), BF16),
                   jax.ShapeDtypeStruct((n_bc, m), BF16), jax.ShapeDtypeStruct((m, n_bc), BF16),
                   jax.ShapeDtypeStruct((ns, m), F32)],
        compiler_params=_params(("parallel",)), name=name,
    )(x2d, x2d, x2d, g, w, w_dt, conv_w, conv_b)


def _norm_proj(x2d, g, w, *, tm, name):
    m, d = x2d.shape
    n = w.shape[1]
    assert n % PROJ_CHUNK == 0
    resident = lambda shape: pl.BlockSpec(shape, lambda i: (0, 0), pipeline_mode=pl.Buffered(1))
    return pl.pallas_call(
        _norm_proj_kernel, grid=(m // tm,),
        in_specs=[pl.BlockSpec((tm, d), lambda i: (i, 0)), pl.BlockSpec((1, d), lambda i: (0, 0)), resident((d, n))],
        out_specs=pl.BlockSpec((tm, n), lambda i: (i, 0)), out_shape=jax.ShapeDtypeStruct((m, n), BF16),
        compiler_params=_params(("parallel",)), name=name)(x2d, g, w)


NA_PREP_ROWS = 256
NA_GROUP_ROWS = 4
NA_WIN_ROWS = NA_KH + NA_GROUP_ROWS
NA_DY = 2 * NA_KH - 1
NA_DX = 2 * NA_KW - 1


def _na_group_plan(rows):
    sigs, starts, classes = [], [], []
    for gq in range(rows // NA_GROUP_ROWS):
        ks = min(max(gq * NA_GROUP_ROWS - NA_KH // 2, 0), rows - NA_WIN_ROWS)
        sig = tuple((min(max(r - NA_KH // 2, 0), rows - NA_KH) - ks, r - ks)
                    for r in range(gq * NA_GROUP_ROWS, (gq + 1) * NA_GROUP_ROWS))
        assert all(0 <= first and first + NA_KH <= NA_WIN_ROWS for first, _ in sig)
        if sig not in sigs:
            sigs.append(sig)
        starts.append(ks)
        classes.append(sigs.index(sig))
    return sigs, starts, classes


def _pair_head_rms(x, g):
    lo = lax.broadcasted_iota(jnp.int32, x.shape, 1) < NA_HEAD_DIM
    x2 = x * x
    s_lo = jnp.sum(jnp.where(lo, x2, 0.0), axis=-1, keepdims=True)
    s_hi = jnp.sum(jnp.where(lo, 0.0, x2), axis=-1, keepdims=True)
    ms = jnp.where(lo, s_lo, s_hi) * (1.0 / NA_HEAD_DIM)
    return x * lax.rsqrt(ms + NORM_EPS) * g


def _na_kernel(plan_ref, q_ref, k_ref, v_ref, qg_ref, kg_ref, bias_ref, o_ref, q_s, k_s, s_a, s_b, m_a, m_b, *, rows):
    scale = NA_HEAD_DIM ** -0.5 * LOG2E

    same_head = (lax.broadcasted_iota(jnp.int32, (V7X_LANES, V7X_LANES), 0) // NA_HEAD_DIM
                 == lax.broadcasted_iota(jnp.int32, (V7X_LANES, V7X_LANES), 1) // NA_HEAD_DIM)
    head_ones = jnp.where(same_head, 1.0, 0.0).astype(BF16)

    def head_rms(x, g):
        ms = _dot((x * x).astype(BF16), head_ones) * (1.0 / NA_HEAD_DIM)
        return x * lax.rsqrt(ms + NORM_EPS) * g

    def prep(i, carry):
        sl = pl.ds(pl.multiple_of(i * NA_PREP_ROWS, NA_PREP_ROWS), NA_PREP_ROWS)
        q_s[sl, :] = (head_rms(q_ref[0, sl, :].astype(F32), qg_ref[...]) * scale).astype(BF16)
        k_s[sl, :] = head_rms(k_ref[0, sl, :].astype(F32), kg_ref[...]).astype(BF16)
        return carry

    lax.fori_loop(0, (rows * GRID_W) // NA_PREP_ROWS, prep, 0, unroll=4)

    n_q = NA_GROUP_ROWS * GRID_W
    n_keys = NA_WIN_ROWS * GRID_W
    n_groups = rows // NA_GROUP_ROWS
    heads = [slice(h * NA_HEAD_DIM, (h + 1) * NA_HEAD_DIM) for h in range(2)]

    def key_rows(g):
        return pl.ds(pl.multiple_of(plan_ref[0, g] * GRID_W, GRID_W), n_keys)

    def query_rows(g):
        return pl.ds(pl.multiple_of(g * n_q, n_q), n_q)

    def scores(g, s_buf, m_buf):
        q = q_s[query_rows(g), :]
        kk = k_s[key_rows(g), :]
        cls = plan_ref[1, g]
        for h, hs in enumerate(heads):
            s = _dot_nt(q[:, hs], kk[:, hs]) + bias_ref[0, cls, h]
            s_buf[h] = s
            m_buf[h] = jnp.max(s, axis=-1, keepdims=True)

    def finish(g, s_buf, m_buf):
        vv = v_ref[0, key_rows(g), :]
        outs = []
        for h, hs in enumerate(heads):
            p = jnp.exp2(s_buf[h] - m_buf[h])
            l = jnp.sum(p, axis=-1, keepdims=True)
            outs.append(_dot(p.astype(BF16), vv[:, hs]) * (1.0 / l))
        o_ref[0, query_rows(g), :] = jnp.concatenate(outs, axis=-1).astype(BF16)

    scores(0, s_a, m_a)

    def pair_body(i, carry):
        g = 2 * i
        scores(g + 1, s_b, m_b)
        finish(g, s_a, m_a)
        scores(g + 2, s_a, m_a)
        finish(g + 1, s_b, m_b)
        return carry

    lax.fori_loop(0, n_groups // 2 - 1, pair_body, 0)
    scores(n_groups - 1, s_b, m_b)
    finish(n_groups - 2, s_a, m_a)
    finish(n_groups - 1, s_b, m_b)


def _na_bias_kernel(rpb_ref, o_ref, t_s, *, sigs):
    h = pl.program_id(0)
    q = lax.broadcasted_iota(jnp.int32, (GRID_W, GRID_W), 0)
    k = lax.broadcasted_iota(jnp.int32, (GRID_W, GRID_W), 1)
    dx = jnp.clip(k - q, -(NA_KW - 1), NA_KW - 1) + (NA_KW - 1)
    col_start = jnp.clip(q - NA_KW // 2, 0, GRID_W - NA_KW)
    in_win = (k >= col_start) & (k < col_start + NA_KW)
    masked = jnp.full((GRID_W, GRID_W), MASK_VALUE, F32)
    for dy in range(NA_DY):
        base = (h * NA_DY + dy) * NA_DX
        t = masked
        for d in range(NA_DX):
            t = jnp.where(dx == d, rpb_ref[base + d] * LOG2E, t)
        t_s[dy] = jnp.where(in_win, t, MASK_VALUE)
    for cls, sig in enumerate(sigs):
        for rq, (first, qrow) in enumerate(sig):
            for jk in range(NA_WIN_ROWS):
                attended = first <= jk < first + NA_KH
                tile = t_s[jk - qrow + NA_KH - 1] if attended else masked
                o_ref[0, cls, 0, rq * GRID_W:(rq + 1) * GRID_W, jk * GRID_W:(jk + 1) * GRID_W] = tile


def _na_bias_table(rpb, sigs):
    n_q = NA_GROUP_ROWS * GRID_W
    n_keys = NA_WIN_ROWS * GRID_W
    kern = functools.partial(_na_bias_kernel, sigs=sigs)
    return pl.pallas_call(
        kern, grid=(NA_HEADS,),
        in_specs=[pl.BlockSpec(memory_space=pltpu.SMEM)],
        out_specs=pl.BlockSpec((1, len(sigs), 1, n_q, n_keys), lambda h: (h // 2, 0, h % 2, 0, 0)),
        out_shape=jax.ShapeDtypeStruct((NA_HEADS // 2, len(sigs), 2, n_q, n_keys), F32),
        scratch_shapes=[pltpu.VMEM((NA_DY, GRID_W, GRID_W), F32)],
        compiler_params=_params(("parallel",)), name="na_bias_table",
    )(rpb.astype(F32).reshape(-1))


def _neighbourhood_attention(proj3d, q_gain, k_gain, rpb):
    b, s, _ = proj3d.shape
    rows = s // GRID_W
    assert rows >= NA_WIN_ROWS and rows % (2 * NA_GROUP_ROWS) == 0
    sigs, starts, classes = _na_group_plan(rows)
    bias = _na_bias_table(rpb, sigs)
    plan = jnp.array([starts, classes], jnp.int32)
    qg = jnp.tile(q_gain.astype(F32), 2)[None, :]
    kg = jnp.tile(k_gain.astype(F32), 2)[None, :]
    n_pairs = NA_HEADS // 2
    blk = (1, s, 2 * NA_HEAD_DIM)
    kern = functools.partial(_na_kernel, rows=rows)
    return pl.pallas_call(
        kern, grid=(n_pairs, b),
        in_specs=[
            pl.BlockSpec(memory_space=pltpu.SMEM),
            pl.BlockSpec(blk, lambda p, i: (i, 0, p)),
            pl.BlockSpec(blk, lambda p, i: (i, 0, n_pairs + p)),
            pl.BlockSpec(blk, lambda p, i: (i, 0, 2 * n_pairs + p)),
            pl.BlockSpec((1, 2 * NA_HEAD_DIM), lambda p, i: (0, 0)),
            pl.BlockSpec((1, 2 * NA_HEAD_DIM), lambda p, i: (0, 0)),
            pl.BlockSpec((1,) + bias.shape[1:], lambda p, i: (p, 0, 0, 0, 0)),
        ],
        out_specs=pl.BlockSpec(blk, lambda p, i: (i, 0, p)),
        out_shape=jax.ShapeDtypeStruct((b, s, NA_WIDTH), BF16),
        scratch_shapes=([pltpu.VMEM((s, 2 * NA_HEAD_DIM), BF16)] * 2
                        + [pltpu.VMEM((2, NA_GROUP_ROWS * GRID_W, NA_WIN_ROWS * GRID_W), F32)] * 2
                        + [pltpu.VMEM((2, NA_GROUP_ROWS * GRID_W, 1), F32)] * 2),
        compiler_params=_params(("parallel", "parallel")), name="neighbourhood_attention",
    )(plan, proj3d, proj3d, proj3d, qg, kg, bias)


def _chunk_scan(a, reverse):
    n = a.shape[1]
    pos = lax.broadcasted_iota(jnp.int32, a.shape, 1) % SSD_CHUNK
    sh = 1
    while sh < SSD_CHUNK:
        if reverse:
            a = a + jnp.where(pos < SSD_CHUNK - sh, pltpu.roll(a, n - sh, axis=1), 0.0)
        else:
            a = a + jnp.where(pos >= sh, pltpu.roll(a, sh, axis=1), 0.0)
        sh *= 2
    return a


def _head_row(tile, first):
    lo = lax.broadcasted_iota(jnp.int32, (1, V7X_LANES), 1) < SSD_HEAD_DIM
    halves = [jnp.where(lo, tile[first + 2 * i:first + 2 * i + 1, :], tile[first + 2 * i + 1:first + 2 * i + 2, :])
              for i in range(SSD_HEADS_PER_GROUP // 2)]
    return jnp.concatenate(halves, axis=1)


def _ssd_kernel(xs_ref, bt_ref, c_ref, z_ref, dtr_ref, biasr_ref, alogr_ref, dskip_ref, gain_ref, o_ref,
                cumr_s, g2r_s, ld2r_s, wr_s, decr_s, st_s, h_s, *, n_chunks):
    L = SSD_CHUNK
    hpg = SSD_HEADS_PER_GROUP
    gw = SSD_GROUP_WIDTH
    ns = SSD_STATE

    dt = _softplus(dtr_ref[...] + biasr_ref[0])
    a = dt * (-jnp.exp(alogr_ref[0]))
    is_fwd = lax.broadcasted_iota(jnp.int32, a.shape, 0) < hpg
    prefix = _chunk_scan(a, False)
    suffix = _chunk_scan(a, True)
    cum2 = jnp.where(is_fwd, prefix, suffix) * LOG2E
    cumr_s[...] = cum2
    g2r_s[...] = cum2 - jnp.log2(dt)
    ld2r_s[...] = jnp.log2(dt + pltpu.roll(dt, hpg, axis=0))
    wr_s[...] = dt * jnp.exp(jnp.where(is_fwd, suffix, prefix) - a)
    decr_s[...] = jnp.exp(prefix + suffix - a)

    def chunk_slice(c):
        return pl.ds(pl.multiple_of(c * L, L), L)

    lane_head = lax.broadcasted_iota(jnp.int32, (L, gw), 1) // SSD_HEAD_DIM

    def block_diag_x(xs):
        return jnp.concatenate([jnp.where(lane_head == j, xs, jnp.zeros_like(xs)) for j in range(hpg)], axis=0)

    def state_body(c, carry):
        sl = chunk_slice(c)
        bt = bt_ref[:, sl].astype(F32)
        w = wr_s[:, sl]
        lhs = jnp.concatenate(
            [jnp.concatenate([(bt * w[d * hpg + j:d * hpg + j + 1, :]).astype(BF16) for j in range(hpg)], axis=1)
             for d in range(2)], axis=0)
        st_s[c] = _dot(lhs, block_diag_x(xs_ref[0, sl, :]))
        return carry

    lax.fori_loop(0, n_chunks, state_body, 0, unroll=8)

    def fwd_rec(c, h):
        h_s[c, :, 0:gw] = h.astype(BF16)
        return h * _head_row(decr_s[:, chunk_slice(c)], 0) + st_s[c, 0:ns, :]

    def bwd_rec(i, h):
        c = n_chunks - 1 - i
        h_s[c, :, gw:2 * gw] = h.astype(BF16)
        return h * _head_row(decr_s[:, chunk_slice(c)], hpg) + st_s[c, ns:2 * ns, :]

    h0 = jnp.zeros((ns, gw), F32)
    lax.fori_loop(0, n_chunks, fwd_rec, h0)
    lax.fori_loop(0, n_chunks, bwd_rec, h0)

    li = lax.broadcasted_iota(jnp.int32, (L, L), 0)
    si = lax.broadcasted_iota(jnp.int32, (L, L), 1)
    below = si < li
    above = si > li
    lane_lo = lax.broadcasted_iota(jnp.int32, (L, V7X_LANES), 1) < SSD_HEAD_DIM

    def out_body(c, carry):
        sl = chunk_slice(c)
        xs = xs_ref[0, sl, :]
        cm = c_ref[0, sl, :]
        cum_r = cumr_s[:, sl]
        g2 = g2r_s[:, sl]
        ld2 = ld2r_s[:, sl]
        cum_t = [jnp.broadcast_to(cum_r[k:k + 1, :], (L, L)).T for k in range(2 * hpg)]
        cb = _dot(cm, bt_ref[:, sl])
        mats = []
        for j in range(hpg):
            seg_f = cum_t[j] - g2[j:j + 1, :]
            seg_b = cum_t[hpg + j] - g2[hpg + j:hpg + j + 1, :]
            arg = jnp.where(below, seg_f, jnp.where(above, seg_b, ld2[j:j + 1, :]))
            mats.append((cb * jnp.exp2(arg)).astype(BF16))
        y = _dot(jnp.concatenate(mats, axis=1), block_diag_x(xs))
        carried = _dot(cm, h_s[c])
        for d in range(2):
            decay = jnp.exp2(jnp.concatenate(
                [jnp.where(lane_lo, cum_t[d * hpg + 2 * i], cum_t[d * hpg + 2 * i + 1]) for i in range(hpg // 2)],
                axis=1))
            y = y + carried[:, d * gw:(d + 1) * gw] * decay
        y = y + dskip_ref[0] * xs.astype(F32)
        gated = y * _silu(z_ref[0, sl, :].astype(F32))
        ms = jnp.mean(gated * gated, axis=-1, keepdims=True)
        o_ref[0, sl, :] = (gated * lax.rsqrt(ms + NORM_EPS) * gain_ref[0]).astype(BF16)
        return carry

    lax.fori_loop(0, n_chunks, out_body, 0, unroll=8)


def _group_major(v):
    return jnp.transpose(v.astype(F32).reshape(2, SSD_GROUPS, SSD_HEADS_PER_GROUP), (1, 0, 2)).reshape(
        SSD_GROUPS, 2 * SSD_HEADS_PER_GROUP)


def _ssd_mixer(proj3d, xs3d, bt2d, c3d, dt_rows, dt_bias, a_log, d_skip, out_gain):
    b, s, _ = proj3d.shape
    n_chunks = s // SSD_CHUNK
    g, hpg, gw, ns = SSD_GROUPS, SSD_HEADS_PER_GROUP, SSD_GROUP_WIDTH, SSD_STATE
    bias_g = _group_major(dt_bias)[:, :, None]
    alog_g = _group_major(a_log)[:, :, None]
    dskip = jnp.repeat(d_skip.astype(F32), SSD_HEAD_DIM).reshape(g, 1, gw)
    gain = out_gain.astype(F32).reshape(g, 1, gw)
    z_blk = (3 * NA_WIDTH) // gw
    kern = functools.partial(_ssd_kernel, n_chunks=n_chunks)
    small = lambda shape: pl.BlockSpec((1,) + shape, lambda i, k: (k, 0, 0))
    seq_blk = lambda width, blk0: pl.BlockSpec((1, s, width), lambda i, k: (i, 0, blk0 + k))
    row_scratch = pltpu.VMEM((2 * hpg, s), F32)
    return pl.pallas_call(
        kern, grid=(b, g),
        in_specs=[
            seq_blk(gw, 0), pl.BlockSpec((ns, s), lambda i, k: (k, i)), seq_blk(ns, 0), seq_blk(gw, z_blk),
            pl.BlockSpec((2 * hpg, s), lambda i, k: (k, i)),
            small((2 * hpg, 1)), small((2 * hpg, 1)), small((1, gw)), small((1, gw)),
        ],
        out_specs=pl.BlockSpec((1, s, gw), lambda i, k: (i, 0, k)),
        out_shape=jax.ShapeDtypeStruct((b, s, SSD_D_INNER), BF16),
        scratch_shapes=[
            row_scratch, row_scratch, row_scratch, row_scratch, row_scratch,
            pltpu.VMEM((n_chunks, 2 * ns, gw), F32), pltpu.VMEM((n_chunks, ns, 2 * gw), BF16),
        ],
        compiler_params=_params(("parallel", "parallel")), name="ssd_bidirectional",
    )(xs3d, bt2d, c3d, proj3d, dt_rows, bias_g, alog_g, dskip, gain)


FFN_CHUNK = 256


def _mix_ffn_kernel(*refs, n_acts, hidden):
    act_refs = refs[:n_acts]
    wout_refs = refs[n_acts:2 * n_acts]
    x_ref, g_ref, w13_ref, w2_ref, o_ref, hid_ref = refs[2 * n_acts:]
    h = x_ref[...]
    for a_ref, w_ref in zip(act_refs, wout_refs):
        h = h + _dot(a_ref[...], w_ref[...])
    hn = _rms_rows(h, g_ref[...]).astype(BF16)
    for c in range(hidden // FFN_CHUNK):
        gate = slice(c * FFN_CHUNK, (c + 1) * FFN_CHUNK)
        up = slice(hidden + c * FFN_CHUNK, hidden + (c + 1) * FFN_CHUNK)
        hid_ref[:, gate] = (_silu(_dot(hn, w13_ref[:, gate])) * _dot(hn, w13_ref[:, up])).astype(BF16)
    o_ref[...] = h + _dot(hid_ref[...], w2_ref[...])


def _mix_ffn(acts, w_outs, x2d, g, w13, w2, layer, *, tm, name):
    m, d = x2d.shape
    hid = w2.shape[1]
    assert hid % FFN_CHUNK == 0
    row = lambda width: pl.BlockSpec((tm, width), lambda i: (i, 0))
    resident = lambda shape: pl.BlockSpec(shape, lambda i: (0, 0), pipeline_mode=pl.Buffered(1))
    stacked = lambda shape: pl.BlockSpec((None,) + shape, lambda i: (layer, 0, 0), pipeline_mode=pl.Buffered(1))
    return pl.pallas_call(
        functools.partial(_mix_ffn_kernel, n_acts=len(acts), hidden=hid), grid=(m // tm,),
        in_specs=([row(a.shape[1]) for a in acts] + [resident(w.shape) for w in w_outs]
                  + [row(d), pl.BlockSpec((1, d), lambda i: (0, 0)), stacked((d, 2 * hid)), stacked((hid, d))]),
        out_specs=row(d),
        out_shape=jax.ShapeDtypeStruct((m, d), F32),
        scratch_shapes=[pltpu.VMEM((tm, hid), BF16)],
        compiler_params=_params(("parallel",)), name=name,
    )(*acts, *w_outs, x2d, g, w13, w2)


def _rope_prep_kernel(p_ref, qg_ref, kg_ref, cos_ref, sin_ref, cost_ref, sint_ref, q_ref, k_ref, v_ref):
    scale = GQA_HEAD_DIM ** -0.5 * LOG2E
    ts = p_ref.shape[1]
    cos_t = cost_ref[...]
    sin_t = sint_ref[...]
    even_row3 = (lax.broadcasted_iota(jnp.int32, (2 * GQA_HEAD_DIM // 8, 8, ts), 1) % 2) == 0
    q_gain = qg_ref[...] * scale
    hd = GQA_HEAD_DIM
    for pair in range(GQA_HEADS // 2):
        xt = p_ref[0, :, pair * GQA_PAIR:(pair + 1) * GQA_PAIR].astype(F32).T
        x2 = xt * xt
        inv = [lax.rsqrt(jnp.mean(x2[h * hd:(h + 1) * hd], axis=0, keepdims=True) + NORM_EPS) for h in range(2)]
        xn = jnp.concatenate([xt[:hd] * inv[0], xt[hd:] * inv[1]], axis=0) * q_gain
        x3 = xn.reshape(2 * hd // 8, 8, ts)
        swapped = jnp.where(even_row3, pltpu.roll(x3, 7, axis=1), pltpu.roll(x3, 1, axis=1)).reshape(2 * hd, ts)
        out = xn * cos_t + swapped * sin_t
        q_ref[0, 2 * pair] = out[:hd].astype(BF16)
        q_ref[0, 2 * pair + 1] = out[hd:].astype(BF16)

    cos = cos_ref[...]
    sin = sin_ref[...]
    even = (lax.broadcasted_iota(jnp.int32, cos.shape, 1) % 2) == 0
    for pair in range(GQA_KV_HEADS // 2):
        c0 = GQA_Q_WIDTH + pair * GQA_PAIR
        xn = _pair_head_rms(p_ref[0, :, c0:c0 + GQA_PAIR].astype(F32), kg_ref[...])
        swapped = jnp.where(even, pltpu.roll(xn, V7X_LANES - 1, axis=1), pltpu.roll(xn, 1, axis=1))
        blk = xn * cos + swapped * sin
        k_ref[0, 2 * pair] = blk[:, :hd].astype(BF16)
        k_ref[0, 2 * pair + 1] = blk[:, hd:].astype(BF16)
        c1 = GQA_Q_WIDTH + GQA_KV_WIDTH + pair * GQA_PAIR
        vt = p_ref[0, :, c1:c1 + GQA_PAIR].astype(F32).T.astype(BF16)
        v_ref[0, 2 * pair] = vt[:hd]
        v_ref[0, 2 * pair + 1] = vt[hd:]


def _axial_rope_tables(s):
    t = np.arange(s)
    row = (t // GRID_W).astype(np.float32)
    col = (t % GRID_W).astype(np.float32)
    axis_dims = GQA_HEAD_DIM // 2
    freqs = np.float32(ROPE_THETA) ** (-np.arange(0, axis_dims, 2, dtype=np.float32) / np.float32(axis_dims))
    ang = np.concatenate([row[:, None] * freqs, col[:, None] * freqs], axis=-1).astype(np.float32)
    cos = np.repeat(np.cos(ang), 2, axis=-1)
    sin = np.stack([-np.sin(ang), np.sin(ang)], axis=-1).reshape(s, GQA_HEAD_DIM)
    return np.tile(cos, (1, 2)).astype(np.float32), np.tile(sin, (1, 2)).astype(np.float32)


def _rope_prep(proj3d, q_gain, k_gain, *, ts):
    b, s, width = proj3d.shape
    cos, sin = _axial_rope_tables(s)
    qg = jnp.tile(q_gain.astype(F32), 2)[:, None]
    kg = jnp.tile(k_gain.astype(F32), 2)[None, :]
    head_out = lambda n: pl.BlockSpec((1, n, ts, GQA_HEAD_DIM), lambda i, t: (i, 0, t, 0))
    shape = lambda n: jax.ShapeDtypeStruct((b, n, s, GQA_HEAD_DIM), BF16)
    t_out = lambda n: pl.BlockSpec((1, n, GQA_HEAD_DIM, ts), lambda i, t: (i, 0, 0, t))
    t_shape = lambda n: jax.ShapeDtypeStruct((b, n, GQA_HEAD_DIM, s), BF16)
    return pl.pallas_call(
        _rope_prep_kernel, grid=(b, s // ts),
        in_specs=[
            pl.BlockSpec((1, ts, width), lambda i, t: (i, t, 0)),
            pl.BlockSpec((GQA_PAIR, 1), lambda i, t: (0, 0)),
            pl.BlockSpec((1, GQA_PAIR), lambda i, t: (0, 0)),
            pl.BlockSpec((ts, GQA_PAIR), lambda i, t: (t, 0)),
            pl.BlockSpec((ts, GQA_PAIR), lambda i, t: (t, 0)),
            pl.BlockSpec((GQA_PAIR, ts), lambda i, t: (0, t)),
            pl.BlockSpec((GQA_PAIR, ts), lambda i, t: (0, t)),
        ],
        out_specs=[t_out(GQA_HEADS), head_out(GQA_KV_HEADS), t_out(GQA_KV_HEADS)],
        out_shape=[t_shape(GQA_HEADS), shape(GQA_KV_HEADS), t_shape(GQA_KV_HEADS)],
        compiler_params=_params(("parallel", "parallel")), name="gqa_norm_rope",
    )(proj3d, qg, kg, jnp.asarray(cos), jnp.asarray(sin), jnp.asarray(cos.T.copy()), jnp.asarray(sin.T.copy()))


GQA_KV_CHUNK = 256
GQA_SUM_ROWS = 16
GQA_SUB_Q = 128


def _gqa_kernel(q_ref, k_ref, vt_ref, o_ref, s_a, s_b, m_a, m_b, *, tq, seq):
    t = pl.program_id(0)
    n_sub = tq // GQA_SUB_Q
    cols = GQA_REP * GQA_SUB_Q

    @pl.when(t == 0)
    def _():
        s_b[...] = jnp.zeros(s_b.shape, F32)
        m_b[...] = jnp.zeros(m_b.shape, F32)

    def sub_step(sub, s_cur, m_cur, s_prev, m_prev_ref):
        qs = slice(sub * GQA_SUB_Q, (sub + 1) * GQA_SUB_Q)
        qt = jnp.concatenate([q_ref[0, r, :, qs] for r in range(GQA_REP)], axis=1)
        m_prev = m_prev_ref[sub]
        m_run = None
        ones = jnp.ones((GQA_SUM_ROWS, GQA_KV_CHUNK), BF16)
        acc = jnp.zeros((GQA_HEAD_DIM + GQA_SUM_ROWS, cols), F32)
        for i in range(seq // GQA_KV_CHUNK):
            rows = slice(i * GQA_KV_CHUNK, (i + 1) * GQA_KV_CHUNK)
            st = _dot(k_ref[0, 0, rows, :], qt)
            s_cur[sub, rows, :] = st
            cm = jnp.max(st, axis=0, keepdims=True)
            m_run = cm if m_run is None else jnp.maximum(m_run, cm)
            p = jnp.exp2(s_prev[sub, rows, :] - m_prev)
            vt_aug = jnp.concatenate([vt_ref[0, 0, :, rows], ones], axis=0)
            acc = acc + _dot(vt_aug, p.astype(BF16))
        m_cur[sub] = m_run
        ot = acc[:GQA_HEAD_DIM] * (1.0 / acc[GQA_HEAD_DIM:GQA_HEAD_DIM + 1])
        for r in range(GQA_REP):
            o_ref[0, qs, r * GQA_HEAD_DIM:(r + 1) * GQA_HEAD_DIM] = (
                ot[:, r * GQA_SUB_Q:(r + 1) * GQA_SUB_Q].T.astype(BF16))

    def step(*bufs):
        for sub in range(n_sub):
            sub_step(sub, *bufs)

    pl.when(t % 2 == 0)(lambda: step(s_a, m_a, s_b, m_b))
    pl.when(t % 2 == 1)(lambda: step(s_b, m_b, s_a, m_a))


def _gqa_attention(q, k, vt, *, tq):
    b, _, _, s = q.shape
    nq = s // tq
    n_blocks = b * GQA_KV_HEADS * nq
    n_sub = tq // GQA_SUB_Q
    cols = GQA_REP * GQA_SUB_Q

    def unravel(u):
        return u // (nq * GQA_KV_HEADS), (u // nq) % GQA_KV_HEADS, u % nq

    def score_block(t):
        return unravel(jnp.minimum(t, n_blocks - 1))

    def finish_block(t):
        return unravel(jnp.maximum(t - 1, 0))

    def q_map(t):
        i, g, j = score_block(t)
        return (i, g, 0, j)

    def k_map(t):
        i, g, _ = score_block(t)
        return (i, g, 0, 0)

    def vt_map(t):
        i, g, _ = finish_block(t)
        return (i, g, 0, 0)

    def o_map(t):
        i, g, j = finish_block(t)
        return (i, j, g)

    kern = functools.partial(_gqa_kernel, tq=tq, seq=s)
    return pl.pallas_call(
        kern, grid=(n_blocks + 1,),
        in_specs=[
            pl.BlockSpec((1, GQA_REP, GQA_HEAD_DIM, tq), q_map),
            pl.BlockSpec((1, 1, s, GQA_HEAD_DIM), k_map),
            pl.BlockSpec((1, 1, GQA_HEAD_DIM, s), vt_map),
        ],
        out_specs=pl.BlockSpec((1, tq, GQA_REP * GQA_HEAD_DIM), o_map),
        out_shape=jax.ShapeDtypeStruct((b, s, GQA_Q_WIDTH), BF16),
        scratch_shapes=[pltpu.VMEM((n_sub, s, cols), F32), pltpu.VMEM((n_sub, s, cols), F32),
                        pltpu.VMEM((n_sub, 1, cols), F32), pltpu.VMEM((n_sub, 1, cols), F32)],
        compiler_params=_params(("arbitrary",)), name="gqa_attention",
    )(q, k, vt)


def _even_mixer(x2d, b, s, mix_norm, w_in, q_gain, k_gain, rpb, conv_w, conv_b, dt_bias, a_log, d_skip, out_gain,
                w_out):
    w_main = w_in.astype(BF16)
    w_dt = jnp.transpose(w_in[:, EVEN_MAIN_WIDTH:].reshape(-1, 2, SSD_GROUPS, SSD_HEADS_PER_GROUP),
                         (2, 1, 3, 0)).reshape(2 * SSD_HEADS, -1).astype(BF16)
    proj, xs, bt, cc, dt_rows = _even_in_proj(x2d, mix_norm.astype(F32)[None, :], w_main, w_dt, conv_w.astype(F32),
                                              conv_b.astype(F32)[None, :], tm=1024, seq=s, n_main=EVEN_XBC_OFFSET,
                                              name="even_in_proj")
    proj3d = proj.reshape(b, s, EVEN_XBC_OFFSET)
    na_out = _neighbourhood_attention(proj3d, q_gain, k_gain, rpb)
    ssd_out = _ssd_mixer(proj3d, xs.reshape(b, s, -1), bt, cc.reshape(b, s, -1), dt_rows, dt_bias, a_log, d_skip,
                         out_gain)
    w_out_bf = w_out.astype(BF16)
    return ([na_out.reshape(b * s, NA_WIDTH), ssd_out.reshape(b * s, SSD_D_INNER)],
            [w_out_bf[:NA_WIDTH], w_out_bf[NA_WIDTH:]])


def _odd_mixer(x2d, b, s, mix_norm, w_qkv, q_gain, k_gain, w_out):
    proj = _norm_proj(x2d, mix_norm.astype(F32)[None, :], w_qkv.astype(BF16), tm=1024, name="odd_qkv_proj")
    q, k, vt = _rope_prep(proj.reshape(b, s, -1), q_gain, k_gain, ts=1024)
    attn = _gqa_attention(q, k, vt, tq=512)
    return [attn.reshape(b * s, GQA_Q_WIDTH)], [w_out.astype(BF16)]


def kernel(x, even_mix_norm, even_w_in, na_q_norm, na_k_norm, na_rel_bias, ssd_conv_w, ssd_conv_b, ssd_dt_bias, ssd_A_log, ssd_D, ssd_out_norm, even_w_out, odd_mix_norm, odd_w_qkv, gqa_q_norm, gqa_k_norm, odd_w_out, ffn_norm, ffn_w13, ffn_w2):
    b, s, d = x.shape
    depth = ffn_norm.shape[0]
    h = x.reshape(b * s, d)
    w13_bf = ffn_w13.astype(BF16)
    w2_bf = ffn_w2.astype(BF16)
    for layer in range(depth):
        i = layer // 2
        if layer % 2 == 0:
            acts, w_outs = _even_mixer(h, b, s, even_mix_norm[i], even_w_in[i], na_q_norm[i], na_k_norm[i],
                                       na_rel_bias[i], ssd_conv_w[i], ssd_conv_b[i], ssd_dt_bias[i], ssd_A_log[i],
                                       ssd_D[i], ssd_out_norm[i], even_w_out[i])
        else:
            acts, w_outs = _odd_mixer(h, b, s, odd_mix_norm[i], odd_w_qkv[i], gqa_q_norm[i], gqa_k_norm[i],
                                      odd_w_out[i])
        h = _mix_ffn(acts, w_outs, h, ffn_norm[layer].astype(F32)[None, :], w13_bf, w2_bf, layer, tm=1024,
                     name="mix_out_ffn_even" if layer % 2 == 0 else "mix_out_ffn_odd")
    return h.reshape(b, s, d)
```
